```python
import jax, jax.numpy as jnp
from jax import lax
import numpy as np

D_MODEL = 1024
BATCH = 2
SEQ = 16384
DEPTH = 2

N_EVEN = (DEPTH + 1) // 2
N_ODD = DEPTH // 2
EPS = 1e-6

D_POOL = D_MODEL // 2
POOL_WINDOWS = (2, 4, 8, 16)
N_POOL_GROUPS = len(POOL_WINDOWS)
POOL_GROUP = D_POOL // N_POOL_GROUPS
D_CONF = D_MODEL // 2
CONF_KERNEL = 31

D_SCONV = D_MODEL // 2
SCONV_KERNEL = 3
D_GMLP = D_MODEL // 2
GMLP_HEADS = 4
GMLP_HEAD_DIM = D_GMLP // GMLP_HEADS
CHUNK = 128

N_EXPERTS = 16
N_GROUPS = 4
EXPERTS_PER_GROUP = N_EXPERTS // N_GROUPS
TOP_K = 2
D_EXPERT = 512

kernel_name = "hybrid_pool_conformer_shortconv_gmlp_moe"


def rms_norm(x, g):
    xf = x.astype(jnp.float32)
    y = xf * lax.rsqrt(jnp.mean(xf * xf, axis=-1, keepdims=True) + EPS)
    return (y * g.astype(jnp.float32)).astype(x.dtype)


def layer_norm(x, g, b):
    xf = x.astype(jnp.float32)
    mu = jnp.mean(xf, axis=-1, keepdims=True)
    var = jnp.mean(jnp.square(xf - mu), axis=-1, keepdims=True)
    y = (xf - mu) * lax.rsqrt(var + EPS)
    return (y * g.astype(jnp.float32) + b.astype(jnp.float32)).astype(x.dtype)


def modulate(h, shift, scale):
    return h * (1.0 + scale[:, None, :]) + shift[:, None, :]


def causal_window_mean(u, w):
    s = u.shape[1]
    cs = jnp.cumsum(u.astype(jnp.float32), axis=1)
    cs_lag = jnp.pad(cs, ((0, 0), (w, 0), (0, 0)))[:, :s]
    cnt = jnp.minimum(jnp.arange(1, s + 1), w).astype(jnp.float32)
    return ((cs - cs_lag) / cnt[None, :, None]).astype(u.dtype)


def depthwise_causal_conv(x, w, b=None):
    k, ch = w.shape
    y = lax.conv_general_dilated(x, w[:, None, :], window_strides=(1,),
                                 padding=[(k - 1, 0)],
                                 dimension_numbers=('NWC', 'WIO', 'NWC'),
                                 feature_group_count=ch)
    return y if b is None else y + b


def mixer_pool_conformer(h, w_in, pool_w, pool_scale, conv_w, conv_b, ln_g, ln_b, w_out):
    zp, za, zg = jnp.split(h @ w_in, [D_POOL, D_POOL + D_CONF], axis=-1)
    bsz, s, _ = zp.shape
    up = zp.reshape(bsz, s, N_POOL_GROUPS, POOL_GROUP)
    diffs = jnp.stack([causal_window_mean(up[:, :, g], w) - up[:, :, g]
                       for g, w in enumerate(POOL_WINDOWS)], axis=2)
    pool_out = jnp.einsum('bsgc,gcd->bsgd', diffs, pool_w).reshape(bsz, s, D_POOL) * pool_scale
    glu = za * jax.nn.sigmoid(zg)
    conv = depthwise_causal_conv(glu, conv_w, conv_b)
    conf_out = jax.nn.silu(layer_norm(conv, ln_g, ln_b))
    return jnp.concatenate([pool_out, conf_out], axis=-1) @ w_out


def mixer_shortconv_gmlp(h, w_in, sconv_w, ln_g, ln_b, ws, bs, w_out):
    bg, cg, hv, zd = jnp.split(h @ w_in, [D_SCONV, 2 * D_SCONV, 3 * D_SCONV], axis=-1)
    sc_out = bg * depthwise_causal_conv(cg * hv, sconv_w)
    u, v = jnp.split(jax.nn.gelu(zd), 2, axis=-1)
    v = layer_norm(v, ln_g, ln_b)
    bsz, s, _ = v.shape
    vc = v.reshape(bsz, s // CHUNK, CHUNK, GMLP_HEADS, GMLP_HEAD_DIM)
    mask = jnp.tril(jnp.ones((CHUNK, CHUNK), dtype=ws.dtype))
    mixed = jnp.einsum('hts,bnshd->bnthd', ws * mask, vc) + bs.T[None, None, :, :, None]
    gm_out = u * mixed.reshape(bsz, s, D_GMLP)
    return jnp.concatenate([sc_out, gm_out], axis=-1) @ w_out


def grouped_top2_gates(h, router_w, router_bias):
    probs = jax.nn.softmax((h @ router_w).astype(jnp.float32), axis=-1)
    sel = probs + router_bias.astype(jnp.float32)
    sel_g = sel.reshape(sel.shape[:-1] + (N_GROUPS, EXPERTS_PER_GROUP))
    group_score = jnp.sum(lax.top_k(sel_g, TOP_K)[0], axis=-1)
    gidx = jnp.argmax(group_score, axis=-1)
    in_group = jax.nn.one_hot(gidx, N_GROUPS, dtype=jnp.float32)[..., None] > 0
    masked = jnp.where(in_group, sel_g, -jnp.inf).reshape(sel.shape)
    _, idx = lax.top_k(masked, TOP_K)
    w_sel = jnp.take_along_axis(probs, idx, axis=-1)
    w_sel = w_sel / jnp.sum(w_sel, axis=-1, keepdims=True)
    return jnp.sum(jax.nn.one_hot(idx, N_EXPERTS, dtype=jnp.float32) * w_sel[..., None], axis=-2)


def moe(h, router_w, router_bias, w_gate, w_up, w_down):
    gates = grouped_top2_gates(h, router_w, router_bias).astype(h.dtype)
    out = jnp.zeros_like(h)
    for e in range(N_EXPERTS):
        y = (jax.nn.silu(h @ w_gate[e]) * (h @ w_up[e])) @ w_down[e]
        out = out + gates[..., e:e + 1] * y
    return out


def setup_inputs(seed: int = 0) -> dict:
    key = jax.random.key(seed)
    ks = jax.random.split(key, 32)
    f32 = jnp.float32
    nrm = lambda k, shape, s: jax.random.normal(k, shape, f32) * s
    D = D_MODEL
    return {
        "x": nrm(ks[0], (BATCH, SEQ, D), 1.0),
        "c": nrm(ks[1], (BATCH, D), 1.0),
        "norm1_g": 1.0 + nrm(ks[2], (DEPTH, D), 0.05),
        "norm2_g": 1.0 + nrm(ks[3], (DEPTH, D), 0.05),
        "ada_w": nrm(ks[4], (DEPTH, D, 6 * D), 0.5 * D ** -0.5),
        "ada_b": nrm(ks[5], (DEPTH, 6 * D), 0.02),
        "ab_w_in": nrm(ks[6], (N_EVEN, D, D_POOL + 2 * D_CONF), D ** -0.5),
        "pool_w": nrm(ks[7], (N_EVEN, N_POOL_GROUPS, POOL_GROUP, POOL_GROUP), POOL_GROUP ** -0.5),
        "pool_scale": 1.0 + nrm(ks[8], (N_EVEN, D_POOL), 0.1),
        "conf_conv_w": nrm(ks[9], (N_EVEN, CONF_KERNEL, D_CONF), CONF_KERNEL ** -0.5),
        "conf_conv_b": nrm(ks[10], (N_EVEN, D_CONF), 0.02),
        "conf_ln_g": 1.0 + nrm(ks[11], (N_EVEN, D_CONF), 0.05),
        "conf_ln_b": nrm(ks[12], (N_EVEN, D_CONF), 0.02),
        "ab_w_out": nrm(ks[13], (N_EVEN, D_POOL + D_CONF, D), (D_POOL + D_CONF) ** -0.5),
        "cd_w_in": nrm(ks[14], (N_ODD, D, 3 * D_SCONV + 2 * D_GMLP), D ** -0.5),
        "sconv_w": nrm(ks[15], (N_ODD, SCONV_KERNEL, D_SCONV), SCONV_KERNEL ** -0.5),
        "gmlp_ln_g": 1.0 + nrm(ks[16], (N_ODD, D_GMLP), 0.05),
        "gmlp_ln_b": nrm(ks[17], (N_ODD, D_GMLP), 0.02),
        "gmlp_ws": nrm(ks[18], (N_ODD, GMLP_HEADS, CHUNK, CHUNK), CHUNK ** -0.5),
        "gmlp_bs": 1.0 + nrm(ks[19], (N_ODD, GMLP_HEADS, CHUNK), 0.1),
        "cd_w_out": nrm(ks[20], (N_ODD, D_SCONV + D_GMLP, D), (D_SCONV + D_GMLP) ** -0.5),
        "router_w": nrm(ks[21], (D, N_EXPERTS), D ** -0.5),
        "router_bias": nrm(ks[22], (N_EXPERTS,), 0.01),
        "exp_w_gate": nrm(ks[23], (DEPTH, N_EXPERTS, D, D_EXPERT), D ** -0.5),
        "exp_w_up": nrm(ks[24], (DEPTH, N_EXPERTS, D, D_EXPERT), D ** -0.5),
        "exp_w_down": nrm(ks[25], (DEPTH, N_EXPERTS, D_EXPERT, D), D_EXPERT ** -0.5),
        "final_g": 1.0 + nrm(ks[26], (D,), 0.05),
    }


def reference(x, c, norm1_g, norm2_g, ada_w, ada_b, ab_w_in, pool_w, pool_scale,
              conf_conv_w, conf_conv_b, conf_ln_g, conf_ln_b, ab_w_out, cd_w_in, sconv_w,
              gmlp_ln_g, gmlp_ln_b, gmlp_ws, gmlp_bs, cd_w_out, router_w, router_bias,
              exp_w_gate, exp_w_up, exp_w_down, final_g):
    cond = jax.nn.silu(c)
    for l in range(DEPTH):
        mod = cond @ ada_w[l] + ada_b[l]
        sh1, sc1, g1, sh2, sc2, g2 = jnp.split(mod, 6, axis=-1)
        h = modulate(rms_norm(x, norm1_g[l]), sh1, sc1)
        i = l // 2
        if l % 2 == 0:
            m = mixer_pool_conformer(h, ab_w_in[i], pool_w[i], pool_scale[i], conf_conv_w[i],
                                     conf_conv_b[i], conf_ln_g[i], conf_ln_b[i], ab_w_out[i])
        else:
            m = mixer_shortconv_gmlp(h, cd_w_in[i], sconv_w[i], gmlp_ln_g[i], gmlp_ln_b[i],
                                     gmlp_ws[i], gmlp_bs[i], cd_w_out[i])
        x = x + g1[:, None, :] * m
        h = modulate(rms_norm(x, norm2_g[l]), sh2, sc2)
        x = x + g2[:, None, :] * moe(h, router_w, router_bias,
                                     exp_w_gate[l], exp_w_up[l], exp_w_down[l])
    return rms_norm(x, final_g)
```

```python
import functools

import jax
import jax.numpy as jnp
from jax import lax
from jax.experimental import pallas as pl
from jax.experimental.pallas import tpu as pltpu

D_MODEL = 1024
EPS = 1e-6
POOL_WINDOWS = (2, 4, 8, 16)
POOL_GROUP = 128
D_HALF = 512
CONF_KERNEL = 31
SCONV_KERNEL = 3
CHUNK = 128
GMLP_HEADS = 4
N_EXPERTS = 16
N_GROUPS = 4
EXPERTS_PER_GROUP = 4
D_EXPERT = 512

V7X_LANES = 128
V7X_SUBLANES = 8
V7X_VMEM_LIMIT_BYTES = 56 * 1024 * 1024

MIX_TILE = 512
MOE_TILE = 1024
CONV_HIST = 32
POOL_HIST = 16
SCONV_HIST = 8

BF16 = jnp.bfloat16
F32 = jnp.float32


def _rms_mod(x, g_row, shift_row, scale_row):
    ms = jnp.mean(x * x, axis=-1, keepdims=True)
    y = x * lax.rsqrt(ms + EPS)
    return (y * g_row) * (1.0 + scale_row) + shift_row


def _layer_norm(x, g_row, b_row):
    mu = jnp.mean(x, axis=-1, keepdims=True)
    xc = x - mu
    var = jnp.mean(xc * xc, axis=-1, keepdims=True)
    return xc * lax.rsqrt(var + EPS) * g_row + b_row


def _sigmoid(x):
    return 1.0 / (1.0 + jnp.exp(-x))


def _silu(x):
    return x * _sigmoid(x)


def _gelu_tanh(x):
    c = 0.7978845608028654
    return 0.5 * x * (1.0 + jnp.tanh(c * (x + 0.044715 * (x * x * x))))


def _ada_kernel(ct_ref, w_ref, b_ref, o_ref):
    ct = ct_ref[...]
    cond = _silu(ct)
    w = w_ref[0]
    nb = ct.shape[1]
    for b in range(nb):
        col = cond[:, b:b + 1]
        o_ref[0, b:b + 1, :] = jnp.sum(col * w, axis=0, keepdims=True) + b_ref[0]


def _ada_mod(c, ada_w, ada_b):
    depth, d, six_d = ada_w.shape
    bsz = c.shape[0]
    nb = D_MODEL
    return pl.pallas_call(
        _ada_kernel,
        grid=(depth, six_d // nb),
        in_specs=[
            pl.BlockSpec((d, bsz), lambda l, j: (0, 0)),
            pl.BlockSpec((1, d, nb), lambda l, j: (l, 0, j)),
            pl.BlockSpec((1, 1, nb), lambda l, j: (l, 0, j)),
        ],
        out_specs=pl.BlockSpec((1, bsz, nb), lambda l, j: (l, 0, j)),
        out_shape=jax.ShapeDtypeStruct((depth, bsz, six_d), F32),
        compiler_params=pltpu.CompilerParams(
            dimension_semantics=("arbitrary", "arbitrary"),
            vmem_limit_bytes=V7X_VMEM_LIMIT_BYTES),
        name="ada_mod",
    )(c.T, ada_w, ada_b.reshape(depth, 1, six_d))


def _route(h2_bf, rwt_ref, rbias_ref, eidx_ref, wsel_ref):
    nt = (((1,), (1,)), ((), ()))
    r = lax.dot_general(rwt_ref[...], h2_bf, nt, preferred_element_type=F32)
    logits = r[:N_EXPERTS] + r[N_EXPERTS:]
    m = jnp.max(logits, axis=0, keepdims=True)
    ex = jnp.exp(logits - m)
    probs = ex / jnp.sum(ex, axis=0, keepdims=True)
    sel = probs + rbias_ref[...]
    s = [sel[e:e + 1] for e in range(N_EXPERTS)]
    p = [probs[e:e + 1] for e in range(N_EXPERTS)]
    best = None
    gi = None
    for g in range(N_GROUPS):
        a, b, c, d = s[4 * g:4 * g + 4]
        hi1, lo1 = jnp.maximum(a, b), jnp.minimum(a, b)
        hi2, lo2 = jnp.maximum(c, d), jnp.minimum(c, d)
        top1 = jnp.maximum(hi1, hi2)
        top2 = jnp.maximum(jnp.minimum(hi1, hi2), jnp.maximum(lo1, lo2))
        score = top1 + top2
        if g == 0:
            best, gi = score, jnp.zeros(score.shape, jnp.int32)
        else:
            upd = score > best
            gi = jnp.where(upd, g, gi)
            best = jnp.where(upd, score, best)
    v, q = [], []
    for j in range(EXPERTS_PER_GROUP):
        vj, qj = s[j], p[j]
        for g in range(1, N_GROUPS):
            pick = gi == g
            vj = jnp.where(pick, s[4 * g + j], vj)
            qj = jnp.where(pick, p[4 * g + j], qj)
        v.append(vj)
        q.append(qj)
    i1 = jnp.zeros(gi.shape, jnp.int32)
    m1 = v[0]
    for j in range(1, EXPERTS_PER_GROUP):
        upd = v[j] > m1
        i1 = jnp.where(upd, j, i1)
        m1 = jnp.where(upd, v[j], m1)
    i2 = jnp.zeros(gi.shape, jnp.int32)
    m2 = jnp.full(m1.shape, -jnp.inf, F32)
    for j in range(EXPERTS_PER_GROUP):
        cand = (i1 != j) & (v[j] > m2)
        i2 = jnp.where(cand, j, i2)
        m2 = jnp.where(cand, v[j], m2)
    pa = q[0]
    pb = q[0]
    for j in range(1, EXPERTS_PER_GROUP):
        pa = jnp.where(i1 == j, q[j], pa)
        pb = jnp.where(i2 == j, q[j], pb)
    tot = pa + pb
    eidx_ref[0:1, :] = gi * EXPERTS_PER_GROUP + i1
    eidx_ref[1:2, :] = gi * EXPERTS_PER_GROUP + i2
    wsel_ref[0:1, :] = pa / tot
    wsel_ref[1:2, :] = pb / tot


def _finish_mixer(x, m, mod, n2g_ref, rwt_ref, rbias_ref, x1_ref, h2_ref, eidx_ref, wsel_ref):
    x1 = x + mod[2:3] * m
    x1_ref[...] = x1
    h2 = _rms_mod(x1, n2g_ref[...], mod[3:4], mod[4:5])
    h2_bf = h2.astype(BF16)
    h2_ref[...] = h2_bf
    _route(h2_bf, rwt_ref, rbias_ref, eidx_ref, wsel_ref)


def _mixer_ab_kernel(tiles_per_seq,
                     x_ref, mod_ref, n1g_ref, n2g_ref, win_ref, poolw_ref, pscale_ref,
                     convw_ref, convb_ref, lng_ref, lnb_ref, wout_ref, rwt_ref, rbias_ref,
                     x1_ref, h2_ref, eidx_ref, wsel_ref,
                     pool_ext, conv_ext):
    i = pl.program_id(0)
    tm = x_ref.shape[0]
    seq_tile = i % tiles_per_seq

    @pl.when(seq_tile == 0)
    def _():
        pool_ext[0:POOL_HIST, :] = jnp.zeros((POOL_HIST, D_HALF), F32)
        conv_ext[0:CONV_HIST, :] = jnp.zeros((CONV_HIST, D_HALF), F32)

    x = x_ref[...]
    mod = mod_ref[0, 0]
    h = _rms_mod(x, n1g_ref[...], mod[0:1], mod[1:2]).astype(BF16)
    z = jnp.dot(h, win_ref[...], preferred_element_type=F32)
    zp = z[:, :D_HALF]
    glu = z[:, D_HALF:2 * D_HALF] * _sigmoid(z[:, 2 * D_HALF:])
    pool_ext[POOL_HIST:POOL_HIST + tm, :] = zp
    conv_ext[CONV_HIST:CONV_HIST + tm, :] = glu

    row = lax.broadcasted_iota(jnp.int32, (tm, 1), 0)
    pos1 = (seq_tile * tm + row + 1).astype(F32)
    pool_out = []
    for g, w in enumerate(POOL_WINDOWS):
        cols = slice(g * POOL_GROUP, (g + 1) * POOL_GROUP)
        u = pool_ext[:, cols]
        acc = u
        span = 1
        while span < w:
            acc = acc + pltpu.roll(acc, span, axis=0)
            span *= 2
        wsum = acc[POOL_HIST:POOL_HIST + tm]
        inv_cnt = 1.0 / jnp.minimum(pos1, float(w))
        diff = wsum * inv_cnt - zp[:, cols]
        po = jnp.dot(diff.astype(BF16), poolw_ref[g], preferred_element_type=F32)
        pool_out.append(po * pscale_ref[:, cols])

    convw = convw_ref[...]
    ext_rows = tm + V7X_SUBLANES
    conv = None
    for r in range(V7X_SUBLANES):
        vr = None
        for a in range(CONV_HIST // V7X_SUBLANES):
            lag = V7X_SUBLANES * a + r
            if lag >= CONF_KERNEL:
                continue
            k = CONF_KERNEL - 1 - lag
            start = CONV_HIST - V7X_SUBLANES - V7X_SUBLANES * a
            term = convw[k:k + 1, :] * conv_ext[start:start + ext_rows, :]
            vr = term if vr is None else vr + term
        if r:
            vr = pltpu.roll(vr, r, axis=0)
        conv = vr if conv is None else conv + vr
    conv = conv[V7X_SUBLANES:V7X_SUBLANES + tm] + convb_ref[...]
    conf = _silu(_layer_norm(conv, lng_ref[...], lnb_ref[...]))

    pool_ext[0:POOL_HIST, :] = zp[tm - POOL_HIST:tm]
    conv_ext[0:CONV_HIST, :] = glu[tm - CONV_HIST:tm]

    m = jnp.dot(conf.astype(BF16), wout_ref[D_HALF:, :], preferred_element_type=F32)
    for g in range(len(POOL_WINDOWS)):
        rows = slice(g * POOL_GROUP, (g + 1) * POOL_GROUP)
        m = m + jnp.dot(pool_out[g].astype(BF16), wout_ref[rows, :], preferred_element_type=F32)
    _finish_mixer(x, m, mod, n2g_ref, rwt_ref, rbias_ref, x1_ref, h2_ref, eidx_ref, wsel_ref)


def _mixer_cd_kernel(tiles_per_seq,
                     x_ref, mod_ref, n1g_ref, n2g_ref, win_ref, sconvw_ref, lng_ref, lnb_ref,
                     ws_ref, bsf_ref, wout_ref, rwt_ref, rbias_ref,
                     x1_ref, h2_ref, eidx_ref, wsel_ref,
                     sconv_ext):
    i = pl.program_id(0)
    tm = x_ref.shape[0]
    seq_tile = i % tiles_per_seq

    @pl.when(seq_tile == 0)
    def _():
        sconv_ext[0:SCONV_HIST, :] = jnp.zeros((SCONV_HIST, D_HALF), F32)

    x = x_ref[...]
    mod = mod_ref[0, 0]
    h = _rms_mod(x, n1g_ref[...], mod[0:1], mod[1:2]).astype(BF16)
    z = jnp.dot(h, win_ref[...], preferred_element_type=F32)
    bg = z[:, :D_HALF]
    ch = z[:, D_HALF:2 * D_HALF] * z[:, 2 * D_HALF:3 * D_HALF]
    zd = _gelu_tanh(z[:, 3 * D_HALF:])
    u = zd[:, :D_HALF]
    v = _layer_norm(zd[:, D_HALF:], lng_ref[...], lnb_ref[...])

    sconv_ext[SCONV_HIST:SCONV_HIST + tm, :] = ch
    sw = sconvw_ref[...]
    ext = sconv_ext[...]
    conv = sw[2:3, :] * ext
    conv = conv + sw[1:2, :] * pltpu.roll(ext, 1, axis=0)
    conv = conv + sw[0:1, :] * pltpu.roll(ext, 2, axis=0)
    sc_out = bg * conv[SCONV_HIST:SCONV_HIST + tm]
    sconv_ext[0:SCONV_HIST, :] = ch[tm - SCONV_HIST:tm]

    r_i = lax.broadcasted_iota(jnp.int32, (CHUNK, CHUNK), 0)
    c_i = lax.broadcasted_iota(jnp.int32, (CHUNK, CHUNK), 1)
    tril = c_i <= r_i
    wm = [jnp.where(tril, ws_ref[hd], 0.0).astype(BF16) for hd in range(GMLP_HEADS)]
    v_bf = v.astype(BF16)
    bsf = bsf_ref[...]
    gm_rows = []
    for n in range(tm // CHUNK):
        rows = slice(n * CHUNK, (n + 1) * CHUNK)
        heads = []
        for hd in range(GMLP_HEADS):
            cols = slice(hd * POOL_GROUP, (hd + 1) * POOL_GROUP)
            heads.append(jnp.dot(wm[hd], v_bf[rows, cols], preferred_element_type=F32))
        mixed = jnp.concatenate(heads, axis=1) + bsf
        gm_rows.append(u[rows] * mixed)
    gm_out = jnp.concatenate(gm_rows, axis=0)

    m = jnp.dot(sc_out.astype(BF16), wout_ref[:D_HALF, :], preferred_element_type=F32)
    m = m + jnp.dot(gm_out.astype(BF16), wout_ref[D_HALF:, :], preferred_element_type=F32)
    _finish_mixer(x, m, mod, n2g_ref, rwt_ref, rbias_ref, x1_ref, h2_ref, eidx_ref, wsel_ref)


def _const_spec(shape):
    nd = len(shape)
    return pl.BlockSpec(shape, lambda i: (0,) * nd)


def _mixer_call(kernel_fn, layer, x, mod4, n1g, n2g, weights, rwt, rbias, scratch, seq_len, name):
    n_tok, d = x.shape
    tm = MIX_TILE
    tiles_per_seq = seq_len // tm
    in_specs = [
        pl.BlockSpec((tm, d), lambda i: (i, 0)),
        pl.BlockSpec((1, 1, 6, d), lambda i: (layer, i // tiles_per_seq, 0, 0)),
        _const_spec(n1g.shape),
        _const_spec(n2g.shape),
    ] + [_const_spec(w.shape) for w in weights] + [_const_spec(rwt.shape), _const_spec(rbias.shape)]
    out_specs = [
        pl.BlockSpec((tm, d), lambda i: (i, 0)),
        pl.BlockSpec((tm, d), lambda i: (i, 0)),
        pl.BlockSpec((2, tm), lambda i: (0, i)),
        pl.BlockSpec((2, tm), lambda i: (0, i)),
    ]
    out_shape = [
        jax.ShapeDtypeStruct((n_tok, d), F32),
        jax.ShapeDtypeStruct((n_tok, d), BF16),
        jax.ShapeDtypeStruct((2, n_tok), jnp.int32),
        jax.ShapeDtypeStruct((2, n_tok), F32),
    ]
    return pl.pallas_call(
        functools.partial(kernel_fn, tiles_per_seq),
        grid=(n_tok // tm,),
        in_specs=in_specs,
        out_specs=out_specs,
        out_shape=out_shape,
        scratch_shapes=scratch,
        compiler_params=pltpu.CompilerParams(
            dimension_semantics=("arbitrary",),
            vmem_limit_bytes=V7X_VMEM_LIMIT_BYTES),
        name=name,
    )(x, mod4, n1g, n2g, *weights, rwt, rbias)


def _moe_kernel(final_norm, x_ref, h_ref, gates_ref, mod_ref, wg_ref, wu_ref, wd_ref, fg_ref,
                o_ref, acc_ref):
    e = pl.program_id(1)

    @pl.when(e == 0)
    def _():
        acc_ref[...] = jnp.zeros_like(acc_ref)

    h = h_ref[...]
    a = jnp.dot(h, wg_ref[0, 0].astype(BF16), preferred_element_type=F32)
    b = jnp.dot(h, wu_ref[0, 0].astype(BF16), preferred_element_type=F32)
    gates = gates_ref[...]
    lane = lax.broadcasted_iota(jnp.int32, gates.shape, 1)
    gcol = jnp.sum(jnp.where(lane == e, gates, 0.0), axis=1, keepdims=True)
    t = (_silu(a) * b * gcol).astype(BF16)
    acc_ref[...] += jnp.dot(t, wd_ref[0, 0].astype(BF16), preferred_element_type=F32)

    @pl.when(e == N_EXPERTS - 1)
    def _():
        mod = mod_ref[0, 0]
        y = x_ref[...] + mod[5:6] * acc_ref[...]
        if final_norm:
            ms = jnp.mean(y * y, axis=-1, keepdims=True)
            y = y * lax.rsqrt(ms + EPS) * fg_ref[...]
        o_ref[...] = y


def _moe_call(layer, x1, h2, gates, mod4, w_gate, w_up, w_down, final_g, seq_len, final_norm):
    n_tok, d = x1.shape
    tm = MOE_TILE
    tiles_per_seq = seq_len // tm
    return pl.pallas_call(
        functools.partial(_moe_kernel, final_norm),
        grid=(n_tok // tm, N_EXPERTS),
        in_specs=[
            pl.BlockSpec((tm, d), lambda i, e: (i, 0)),
            pl.BlockSpec((tm, d), lambda i, e: (i, 0)),
            pl.BlockSpec((tm, N_EXPERTS), lambda i, e: (i, 0)),
            pl.BlockSpec((1, 1, 6, d), lambda i, e: (layer, i // tiles_per_seq, 0, 0)),
            pl.BlockSpec((1, 1, d, D_EXPERT), lambda i, e: (layer, e, 0, 0)),
            pl.BlockSpec((1, 1, d, D_EXPERT), lambda i, e: (layer, e, 0, 0)),
            pl.BlockSpec((1, 1, D_EXPERT, d), lambda i, e: (layer, e, 0, 0)),
            pl.BlockSpec((1, d), lambda i, e: (0, 0)),
        ],
        out_specs=pl.BlockSpec((tm, d), lambda i, e: (i, 0)),
        out_shape=jax.ShapeDtypeStruct((n_tok, d), F32),
        scratch_shapes=[pltpu.VMEM((tm, d), F32)],
        compiler_params=pltpu.CompilerParams(
            dimension_semantics=("arbitrary", "arbitrary"),
            vmem_limit_bytes=V7X_VMEM_LIMIT_BYTES),
        name="moe_dense_l%d" % layer,
    )(x1, h2, gates, mod4, w_gate, w_up, w_down, final_g)


def _dense_gates(eidx, wsel):
    onehot = jax.nn.one_hot(eidx, N_EXPERTS, dtype=F32)
    return jnp.sum(onehot * wsel[..., None], axis=0)


def kernel(x, c, norm1_g, norm2_g, ada_w, ada_b, ab_w_in, pool_w, pool_scale, conf_conv_w, conf_conv_b, conf_ln_g, conf_ln_b, ab_w_out, cd_w_in, sconv_w, gmlp_ln_g, gmlp_ln_b, gmlp_ws, gmlp_bs, cd_w_out, router_w, router_bias, exp_w_gate, exp_w_up, exp_w_down, final_g):
    bsz, seq_len, d = x.shape
    n_tok = bsz * seq_len
    xf = x.reshape(n_tok, d)

    mod = _ada_mod(c, ada_w, ada_b)
    mod4 = mod.reshape(mod.shape[0], bsz, 6, d)

    rw_hi = router_w.astype(BF16)
    rw_lo = (router_w - rw_hi.astype(F32)).astype(BF16)
    rwt = jnp.concatenate([rw_hi.T, rw_lo.T], axis=0)
    rbias = router_bias.reshape(N_EXPERTS, 1)
    fg = final_g.reshape(1, d)

    weights_ab = [
        ab_w_in[0].astype(BF16), pool_w[0].astype(BF16), pool_scale[0].reshape(1, D_HALF),
        conf_conv_w[0], conf_conv_b[0].reshape(1, D_HALF), conf_ln_g[0].reshape(1, D_HALF),
        conf_ln_b[0].reshape(1, D_HALF), ab_w_out[0].astype(BF16),
    ]
    scratch_ab = [pltpu.VMEM((POOL_HIST + MIX_TILE, D_HALF), F32),
                  pltpu.VMEM((CONV_HIST + MIX_TILE, D_HALF), F32)]
    x1, h2, eidx, wsel = _mixer_call(
        _mixer_ab_kernel, 0, xf, mod4, norm1_g[0:1], norm2_g[0:1], weights_ab, rwt, rbias,
        scratch_ab, seq_len, "mixer_ab")
    x2 = _moe_call(0, x1, h2, _dense_gates(eidx, wsel), mod4, exp_w_gate, exp_w_up, exp_w_down,
                   fg, seq_len, False)

    bsf = jnp.repeat(gmlp_bs[0].T, POOL_GROUP, axis=1)
    weights_cd = [
        cd_w_in[0].astype(BF16), sconv_w[0], gmlp_ln_g[0].reshape(1, D_HALF),
        gmlp_ln_b[0].reshape(1, D_HALF), gmlp_ws[0], bsf, cd_w_out[0].astype(BF16),
    ]
    scratch_cd = [pltpu.VMEM((SCONV_HIST + MIX_TILE, D_HALF), F32)]
    x3, h4, eidx1, wsel1 = _mixer_call(
        _mixer_cd_kernel, 1, x2, mod4, norm1_g[1:2], norm2_g[1:2], weights_cd, rwt, rbias,
        scratch_cd, seq_len, "mixer_cd")
    out = _moe_call(1, x3, h4, _dense_gates(eidx1, wsel1), mod4, exp_w_gate, exp_w_up, exp_w_down,
                    fg, seq_len, True)
    return out.reshape(bsz, seq_len, d)
```

```python
import functools

import jax
import jax.numpy as jnp
from jax import lax
from jax.experimental import pallas as pl
from jax.experimental.pallas import tpu as pltpu

D_MODEL = 1024
EPS = 1e-6
POOL_WINDOWS = (2, 4, 8, 16)
POOL_GROUP = 128
D_HALF = 512
CONF_KERNEL = 31
SCONV_KERNEL = 3
CHUNK = 128
GMLP_HEADS = 4
N_EXPERTS = 16
N_GROUPS = 4
EXPERTS_PER_GROUP = 4
TOP_K = 2
D_EXPERT = 512

V7X_LANES = 128
V7X_SUBLANES = 8
V7X_VMEM_LIMIT_BYTES = 56 * 1024 * 1024

MIX_TILE = 512
MOE_TILE = 256
ROW_TILE = 512
ROW_UNROLL = 8
ROW_CHUNKS = D_MODEL // V7X_LANES
CONV_HIST = 32
POOL_HIST = 16
SCONV_HIST = 8

BF16 = jnp.bfloat16
F32 = jnp.float32


def _rms_mod(x, g_row, shift_row, scale_row):
    ms = jnp.mean(x * x, axis=-1, keepdims=True)
    y = x * lax.rsqrt(ms + EPS)
    return (y * g_row) * (1.0 + scale_row) + shift_row


def _layer_norm(x, g_row, b_row):
    mu = jnp.mean(x, axis=-1, keepdims=True)
    xc = x - mu
    var = jnp.mean(xc * xc, axis=-1, keepdims=True)
    return xc * lax.rsqrt(var + EPS) * g_row + b_row


def _sigmoid(x):
    return 1.0 / (1.0 + jnp.exp(-x))


def _silu(x):
    return x * _sigmoid(x)


def _gelu_tanh(x):
    c = 0.7978845608028654
    return 0.5 * x * (1.0 + jnp.tanh(c * (x + 0.044715 * (x * x * x))))


def _load_rows(ref, n_rows):
    return jnp.concatenate(
        [ref[pl.ds(c, n_rows, stride=ROW_CHUNKS), :] for c in range(ROW_CHUNKS)], axis=1)


def _store_rows(ref, val):
    n_rows = val.shape[0]
    for c in range(ROW_CHUNKS):
        ref[pl.ds(c, n_rows, stride=ROW_CHUNKS), :] = val[:, c * V7X_LANES:(c + 1) * V7X_LANES]


def _ada_kernel(ct_ref, w_ref, b_ref, o_ref):
    ct = ct_ref[...]
    cond = _silu(ct)
    w = w_ref[0]
    nb = ct.shape[1]
    for b in range(nb):
        col = cond[:, b:b + 1]
        o_ref[0, b:b + 1, :] = jnp.sum(col * w, axis=0, keepdims=True) + b_ref[0]


def _ada_mod(c, ada_w, ada_b):
    depth, d, six_d = ada_w.shape
    bsz = c.shape[0]
    nb = D_MODEL
    return pl.pallas_call(
        _ada_kernel,
        grid=(depth, six_d // nb),
        in_specs=[
            pl.BlockSpec((d, bsz), lambda l, j: (0, 0)),
            pl.BlockSpec((1, d, nb), lambda l, j: (l, 0, j)),
            pl.BlockSpec((1, 1, nb), lambda l, j: (l, 0, j)),
        ],
        out_specs=pl.BlockSpec((1, bsz, nb), lambda l, j: (l, 0, j)),
        out_shape=jax.ShapeDtypeStruct((depth, bsz, six_d), F32),
        compiler_params=pltpu.CompilerParams(
            dimension_semantics=("arbitrary", "arbitrary"),
            vmem_limit_bytes=V7X_VMEM_LIMIT_BYTES),
        name="ada_mod",
    )(c.T, ada_w, ada_b.reshape(depth, 1, six_d))


def _route(h2_bf, rwt_ref, rbias_ref, eidx_ref, wsel_ref, rank_ref, counts_ref, cnt_ref):
    nt = (((1,), (1,)), ((), ()))
    r = lax.dot_general(rwt_ref[...], h2_bf, nt, preferred_element_type=F32)
    logits = r[:N_EXPERTS] + r[N_EXPERTS:]
    m = jnp.max(logits, axis=0, keepdims=True)
    ex = jnp.exp(logits - m)
    probs = ex / jnp.sum(ex, axis=0, keepdims=True)
    sel = probs + rbias_ref[...]
    s = [sel[e:e + 1] for e in range(N_EXPERTS)]
    p = [probs[e:e + 1] for e in range(N_EXPERTS)]
    best = None
    gi = None
    for g in range(N_GROUPS):
        a, b, c, d = s[4 * g:4 * g + 4]
        hi1, lo1 = jnp.maximum(a, b), jnp.minimum(a, b)
        hi2, lo2 = jnp.maximum(c, d), jnp.minimum(c, d)
        top1 = jnp.maximum(hi1, hi2)
        top2 = jnp.maximum(jnp.minimum(hi1, hi2), jnp.maximum(lo1, lo2))
        score = top1 + top2
        if g == 0:
            best, gi = score, jnp.zeros(score.shape, jnp.int32)
        else:
            upd = score > best
            gi = jnp.where(upd, g, gi)
            best = jnp.where(upd, score, best)
    v, q = [], []
    for j in range(EXPERTS_PER_GROUP):
        vj, qj = s[j], p[j]
        for g in range(1, N_GROUPS):
            pick = gi == g
            vj = jnp.where(pick, s[4 * g + j], vj)
            qj = jnp.where(pick, p[4 * g + j], qj)
        v.append(vj)
        q.append(qj)
    i1 = jnp.zeros(gi.shape, jnp.int32)
    m1 = v[0]
    for j in range(1, EXPERTS_PER_GROUP):
        upd = v[j] > m1
        i1 = jnp.where(upd, j, i1)
        m1 = jnp.where(upd, v[j], m1)
    i2 = jnp.zeros(gi.shape, jnp.int32)
    m2 = jnp.full(m1.shape, -jnp.inf, F32)
    for j in range(EXPERTS_PER_GROUP):
        cand = (i1 != j) & (v[j] > m2)
        i2 = jnp.where(cand, j, i2)
        m2 = jnp.where(cand, v[j], m2)
    pa = q[0]
    pb = q[0]
    for j in range(1, EXPERTS_PER_GROUP):
        pa = jnp.where(i1 == j, q[j], pa)
        pb = jnp.where(i2 == j, q[j], pb)
    tot = pa + pb
    e0 = gi * EXPERTS_PER_GROUP + i1
    e1 = gi * EXPERTS_PER_GROUP + i2
    eidx_ref[0:1, :] = e0
    eidx_ref[1:2, :] = e1
    wsel_ref[0:1, :] = pa / tot
    wsel_ref[1:2, :] = pb / tot

    t = h2_bf.shape[0]
    e_iota = lax.broadcasted_iota(jnp.int32, (N_EXPERTS, t), 0)
    oh0 = e_iota == e0
    oh1 = e_iota == e1
    both = jnp.where(oh0 | oh1, 1.0, 0.0)
    r_i = lax.broadcasted_iota(jnp.int32, (V7X_LANES, V7X_LANES), 0)
    c_i = lax.broadcasted_iota(jnp.int32, (V7X_LANES, V7X_LANES), 1)
    before = jnp.where(r_i < c_i, 1.0, 0.0).astype(BF16)
    run = cnt_ref[...]
    rank0, rank1 = [], []
    for blk in range(t // V7X_LANES):
        lanes = slice(blk * V7X_LANES, (blk + 1) * V7X_LANES)
        b = both[:, lanes]
        pre = jnp.dot(b.astype(BF16), before, preferred_element_type=F32) + run
        rank0.append(jnp.sum(jnp.where(oh0[:, lanes], pre, 0.0), axis=0, keepdims=True))
        rank1.append(jnp.sum(jnp.where(oh1[:, lanes], pre, 0.0), axis=0, keepdims=True))
        run = run + jnp.sum(b, axis=1, keepdims=True)
    cnt_ref[...] = run
    rank_ref[0:1, :] = jnp.concatenate(rank0, axis=1).astype(jnp.int32)
    rank_ref[1:2, :] = jnp.concatenate(rank1, axis=1).astype(jnp.int32)
    counts_ref[...] = jnp.broadcast_to(run, counts_ref.shape).astype(jnp.int32)


def _finish_mixer(x, m, mod, n2g_ref, rwt_ref, rbias_ref,
                  x1_ref, h2_ref, eidx_ref, wsel_ref, rank_ref, counts_ref, cnt_ref):
    @pl.when(pl.program_id(0) == 0)
    def _():
        cnt_ref[...] = jnp.zeros_like(cnt_ref)

    x1 = x + mod[2:3] * m
    x1_ref[...] = x1
    h2 = _rms_mod(x1, n2g_ref[...], mod[3:4], mod[4:5])
    h2_bf = h2.astype(BF16)
    _store_rows(h2_ref, h2_bf.astype(F32))
    _route(h2_bf, rwt_ref, rbias_ref, eidx_ref, wsel_ref, rank_ref, counts_ref, cnt_ref)


def _moe_residual(x_ref, y0_ref, y1_ref, wt_ref, g2_row):
    tm = x_ref.shape[0]
    wt = wt_ref[...]
    y = wt[:, 0:1] * _load_rows(y0_ref, tm) + wt[:, 1:2] * _load_rows(y1_ref, tm)
    return x_ref[...] + g2_row * y


def _mixer_ab_kernel(tiles_per_seq,
                     x_ref, mod_ref, n1g_ref, n2g_ref, win_ref, poolw_ref, pscale_ref,
                     convw_ref, convb_ref, lng_ref, lnb_ref, wout_ref, rwt_ref, rbias_ref,
                     x1_ref, h2_ref, eidx_ref, wsel_ref, rank_ref, counts_ref,
                     pool_ext, conv_ext, cnt_ref):
    i = pl.program_id(0)
    tm = x_ref.shape[0]
    seq_tile = i % tiles_per_seq

    @pl.when(seq_tile == 0)
    def _():
        pool_ext[0:POOL_HIST, :] = jnp.zeros((POOL_HIST, D_HALF), F32)
        conv_ext[0:CONV_HIST, :] = jnp.zeros((CONV_HIST, D_HALF), F32)

    x = x_ref[...]
    mod = mod_ref[0, 0]
    h = _rms_mod(x, n1g_ref[...], mod[0:1], mod[1:2]).astype(BF16)
    z = jnp.dot(h, win_ref[...], preferred_element_type=F32)
    zp = z[:, :D_HALF]
    glu = z[:, D_HALF:2 * D_HALF] * _sigmoid(z[:, 2 * D_HALF:])
    pool_ext[POOL_HIST:POOL_HIST + tm, :] = zp
    conv_ext[CONV_HIST:CONV_HIST + tm, :] = glu

    row = lax.broadcasted_iota(jnp.int32, (tm, 1), 0)
    pos1 = (seq_tile * tm + row + 1).astype(F32)
    pool_out = []
    for g, w in enumerate(POOL_WINDOWS):
        cols = slice(g * POOL_GROUP, (g + 1) * POOL_GROUP)
        u = pool_ext[:, cols]
        acc = u
        span = 1
        while span < w:
            acc = acc + pltpu.roll(acc, span, axis=0)
            span *= 2
        wsum = acc[POOL_HIST:POOL_HIST + tm]
        inv_cnt = 1.0 / jnp.minimum(pos1, float(w))
        diff = wsum * inv_cnt - zp[:, cols]
        po = jnp.dot(diff.astype(BF16), poolw_ref[g], preferred_element_type=F32)
        pool_out.append(po * pscale_ref[:, cols])

    convw = convw_ref[...]
    ext_rows = tm + V7X_SUBLANES
    conv = None
    for r in range(V7X_SUBLANES):
        vr = None
        for a in range(CONV_HIST // V7X_SUBLANES):
            lag = V7X_SUBLANES * a + r
            if lag >= CONF_KERNEL:
                continue
            k = CONF_KERNEL - 1 - lag
            start = CONV_HIST - V7X_SUBLANES - V7X_SUBLANES * a
            term = convw[k:k + 1, :] * conv_ext[start:start + ext_rows, :]
            vr = term if vr is None else vr + term
        if r:
            vr = pltpu.roll(vr, r, axis=0)
        conv = vr if conv is None else conv + vr
    conv = conv[V7X_SUBLANES:V7X_SUBLANES + tm] + convb_ref[...]
    conf = _silu(_layer_norm(conv, lng_ref[...], lnb_ref[...]))

    pool_ext[0:POOL_HIST, :] = zp[tm - POOL_HIST:tm]
    conv_ext[0:CONV_HIST, :] = glu[tm - CONV_HIST:tm]

    m = jnp.dot(conf.astype(BF16), wout_ref[D_HALF:, :], preferred_element_type=F32)
    for g in range(len(POOL_WINDOWS)):
        rows = slice(g * POOL_GROUP, (g + 1) * POOL_GROUP)
        m = m + jnp.dot(pool_out[g].astype(BF16), wout_ref[rows, :], preferred_element_type=F32)
    _finish_mixer(x, m, mod, n2g_ref, rwt_ref, rbias_ref,
                  x1_ref, h2_ref, eidx_ref, wsel_ref, rank_ref, counts_ref, cnt_ref)


def _mixer_cd_kernel(tiles_per_seq,
                     x_ref, y0_ref, y1_ref, wt_ref, modp_ref,
                     mod_ref, n1g_ref, n2g_ref, win_ref, sconvw_ref, lng_ref, lnb_ref,
                     ws_ref, bsf_ref, wout_ref, rwt_ref, rbias_ref,
                     x1_ref, h2_ref, eidx_ref, wsel_ref, rank_ref, counts_ref,
                     sconv_ext, cnt_ref):
    i = pl.program_id(0)
    tm = x_ref.shape[0]
    seq_tile = i % tiles_per_seq

    @pl.when(seq_tile == 0)
    def _():
        sconv_ext[0:SCONV_HIST, :] = jnp.zeros((SCONV_HIST, D_HALF), F32)

    x = _moe_residual(x_ref, y0_ref, y1_ref, wt_ref, modp_ref[0, 0][5:6])
    mod = mod_ref[0, 0]
    h = _rms_mod(x, n1g_ref[...], mod[0:1], mod[1:2]).astype(BF16)
    z = jnp.dot(h, win_ref[...], preferred_element_type=F32)
    bg = z[:, :D_HALF]
    ch = z[:, D_HALF:2 * D_HALF] * z[:, 2 * D_HALF:3 * D_HALF]
    zd = _gelu_tanh(z[:, 3 * D_HALF:])
    u = zd[:, :D_HALF]
    v = _layer_norm(zd[:, D_HALF:], lng_ref[...], lnb_ref[...])

    sconv_ext[SCONV_HIST:SCONV_HIST + tm, :] = ch
    sw = sconvw_ref[...]
    ext = sconv_ext[...]
    conv = sw[2:3, :] * ext
    conv = conv + sw[1:2, :] * pltpu.roll(ext, 1, axis=0)
    conv = conv + sw[0:1, :] * pltpu.roll(ext, 2, axis=0)
    sc_out = bg * conv[SCONV_HIST:SCONV_HIST + tm]
    sconv_ext[0:SCONV_HIST, :] = ch[tm - SCONV_HIST:tm]

    r_i = lax.broadcasted_iota(jnp.int32, (CHUNK, CHUNK), 0)
    c_i = lax.broadcasted_iota(jnp.int32, (CHUNK, CHUNK), 1)
    tril = c_i <= r_i
    wm = [jnp.where(tril, ws_ref[hd], 0.0).astype(BF16) for hd in range(GMLP_HEADS)]
    v_bf = v.astype(BF16)
    bsf = bsf_ref[...]
    gm_rows = []
    for n in range(tm // CHUNK):
        rows = slice(n * CHUNK, (n + 1) * CHUNK)
        heads = []
        for hd in range(GMLP_HEADS):
            cols = slice(hd * POOL_GROUP, (hd + 1) * POOL_GROUP)
            heads.append(jnp.dot(wm[hd], v_bf[rows, cols], preferred_element_type=F32))
        mixed = jnp.concatenate(heads, axis=1) + bsf
        gm_rows.append(u[rows] * mixed)
    gm_out = jnp.concatenate(gm_rows, axis=0)

    m = jnp.dot(sc_out.astype(BF16), wout_ref[:D_HALF, :], preferred_element_type=F32)
    m = m + jnp.dot(gm_out.astype(BF16), wout_ref[D_HALF:, :], preferred_element_type=F32)
    _finish_mixer(x, m, mod, n2g_ref, rwt_ref, rbias_ref,
                  x1_ref, h2_ref, eidx_ref, wsel_ref, rank_ref, counts_ref, cnt_ref)


def _const_spec(shape):
    nd = len(shape)
    return pl.BlockSpec(shape, lambda i: (0,) * nd)


def _mixer_call(kernel_fn, layer, stream_inputs, stream_specs, mod4, n1g, n2g, weights, rwt, rbias,
                scratch, seq_len, name):
    n_tok = mod4.shape[1] * seq_len
    d = D_MODEL
    tm = MIX_TILE
    tiles_per_seq = seq_len // tm
    in_specs = stream_specs + [
        pl.BlockSpec((1, 1, 6, d), lambda i: (layer, i // tiles_per_seq, 0, 0)),
        _const_spec(n1g.shape),
        _const_spec(n2g.shape),
    ] + [_const_spec(w.shape) for w in weights] + [_const_spec(rwt.shape), _const_spec(rbias.shape)]
    out_specs = [
        pl.BlockSpec((tm, d), lambda i: (i, 0)),
        pl.BlockSpec((tm * ROW_CHUNKS, V7X_LANES), lambda i: (i, 0)),
        pl.BlockSpec((TOP_K, tm), lambda i: (0, i)),
        pl.BlockSpec((TOP_K, tm), lambda i: (0, i)),
        pl.BlockSpec((TOP_K, tm), lambda i: (0, i)),
        pl.BlockSpec((N_EXPERTS, V7X_LANES), lambda i: (0, 0)),
    ]
    out_shape = [
        jax.ShapeDtypeStruct((n_tok, d), F32),
        jax.ShapeDtypeStruct((n_tok * ROW_CHUNKS, V7X_LANES), F32),
        jax.ShapeDtypeStruct((TOP_K, n_tok), jnp.int32),
        jax.ShapeDtypeStruct((TOP_K, n_tok), F32),
        jax.ShapeDtypeStruct((TOP_K, n_tok), jnp.int32),
        jax.ShapeDtypeStruct((N_EXPERTS, V7X_LANES), jnp.int32),
    ]
    return pl.pallas_call(
        functools.partial(kernel_fn, tiles_per_seq),
        grid=(n_tok // tm,),
        in_specs=in_specs,
        out_specs=out_specs,
        out_shape=out_shape,
        scratch_shapes=scratch + [pltpu.VMEM((N_EXPERTS, 1), F32)],
        compiler_params=pltpu.CompilerParams(
            dimension_semantics=("arbitrary",),
            vmem_limit_bytes=V7X_VMEM_LIMIT_BYTES),
        name=name,
    )(*stream_inputs, mod4, n1g, n2g, *weights, rwt, rbias)


def _combine_specs(n_tok):
    tm = MIX_TILE
    n_tiles = n_tok // tm
    return [
        pl.BlockSpec((tm, D_MODEL), lambda i: (i, 0)),
        pl.BlockSpec((tm * ROW_CHUNKS, V7X_LANES), lambda i: (i, 0)),
        pl.BlockSpec((tm * ROW_CHUNKS, V7X_LANES), lambda i: (n_tiles + i, 0)),
        pl.BlockSpec((tm, TOP_K), lambda i: (i, 0)),
    ]


def _sorted_positions(eidx, rank, counts):
    n_pairs = eidx.shape[0] * eidx.shape[1]
    n_rows = n_pairs + N_EXPERTS * MOE_TILE
    n_tiles = n_rows // MOE_TILE
    padded = (counts + MOE_TILE - 1) // MOE_TILE * MOE_TILE
    seg_end = jnp.cumsum(padded)
    seg_start = seg_end - padded
    pos = seg_start[eidx] + rank
    tile_row0 = jnp.arange(n_tiles, dtype=jnp.int32) * MOE_TILE
    tile_expert = jnp.minimum(
        jnp.sum((tile_row0[:, None] >= seg_end[None, :]).astype(jnp.int32), axis=1), N_EXPERTS - 1)
    n_used = (seg_end[N_EXPERTS - 1] // MOE_TILE).reshape(1)
    n_tok = eidx.shape[1]
    pos_tiles = pos.reshape(TOP_K, n_tok // ROW_TILE, ROW_TILE).transpose(1, 0, 2)
    return (pos_tiles, seg_end.astype(jnp.int32), tile_expert.astype(jnp.int32),
            n_used.astype(jnp.int32), n_rows)


def _row_copies(n_tok_tile, make_copy):
    def issue(g, carry):
        for u in range(ROW_UNROLL):
            t = g * ROW_UNROLL + u
            for k in range(TOP_K):
                make_copy(t, k).start()
        return carry
    lax.fori_loop(0, n_tok_tile // ROW_UNROLL, issue, 0)


def _scatter_kernel(segend_ref, pos_hbm, h_ref, o_hbm, pos_smem, zero_ref, pos_sem, row_sem):
    i = pl.program_id(0)
    ts = h_ref.shape[0]

    @pl.when(i == 0)
    def _():
        zero_ref[...] = jnp.zeros_like(zero_ref)

        def zero_tile(row0):
            cp = pltpu.make_async_copy(zero_ref, o_hbm.at[pl.ds(row0, MOE_TILE)], row_sem)
            cp.start()
            cp.wait()

        for e in range(N_EXPERTS):
            seg_lo = segend_ref[e - 1] if e else 0
            pl.when(segend_ref[e] > seg_lo)(
                functools.partial(zero_tile, segend_ref[e] - MOE_TILE))
        n_rows = o_hbm.shape[0]
        for j in range(N_EXPERTS):
            row0 = n_rows - (j + 1) * MOE_TILE
            pl.when(row0 >= segend_ref[N_EXPERTS - 1])(functools.partial(zero_tile, row0))

    pos_cp = pltpu.make_async_copy(pos_hbm.at[i], pos_smem, pos_sem)
    pos_cp.start()
    pos_cp.wait()
    _row_copies(ts, lambda t, k: pltpu.make_async_copy(
        h_ref.at[t], o_hbm.at[pos_smem[k, t]], row_sem))
    for _ in range(TOP_K):
        pltpu.make_async_copy(h_ref, o_hbm.at[pl.ds(0, ts)], row_sem).wait()


def _scatter_rows(h_rows, pos_tiles, seg_end, n_rows):
    n_tok = h_rows.shape[0]
    ts = ROW_TILE
    grid_spec = pltpu.PrefetchScalarGridSpec(
        num_scalar_prefetch=1,
        grid=(n_tok // ts,),
        in_specs=[
            pl.BlockSpec(memory_space=pl.ANY),
            pl.BlockSpec((ts, ROW_CHUNKS, V7X_LANES), lambda i, se: (i, 0, 0)),
        ],
        out_specs=pl.BlockSpec(memory_space=pl.ANY),
        scratch_shapes=[
            pltpu.SMEM((TOP_K, ts), jnp.int32),
            pltpu.VMEM((MOE_TILE, ROW_CHUNKS, V7X_LANES), F32),
            pltpu.SemaphoreType.DMA,
            pltpu.SemaphoreType.DMA,
        ],
    )
    return pl.pallas_call(
        _scatter_kernel,
        grid_spec=grid_spec,
        out_shape=jax.ShapeDtypeStruct((n_rows, ROW_CHUNKS, V7X_LANES), F32),
        compiler_params=pltpu.CompilerParams(
            dimension_semantics=("arbitrary",),
            vmem_limit_bytes=V7X_VMEM_LIMIT_BYTES),
        name="scatter_rows",
    )(seg_end, pos_tiles, h_rows)


def _gather_kernel(pos_hbm, y_hbm, o_ref, pos_smem, pos_sem, row_sem):
    i = pl.program_id(0)
    ts = o_ref.shape[1]
    pos_cp = pltpu.make_async_copy(pos_hbm.at[i], pos_smem, pos_sem)
    pos_cp.start()
    pos_cp.wait()
    _row_copies(ts, lambda t, k: pltpu.make_async_copy(
        y_hbm.at[pos_smem[k, t]], o_ref.at[k, t], row_sem))
    for k in range(TOP_K):
        pltpu.make_async_copy(y_hbm.at[pl.ds(0, ts)], o_ref.at[k], row_sem).wait()


def _gather_rows(rows, pos_tiles):
    n_tiles, _, ts = pos_tiles.shape
    n_tok = n_tiles * ts
    return pl.pallas_call(
        _gather_kernel,
        grid=(n_tiles,),
        in_specs=[pl.BlockSpec(memory_space=pl.ANY), pl.BlockSpec(memory_space=pl.ANY)],
        out_specs=pl.BlockSpec((TOP_K, ts, ROW_CHUNKS, V7X_LANES), lambda i: (0, i, 0, 0)),
        out_shape=jax.ShapeDtypeStruct((TOP_K, n_tok, ROW_CHUNKS, V7X_LANES), F32),
        scratch_shapes=[
            pltpu.SMEM((TOP_K, ts), jnp.int32),
            pltpu.SemaphoreType.DMA,
            pltpu.SemaphoreType.DMA,
        ],
        compiler_params=pltpu.CompilerParams(
            dimension_semantics=("arbitrary",),
            vmem_limit_bytes=V7X_VMEM_LIMIT_BYTES),
        name="gather_rows",
    )(pos_tiles, rows)


def _experts_kernel(te_ref, nu_ref, x_ref, wg_ref, wu_ref, wd_ref, y_ref):
    j = pl.program_id(0)
    tm = x_ref.shape[0] // ROW_CHUNKS

    @pl.when(j < nu_ref[0])
    def _():
        h = _load_rows(x_ref, tm).astype(BF16)
        a = jnp.dot(h, wg_ref[0, 0].astype(BF16), preferred_element_type=F32)
        b = jnp.dot(h, wu_ref[0, 0].astype(BF16), preferred_element_type=F32)
        t = (_silu(a) * b).astype(BF16)
        _store_rows(y_ref, jnp.dot(t, wd_ref[0, 0].astype(BF16), preferred_element_type=F32))

    @pl.when(j >= nu_ref[0])
    def _():
        y_ref[...] = jnp.zeros_like(y_ref)


def _experts_call(layer, rows, tile_expert, n_used, w_gate, w_up, w_down):
    n_rows = rows.shape[0]
    tm = MOE_TILE
    d = D_MODEL
    rows2 = rows.reshape(n_rows * ROW_CHUNKS, V7X_LANES)
    grid_spec = pltpu.PrefetchScalarGridSpec(
        num_scalar_prefetch=2,
        grid=(n_rows // tm,),
        in_specs=[
            pl.BlockSpec((tm * ROW_CHUNKS, V7X_LANES), lambda j, te, nu: (j, 0)),
            pl.BlockSpec((1, 1, d, D_EXPERT), lambda j, te, nu: (layer, te[j], 0, 0)),
            pl.BlockSpec((1, 1, d, D_EXPERT), lambda j, te, nu: (layer, te[j], 0, 0)),
            pl.BlockSpec((1, 1, D_EXPERT, d), lambda j, te, nu: (layer, te[j], 0, 0)),
        ],
        out_specs=pl.BlockSpec((tm * ROW_CHUNKS, V7X_LANES), lambda j, te, nu: (j, 0)),
    )
    y = pl.pallas_call(
        _experts_kernel,
        grid_spec=grid_spec,
        out_shape=jax.ShapeDtypeStruct((n_rows * ROW_CHUNKS, V7X_LANES), F32),
        compiler_params=pltpu.CompilerParams(
            dimension_semantics=("arbitrary",),
            vmem_limit_bytes=V7X_VMEM_LIMIT_BYTES),
        name="experts_l%d" % layer,
    )(tile_expert, n_used, rows2, w_gate, w_up, w_down)
    return y.reshape(n_rows, ROW_CHUNKS, V7X_LANES)


def _moe_rows(layer, h_rows, eidx, rank, counts, w_gate, w_up, w_down):
    n_tok = eidx.shape[1]
    pos_tiles, seg_end, tile_expert, n_used, n_rows = _sorted_positions(eidx, rank, counts[:, 0])
    sorted_rows = _scatter_rows(
        h_rows.reshape(n_tok, ROW_CHUNKS, V7X_LANES), pos_tiles, seg_end, n_rows)
    y_sorted = _experts_call(layer, sorted_rows, tile_expert, n_used, w_gate, w_up, w_down)
    y_pairs = _gather_rows(y_sorted, pos_tiles)
    return y_pairs.reshape(TOP_K * n_tok * ROW_CHUNKS, V7X_LANES)


def _final_kernel(x_ref, y0_ref, y1_ref, wt_ref, modp_ref, fg_ref, o_ref):
    y = _moe_residual(x_ref, y0_ref, y1_ref, wt_ref, modp_ref[0, 0][5:6])
    ms = jnp.mean(y * y, axis=-1, keepdims=True)
    o_ref[...] = y * lax.rsqrt(ms + EPS) * fg_ref[...]


def _final_call(layer, x, y_pairs, wt, mod4, fg, seq_len):
    n_tok, d = x.shape
    tm = MIX_TILE
    tiles_per_seq = seq_len // tm
    return pl.pallas_call(
        _final_kernel,
        grid=(n_tok // tm,),
        in_specs=_combine_specs(n_tok) + [
            pl.BlockSpec((1, 1, 6, d), lambda i: (layer, i // tiles_per_seq, 0, 0)),
            _const_spec(fg.shape),
        ],
        out_specs=pl.BlockSpec((tm, d), lambda i: (i, 0)),
        out_shape=jax.ShapeDtypeStruct((n_tok, d), F32),
        compiler_params=pltpu.CompilerParams(
            dimension_semantics=("arbitrary",),
            vmem_limit_bytes=V7X_VMEM_LIMIT_BYTES),
        name="final_norm",
    )(x, y_pairs, y_pairs, wt, mod4, fg)


def kernel(x, c, norm1_g, norm2_g, ada_w, ada_b, ab_w_in, pool_w, pool_scale, conf_conv_w, conf_conv_b, conf_ln_g, conf_ln_b, ab_w_out, cd_w_in, sconv_w, gmlp_ln_g, gmlp_ln_b, gmlp_ws, gmlp_bs, cd_w_out, router_w, router_bias, exp_w_gate, exp_w_up, exp_w_down, final_g):
    bsz, seq_len, d = x.shape
    n_tok = bsz * seq_len
    tm = MIX_TILE
    tiles_per_seq = seq_len // tm
    xf = x.reshape(n_tok, d)

    mod = _ada_mod(c, ada_w, ada_b)
    mod4 = mod.reshape(mod.shape[0], bsz, 6, d)

    rw_hi = router_w.astype(BF16)
    rw_lo = (router_w - rw_hi.astype(F32)).astype(BF16)
    rwt = jnp.concatenate([rw_hi.T, rw_lo.T], axis=0)
    rbias = router_bias.reshape(N_EXPERTS, 1)
    fg = final_g.reshape(1, d)

    weights_ab = [
        ab_w_in[0].astype(BF16), pool_w[0].astype(BF16), pool_scale[0].reshape(1, D_HALF),
        conf_conv_w[0], conf_conv_b[0].reshape(1, D_HALF), conf_ln_g[0].reshape(1, D_HALF),
        conf_ln_b[0].reshape(1, D_HALF), ab_w_out[0].astype(BF16),
    ]
    scratch_ab = [pltpu.VMEM((POOL_HIST + tm, D_HALF), F32),
                  pltpu.VMEM((CONV_HIST + tm, D_HALF), F32)]
    x1, h2, eidx0, wsel0, rank0, counts0 = _mixer_call(
        _mixer_ab_kernel, 0, [xf], [pl.BlockSpec((tm, d), lambda i: (i, 0))],
        mod4, norm1_g[0:1], norm2_g[0:1], weights_ab, rwt, rbias, scratch_ab, seq_len, "mixer_ab")
    y_pairs0 = _moe_rows(0, h2, eidx0, rank0, counts0, exp_w_gate, exp_w_up, exp_w_down)

    bsf = jnp.repeat(gmlp_bs[0].T, POOL_GROUP, axis=1)
    weights_cd = [
        cd_w_in[0].astype(BF16), sconv_w[0], gmlp_ln_g[0].reshape(1, D_HALF),
        gmlp_ln_b[0].reshape(1, D_HALF), gmlp_ws[0], bsf, cd_w_out[0].astype(BF16),
    ]
    scratch_cd = [pltpu.VMEM((SCONV_HIST + tm, D_HALF), F32)]
    prev_mod_spec = pl.BlockSpec((1, 1, 6, d), lambda i: (0, i // tiles_per_seq, 0, 0))
    x3, h4, eidx1, wsel1, rank1, counts1 = _mixer_call(
        _mixer_cd_kernel, 1, [x1, y_pairs0, y_pairs0, wsel0.T, mod4],
        _combine_specs(n_tok) + [prev_mod_spec],
        mod4, norm1_g[1:2], norm2_g[1:2], weights_cd, rwt, rbias, scratch_cd, seq_len, "mixer_cd")
    y_pairs1 = _moe_rows(1, h4, eidx1, rank1, counts1, exp_w_gate, exp_w_up, exp_w_down)

    out = _final_call(1, x3, y_pairs1, wsel1.T, mod4, fg, seq_len)
    return out.reshape(bsz, seq_len, d)
```

```python
import functools

import jax
import jax.numpy as jnp
from jax import lax
from jax.experimental import pallas as pl
from jax.experimental.pallas import tpu as pltpu

D_MODEL = 1024
EPS = 1e-6
POOL_WINDOWS = (2, 4, 8, 16)
POOL_GROUP = 128
D_HALF = 512
CONF_KERNEL = 31
SCONV_KERNEL = 3
CHUNK = 128
GMLP_HEADS = 4
N_EXPERTS = 16
N_GROUPS = 4
EXPERTS_PER_GROUP = 4
TOP_K = 2
D_EXPERT = 512

V7X_LANES = 128
V7X_SUBLANES = 8
V7X_VMEM_LIMIT_BYTES = 56 * 1024 * 1024

MIX_TILE = 512
MOE_TILE = 512
ROW_TILE = 512
ROW_UNROLL = 8
ROW_CHUNKS = D_MODEL // V7X_LANES
CONV_HIST = 32
POOL_HIST = 16
SCONV_HIST = 8

BF16 = jnp.bfloat16
F32 = jnp.float32


def _rms_mod(x, g_row, shift_row, scale_row):
    ms = jnp.mean(x * x, axis=-1, keepdims=True)
    y = x * lax.rsqrt(ms + EPS)
    return (y * g_row) * (1.0 + scale_row) + shift_row


def _layer_norm(x, g_row, b_row):
    mu = jnp.mean(x, axis=-1, keepdims=True)
    xc = x - mu
    var = jnp.mean(xc * xc, axis=-1, keepdims=True)
    return xc * lax.rsqrt(var + EPS) * g_row + b_row


def _sigmoid(x):
    return 1.0 / (1.0 + jnp.exp(-x))


def _silu(x):
    return x * _sigmoid(x)


def _gelu_tanh(x):
    c = 0.7978845608028654
    return 0.5 * x * (1.0 + jnp.tanh(c * (x + 0.044715 * (x * x * x))))


def _load_rows(ref, n_rows):
    return jnp.concatenate(
        [ref[pl.ds(c, n_rows, stride=ROW_CHUNKS), :] for c in range(ROW_CHUNKS)], axis=1)


def _store_rows(ref, val):
    n_rows = val.shape[0]
    for c in range(ROW_CHUNKS):
        ref[pl.ds(c, n_rows, stride=ROW_CHUNKS), :] = val[:, c * V7X_LANES:(c + 1) * V7X_LANES]


def _ada_kernel(ct_ref, w_ref, b_ref, o_ref):
    ct = ct_ref[...]
    cond = _silu(ct)
    w = w_ref[0]
    nb = ct.shape[1]
    for b in range(nb):
        col = cond[:, b:b + 1]
        o_ref[0, b:b + 1, :] = jnp.sum(col * w, axis=0, keepdims=True) + b_ref[0]


def _ada_mod(c, ada_w, ada_b):
    depth, d, six_d = ada_w.shape
    bsz = c.shape[0]
    nb = D_MODEL
    return pl.pallas_call(
        _ada_kernel,
        grid=(depth, six_d // nb),
        in_specs=[
            pl.BlockSpec((d, bsz), lambda l, j: (0, 0)),
            pl.BlockSpec((1, d, nb), lambda l, j: (l, 0, j)),
            pl.BlockSpec((1, 1, nb), lambda l, j: (l, 0, j)),
        ],
        out_specs=pl.BlockSpec((1, bsz, nb), lambda l, j: (l, 0, j)),
        out_shape=jax.ShapeDtypeStruct((depth, bsz, six_d), F32),
        compiler_params=pltpu.CompilerParams(
            dimension_semantics=("arbitrary", "arbitrary"),
            vmem_limit_bytes=V7X_VMEM_LIMIT_BYTES),
        name="ada_mod",
    )(c.T, ada_w, ada_b.reshape(depth, 1, six_d))


def _route(h2_bf, rwt_ref, rbias_ref, eidx_ref, wsel_ref, rank_ref, counts_ref, cnt_ref):
    nt = (((1,), (1,)), ((), ()))
    r = lax.dot_general(rwt_ref[...], h2_bf, nt, preferred_element_type=F32)
    logits = r[:N_EXPERTS] + r[N_EXPERTS:]
    m = jnp.max(logits, axis=0, keepdims=True)
    ex = jnp.exp(logits - m)
    probs = ex / jnp.sum(ex, axis=0, keepdims=True)
    sel = probs + rbias_ref[...]
    s = [sel[e:e + 1] for e in range(N_EXPERTS)]
    p = [probs[e:e + 1] for e in range(N_EXPERTS)]
    best = None
    gi = None
    for g in range(N_GROUPS):
        a, b, c, d = s[4 * g:4 * g + 4]
        hi1, lo1 = jnp.maximum(a, b), jnp.minimum(a, b)
        hi2, lo2 = jnp.maximum(c, d), jnp.minimum(c, d)
        top1 = jnp.maximum(hi1, hi2)
        top2 = jnp.maximum(jnp.minimum(hi1, hi2), jnp.maximum(lo1, lo2))
        score = top1 + top2
        if g == 0:
            best, gi = score, jnp.zeros(score.shape, jnp.int32)
        else:
            upd = score > best
            gi = jnp.where(upd, g, gi)
            best = jnp.where(upd, score, best)
    v, q = [], []
    for j in range(EXPERTS_PER_GROUP):
        vj, qj = s[j], p[j]
        for g in range(1, N_GROUPS):
            pick = gi == g
            vj = jnp.where(pick, s[4 * g + j], vj)
            qj = jnp.where(pick, p[4 * g + j], qj)
        v.append(vj)
        q.append(qj)
    i1 = jnp.zeros(gi.shape, jnp.int32)
    m1 = v[0]
    for j in range(1, EXPERTS_PER_GROUP):
        upd = v[j] > m1
        i1 = jnp.where(upd, j, i1)
        m1 = jnp.where(upd, v[j], m1)
    i2 = jnp.zeros(gi.shape, jnp.int32)
    m2 = jnp.full(m1.shape, -jnp.inf, F32)
    for j in range(EXPERTS_PER_GROUP):
        cand = (i1 != j) & (v[j] > m2)
        i2 = jnp.where(cand, j, i2)
        m2 = jnp.where(cand, v[j], m2)
    pa = q[0]
    pb = q[0]
    for j in range(1, EXPERTS_PER_GROUP):
        pa = jnp.where(i1 == j, q[j], pa)
        pb = jnp.where(i2 == j, q[j], pb)
    tot = pa + pb
    e0 = gi * EXPERTS_PER_GROUP + i1
    e1 = gi * EXPERTS_PER_GROUP + i2
    eidx_ref[0:1, :] = e0
    eidx_ref[1:2, :] = e1
    wsel_ref[0:1, :] = pa / tot
    wsel_ref[1:2, :] = pb / tot

    t = h2_bf.shape[0]
    e_iota = lax.broadcasted_iota(jnp.int32, (N_EXPERTS, t), 0)
    oh0 = e_iota == e0
    oh1 = e_iota == e1
    both = jnp.where(oh0 | oh1, 1.0, 0.0)
    r_i = lax.broadcasted_iota(jnp.int32, (V7X_LANES, V7X_LANES), 0)
    c_i = lax.broadcasted_iota(jnp.int32, (V7X_LANES, V7X_LANES), 1)
    before = jnp.where(r_i < c_i, 1.0, 0.0).astype(BF16)
    run = cnt_ref[...]
    rank0, rank1 = [], []
    for blk in range(t // V7X_LANES):
        lanes = slice(blk * V7X_LANES, (blk + 1) * V7X_LANES)
        b = both[:, lanes]
        pre = jnp.dot(b.astype(BF16), before, preferred_element_type=F32) + run
        rank0.append(jnp.sum(jnp.where(oh0[:, lanes], pre, 0.0), axis=0, keepdims=True))
        rank1.append(jnp.sum(jnp.where(oh1[:, lanes], pre, 0.0), axis=0, keepdims=True))
        run = run + jnp.sum(b, axis=1, keepdims=True)
    cnt_ref[...] = run
    rank_ref[0:1, :] = jnp.concatenate(rank0, axis=1).astype(jnp.int32)
    rank_ref[1:2, :] = jnp.concatenate(rank1, axis=1).astype(jnp.int32)
    counts_ref[...] = jnp.broadcast_to(run, counts_ref.shape).astype(jnp.int32)


def _finish_mixer(x, m, mod, n2g_ref, rwt_ref, rbias_ref,
                  x1_ref, h2_ref, eidx_ref, wsel_ref, rank_ref, counts_ref, cnt_ref):
    @pl.when(pl.program_id(0) == 0)
    def _():
        cnt_ref[...] = jnp.zeros_like(cnt_ref)

    x1 = x + mod[2:3] * m
    x1_ref[...] = x1
    h2 = _rms_mod(x1, n2g_ref[...], mod[3:4], mod[4:5])
    h2_bf = h2.astype(BF16)
    _store_rows(h2_ref, h2_bf.astype(F32))
    _route(h2_bf, rwt_ref, rbias_ref, eidx_ref, wsel_ref, rank_ref, counts_ref, cnt_ref)


def _moe_residual(x_ref, y0_ref, y1_ref, wt_ref, g2_row):
    tm = x_ref.shape[0]
    wt = wt_ref[...]
    y = wt[:, 0:1] * _load_rows(y0_ref, tm) + wt[:, 1:2] * _load_rows(y1_ref, tm)
    return x_ref[...] + g2_row * y


def _mixer_ab_kernel(tiles_per_seq,
                     x_ref, mod_ref, n1g_ref, n2g_ref, win_ref, poolw_ref, pscale_ref,
                     convw_ref, convb_ref, lng_ref, lnb_ref, wout_ref, rwt_ref, rbias_ref,
                     x1_ref, h2_ref, eidx_ref, wsel_ref, rank_ref, counts_ref,
                     pool_ext, conv_ext, cnt_ref):
    i = pl.program_id(0)
    tm = x_ref.shape[0]
    seq_tile = i % tiles_per_seq

    @pl.when(seq_tile == 0)
    def _():
        pool_ext[0:POOL_HIST, :] = jnp.zeros((POOL_HIST, D_HALF), F32)
        conv_ext[0:CONV_HIST, :] = jnp.zeros((CONV_HIST, D_HALF), F32)

    x = x_ref[...]
    mod = mod_ref[0, 0]
    h = _rms_mod(x, n1g_ref[...], mod[0:1], mod[1:2]).astype(BF16)
    z = jnp.dot(h, win_ref[...], preferred_element_type=F32)
    zp = z[:, :D_HALF]
    glu = z[:, D_HALF:2 * D_HALF] * _sigmoid(z[:, 2 * D_HALF:])
    pool_ext[POOL_HIST:POOL_HIST + tm, :] = zp
    conv_ext[CONV_HIST:CONV_HIST + tm, :] = glu

    row = lax.broadcasted_iota(jnp.int32, (tm, 1), 0)
    pos1 = (seq_tile * tm + row + 1).astype(F32)
    pool_out = []
    for g, w in enumerate(POOL_WINDOWS):
        cols = slice(g * POOL_GROUP, (g + 1) * POOL_GROUP)
        u = pool_ext[:, cols]
        acc = u
        span = 1
        while span < w:
            acc = acc + pltpu.roll(acc, span, axis=0)
            span *= 2
        wsum = acc[POOL_HIST:POOL_HIST + tm]
        inv_cnt = 1.0 / jnp.minimum(pos1, float(w))
        diff = wsum * inv_cnt - zp[:, cols]
        po = jnp.dot(diff.astype(BF16), poolw_ref[g], preferred_element_type=F32)
        pool_out.append(po * pscale_ref[:, cols])

    convw = convw_ref[...]
    ext_rows = tm + V7X_SUBLANES
    conv = None
    for r in range(V7X_SUBLANES):
        vr = None
        for a in range(CONV_HIST // V7X_SUBLANES):
            lag = V7X_SUBLANES * a + r
            if lag >= CONF_KERNEL:
                continue
            k = CONF_KERNEL - 1 - lag
            start = CONV_HIST - V7X_SUBLANES - V7X_SUBLANES * a
            term = convw[k:k + 1, :] * conv_ext[start:start + ext_rows, :]
            vr = term if vr is None else vr + term
        if r:
            vr = pltpu.roll(vr, r, axis=0)
        conv = vr if conv is None else conv + vr
    conv = conv[V7X_SUBLANES:V7X_SUBLANES + tm] + convb_ref[...]
    conf = _silu(_layer_norm(conv, lng_ref[...], lnb_ref[...]))

    pool_ext[0:POOL_HIST, :] = zp[tm - POOL_HIST:tm]
    conv_ext[0:CONV_HIST, :] = glu[tm - CONV_HIST:tm]

    m = jnp.dot(conf.astype(BF16), wout_ref[D_HALF:, :], preferred_element_type=F32)
    for g in range(len(POOL_WINDOWS)):
        rows = slice(g * POOL_GROUP, (g + 1) * POOL_GROUP)
        m = m + jnp.dot(pool_out[g].astype(BF16), wout_ref[rows, :], preferred_element_type=F32)
    _finish_mixer(x, m, mod, n2g_ref, rwt_ref, rbias_ref,
                  x1_ref, h2_ref, eidx_ref, wsel_ref, rank_ref, counts_ref, cnt_ref)


def _mixer_cd_kernel(tiles_per_seq,
                     x_ref, y0_ref, y1_ref, wt_ref, modp_ref,
                     mod_ref, n1g_ref, n2g_ref, win_ref, sconvw_ref, lng_ref, lnb_ref,
                     ws_ref, bsf_ref, wout_ref, rwt_ref, rbias_ref,
                     x1_ref, h2_ref, eidx_ref, wsel_ref, rank_ref, counts_ref,
                     sconv_ext, cnt_ref):
    i = pl.program_id(0)
    tm = x_ref.shape[0]
    seq_tile = i % tiles_per_seq

    @pl.when(seq_tile == 0)
    def _():
        sconv_ext[0:SCONV_HIST, :] = jnp.zeros((SCONV_HIST, D_HALF), F32)

    x = _moe_residual(x_ref, y0_ref, y1_ref, wt_ref, modp_ref[0, 0][5:6])
    mod = mod_ref[0, 0]
    h = _rms_mod(x, n1g_ref[...], mod[0:1], mod[1:2]).astype(BF16)
    z = jnp.dot(h, win_ref[...], preferred_element_type=F32)
    bg = z[:, :D_HALF]
    ch = z[:, D_HALF:2 * D_HALF] * z[:, 2 * D_HALF:3 * D_HALF]
    zd = _gelu_tanh(z[:, 3 * D_HALF:])
    u = zd[:, :D_HALF]
    v = _layer_norm(zd[:, D_HALF:], lng_ref[...], lnb_ref[...])

    sconv_ext[SCONV_HIST:SCONV_HIST + tm, :] = ch
    sw = sconvw_ref[...]
    ext = sconv_ext[...]
    conv = sw[2:3, :] * ext
    conv = conv + sw[1:2, :] * pltpu.roll(ext, 1, axis=0)
    conv = conv + sw[0:1, :] * pltpu.roll(ext, 2, axis=0)
    sc_out = bg * conv[SCONV_HIST:SCONV_HIST + tm]
    sconv_ext[0:SCONV_HIST, :] = ch[tm - SCONV_HIST:tm]

    r_i = lax.broadcasted_iota(jnp.int32, (CHUNK, CHUNK), 0)
    c_i = lax.broadcasted_iota(jnp.int32, (CHUNK, CHUNK), 1)
    tril = c_i <= r_i
    wm = [jnp.where(tril, ws_ref[hd], 0.0).astype(BF16) for hd in range(GMLP_HEADS)]
    v_bf = v.astype(BF16)
    bsf = bsf_ref[...]
    gm_rows = []
    for n in range(tm // CHUNK):
        rows = slice(n * CHUNK, (n + 1) * CHUNK)
        heads = []
        for hd in range(GMLP_HEADS):
            cols = slice(hd * POOL_GROUP, (hd + 1) * POOL_GROUP)
            heads.append(jnp.dot(wm[hd], v_bf[rows, cols], preferred_element_type=F32))
        mixed = jnp.concatenate(heads, axis=1) + bsf
        gm_rows.append(u[rows] * mixed)
    gm_out = jnp.concatenate(gm_rows, axis=0)

    m = jnp.dot(sc_out.astype(BF16), wout_ref[:D_HALF, :], preferred_element_type=F32)
    m = m + jnp.dot(gm_out.astype(BF16), wout_ref[D_HALF:, :], preferred_element_type=F32)
    _finish_mixer(x, m, mod, n2g_ref, rwt_ref, rbias_ref,
                  x1_ref, h2_ref, eidx_ref, wsel_ref, rank_ref, counts_ref, cnt_ref)


def _const_spec(shape):
    nd = len(shape)
    return pl.BlockSpec(shape, lambda i: (0,) * nd)


def _mixer_call(kernel_fn, layer, stream_inputs, stream_specs, mod4, n1g, n2g, weights, rwt, rbias,
                scratch, seq_len, name):
    n_tok = mod4.shape[1] * seq_len
    d = D_MODEL
    tm = MIX_TILE
    tiles_per_seq = seq_len // tm
    in_specs = stream_specs + [
        pl.BlockSpec((1, 1, 6, d), lambda i: (layer, i // tiles_per_seq, 0, 0)),
        _const_spec(n1g.shape),
        _const_spec(n2g.shape),
    ] + [_const_spec(w.shape) for w in weights] + [_const_spec(rwt.shape), _const_spec(rbias.shape)]
    out_specs = [
        pl.BlockSpec((tm, d), lambda i: (i, 0)),
        pl.BlockSpec((tm * ROW_CHUNKS, V7X_LANES), lambda i: (i, 0)),
        pl.BlockSpec((TOP_K, tm), lambda i: (0, i)),
        pl.BlockSpec((TOP_K, tm), lambda i: (0, i)),
        pl.BlockSpec((TOP_K, tm), lambda i: (0, i)),
        pl.BlockSpec((N_EXPERTS, V7X_LANES), lambda i: (0, 0)),
    ]
    out_shape = [
        jax.ShapeDtypeStruct((n_tok, d), F32),
        jax.ShapeDtypeStruct((n_tok * ROW_CHUNKS, V7X_LANES), F32),
        jax.ShapeDtypeStruct((TOP_K, n_tok), jnp.int32),
        jax.ShapeDtypeStruct((TOP_K, n_tok), F32),
        jax.ShapeDtypeStruct((TOP_K, n_tok), jnp.int32),
        jax.ShapeDtypeStruct((N_EXPERTS, V7X_LANES), jnp.int32),
    ]
    return pl.pallas_call(
        functools.partial(kernel_fn, tiles_per_seq),
        grid=(n_tok // tm,),
        in_specs=in_specs,
        out_specs=out_specs,
        out_shape=out_shape,
        scratch_shapes=scratch + [pltpu.VMEM((N_EXPERTS, 1), F32)],
        compiler_params=pltpu.CompilerParams(
            dimension_semantics=("arbitrary",),
            vmem_limit_bytes=V7X_VMEM_LIMIT_BYTES),
        name=name,
    )(*stream_inputs, mod4, n1g, n2g, *weights, rwt, rbias)


def _combine_specs(n_tok):
    tm = MIX_TILE
    n_tiles = n_tok // tm
    return [
        pl.BlockSpec((tm, D_MODEL), lambda i: (i, 0)),
        pl.BlockSpec((tm * ROW_CHUNKS, V7X_LANES), lambda i: (i, 0)),
        pl.BlockSpec((tm * ROW_CHUNKS, V7X_LANES), lambda i: (n_tiles + i, 0)),
        pl.BlockSpec((tm, TOP_K), lambda i: (i, 0)),
    ]


def _sorted_positions(eidx, rank, counts):
    n_pairs = eidx.shape[0] * eidx.shape[1]
    n_rows = n_pairs + N_EXPERTS * MOE_TILE
    n_tiles = n_rows // MOE_TILE
    padded = (counts + MOE_TILE - 1) // MOE_TILE * MOE_TILE
    seg_end = jnp.cumsum(padded)
    seg_start = seg_end - padded
    expert_ids = jnp.arange(N_EXPERTS, dtype=jnp.int32)
    pos = jnp.sum(jnp.where(eidx[..., None] == expert_ids, seg_start, 0), axis=-1) + rank
    tile_row0 = jnp.arange(n_tiles, dtype=jnp.int32) * MOE_TILE
    tile_expert = jnp.minimum(
        jnp.sum((tile_row0[:, None] >= seg_end[None, :]).astype(jnp.int32), axis=1), N_EXPERTS - 1)
    n_used = (seg_end[N_EXPERTS - 1] // MOE_TILE).reshape(1)
    n_tok = eidx.shape[1]
    pos_tiles = pos.reshape(TOP_K, n_tok // ROW_TILE, ROW_TILE).transpose(1, 0, 2)
    return (pos_tiles, seg_end.astype(jnp.int32), tile_expert.astype(jnp.int32),
            n_used.astype(jnp.int32), n_rows)


def _row_copies(n_tok_tile, make_copy):
    def issue(g, carry):
        for u in range(ROW_UNROLL):
            t = g * ROW_UNROLL + u
            for k in range(TOP_K):
                make_copy(t, k).start(priority=k)
        return carry
    lax.fori_loop(0, n_tok_tile // ROW_UNROLL, issue, 0)


def _scatter_kernel(segend_ref, pos_hbm, h_ref, o_hbm, pos_smem, zero_ref, pos_sem, row_sem):
    i = pl.program_id(0)
    ts = h_ref.shape[0]

    @pl.when(i == 0)
    def _():
        zero_ref[...] = jnp.zeros_like(zero_ref)

        def zero_tile(row0):
            cp = pltpu.make_async_copy(zero_ref, o_hbm.at[pl.ds(row0, MOE_TILE)], row_sem)
            cp.start()
            cp.wait()

        for e in range(N_EXPERTS):
            seg_lo = segend_ref[e - 1] if e else 0
            pl.when(segend_ref[e] > seg_lo)(
                functools.partial(zero_tile, segend_ref[e] - MOE_TILE))
        n_rows = o_hbm.shape[0]
        for j in range(N_EXPERTS):
            row0 = n_rows - (j + 1) * MOE_TILE
            pl.when(row0 >= segend_ref[N_EXPERTS - 1])(functools.partial(zero_tile, row0))

    pos_cp = pltpu.make_async_copy(pos_hbm.at[i], pos_smem, pos_sem)
    pos_cp.start()
    pos_cp.wait()
    _row_copies(ts, lambda t, k: pltpu.make_async_copy(
        h_ref.at[t], o_hbm.at[pos_smem[k, t]], row_sem))
    for _ in range(TOP_K):
        pltpu.make_async_copy(h_ref, o_hbm.at[pl.ds(0, ts)], row_sem).wait()


def _scatter_rows(h_rows, pos_tiles, seg_end, n_rows):
    n_tok = h_rows.shape[0]
    ts = ROW_TILE
    grid_spec = pltpu.PrefetchScalarGridSpec(
        num_scalar_prefetch=1,
        grid=(n_tok // ts,),
        in_specs=[
            pl.BlockSpec(memory_space=pl.ANY),
            pl.BlockSpec((ts, ROW_CHUNKS, V7X_LANES), lambda i, se: (i, 0, 0)),
        ],
        out_specs=pl.BlockSpec(memory_space=pl.ANY),
        scratch_shapes=[
            pltpu.SMEM((TOP_K, ts), jnp.int32),
            pltpu.VMEM((MOE_TILE, ROW_CHUNKS, V7X_LANES), F32),
            pltpu.SemaphoreType.DMA,
            pltpu.SemaphoreType.DMA,
        ],
    )
    return pl.pallas_call(
        _scatter_kernel,
        grid_spec=grid_spec,
        out_shape=jax.ShapeDtypeStruct((n_rows, ROW_CHUNKS, V7X_LANES), F32),
        compiler_params=pltpu.CompilerParams(
            dimension_semantics=("arbitrary",),
            vmem_limit_bytes=V7X_VMEM_LIMIT_BYTES),
        name="scatter_rows",
    )(seg_end, pos_tiles, h_rows)


def _gather_kernel(pos_hbm, y_hbm, o_ref, pos_smem, pos_sem, row_sem):
    i = pl.program_id(0)
    ts = o_ref.shape[1]
    pos_cp = pltpu.make_async_copy(pos_hbm.at[i], pos_smem, pos_sem)
    pos_cp.start()
    pos_cp.wait()
    _row_copies(ts, lambda t, k: pltpu.make_async_copy(
        y_hbm.at[pos_smem[k, t]], o_ref.at[k, t], row_sem))
    for k in range(TOP_K):
        pltpu.make_async_copy(y_hbm.at[pl.ds(0, ts)], o_ref.at[k], row_sem).wait()


def _gather_rows(rows, pos_tiles):
    n_tiles, _, ts = pos_tiles.shape
    n_tok = n_tiles * ts
    return pl.pallas_call(
        _gather_kernel,
        grid=(n_tiles,),
        in_specs=[pl.BlockSpec(memory_space=pl.ANY), pl.BlockSpec(memory_space=pl.ANY)],
        out_specs=pl.BlockSpec((TOP_K, ts, ROW_CHUNKS, V7X_LANES), lambda i: (0, i, 0, 0)),
        out_shape=jax.ShapeDtypeStruct((TOP_K, n_tok, ROW_CHUNKS, V7X_LANES), F32),
        scratch_shapes=[
            pltpu.SMEM((TOP_K, ts), jnp.int32),
            pltpu.SemaphoreType.DMA,
            pltpu.SemaphoreType.DMA,
        ],
        compiler_params=pltpu.CompilerParams(
            dimension_semantics=("arbitrary",),
            vmem_limit_bytes=V7X_VMEM_LIMIT_BYTES),
        name="gather_rows",
    )(pos_tiles, rows)


def _experts_kernel(te_ref, nu_ref, x_ref, wg_ref, wu_ref, wd_ref, y_ref, wg_bf, wu_bf, wd_bf):
    j = pl.program_id(0)
    tm = x_ref.shape[0] // ROW_CHUNKS

    @pl.when((j == 0) | (te_ref[j] != te_ref[jnp.maximum(j - 1, 0)]))
    def _():
        wg_bf[...] = wg_ref[0, 0].astype(BF16)
        wu_bf[...] = wu_ref[0, 0].astype(BF16)
        wd_bf[...] = wd_ref[0, 0].astype(BF16)

    @pl.when(j < nu_ref[0])
    def _():
        h = _load_rows(x_ref, tm).astype(BF16)
        a = jnp.dot(h, wg_bf[...], preferred_element_type=F32)
        b = jnp.dot(h, wu_bf[...], preferred_element_type=F32)
        t = (_silu(a) * b).astype(BF16)
        _store_rows(y_ref, jnp.dot(t, wd_bf[...], preferred_element_type=F32))

    @pl.when(j >= nu_ref[0])
    def _():
        y_ref[...] = jnp.zeros_like(y_ref)


def _experts_call(layer, rows, tile_expert, n_used, w_gate, w_up, w_down):
    n_rows = rows.shape[0]
    tm = MOE_TILE
    d = D_MODEL
    rows2 = rows.reshape(n_rows * ROW_CHUNKS, V7X_LANES)
    grid_spec = pltpu.PrefetchScalarGridSpec(
        num_scalar_prefetch=2,
        grid=(n_rows // tm,),
        in_specs=[
            pl.BlockSpec((tm * ROW_CHUNKS, V7X_LANES),
                         lambda j, te, nu: (jnp.minimum(j, nu[0] - 1), 0)),
            pl.BlockSpec((1, 1, d, D_EXPERT), lambda j, te, nu: (layer, te[j], 0, 0)),
            pl.BlockSpec((1, 1, d, D_EXPERT), lambda j, te, nu: (layer, te[j], 0, 0)),
            pl.BlockSpec((1, 1, D_EXPERT, d), lambda j, te, nu: (layer, te[j], 0, 0)),
        ],
        out_specs=pl.BlockSpec((tm * ROW_CHUNKS, V7X_LANES), lambda j, te, nu: (j, 0)),
        scratch_shapes=[
            pltpu.VMEM((d, D_EXPERT), BF16),
            pltpu.VMEM((d, D_EXPERT), BF16),
            pltpu.VMEM((D_EXPERT, d), BF16),
        ],
    )
    y = pl.pallas_call(
        _experts_kernel,
        grid_spec=grid_spec,
        out_shape=jax.ShapeDtypeStruct((n_rows * ROW_CHUNKS, V7X_LANES), F32),
        compiler_params=pltpu.CompilerParams(
            dimension_semantics=("arbitrary",),
            vmem_limit_bytes=V7X_VMEM_LIMIT_BYTES),
        name="experts_l%d" % layer,
    )(tile_expert, n_used, rows2, w_gate, w_up, w_down)
    return y.reshape(n_rows, ROW_CHUNKS, V7X_LANES)


def _moe_rows(layer, h_rows, eidx, rank, counts, w_gate, w_up, w_down):
    n_tok = eidx.shape[1]
    pos_tiles, seg_end, tile_expert, n_used, n_rows = _sorted_positions(eidx, rank, counts[:, 0])
    sorted_rows = _scatter_rows(
        h_rows.reshape(n_tok, ROW_CHUNKS, V7X_LANES), pos_tiles, seg_end, n_rows)
    y_sorted = _experts_call(layer, sorted_rows, tile_expert, n_used, w_gate, w_up, w_down)
    y_pairs = _gather_rows(y_sorted, pos_tiles)
    return y_pairs.reshape(TOP_K * n_tok * ROW_CHUNKS, V7X_LANES)


def _final_kernel(x_ref, y0_ref, y1_ref, wt_ref, modp_ref, fg_ref, o_ref):
    y = _moe_residual(x_ref, y0_ref, y1_ref, wt_ref, modp_ref[0, 0][5:6])
    ms = jnp.mean(y * y, axis=-1, keepdims=True)
    o_ref[...] = y * lax.rsqrt(ms + EPS) * fg_ref[...]


def _final_call(layer, x, y_pairs, wt, mod4, fg, seq_len):
    n_tok, d = x.shape
    tm = MIX_TILE
    tiles_per_seq = seq_len // tm
    return pl.pallas_call(
        _final_kernel,
        grid=(n_tok // tm,),
        in_specs=_combine_specs(n_tok) + [
            pl.BlockSpec((1, 1, 6, d), lambda i: (layer, i // tiles_per_seq, 0, 0)),
            _const_spec(fg.shape),
        ],
        out_specs=pl.BlockSpec((tm, d), lambda i: (i, 0)),
        out_shape=jax.ShapeDtypeStruct((n_tok, d), F32),
        compiler_params=pltpu.CompilerParams(
            dimension_semantics=("arbitrary",),
            vmem_limit_bytes=V7X_VMEM_LIMIT_BYTES),
        name="final_norm",
    )(x, y_pairs, y_pairs, wt, mod4, fg)


def kernel(x, c, norm1_g, norm2_g, ada_w, ada_b, ab_w_in, pool_w, pool_scale, conf_conv_w, conf_conv_b, conf_ln_g, conf_ln_b, ab_w_out, cd_w_in, sconv_w, gmlp_ln_g, gmlp_ln_b, gmlp_ws, gmlp_bs, cd_w_out, router_w, router_bias, exp_w_gate, exp_w_up, exp_w_down, final_g):
    bsz, seq_len, d = x.shape
    n_tok = bsz * seq_len
    tm = MIX_TILE
    tiles_per_seq = seq_len // tm
    xf = x.reshape(n_tok, d)

    mod = _ada_mod(c, ada_w, ada_b)
    mod4 = mod.reshape(mod.shape[0], bsz, 6, d)

    rw_hi = router_w.astype(BF16)
    rw_lo = (router_w - rw_hi.astype(F32)).astype(BF16)
    rwt = jnp.concatenate([rw_hi.T, rw_lo.T], axis=0)
    rbias = router_bias.reshape(N_EXPERTS, 1)
    fg = final_g.reshape(1, d)

    weights_ab = [
        ab_w_in[0].astype(BF16), pool_w[0].astype(BF16), pool_scale[0].reshape(1, D_HALF),
        conf_conv_w[0], conf_conv_b[0].reshape(1, D_HALF), conf_ln_g[0].reshape(1, D_HALF),
        conf_ln_b[0].reshape(1, D_HALF), ab_w_out[0].astype(BF16),
    ]
    scratch_ab = [pltpu.VMEM((POOL_HIST + tm, D_HALF), F32),
                  pltpu.VMEM((CONV_HIST + tm, D_HALF), F32)]
    x1, h2, eidx0, wsel0, rank0, counts0 = _mixer_call(
        _mixer_ab_kernel, 0, [xf], [pl.BlockSpec((tm, d), lambda i: (i, 0))],
        mod4, norm1_g[0:1], norm2_g[0:1], weights_ab, rwt, rbias, scratch_ab, seq_len, "mixer_ab")
    y_pairs0 = _moe_rows(0, h2, eidx0, rank0, counts0, exp_w_gate, exp_w_up, exp_w_down)

    bsf = jnp.repeat(gmlp_bs[0].T, POOL_GROUP, axis=1)
    weights_cd = [
        cd_w_in[0].astype(BF16), sconv_w[0], gmlp_ln_g[0].reshape(1, D_HALF),
        gmlp_ln_b[0].reshape(1, D_HALF), gmlp_ws[0], bsf, cd_w_out[0].astype(BF16),
    ]
    scratch_cd = [pltpu.VMEM((SCONV_HIST + tm, D_HALF), F32)]
    prev_mod_spec = pl.BlockSpec((1, 1, 6, d), lambda i: (0, i // tiles_per_seq, 0, 0))
    x3, h4, eidx1, wsel1, rank1, counts1 = _mixer_call(
        _mixer_cd_kernel, 1, [x1, y_pairs0, y_pairs0, wsel0.T, mod4],
        _combine_specs(n_tok) + [prev_mod_spec],
        mod4, norm1_g[1:2], norm2_g[1:2], weights_cd, rwt, rbias, scratch_cd, seq_len, "mixer_cd")
    y_pairs1 = _moe_rows(1, h4, eidx1, rank1, counts1, exp_w_gate, exp_w_up, exp_w_down)

    out = _final_call(1, x3, y_pairs1, wsel1.T, mod4, fg, seq_len)
    return out.reshape(bsz, seq_len, d)
```

```python
import functools

import jax
import jax.numpy as jnp
from jax import lax
from jax.experimental import pallas as pl
from jax.experimental.pallas import tpu as pltpu
from jax.experimental.pallas import tpu_sc as plsc

D_MODEL = 1024
EPS = 1e-6
POOL_WINDOWS = (2, 4, 8, 16)
POOL_GROUP = 128
D_HALF = 512
CONF_KERNEL = 31
SCONV_KERNEL = 3
CHUNK = 128
GMLP_HEADS = 4
N_EXPERTS = 16
N_GROUPS = 4
EXPERTS_PER_GROUP = 4
TOP_K = 2
D_EXPERT = 512

V7X_LANES = 128
V7X_SUBLANES = 8
V7X_VMEM_LIMIT_BYTES = 56 * 1024 * 1024

MIX_TILE = 512
MOE_TILE = 512
ROW_TILE = 512
ROW_UNROLL = 8
SC_CHUNK = 32
ROW_CHUNKS = D_MODEL // V7X_LANES
CONV_HIST = 32
POOL_HIST = 16
SCONV_HIST = 8

BF16 = jnp.bfloat16
F32 = jnp.float32


def _rms_mod(x, g_row, shift_row, scale_row):
    ms = jnp.mean(x * x, axis=-1, keepdims=True)
    y = x * lax.rsqrt(ms + EPS)
    return (y * g_row) * (1.0 + scale_row) + shift_row


def _layer_norm(x, g_row, b_row):
    mu = jnp.mean(x, axis=-1, keepdims=True)
    xc = x - mu
    var = jnp.mean(xc * xc, axis=-1, keepdims=True)
    return xc * lax.rsqrt(var + EPS) * g_row + b_row


def _sigmoid(x):
    return 1.0 / (1.0 + jnp.exp(-x))


def _silu(x):
    return x * _sigmoid(x)


def _gelu_tanh(x):
    c = 0.7978845608028654
    return 0.5 * x * (1.0 + jnp.tanh(c * (x + 0.044715 * (x * x * x))))


def _load_rows(ref, n_rows):
    return jnp.concatenate(
        [ref[pl.ds(c, n_rows, stride=ROW_CHUNKS), :] for c in range(ROW_CHUNKS)], axis=1)


def _store_rows(ref, val):
    n_rows = val.shape[0]
    for c in range(ROW_CHUNKS):
        ref[pl.ds(c, n_rows, stride=ROW_CHUNKS), :] = val[:, c * V7X_LANES:(c + 1) * V7X_LANES]


def _ada_kernel(ct_ref, w_ref, b_ref, o_ref):
    ct = ct_ref[...]
    cond = _silu(ct)
    w = w_ref[0]
    nb = ct.shape[1]
    for b in range(nb):
        col = cond[:, b:b + 1]
        o_ref[0, b:b + 1, :] = jnp.sum(col * w, axis=0, keepdims=True) + b_ref[0]


def _ada_mod(c, ada_w, ada_b):
    depth, d, six_d = ada_w.shape
    bsz = c.shape[0]
    nb = D_MODEL
    return pl.pallas_call(
        _ada_kernel,
        grid=(depth, six_d // nb),
        in_specs=[
            pl.BlockSpec((d, bsz), lambda l, j: (0, 0)),
            pl.BlockSpec((1, d, nb), lambda l, j: (l, 0, j)),
            pl.BlockSpec((1, 1, nb), lambda l, j: (l, 0, j)),
        ],
        out_specs=pl.BlockSpec((1, bsz, nb), lambda l, j: (l, 0, j)),
        out_shape=jax.ShapeDtypeStruct((depth, bsz, six_d), F32),
        compiler_params=pltpu.CompilerParams(
            dimension_semantics=("arbitrary", "arbitrary"),
            vmem_limit_bytes=V7X_VMEM_LIMIT_BYTES),
        name="ada_mod",
    )(c.T, ada_w, ada_b.reshape(depth, 1, six_d))


def _route(h2_bf, rwt_ref, rbias_ref, eidx_ref, wsel_ref, rank_ref, counts_ref, cnt_ref):
    nt = (((1,), (1,)), ((), ()))
    r = lax.dot_general(rwt_ref[...], h2_bf, nt, preferred_element_type=F32)
    logits = r[:N_EXPERTS] + r[N_EXPERTS:]
    m = jnp.max(logits, axis=0, keepdims=True)
    ex = jnp.exp(logits - m)
    probs = ex / jnp.sum(ex, axis=0, keepdims=True)
    sel = probs + rbias_ref[...]
    s = [sel[e:e + 1] for e in range(N_EXPERTS)]
    p = [probs[e:e + 1] for e in range(N_EXPERTS)]
    best = None
    gi = None
    for g in range(N_GROUPS):
        a, b, c, d = s[4 * g:4 * g + 4]
        hi1, lo1 = jnp.maximum(a, b), jnp.minimum(a, b)
        hi2, lo2 = jnp.maximum(c, d), jnp.minimum(c, d)
        top1 = jnp.maximum(hi1, hi2)
        top2 = jnp.maximum(jnp.minimum(hi1, hi2), jnp.maximum(lo1, lo2))
        score = top1 + top2
        if g == 0:
            best, gi = score, jnp.zeros(score.shape, jnp.int32)
        else:
            upd = score > best
            gi = jnp.where(upd, g, gi)
            best = jnp.where(upd, score, best)
    v, q = [], []
    for j in range(EXPERTS_PER_GROUP):
        vj, qj = s[j], p[j]
        for g in range(1, N_GROUPS):
            pick = gi == g
            vj = jnp.where(pick, s[4 * g + j], vj)
            qj = jnp.where(pick, p[4 * g + j], qj)
        v.append(vj)
        q.append(qj)
    i1 = jnp.zeros(gi.shape, jnp.int32)
    m1 = v[0]
    for j in range(1, EXPERTS_PER_GROUP):
        upd = v[j] > m1
        i1 = jnp.where(upd, j, i1)
        m1 = jnp.where(upd, v[j], m1)
    i2 = jnp.zeros(gi.shape, jnp.int32)
    m2 = jnp.full(m1.shape, -jnp.inf, F32)
    for j in range(EXPERTS_PER_GROUP):
        cand = (i1 != j) & (v[j] > m2)
        i2 = jnp.where(cand, j, i2)
        m2 = jnp.where(cand, v[j], m2)
    pa = q[0]
    pb = q[0]
    for j in range(1, EXPERTS_PER_GROUP):
        pa = jnp.where(i1 == j, q[j], pa)
        pb = jnp.where(i2 == j, q[j], pb)
    tot = pa + pb
    e0 = gi * EXPERTS_PER_GROUP + i1
    e1 = gi * EXPERTS_PER_GROUP + i2
    eidx_ref[0:1, :] = e0
    eidx_ref[1:2, :] = e1
    wsel_ref[0:1, :] = pa / tot
    wsel_ref[1:2, :] = pb / tot

    t = h2_bf.shape[0]
    e_iota = lax.broadcasted_iota(jnp.int32, (N_EXPERTS, t), 0)
    oh0 = e_iota == e0
    oh1 = e_iota == e1
    both = jnp.where(oh0 | oh1, 1.0, 0.0)
    r_i = lax.broadcasted_iota(jnp.int32, (V7X_LANES, V7X_LANES), 0)
    c_i = lax.broadcasted_iota(jnp.int32, (V7X_LANES, V7X_LANES), 1)
    before = jnp.where(r_i < c_i, 1.0, 0.0).astype(BF16)
    run = cnt_ref[...]
    rank0, rank1 = [], []
    for blk in range(t // V7X_LANES):
        lanes = slice(blk * V7X_LANES, (blk + 1) * V7X_LANES)
        b = both[:, lanes]
        pre = jnp.dot(b.astype(BF16), before, preferred_element_type=F32) + run
        rank0.append(jnp.sum(jnp.where(oh0[:, lanes], pre, 0.0), axis=0, keepdims=True))
        rank1.append(jnp.sum(jnp.where(oh1[:, lanes], pre, 0.0), axis=0, keepdims=True))
        run = run + jnp.sum(b, axis=1, keepdims=True)
    cnt_ref[...] = run
    rank_ref[0:1, :] = jnp.concatenate(rank0, axis=1).astype(jnp.int32)
    rank_ref[1:2, :] = jnp.concatenate(rank1, axis=1).astype(jnp.int32)
    counts_ref[...] = jnp.broadcast_to(run, counts_ref.shape).astype(jnp.int32)


def _finish_mixer(x, m, mod, n2g_ref, rwt_ref, rbias_ref,
                  x1_ref, h2_ref, eidx_ref, wsel_ref, rank_ref, counts_ref, cnt_ref):
    @pl.when(pl.program_id(0) == 0)
    def _():
        cnt_ref[...] = jnp.zeros_like(cnt_ref)

    x1 = x + mod[2:3] * m
    x1_ref[...] = x1
    h2 = _rms_mod(x1, n2g_ref[...], mod[3:4], mod[4:5])
    h2_bf = h2.astype(BF16)
    _store_rows(h2_ref, h2_bf.astype(F32))
    _route(h2_bf, rwt_ref, rbias_ref, eidx_ref, wsel_ref, rank_ref, counts_ref, cnt_ref)


def _moe_residual(x_ref, y0_ref, y1_ref, wt_ref, g2_row):
    tm = x_ref.shape[0]
    wt = wt_ref[...]
    y = wt[:, 0:1] * _load_rows(y0_ref, tm) + wt[:, 1:2] * _load_rows(y1_ref, tm)
    return x_ref[...] + g2_row * y


def _mixer_ab_kernel(tiles_per_seq,
                     x_ref, mod_ref, n1g_ref, n2g_ref, win_ref, poolw_ref, pscale_ref,
                     convw_ref, convb_ref, lng_ref, lnb_ref, wout_ref, rwt_ref, rbias_ref,
                     x1_ref, h2_ref, eidx_ref, wsel_ref, rank_ref, counts_ref,
                     pool_ext, conv_ext, cnt_ref):
    i = pl.program_id(0)
    tm = x_ref.shape[0]
    seq_tile = i % tiles_per_seq

    @pl.when(seq_tile == 0)
    def _():
        pool_ext[0:POOL_HIST, :] = jnp.zeros((POOL_HIST, D_HALF), F32)
        conv_ext[0:CONV_HIST, :] = jnp.zeros((CONV_HIST, D_HALF), F32)

    x = x_ref[...]
    mod = mod_ref[0, 0]
    h = _rms_mod(x, n1g_ref[...], mod[0:1], mod[1:2]).astype(BF16)
    z = jnp.dot(h, win_ref[...], preferred_element_type=F32)
    zp = z[:, :D_HALF]
    glu = z[:, D_HALF:2 * D_HALF] * _sigmoid(z[:, 2 * D_HALF:])
    pool_ext[POOL_HIST:POOL_HIST + tm, :] = zp
    conv_ext[CONV_HIST:CONV_HIST + tm, :] = glu

    row = lax.broadcasted_iota(jnp.int32, (tm, 1), 0)
    pos1 = (seq_tile * tm + row + 1).astype(F32)
    pool_out = []
    for g, w in enumerate(POOL_WINDOWS):
        cols = slice(g * POOL_GROUP, (g + 1) * POOL_GROUP)
        u = pool_ext[:, cols]
        acc = u
        span = 1
        while span < w:
            acc = acc + pltpu.roll(acc, span, axis=0)
            span *= 2
        wsum = acc[POOL_HIST:POOL_HIST + tm]
        inv_cnt = 1.0 / jnp.minimum(pos1, float(w))
        diff = wsum * inv_cnt - zp[:, cols]
        po = jnp.dot(diff.astype(BF16), poolw_ref[g], preferred_element_type=F32)
        pool_out.append(po * pscale_ref[:, cols])

    convw = convw_ref[...]
    ext_rows = tm + V7X_SUBLANES
    conv = None
    for r in range(V7X_SUBLANES):
        vr = None
        for a in range(CONV_HIST // V7X_SUBLANES):
            lag = V7X_SUBLANES * a + r
            if lag >= CONF_KERNEL:
                continue
            k = CONF_KERNEL - 1 - lag
            start = CONV_HIST - V7X_SUBLANES - V7X_SUBLANES * a
            term = convw[k:k + 1, :] * conv_ext[start:start + ext_rows, :]
            vr = term if vr is None else vr + term
        if r:
            vr = pltpu.roll(vr, r, axis=0)
        conv = vr if conv is None else conv + vr
    conv = conv[V7X_SUBLANES:V7X_SUBLANES + tm] + convb_ref[...]
    conf = _silu(_layer_norm(conv, lng_ref[...], lnb_ref[...]))

    pool_ext[0:POOL_HIST, :] = zp[tm - POOL_HIST:tm]
    conv_ext[0:CONV_HIST, :] = glu[tm - CONV_HIST:tm]

    m = jnp.dot(conf.astype(BF16), wout_ref[D_HALF:, :], preferred_element_type=F32)
    for g in range(len(POOL_WINDOWS)):
        rows = slice(g * POOL_GROUP, (g + 1) * POOL_GROUP)
        m = m + jnp.dot(pool_out[g].astype(BF16), wout_ref[rows, :], preferred_element_type=F32)
    _finish_mixer(x, m, mod, n2g_ref, rwt_ref, rbias_ref,
                  x1_ref, h2_ref, eidx_ref, wsel_ref, rank_ref, counts_ref, cnt_ref)


def _mixer_cd_kernel(tiles_per_seq,
                     x_ref, y0_ref, y1_ref, wt_ref, modp_ref,
                     mod_ref, n1g_ref, n2g_ref, win_ref, sconvw_ref, lng_ref, lnb_ref,
                     ws_ref, bsf_ref, wout_ref, rwt_ref, rbias_ref,
                     x1_ref, h2_ref, eidx_ref, wsel_ref, rank_ref, counts_ref,
                     sconv_ext, cnt_ref):
    i = pl.program_id(0)
    tm = x_ref.shape[0]
    seq_tile = i % tiles_per_seq

    @pl.when(seq_tile == 0)
    def _():
        sconv_ext[0:SCONV_HIST, :] = jnp.zeros((SCONV_HIST, D_HALF), F32)

    x = _moe_residual(x_ref, y0_ref, y1_ref, wt_ref, modp_ref[0, 0][5:6])
    mod = mod_ref[0, 0]
    h = _rms_mod(x, n1g_ref[...], mod[0:1], mod[1:2]).astype(BF16)
    z = jnp.dot(h, win_ref[...], preferred_element_type=F32)
    bg = z[:, :D_HALF]
    ch = z[:, D_HALF:2 * D_HALF] * z[:, 2 * D_HALF:3 * D_HALF]
    zd = _gelu_tanh(z[:, 3 * D_HALF:])
    u = zd[:, :D_HALF]
    v = _layer_norm(zd[:, D_HALF:], lng_ref[...], lnb_ref[...])

    sconv_ext[SCONV_HIST:SCONV_HIST + tm, :] = ch
    sw = sconvw_ref[...]
    ext = sconv_ext[...]
    conv = sw[2:3, :] * ext
    conv = conv + sw[1:2, :] * pltpu.roll(ext, 1, axis=0)
    conv = conv + sw[0:1, :] * pltpu.roll(ext, 2, axis=0)
    sc_out = bg * conv[SCONV_HIST:SCONV_HIST + tm]
    sconv_ext[0:SCONV_HIST, :] = ch[tm - SCONV_HIST:tm]

    r_i = lax.broadcasted_iota(jnp.int32, (CHUNK, CHUNK), 0)
    c_i = lax.broadcasted_iota(jnp.int32, (CHUNK, CHUNK), 1)
    tril = c_i <= r_i
    wm = [jnp.where(tril, ws_ref[hd], 0.0).astype(BF16) for hd in range(GMLP_HEADS)]
    v_bf = v.astype(BF16)
    bsf = bsf_ref[...]
    gm_rows = []
    for n in range(tm // CHUNK):
        rows = slice(n * CHUNK, (n + 1) * CHUNK)
        heads = []
        for hd in range(GMLP_HEADS):
            cols = slice(hd * POOL_GROUP, (hd + 1) * POOL_GROUP)
            heads.append(jnp.dot(wm[hd], v_bf[rows, cols], preferred_element_type=F32))
        mixed = jnp.concatenate(heads, axis=1) + bsf
        gm_rows.append(u[rows] * mixed)
    gm_out = jnp.concatenate(gm_rows, axis=0)

    m = jnp.dot(sc_out.astype(BF16), wout_ref[:D_HALF, :], preferred_element_type=F32)
    m = m + jnp.dot(gm_out.astype(BF16), wout_ref[D_HALF:, :], preferred_element_type=F32)
    _finish_mixer(x, m, mod, n2g_ref, rwt_ref, rbias_ref,
                  x1_ref, h2_ref, eidx_ref, wsel_ref, rank_ref, counts_ref, cnt_ref)


def _const_spec(shape):
    nd = len(shape)
    return pl.BlockSpec(shape, lambda i: (0,) * nd)


def _mixer_call(kernel_fn, layer, stream_inputs, stream_specs, mod4, n1g, n2g, weights, rwt, rbias,
                scratch, seq_len, name):
    n_tok = mod4.shape[1] * seq_len
    d = D_MODEL
    tm = MIX_TILE
    tiles_per_seq = seq_len // tm
    in_specs = stream_specs + [
        pl.BlockSpec((1, 1, 6, d), lambda i: (layer, i // tiles_per_seq, 0, 0)),
        _const_spec(n1g.shape),
        _const_spec(n2g.shape),
    ] + [_const_spec(w.shape) for w in weights] + [_const_spec(rwt.shape), _const_spec(rbias.shape)]
    out_specs = [
        pl.BlockSpec((tm, d), lambda i: (i, 0)),
        pl.BlockSpec((tm * ROW_CHUNKS, V7X_LANES), lambda i: (i, 0)),
        pl.BlockSpec((TOP_K, tm), lambda i: (0, i)),
        pl.BlockSpec((TOP_K, tm), lambda i: (0, i)),
        pl.BlockSpec((TOP_K, tm), lambda i: (0, i)),
        pl.BlockSpec((N_EXPERTS, V7X_LANES), lambda i: (0, 0)),
    ]
    out_shape = [
        jax.ShapeDtypeStruct((n_tok, d), F32),
        jax.ShapeDtypeStruct((n_tok * ROW_CHUNKS, V7X_LANES), F32),
        jax.ShapeDtypeStruct((TOP_K, n_tok), jnp.int32),
        jax.ShapeDtypeStruct((TOP_K, n_tok), F32),
        jax.ShapeDtypeStruct((TOP_K, n_tok), jnp.int32),
        jax.ShapeDtypeStruct((N_EXPERTS, V7X_LANES), jnp.int32),
    ]
    return pl.pallas_call(
        functools.partial(kernel_fn, tiles_per_seq),
        grid=(n_tok // tm,),
        in_specs=in_specs,
        out_specs=out_specs,
        out_shape=out_shape,
        scratch_shapes=scratch + [pltpu.VMEM((N_EXPERTS, 1), F32)],
        compiler_params=pltpu.CompilerParams(
            dimension_semantics=("arbitrary",),
            vmem_limit_bytes=V7X_VMEM_LIMIT_BYTES),
        name=name,
    )(*stream_inputs, mod4, n1g, n2g, *weights, rwt, rbias)


def _combine_specs(n_tok):
    tm = MIX_TILE
    n_tiles = n_tok // tm
    return [
        pl.BlockSpec((tm, D_MODEL), lambda i: (i, 0)),
        pl.BlockSpec((tm * ROW_CHUNKS, V7X_LANES), lambda i: (i, 0)),
        pl.BlockSpec((tm * ROW_CHUNKS, V7X_LANES), lambda i: (n_tiles + i, 0)),
        pl.BlockSpec((tm, TOP_K), lambda i: (i, 0)),
    ]


def _sorted_positions(eidx, rank, counts):
    n_pairs = eidx.shape[0] * eidx.shape[1]
    n_rows = n_pairs + N_EXPERTS * MOE_TILE
    n_tiles = n_rows // MOE_TILE
    padded = (counts + MOE_TILE - 1) // MOE_TILE * MOE_TILE
    seg_end = jnp.cumsum(padded)
    seg_start = seg_end - padded
    expert_ids = jnp.arange(N_EXPERTS, dtype=jnp.int32)
    pos = jnp.sum(jnp.where(eidx[..., None] == expert_ids, seg_start, 0), axis=-1) + rank
    tile_row0 = jnp.arange(n_tiles, dtype=jnp.int32) * MOE_TILE
    tile_expert = jnp.minimum(
        jnp.sum((tile_row0[:, None] >= seg_end[None, :]).astype(jnp.int32), axis=1), N_EXPERTS - 1)
    n_used = (seg_end[N_EXPERTS - 1] // MOE_TILE).reshape(1)
    n_tok = eidx.shape[1]
    pos_tiles = pos.reshape(TOP_K, n_tok // ROW_TILE, ROW_TILE).transpose(1, 0, 2)
    return (pos_tiles, seg_end.astype(jnp.int32), tile_expert.astype(jnp.int32),
            n_used.astype(jnp.int32), n_rows)


def _row_copies(n_tok_tile, make_copy):
    def issue(g, carry):
        for u in range(ROW_UNROLL):
            t = g * ROW_UNROLL + u
            for k in range(TOP_K):
                make_copy(t, k).start(priority=k)
        return carry
    lax.fori_loop(0, n_tok_tile // ROW_UNROLL, issue, 0)


def _scatter_kernel(segend_ref, pos_hbm, h_ref, o_hbm, pos_smem, zero_ref, pos_sem, row_sem):
    i = pl.program_id(0)
    ts = h_ref.shape[0]

    @pl.when(i == 0)
    def _():
        zero_ref[...] = jnp.zeros_like(zero_ref)

        def zero_tile(row0):
            cp = pltpu.make_async_copy(zero_ref, o_hbm.at[pl.ds(row0, MOE_TILE)], row_sem)
            cp.start()
            cp.wait()

        for e in range(N_EXPERTS):
            seg_lo = segend_ref[e - 1] if e else 0
            pl.when(segend_ref[e] > seg_lo)(
                functools.partial(zero_tile, segend_ref[e] - MOE_TILE))
        n_rows = o_hbm.shape[0]
        for j in range(N_EXPERTS):
            row0 = n_rows - (j + 1) * MOE_TILE
            pl.when(row0 >= segend_ref[N_EXPERTS - 1])(functools.partial(zero_tile, row0))

    pos_cp = pltpu.make_async_copy(pos_hbm.at[i], pos_smem, pos_sem)
    pos_cp.start()
    pos_cp.wait()
    _row_copies(ts, lambda t, k: pltpu.make_async_copy(
        h_ref.at[t], o_hbm.at[pos_smem[k, t]], row_sem))
    for _ in range(TOP_K):
        pltpu.make_async_copy(h_ref, o_hbm.at[pl.ds(0, ts)], row_sem).wait()


def _scatter_rows(h_rows, pos_tiles, seg_end, n_rows):
    n_tok = h_rows.shape[0]
    ts = ROW_TILE
    grid_spec = pltpu.PrefetchScalarGridSpec(
        num_scalar_prefetch=1,
        grid=(n_tok // ts,),
        in_specs=[
            pl.BlockSpec(memory_space=pl.ANY),
            pl.BlockSpec((ts, ROW_CHUNKS, V7X_LANES), lambda i, se: (i, 0, 0)),
        ],
        out_specs=pl.BlockSpec(memory_space=pl.ANY),
        scratch_shapes=[
            pltpu.SMEM((TOP_K, ts), jnp.int32),
            pltpu.VMEM((MOE_TILE, ROW_CHUNKS, V7X_LANES), F32),
            pltpu.SemaphoreType.DMA,
            pltpu.SemaphoreType.DMA,
        ],
    )
    return pl.pallas_call(
        _scatter_kernel,
        grid_spec=grid_spec,
        out_shape=jax.ShapeDtypeStruct((n_rows, ROW_CHUNKS, V7X_LANES), F32),
        compiler_params=pltpu.CompilerParams(
            dimension_semantics=("arbitrary",),
            vmem_limit_bytes=V7X_VMEM_LIMIT_BYTES),
        name="scatter_rows",
    )(seg_end, pos_tiles, h_rows)


def _gather_kernel(pos_hbm, y_hbm, o_ref, pos_smem, pos_sem, row_sem):
    i = pl.program_id(0)
    ts = o_ref.shape[1]
    pos_cp = pltpu.make_async_copy(pos_hbm.at[i], pos_smem, pos_sem)
    pos_cp.start()
    pos_cp.wait()
    _row_copies(ts, lambda t, k: pltpu.make_async_copy(
        y_hbm.at[pos_smem[k, t]], o_ref.at[k, t], row_sem))
    for k in range(TOP_K):
        pltpu.make_async_copy(y_hbm.at[pl.ds(0, ts)], o_ref.at[k], row_sem).wait()


def _sc_gather_rows(rows, pos_flat):
    info = plsc.get_sparse_core_info()
    n_workers = info.num_cores * info.num_subcores
    n_pairs = pos_flat.shape[0]
    per_worker = n_pairs // n_workers
    n_chunks = per_worker // SC_CHUNK
    idx = pos_flat.reshape(n_workers, n_chunks, SC_CHUNK)
    mesh = plsc.VectorSubcoreMesh(core_axis_name="core", subcore_axis_name="subcore")

    @functools.partial(
        pl.kernel,
        out_type=jax.ShapeDtypeStruct((n_pairs, ROW_CHUNKS, V7X_LANES), F32),
        mesh=mesh,
        scratch_types=[
            pltpu.VMEM((n_chunks, SC_CHUNK), jnp.int32),
            pltpu.VMEM((2, SC_CHUNK, ROW_CHUNKS, V7X_LANES), F32),
            pltpu.SemaphoreType.DMA((2,)),
            pltpu.SemaphoreType.DMA((2,)),
        ],
        name="sc_gather_rows")
    def gather(y_hbm, i_hbm, o_hbm, idx_v, buf, gsem, wsem):
        wid = lax.axis_index("subcore") * info.num_cores + lax.axis_index("core")
        base = wid * per_worker
        pltpu.sync_copy(i_hbm.at[wid], idx_v)

        def fetch(s, slot):
            return pltpu.make_async_copy(y_hbm.at[idx_v.at[s]], buf.at[slot], gsem.at[slot])

        def flush(s, slot):
            dst = o_hbm.at[pl.ds(base + s * SC_CHUNK, SC_CHUNK)]
            return pltpu.make_async_copy(buf.at[slot], dst, wsem.at[slot])

        fetch(0, 0).start()
        for s in range(n_chunks):
            slot = s % 2
            fetch(s, slot).wait()
            flush(s, slot).start()
            if s + 1 < n_chunks:
                if s >= 1:
                    flush(s - 1, 1 - slot).wait()
                fetch(s + 1, 1 - slot).start()
        flush(n_chunks - 2, n_chunks % 2).wait()
        flush(n_chunks - 1, (n_chunks - 1) % 2).wait()

    return gather(rows, idx)


def _gather_rows(rows, pos_tiles):
    n_tiles, _, ts = pos_tiles.shape
    n_tok = n_tiles * ts
    return pl.pallas_call(
        _gather_kernel,
        grid=(n_tiles,),
        in_specs=[pl.BlockSpec(memory_space=pl.ANY), pl.BlockSpec(memory_space=pl.ANY)],
        out_specs=pl.BlockSpec((TOP_K, ts, ROW_CHUNKS, V7X_LANES), lambda i: (0, i, 0, 0)),
        out_shape=jax.ShapeDtypeStruct((TOP_K, n_tok, ROW_CHUNKS, V7X_LANES), F32),
        scratch_shapes=[
            pltpu.SMEM((TOP_K, ts), jnp.int32),
            pltpu.SemaphoreType.DMA,
            pltpu.SemaphoreType.DMA,
        ],
        compiler_params=pltpu.CompilerParams(
            dimension_semantics=("arbitrary",),
            vmem_limit_bytes=V7X_VMEM_LIMIT_BYTES),
        name="gather_rows",
    )(pos_tiles, rows)


def _experts_kernel(te_ref, nu_ref, x_ref, wg_ref, wu_ref, wd_ref, y_ref, wg_bf, wu_bf, wd_bf):
    j = pl.program_id(0)
    tm = x_ref.shape[0] // ROW_CHUNKS

    @pl.when((j == 0) | (te_ref[j] != te_ref[jnp.maximum(j - 1, 0)]))
    def _():
        wg_bf[...] = wg_ref[0, 0].astype(BF16)
        wu_bf[...] = wu_ref[0, 0].astype(BF16)
        wd_bf[...] = wd_ref[0, 0].astype(BF16)

    @pl.when(j < nu_ref[0])
    def _():
        h = _load_rows(x_ref, tm).astype(BF16)
        a = jnp.dot(h, wg_bf[...], preferred_element_type=F32)
        b = jnp.dot(h, wu_bf[...], preferred_element_type=F32)
        t = (_silu(a) * b).astype(BF16)
        _store_rows(y_ref, jnp.dot(t, wd_bf[...], preferred_element_type=F32))

    @pl.when(j >= nu_ref[0])
    def _():
        y_ref[...] = jnp.zeros_like(y_ref)


def _experts_call(layer, rows, tile_expert, n_used, w_gate, w_up, w_down):
    n_rows = rows.shape[0]
    tm = MOE_TILE
    d = D_MODEL
    rows2 = rows.reshape(n_rows * ROW_CHUNKS, V7X_LANES)
    grid_spec = pltpu.PrefetchScalarGridSpec(
        num_scalar_prefetch=2,
        grid=(n_rows // tm,),
        in_specs=[
            pl.BlockSpec((tm * ROW_CHUNKS, V7X_LANES),
                         lambda j, te, nu: (jnp.minimum(j, nu[0] - 1), 0)),
            pl.BlockSpec((1, 1, d, D_EXPERT), lambda j, te, nu: (layer, te[j], 0, 0)),
            pl.BlockSpec((1, 1, d, D_EXPERT), lambda j, te, nu: (layer, te[j], 0, 0)),
            pl.BlockSpec((1, 1, D_EXPERT, d), lambda j, te, nu: (layer, te[j], 0, 0)),
        ],
        out_specs=pl.BlockSpec((tm * ROW_CHUNKS, V7X_LANES), lambda j, te, nu: (j, 0)),
        scratch_shapes=[
            pltpu.VMEM((d, D_EXPERT), BF16),
            pltpu.VMEM((d, D_EXPERT), BF16),
            pltpu.VMEM((D_EXPERT, d), BF16),
        ],
    )
    y = pl.pallas_call(
        _experts_kernel,
        grid_spec=grid_spec,
        out_shape=jax.ShapeDtypeStruct((n_rows * ROW_CHUNKS, V7X_LANES), F32),
        compiler_params=pltpu.CompilerParams(
            dimension_semantics=("arbitrary",),
            vmem_limit_bytes=V7X_VMEM_LIMIT_BYTES),
        name="experts_l%d" % layer,
    )(tile_expert, n_used, rows2, w_gate, w_up, w_down)
    return y.reshape(n_rows, ROW_CHUNKS, V7X_LANES)


def _moe_rows(layer, h_rows, eidx, rank, counts, w_gate, w_up, w_down):
    n_tok = eidx.shape[1]
    pos_tiles, seg_end, tile_expert, n_used, n_rows = _sorted_positions(eidx, rank, counts[:, 0])
    sorted_rows = _scatter_rows(
        h_rows.reshape(n_tok, ROW_CHUNKS, V7X_LANES), pos_tiles, seg_end, n_rows)
    y_sorted = _experts_call(layer, sorted_rows, tile_expert, n_used, w_gate, w_up, w_down)
    pos_flat = pos_tiles.transpose(1, 0, 2).reshape(TOP_K * n_tok)
    y_pairs = _sc_gather_rows(y_sorted, pos_flat)
    return y_pairs.reshape(TOP_K * n_tok * ROW_CHUNKS, V7X_LANES)


def _final_kernel(x_ref, y0_ref, y1_ref, wt_ref, modp_ref, fg_ref, o_ref):
    y = _moe_residual(x_ref, y0_ref, y1_ref, wt_ref, modp_ref[0, 0][5:6])
    ms = jnp.mean(y * y, axis=-1, keepdims=True)
    o_ref[...] = y * lax.rsqrt(ms + EPS) * fg_ref[...]


def _final_call(layer, x, y_pairs, wt, mod4, fg, seq_len):
    n_tok, d = x.shape
    tm = MIX_TILE
    tiles_per_seq = seq_len // tm
    return pl.pallas_call(
        _final_kernel,
        grid=(n_tok // tm,),
        in_specs=_combine_specs(n_tok) + [
            pl.BlockSpec((1, 1, 6, d), lambda i: (layer, i // tiles_per_seq, 0, 0)),
            _const_spec(fg.shape),
        ],
        out_specs=pl.BlockSpec((tm, d), lambda i: (i, 0)),
        out_shape=jax.ShapeDtypeStruct((n_tok, d), F32),
        compiler_params=pltpu.CompilerParams(
            dimension_semantics=("arbitrary",),
            vmem_limit_bytes=V7X_VMEM_LIMIT_BYTES),
        name="final_norm",
    )(x, y_pairs, y_pairs, wt, mod4, fg)


def kernel(x, c, norm1_g, norm2_g, ada_w, ada_b, ab_w_in, pool_w, pool_scale, conf_conv_w, conf_conv_b, conf_ln_g, conf_ln_b, ab_w_out, cd_w_in, sconv_w, gmlp_ln_g, gmlp_ln_b, gmlp_ws, gmlp_bs, cd_w_out, router_w, router_bias, exp_w_gate, exp_w_up, exp_w_down, final_g):
    bsz, seq_len, d = x.shape
    n_tok = bsz * seq_len
    tm = MIX_TILE
    tiles_per_seq = seq_len // tm
    xf = x.reshape(n_tok, d)

    mod = _ada_mod(c, ada_w, ada_b)
    mod4 = mod.reshape(mod.shape[0], bsz, 6, d)

    rw_hi = router_w.astype(BF16)
    rw_lo = (router_w - rw_hi.astype(F32)).astype(BF16)
    rwt = jnp.concatenate([rw_hi.T, rw_lo.T], axis=0)
    rbias = router_bias.reshape(N_EXPERTS, 1)
    fg = final_g.reshape(1, d)

    weights_ab = [
        ab_w_in[0].astype(BF16), pool_w[0].astype(BF16), pool_scale[0].reshape(1, D_HALF),
        conf_conv_w[0], conf_conv_b[0].reshape(1, D_HALF), conf_ln_g[0].reshape(1, D_HALF),
        conf_ln_b[0].reshape(1, D_HALF), ab_w_out[0].astype(BF16),
    ]
    scratch_ab = [pltpu.VMEM((POOL_HIST + tm, D_HALF), F32),
                  pltpu.VMEM((CONV_HIST + tm, D_HALF), F32)]
    x1, h2, eidx0, wsel0, rank0, counts0 = _mixer_call(
        _mixer_ab_kernel, 0, [xf], [pl.BlockSpec((tm, d), lambda i: (i, 0))],
        mod4, norm1_g[0:1], norm2_g[0:1], weights_ab, rwt, rbias, scratch_ab, seq_len, "mixer_ab")
    y_pairs0 = _moe_rows(0, h2, eidx0, rank0, counts0, exp_w_gate, exp_w_up, exp_w_down)

    bsf = jnp.repeat(gmlp_bs[0].T, POOL_GROUP, axis=1)
    weights_cd = [
        cd_w_in[0].astype(BF16), sconv_w[0], gmlp_ln_g[0].reshape(1, D_HALF),
        gmlp_ln_b[0].reshape(1, D_HALF), gmlp_ws[0], bsf, cd_w_out[0].astype(BF16),
    ]
    scratch_cd = [pltpu.VMEM((SCONV_HIST + tm, D_HALF), F32)]
    prev_mod_spec = pl.BlockSpec((1, 1, 6, d), lambda i: (0, i // tiles_per_seq, 0, 0))
    x3, h4, eidx1, wsel1, rank1, counts1 = _mixer_call(
        _mixer_cd_kernel, 1, [x1, y_pairs0, y_pairs0, wsel0.T, mod4],
        _combine_specs(n_tok) + [prev_mod_spec],
        mod4, norm1_g[1:2], norm2_g[1:2], weights_cd, rwt, rbias, scratch_cd, seq_len, "mixer_cd")
    y_pairs1 = _moe_rows(1, h4, eidx1, rank1, counts1, exp_w_gate, exp_w_up, exp_w_down)

    out = _final_call(1, x3, y_pairs1, wsel1.T, mod4, fg, seq_len)
    return out.reshape(bsz, seq_len, d)
```

```python
import functools

import jax
import jax.numpy as jnp
from jax import lax
from jax.experimental import pallas as pl
from jax.experimental.pallas import tpu as pltpu
from jax.experimental.pallas import tpu_sc as plsc

D_MODEL = 1024
EPS = 1e-6
POOL_WINDOWS = (2, 4, 8, 16)
POOL_GROUP = 128
D_HALF = 512
CONF_KERNEL = 31
SCONV_KERNEL = 3
CHUNK = 128
GMLP_HEADS = 4
N_EXPERTS = 16
N_GROUPS = 4
EXPERTS_PER_GROUP = 4
TOP_K = 2
D_EXPERT = 512

V7X_LANES = 128
V7X_SUBLANES = 8
V7X_VMEM_LIMIT_BYTES = 56 * 1024 * 1024

MIX_TILE = 512
MOE_TILE = 512
SC_CHUNK = 32
ROW_CHUNKS = D_MODEL // V7X_LANES
CONV_HIST = 32
POOL_HIST = 16
SCONV_HIST = 8

BF16 = jnp.bfloat16
F32 = jnp.float32


def _rms_mod(x, g_row, shift_row, scale_row):
    ms = jnp.mean(x * x, axis=-1, keepdims=True)
    y = x * lax.rsqrt(ms + EPS)
    return (y * g_row) * (1.0 + scale_row) + shift_row


def _layer_norm(x, g_row, b_row):
    mu = jnp.mean(x, axis=-1, keepdims=True)
    xc = x - mu
    var = jnp.mean(xc * xc, axis=-1, keepdims=True)
    return xc * lax.rsqrt(var + EPS) * g_row + b_row


def _sigmoid(x):
    return 1.0 / (1.0 + jnp.exp(-x))


def _silu(x):
    return x * _sigmoid(x)


def _gelu_tanh(x):
    c = 0.7978845608028654
    return 0.5 * x * (1.0 + jnp.tanh(c * (x + 0.044715 * (x * x * x))))


def _load_rows(ref, n_rows):
    return jnp.concatenate(
        [ref[pl.ds(c, n_rows, stride=ROW_CHUNKS), :] for c in range(ROW_CHUNKS)], axis=1)


def _store_rows(ref, val):
    n_rows = val.shape[0]
    for c in range(ROW_CHUNKS):
        ref[pl.ds(c, n_rows, stride=ROW_CHUNKS), :] = val[:, c * V7X_LANES:(c + 1) * V7X_LANES]


def _ada_kernel(ct_ref, w_ref, b_ref, o_ref):
    ct = ct_ref[...]
    cond = _silu(ct)
    w = w_ref[0]
    nb = ct.shape[1]
    for b in range(nb):
        col = cond[:, b:b + 1]
        o_ref[0, b:b + 1, :] = jnp.sum(col * w, axis=0, keepdims=True) + b_ref[0]


def _ada_mod(c, ada_w, ada_b):
    depth, d, six_d = ada_w.shape
    bsz = c.shape[0]
    nb = D_MODEL
    return pl.pallas_call(
        _ada_kernel,
        grid=(depth, six_d // nb),
        in_specs=[
            pl.BlockSpec((d, bsz), lambda l, j: (0, 0)),
            pl.BlockSpec((1, d, nb), lambda l, j: (l, 0, j)),
            pl.BlockSpec((1, 1, nb), lambda l, j: (l, 0, j)),
        ],
        out_specs=pl.BlockSpec((1, bsz, nb), lambda l, j: (l, 0, j)),
        out_shape=jax.ShapeDtypeStruct((depth, bsz, six_d), F32),
        compiler_params=pltpu.CompilerParams(
            dimension_semantics=("arbitrary", "arbitrary"),
            vmem_limit_bytes=V7X_VMEM_LIMIT_BYTES),
        name="ada_mod",
    )(c.T, ada_w, ada_b.reshape(depth, 1, six_d))


def _route(h2_bf, rwt_ref, rbias_ref, eidx_ref, wsel_ref, rank_ref, counts_ref, cnt_ref):
    nt = (((1,), (1,)), ((), ()))
    r = lax.dot_general(rwt_ref[...], h2_bf, nt, preferred_element_type=F32)
    logits = r[:N_EXPERTS] + r[N_EXPERTS:]
    m = jnp.max(logits, axis=0, keepdims=True)
    ex = jnp.exp(logits - m)
    probs = ex / jnp.sum(ex, axis=0, keepdims=True)
    sel = probs + rbias_ref[...]
    s = [sel[e:e + 1] for e in range(N_EXPERTS)]
    p = [probs[e:e + 1] for e in range(N_EXPERTS)]
    best = None
    gi = None
    for g in range(N_GROUPS):
        a, b, c, d = s[4 * g:4 * g + 4]
        hi1, lo1 = jnp.maximum(a, b), jnp.minimum(a, b)
        hi2, lo2 = jnp.maximum(c, d), jnp.minimum(c, d)
        top1 = jnp.maximum(hi1, hi2)
        top2 = jnp.maximum(jnp.minimum(hi1, hi2), jnp.maximum(lo1, lo2))
        score = top1 + top2
        if g == 0:
            best, gi = score, jnp.zeros(score.shape, jnp.int32)
        else:
            upd = score > best
            gi = jnp.where(upd, g, gi)
            best = jnp.where(upd, score, best)
    v, q = [], []
    for j in range(EXPERTS_PER_GROUP):
        vj, qj = s[j], p[j]
        for g in range(1, N_GROUPS):
            pick = gi == g
            vj = jnp.where(pick, s[4 * g + j], vj)
            qj = jnp.where(pick, p[4 * g + j], qj)
        v.append(vj)
        q.append(qj)
    i1 = jnp.zeros(gi.shape, jnp.int32)
    m1 = v[0]
    for j in range(1, EXPERTS_PER_GROUP):
        upd = v[j] > m1
        i1 = jnp.where(upd, j, i1)
        m1 = jnp.where(upd, v[j], m1)
    i2 = jnp.zeros(gi.shape, jnp.int32)
    m2 = jnp.full(m1.shape, -jnp.inf, F32)
    for j in range(EXPERTS_PER_GROUP):
        cand = (i1 != j) & (v[j] > m2)
        i2 = jnp.where(cand, j, i2)
        m2 = jnp.where(cand, v[j], m2)
    pa = q[0]
    pb = q[0]
    for j in range(1, EXPERTS_PER_GROUP):
        pa = jnp.where(i1 == j, q[j], pa)
        pb = jnp.where(i2 == j, q[j], pb)
    tot = pa + pb
    e0 = gi * EXPERTS_PER_GROUP + i1
    e1 = gi * EXPERTS_PER_GROUP + i2
    eidx_ref[0:1, :] = e0
    eidx_ref[1:2, :] = e1
    wsel_ref[0:1, :] = pa / tot
    wsel_ref[1:2, :] = pb / tot

    t = h2_bf.shape[0]
    e_iota = lax.broadcasted_iota(jnp.int32, (N_EXPERTS, t), 0)
    oh0 = e_iota == e0
    oh1 = e_iota == e1
    both = jnp.where(oh0 | oh1, 1.0, 0.0)
    r_i = lax.broadcasted_iota(jnp.int32, (V7X_LANES, V7X_LANES), 0)
    c_i = lax.broadcasted_iota(jnp.int32, (V7X_LANES, V7X_LANES), 1)
    before = jnp.where(r_i < c_i, 1.0, 0.0).astype(BF16)
    run = cnt_ref[...]
    rank0, rank1 = [], []
    for blk in range(t // V7X_LANES):
        lanes = slice(blk * V7X_LANES, (blk + 1) * V7X_LANES)
        b = both[:, lanes]
        pre = jnp.dot(b.astype(BF16), before, preferred_element_type=F32) + run
        rank0.append(jnp.sum(jnp.where(oh0[:, lanes], pre, 0.0), axis=0, keepdims=True))
        rank1.append(jnp.sum(jnp.where(oh1[:, lanes], pre, 0.0), axis=0, keepdims=True))
        run = run + jnp.sum(b, axis=1, keepdims=True)
    cnt_ref[...] = run
    rank_ref[0:1, :] = jnp.concatenate(rank0, axis=1).astype(jnp.int32)
    rank_ref[1:2, :] = jnp.concatenate(rank1, axis=1).astype(jnp.int32)
    counts_ref[...] = jnp.broadcast_to(run, counts_ref.shape).astype(jnp.int32)


def _finish_mixer(x, m, mod, n2g_ref, rwt_ref, rbias_ref,
                  x1_ref, h2_ref, eidx_ref, wsel_ref, rank_ref, counts_ref, cnt_ref):
    @pl.when(pl.program_id(0) == 0)
    def _():
        cnt_ref[...] = jnp.zeros_like(cnt_ref)

    x1 = x + mod[2:3] * m
    x1_ref[...] = x1
    h2 = _rms_mod(x1, n2g_ref[...], mod[3:4], mod[4:5])
    h2_bf = h2.astype(BF16)
    _store_rows(h2_ref, h2_bf.astype(F32))
    _route(h2_bf, rwt_ref, rbias_ref, eidx_ref, wsel_ref, rank_ref, counts_ref, cnt_ref)


def _moe_residual(x_ref, y0_ref, y1_ref, wt_ref, g2_row):
    tm = x_ref.shape[0]
    wt = wt_ref[...]
    y = wt[:, 0:1] * _load_rows(y0_ref, tm) + wt[:, 1:2] * _load_rows(y1_ref, tm)
    return x_ref[...] + g2_row * y


def _mixer_ab_kernel(tiles_per_seq,
                     x_ref, mod_ref, n1g_ref, n2g_ref, win_ref, poolw_ref, pscale_ref,
                     convw_ref, convb_ref, lng_ref, lnb_ref, wout_ref, rwt_ref, rbias_ref,
                     x1_ref, h2_ref, eidx_ref, wsel_ref, rank_ref, counts_ref,
                     pool_ext, conv_ext, cnt_ref):
    i = pl.program_id(0)
    tm = x_ref.shape[0]
    seq_tile = i % tiles_per_seq

    @pl.when(seq_tile == 0)
    def _():
        pool_ext[0:POOL_HIST, :] = jnp.zeros((POOL_HIST, D_HALF), F32)
        conv_ext[0:CONV_HIST, :] = jnp.zeros((CONV_HIST, D_HALF), F32)

    x = x_ref[...]
    mod = mod_ref[0, 0]
    h = _rms_mod(x, n1g_ref[...], mod[0:1], mod[1:2]).astype(BF16)
    z = jnp.dot(h, win_ref[...], preferred_element_type=F32)
    zp = z[:, :D_HALF]
    glu = z[:, D_HALF:2 * D_HALF] * _sigmoid(z[:, 2 * D_HALF:])
    pool_ext[POOL_HIST:POOL_HIST + tm, :] = zp
    conv_ext[CONV_HIST:CONV_HIST + tm, :] = glu

    row = lax.broadcasted_iota(jnp.int32, (tm, 1), 0)
    pos1 = (seq_tile * tm + row + 1).astype(F32)
    pool_out = []
    for g, w in enumerate(POOL_WINDOWS):
        cols = slice(g * POOL_GROUP, (g + 1) * POOL_GROUP)
        u = pool_ext[:, cols]
        acc = u
        span = 1
        while span < w:
            acc = acc + pltpu.roll(acc, span, axis=0)
            span *= 2
        wsum = acc[POOL_HIST:POOL_HIST + tm]
        inv_cnt = 1.0 / jnp.minimum(pos1, float(w))
        diff = wsum * inv_cnt - zp[:, cols]
        po = jnp.dot(diff.astype(BF16), poolw_ref[g], preferred_element_type=F32)
        pool_out.append(po * pscale_ref[:, cols])

    convw = convw_ref[...]
    ext_rows = tm + V7X_SUBLANES
    conv = None
    for r in range(V7X_SUBLANES):
        vr = None
        for a in range(CONV_HIST // V7X_SUBLANES):
            lag = V7X_SUBLANES * a + r
            if lag >= CONF_KERNEL:
                continue
            k = CONF_KERNEL - 1 - lag
            start = CONV_HIST - V7X_SUBLANES - V7X_SUBLANES * a
            term = convw[k:k + 1, :] * conv_ext[start:start + ext_rows, :]
            vr = term if vr is None else vr + term
        if r:
            vr = pltpu.roll(vr, r, axis=0)
        conv = vr if conv is None else conv + vr
    conv = conv[V7X_SUBLANES:V7X_SUBLANES + tm] + convb_ref[...]
    conf = _silu(_layer_norm(conv, lng_ref[...], lnb_ref[...]))

    pool_ext[0:POOL_HIST, :] = zp[tm - POOL_HIST:tm]
    conv_ext[0:CONV_HIST, :] = glu[tm - CONV_HIST:tm]

    m = jnp.dot(conf.astype(BF16), wout_ref[D_HALF:, :], preferred_element_type=F32)
    for g in range(len(POOL_WINDOWS)):
        rows = slice(g * POOL_GROUP, (g + 1) * POOL_GROUP)
        m = m + jnp.dot(pool_out[g].astype(BF16), wout_ref[rows, :], preferred_element_type=F32)
    _finish_mixer(x, m, mod, n2g_ref, rwt_ref, rbias_ref,
                  x1_ref, h2_ref, eidx_ref, wsel_ref, rank_ref, counts_ref, cnt_ref)


def _mixer_cd_kernel(tiles_per_seq,
                     x_ref, y0_ref, y1_ref, wt_ref, modp_ref,
                     mod_ref, n1g_ref, n2g_ref, win_ref, sconvw_ref, lng_ref, lnb_ref,
                     ws_ref, bsf_ref, wout_ref, rwt_ref, rbias_ref,
                     x1_ref, h2_ref, eidx_ref, wsel_ref, rank_ref, counts_ref,
                     sconv_ext, cnt_ref):
    i = pl.program_id(0)
    tm = x_ref.shape[0]
    seq_tile = i % tiles_per_seq

    @pl.when(seq_tile == 0)
    def _():
        sconv_ext[0:SCONV_HIST, :] = jnp.zeros((SCONV_HIST, D_HALF), F32)

    x = _moe_residual(x_ref, y0_ref, y1_ref, wt_ref, modp_ref[0, 0][5:6])
    mod = mod_ref[0, 0]
    h = _rms_mod(x, n1g_ref[...], mod[0:1], mod[1:2]).astype(BF16)
    z = jnp.dot(h, win_ref[...], preferred_element_type=F32)
    bg = z[:, :D_HALF]
    ch = z[:, D_HALF:2 * D_HALF] * z[:, 2 * D_HALF:3 * D_HALF]
    zd = _gelu_tanh(z[:, 3 * D_HALF:])
    u = zd[:, :D_HALF]
    v = _layer_norm(zd[:, D_HALF:], lng_ref[...], lnb_ref[...])

    sconv_ext[SCONV_HIST:SCONV_HIST + tm, :] = ch
    sw = sconvw_ref[...]
    ext = sconv_ext[...]
    conv = sw[2:3, :] * ext
    conv = conv + sw[1:2, :] * pltpu.roll(ext, 1, axis=0)
    conv = conv + sw[0:1, :] * pltpu.roll(ext, 2, axis=0)
    sc_out = bg * conv[SCONV_HIST:SCONV_HIST + tm]
    sconv_ext[0:SCONV_HIST, :] = ch[tm - SCONV_HIST:tm]

    r_i = lax.broadcasted_iota(jnp.int32, (CHUNK, CHUNK), 0)
    c_i = lax.broadcasted_iota(jnp.int32, (CHUNK, CHUNK), 1)
    tril = c_i <= r_i
    wm = [jnp.where(tril, ws_ref[hd], 0.0).astype(BF16) for hd in range(GMLP_HEADS)]
    v_bf = v.astype(BF16)
    bsf = bsf_ref[...]
    gm_rows = []
    for n in range(tm // CHUNK):
        rows = slice(n * CHUNK, (n + 1) * CHUNK)
        heads = []
        for hd in range(GMLP_HEADS):
            cols = slice(hd * POOL_GROUP, (hd + 1) * POOL_GROUP)
            heads.append(jnp.dot(wm[hd], v_bf[rows, cols], preferred_element_type=F32))
        mixed = jnp.concatenate(heads, axis=1) + bsf
        gm_rows.append(u[rows] * mixed)
    gm_out = jnp.concatenate(gm_rows, axis=0)

    m = jnp.dot(sc_out.astype(BF16), wout_ref[:D_HALF, :], preferred_element_type=F32)
    m = m + jnp.dot(gm_out.astype(BF16), wout_ref[D_HALF:, :], preferred_element_type=F32)
    _finish_mixer(x, m, mod, n2g_ref, rwt_ref, rbias_ref,
                  x1_ref, h2_ref, eidx_ref, wsel_ref, rank_ref, counts_ref, cnt_ref)


def _const_spec(shape):
    nd = len(shape)
    return pl.BlockSpec(shape, lambda i: (0,) * nd)


def _mixer_call(kernel_fn, layer, stream_inputs, stream_specs, mod4, n1g, n2g, weights, rwt, rbias,
                scratch, seq_len, name):
    n_tok = mod4.shape[1] * seq_len
    d = D_MODEL
    tm = MIX_TILE
    tiles_per_seq = seq_len // tm
    in_specs = stream_specs + [
        pl.BlockSpec((1, 1, 6, d), lambda i: (layer, i // tiles_per_seq, 0, 0)),
        _const_spec(n1g.shape),
        _const_spec(n2g.shape),
    ] + [_const_spec(w.shape) for w in weights] + [_const_spec(rwt.shape), _const_spec(rbias.shape)]
    out_specs = [
        pl.BlockSpec((tm, d), lambda i: (i, 0)),
        pl.BlockSpec((tm * ROW_CHUNKS, V7X_LANES), lambda i: (i, 0)),
        pl.BlockSpec((TOP_K, tm), lambda i: (0, i)),
        pl.BlockSpec((TOP_K, tm), lambda i: (0, i)),
        pl.BlockSpec((TOP_K, tm), lambda i: (0, i)),
        pl.BlockSpec((N_EXPERTS, V7X_LANES), lambda i: (0, 0)),
    ]
    out_shape = [
        jax.ShapeDtypeStruct((n_tok, d), F32),
        jax.ShapeDtypeStruct((n_tok * ROW_CHUNKS, V7X_LANES), F32),
        jax.ShapeDtypeStruct((TOP_K, n_tok), jnp.int32),
        jax.ShapeDtypeStruct((TOP_K, n_tok), F32),
        jax.ShapeDtypeStruct((TOP_K, n_tok), jnp.int32),
        jax.ShapeDtypeStruct((N_EXPERTS, V7X_LANES), jnp.int32),
    ]
    return pl.pallas_call(
        functools.partial(kernel_fn, tiles_per_seq),
        grid=(n_tok // tm,),
        in_specs=in_specs,
        out_specs=out_specs,
        out_shape=out_shape,
        scratch_shapes=scratch + [pltpu.VMEM((N_EXPERTS, 1), F32)],
        compiler_params=pltpu.CompilerParams(
            dimension_semantics=("arbitrary",),
            vmem_limit_bytes=V7X_VMEM_LIMIT_BYTES),
        name=name,
    )(*stream_inputs, mod4, n1g, n2g, *weights, rwt, rbias)


def _combine_specs(n_tok):
    tm = MIX_TILE
    n_tiles = n_tok // tm
    return [
        pl.BlockSpec((tm, D_MODEL), lambda i: (i, 0)),
        pl.BlockSpec((tm * ROW_CHUNKS, V7X_LANES), lambda i: (i, 0)),
        pl.BlockSpec((tm * ROW_CHUNKS, V7X_LANES), lambda i: (n_tiles + i, 0)),
        pl.BlockSpec((tm, TOP_K), lambda i: (i, 0)),
    ]


def _sorted_positions(eidx, rank, counts):
    n_pairs = eidx.shape[0] * eidx.shape[1]
    n_rows = n_pairs + N_EXPERTS * MOE_TILE
    n_tiles = n_rows // MOE_TILE
    padded = (counts + MOE_TILE - 1) // MOE_TILE * MOE_TILE
    seg_end = jnp.cumsum(padded)
    seg_start = seg_end - padded
    expert_ids = jnp.arange(N_EXPERTS, dtype=jnp.int32)
    pos = jnp.sum(jnp.where(eidx[..., None] == expert_ids, seg_start, 0), axis=-1) + rank
    tile_row0 = jnp.arange(n_tiles, dtype=jnp.int32) * MOE_TILE
    tile_expert = jnp.minimum(
        jnp.sum((tile_row0[:, None] >= seg_end[None, :]).astype(jnp.int32), axis=1), N_EXPERTS - 1)
    n_used = (seg_end[N_EXPERTS - 1] // MOE_TILE).reshape(1)
    valid_end = seg_start + counts
    tile_valid_end = jnp.sum(
        jnp.where(tile_expert[:, None] == expert_ids, valid_end, 0), axis=-1)
    tile_valid = jnp.clip(tile_valid_end - tile_row0, 0, MOE_TILE)
    return (pos.reshape(n_pairs).astype(jnp.int32), tile_expert.astype(jnp.int32),
            tile_valid.astype(jnp.int32), n_used.astype(jnp.int32), n_rows)


def _sc_move_rows(scatter, src, pos_flat, n_out_rows, name):
    info = plsc.get_sparse_core_info()
    n_workers = info.num_cores * info.num_subcores
    n_pairs = pos_flat.shape[0]
    n_src = src.shape[0]
    per_worker = n_pairs // n_workers
    n_chunks = per_worker // SC_CHUNK
    assert per_worker * n_workers == n_pairs and n_chunks * SC_CHUNK == per_worker
    assert n_src % per_worker == 0
    idx = pos_flat.reshape(n_workers, n_chunks, SC_CHUNK)
    mesh = plsc.VectorSubcoreMesh(core_axis_name="core", subcore_axis_name="subcore")

    @functools.partial(
        pl.kernel,
        out_type=jax.ShapeDtypeStruct((n_out_rows, ROW_CHUNKS, V7X_LANES), F32),
        mesh=mesh,
        scratch_types=[
            pltpu.VMEM((n_chunks, SC_CHUNK), jnp.int32),
            pltpu.VMEM((2, SC_CHUNK, ROW_CHUNKS, V7X_LANES), F32),
            pltpu.SemaphoreType.DMA((2,)),
            pltpu.SemaphoreType.DMA((2,)),
        ],
        name=name)
    def move(src_hbm, i_hbm, o_hbm, idx_v, buf, in_sem, out_sem):
        wid = lax.axis_index("subcore") * info.num_cores + lax.axis_index("core")
        base = wid * per_worker
        src_base = lax.rem(base, n_src)
        pltpu.sync_copy(i_hbm.at[wid], idx_v)

        def fetch(s, slot):
            if scatter:
                rows = src_hbm.at[pl.ds(src_base + s * SC_CHUNK, SC_CHUNK)]
            else:
                rows = src_hbm.at[idx_v.at[s]]
            return pltpu.make_async_copy(rows, buf.at[slot], in_sem.at[slot])

        def flush(s, slot):
            if scatter:
                rows = o_hbm.at[idx_v.at[s]]
            else:
                rows = o_hbm.at[pl.ds(base + s * SC_CHUNK, SC_CHUNK)]
            return pltpu.make_async_copy(buf.at[slot], rows, out_sem.at[slot])

        fetch(0, 0).start()
        for s in range(n_chunks):
            slot = s % 2
            fetch(s, slot).wait()
            flush(s, slot).start()
            if s + 1 < n_chunks:
                if s >= 1:
                    flush(s - 1, 1 - slot).wait()
                fetch(s + 1, 1 - slot).start()
        flush(n_chunks - 2, n_chunks % 2).wait()
        flush(n_chunks - 1, (n_chunks - 1) % 2).wait()

    return move(src, idx)


def _experts_kernel(te_ref, tv_ref, nu_ref, x_ref, wg_ref, wu_ref, wd_ref, y_ref,
                    wg_bf, wu_bf, wd_bf):
    j = pl.program_id(0)
    tm = x_ref.shape[0] // ROW_CHUNKS

    @pl.when((j == 0) | (te_ref[j] != te_ref[jnp.maximum(j - 1, 0)]))
    def _():
        wg_bf[...] = wg_ref[0, 0].astype(BF16)
        wu_bf[...] = wu_ref[0, 0].astype(BF16)
        wd_bf[...] = wd_ref[0, 0].astype(BF16)

    @pl.when(j < nu_ref[0])
    def _():
        row = lax.broadcasted_iota(jnp.int32, (tm, 1), 0)
        h = jnp.where(row < tv_ref[j], _load_rows(x_ref, tm), 0.0).astype(BF16)
        a = jnp.dot(h, wg_bf[...], preferred_element_type=F32)
        b = jnp.dot(h, wu_bf[...], preferred_element_type=F32)
        t = (_silu(a) * b).astype(BF16)
        _store_rows(y_ref, jnp.dot(t, wd_bf[...], preferred_element_type=F32))

    @pl.when(j >= nu_ref[0])
    def _():
        y_ref[...] = jnp.zeros_like(y_ref)


def _experts_call(layer, rows, tile_expert, tile_valid, n_used, w_gate, w_up, w_down):
    n_rows = rows.shape[0]
    tm = MOE_TILE
    d = D_MODEL
    rows2 = rows.reshape(n_rows * ROW_CHUNKS, V7X_LANES)
    grid_spec = pltpu.PrefetchScalarGridSpec(
        num_scalar_prefetch=3,
        grid=(n_rows // tm,),
        in_specs=[
            pl.BlockSpec((tm * ROW_CHUNKS, V7X_LANES),
                         lambda j, te, tv, nu: (jnp.minimum(j, nu[0] - 1), 0)),
            pl.BlockSpec((1, 1, d, D_EXPERT), lambda j, te, tv, nu: (layer, te[j], 0, 0)),
            pl.BlockSpec((1, 1, d, D_EXPERT), lambda j, te, tv, nu: (layer, te[j], 0, 0)),
            pl.BlockSpec((1, 1, D_EXPERT, d), lambda j, te, tv, nu: (layer, te[j], 0, 0)),
        ],
        out_specs=pl.BlockSpec((tm * ROW_CHUNKS, V7X_LANES), lambda j, te, tv, nu: (j, 0)),
        scratch_shapes=[
            pltpu.VMEM((d, D_EXPERT), BF16),
            pltpu.VMEM((d, D_EXPERT), BF16),
            pltpu.VMEM((D_EXPERT, d), BF16),
        ],
    )
    y = pl.pallas_call(
        _experts_kernel,
        grid_spec=grid_spec,
        out_shape=jax.ShapeDtypeStruct((n_rows * ROW_CHUNKS, V7X_LANES), F32),
        compiler_params=pltpu.CompilerParams(
            dimension_semantics=("arbitrary",),
            vmem_limit_bytes=V7X_VMEM_LIMIT_BYTES),
        name="experts_l%d" % layer,
    )(tile_expert, tile_valid, n_used, rows2, w_gate, w_up, w_down)
    return y.reshape(n_rows, ROW_CHUNKS, V7X_LANES)


def _moe_rows(layer, h_rows, eidx, rank, counts, w_gate, w_up, w_down):
    n_tok = eidx.shape[1]
    pos_flat, tile_expert, tile_valid, n_used, n_rows = _sorted_positions(eidx, rank, counts[:, 0])
    sorted_rows = _sc_move_rows(
        True, h_rows.reshape(n_tok, ROW_CHUNKS, V7X_LANES), pos_flat, n_rows, "sc_scatter_rows")
    y_sorted = _experts_call(layer, sorted_rows, tile_expert, tile_valid, n_used,
                             w_gate, w_up, w_down)
    y_pairs = _sc_move_rows(False, y_sorted, pos_flat, TOP_K * n_tok, "sc_gather_rows")
    return y_pairs.reshape(TOP_K * n_tok * ROW_CHUNKS, V7X_LANES)


def _final_kernel(x_ref, y0_ref, y1_ref, wt_ref, modp_ref, fg_ref, o_ref):
    y = _moe_residual(x_ref, y0_ref, y1_ref, wt_ref, modp_ref[0, 0][5:6])
    ms = jnp.mean(y * y, axis=-1, keepdims=True)
    o_ref[...] = y * lax.rsqrt(ms + EPS) * fg_ref[...]


def _final_call(layer, x, y_pairs, wt, mod4, fg, seq_len):
    n_tok, d = x.shape
    tm = MIX_TILE
    tiles_per_seq = seq_len // tm
    return pl.pallas_call(
        _final_kernel,
        grid=(n_tok // tm,),
        in_specs=_combine_specs(n_tok) + [
            pl.BlockSpec((1, 1, 6, d), lambda i: (layer, i // tiles_per_seq, 0, 0)),
            _const_spec(fg.shape),
        ],
        out_specs=pl.BlockSpec((tm, d), lambda i: (i, 0)),
        out_shape=jax.ShapeDtypeStruct((n_tok, d), F32),
        compiler_params=pltpu.CompilerParams(
            dimension_semantics=("arbitrary",),
            vmem_limit_bytes=V7X_VMEM_LIMIT_BYTES),
        name="final_norm",
    )(x, y_pairs, y_pairs, wt, mod4, fg)


def kernel(x, c, norm1_g, norm2_g, ada_w, ada_b, ab_w_in, pool_w, pool_scale, conf_conv_w, conf_conv_b, conf_ln_g, conf_ln_b, ab_w_out, cd_w_in, sconv_w, gmlp_ln_g, gmlp_ln_b, gmlp_ws, gmlp_bs, cd_w_out, router_w, router_bias, exp_w_gate, exp_w_up, exp_w_down, final_g):
    bsz, seq_len, d = x.shape
    n_tok = bsz * seq_len
    tm = MIX_TILE
    tiles_per_seq = seq_len // tm
    xf = x.reshape(n_tok, d)

    mod = _ada_mod(c, ada_w, ada_b)
    mod4 = mod.reshape(mod.shape[0], bsz, 6, d)

    rw_hi = router_w.astype(BF16)
    rw_lo = (router_w - rw_hi.astype(F32)).astype(BF16)
    rwt = jnp.concatenate([rw_hi.T, rw_lo.T], axis=0)
    rbias = router_bias.reshape(N_EXPERTS, 1)
    fg = final_g.reshape(1, d)

    weights_ab = [
        ab_w_in[0].astype(BF16), pool_w[0].astype(BF16), pool_scale[0].reshape(1, D_HALF),
        conf_conv_w[0], conf_conv_b[0].reshape(1, D_HALF), conf_ln_g[0].reshape(1, D_HALF),
        conf_ln_b[0].reshape(1, D_HALF), ab_w_out[0].astype(BF16),
    ]
    scratch_ab = [pltpu.VMEM((POOL_HIST + tm, D_HALF), F32),
                  pltpu.VMEM((CONV_HIST + tm, D_HALF), F32)]
    x1, h2, eidx0, wsel0, rank0, counts0 = _mixer_call(
        _mixer_ab_kernel, 0, [xf], [pl.BlockSpec((tm, d), lambda i: (i, 0))],
        mod4, norm1_g[0:1], norm2_g[0:1], weights_ab, rwt, rbias, scratch_ab, seq_len, "mixer_ab")
    y_pairs0 = _moe_rows(0, h2, eidx0, rank0, counts0, exp_w_gate, exp_w_up, exp_w_down)

    bsf = jnp.repeat(gmlp_bs[0].T, POOL_GROUP, axis=1)
    weights_cd = [
        cd_w_in[0].astype(BF16), sconv_w[0], gmlp_ln_g[0].reshape(1, D_HALF),
        gmlp_ln_b[0].reshape(1, D_HALF), gmlp_ws[0], bsf, cd_w_out[0].astype(BF16),
    ]
    scratch_cd = [pltpu.VMEM((SCONV_HIST + tm, D_HALF), F32)]
    prev_mod_spec = pl.BlockSpec((1, 1, 6, d), lambda i: (0, i // tiles_per_seq, 0, 0))
    x3, h4, eidx1, wsel1, rank1, counts1 = _mixer_call(
        _mixer_cd_kernel, 1, [x1, y_pairs0, y_pairs0, wsel0.T, mod4],
        _combine_specs(n_tok) + [prev_mod_spec],
        mod4, norm1_g[1:2], norm2_g[1:2], weights_cd, rwt, rbias, scratch_cd, seq_len, "mixer_cd")
    y_pairs1 = _moe_rows(1, h4, eidx1, rank1, counts1, exp_w_gate, exp_w_up, exp_w_down)

    out = _final_call(1, x3, y_pairs1, wsel1.T, mod4, fg, seq_len)
    return out.reshape(bsz, seq_len, d)
```

```python
import functools

import jax
import jax.numpy as jnp
from jax import lax
from jax.experimental import pallas as pl
from jax.experimental.pallas import tpu as pltpu
from jax.experimental.pallas import tpu_sc as plsc

D_MODEL = 1024
EPS = 1e-6
POOL_WINDOWS = (2, 4, 8, 16)
POOL_GROUP = 128
D_HALF = 512
CONF_KERNEL = 31
SCONV_KERNEL = 3
CHUNK = 128
GMLP_HEADS = 4
N_EXPERTS = 16
N_GROUPS = 4
EXPERTS_PER_GROUP = 4
TOP_K = 2
D_EXPERT = 512

V7X_LANES = 128
V7X_SUBLANES = 8
V7X_VMEM_LIMIT_BYTES = 56 * 1024 * 1024

MIX_TILE = 512
MOE_TILE = 512
SC_CHUNK = 32
ROW_CHUNKS = D_MODEL // V7X_LANES
CONV_HIST = 32
POOL_HIST = 16
SCONV_HIST = 8

BF16 = jnp.bfloat16
F32 = jnp.float32


def _rms_mod(x, g_row, shift_row, scale_row):
    ms = jnp.mean(x * x, axis=-1, keepdims=True)
    y = x * lax.rsqrt(ms + EPS)
    return (y * g_row) * (1.0 + scale_row) + shift_row


def _layer_norm(x, g_row, b_row):
    mu = jnp.mean(x, axis=-1, keepdims=True)
    xc = x - mu
    var = jnp.mean(xc * xc, axis=-1, keepdims=True)
    return xc * lax.rsqrt(var + EPS) * g_row + b_row


def _sigmoid(x):
    return 1.0 / (1.0 + jnp.exp(-x))


def _silu(x):
    return x * _sigmoid(x)


def _gelu_tanh(x):
    c = 0.7978845608028654
    return 0.5 * x * (1.0 + jnp.tanh(c * (x + 0.044715 * (x * x * x))))


def _load_rows(ref, n_rows):
    return jnp.concatenate(
        [ref[pl.ds(c, n_rows, stride=ROW_CHUNKS), :] for c in range(ROW_CHUNKS)], axis=1)


def _store_rows(ref, val):
    n_rows = val.shape[0]
    for c in range(ROW_CHUNKS):
        ref[pl.ds(c, n_rows, stride=ROW_CHUNKS), :] = val[:, c * V7X_LANES:(c + 1) * V7X_LANES]


def _ada_kernel(ct_ref, w_ref, b_ref, o_ref):
    ct = ct_ref[...]
    cond = _silu(ct)
    w = w_ref[0]
    nb = ct.shape[1]
    for b in range(nb):
        col = cond[:, b:b + 1]
        o_ref[0, b:b + 1, :] = jnp.sum(col * w, axis=0, keepdims=True) + b_ref[0]


def _ada_mod(c, ada_w, ada_b):
    depth, d, six_d = ada_w.shape
    bsz = c.shape[0]
    nb = D_MODEL
    return pl.pallas_call(
        _ada_kernel,
        grid=(depth, six_d // nb),
        in_specs=[
            pl.BlockSpec((d, bsz), lambda l, j: (0, 0)),
            pl.BlockSpec((1, d, nb), lambda l, j: (l, 0, j)),
            pl.BlockSpec((1, 1, nb), lambda l, j: (l, 0, j)),
        ],
        out_specs=pl.BlockSpec((1, bsz, nb), lambda l, j: (l, 0, j)),
        out_shape=jax.ShapeDtypeStruct((depth, bsz, six_d), F32),
        compiler_params=pltpu.CompilerParams(
            dimension_semantics=("arbitrary", "arbitrary"),
            vmem_limit_bytes=V7X_VMEM_LIMIT_BYTES),
        name="ada_mod",
    )(c.T, ada_w, ada_b.reshape(depth, 1, six_d))


def _route(h2_bf, rwt_ref, rbias_ref, eidx_ref, wsel_ref, rank_ref, counts_ref, cnt_ref):
    nt = (((1,), (1,)), ((), ()))
    r = lax.dot_general(rwt_ref[...], h2_bf, nt, preferred_element_type=F32)
    logits = r[:N_EXPERTS] + r[N_EXPERTS:]
    m = jnp.max(logits, axis=0, keepdims=True)
    ex = jnp.exp(logits - m)
    probs = ex / jnp.sum(ex, axis=0, keepdims=True)
    sel = probs + rbias_ref[...]
    s = [sel[e:e + 1] for e in range(N_EXPERTS)]
    p = [probs[e:e + 1] for e in range(N_EXPERTS)]
    best = None
    gi = None
    for g in range(N_GROUPS):
        a, b, c, d = s[4 * g:4 * g + 4]
        hi1, lo1 = jnp.maximum(a, b), jnp.minimum(a, b)
        hi2, lo2 = jnp.maximum(c, d), jnp.minimum(c, d)
        top1 = jnp.maximum(hi1, hi2)
        top2 = jnp.maximum(jnp.minimum(hi1, hi2), jnp.maximum(lo1, lo2))
        score = top1 + top2
        if g == 0:
            best, gi = score, jnp.zeros(score.shape, jnp.int32)
        else:
            upd = score > best
            gi = jnp.where(upd, g, gi)
            best = jnp.where(upd, score, best)
    v, q = [], []
    for j in range(EXPERTS_PER_GROUP):
        vj, qj = s[j], p[j]
        for g in range(1, N_GROUPS):
            pick = gi == g
            vj = jnp.where(pick, s[4 * g + j], vj)
            qj = jnp.where(pick, p[4 * g + j], qj)
        v.append(vj)
        q.append(qj)
    i1 = jnp.zeros(gi.shape, jnp.int32)
    m1 = v[0]
    for j in range(1, EXPERTS_PER_GROUP):
        upd = v[j] > m1
        i1 = jnp.where(upd, j, i1)
        m1 = jnp.where(upd, v[j], m1)
    i2 = jnp.zeros(gi.shape, jnp.int32)
    m2 = jnp.full(m1.shape, -jnp.inf, F32)
    for j in range(EXPERTS_PER_GROUP):
        cand = (i1 != j) & (v[j] > m2)
        i2 = jnp.where(cand, j, i2)
        m2 = jnp.where(cand, v[j], m2)
    pa = q[0]
    pb = q[0]
    for j in range(1, EXPERTS_PER_GROUP):
        pa = jnp.where(i1 == j, q[j], pa)
        pb = jnp.where(i2 == j, q[j], pb)
    tot = pa + pb
    e0 = gi * EXPERTS_PER_GROUP + i1
    e1 = gi * EXPERTS_PER_GROUP + i2
    eidx_ref[0:1, :] = e0
    eidx_ref[1:2, :] = e1
    wsel_ref[0:1, :] = pa / tot
    wsel_ref[1:2, :] = pb / tot

    t = h2_bf.shape[0]
    e_iota = lax.broadcasted_iota(jnp.int32, (N_EXPERTS, t), 0)
    oh0 = e_iota == e0
    oh1 = e_iota == e1
    both = jnp.where(oh0 | oh1, 1.0, 0.0)
    r_i = lax.broadcasted_iota(jnp.int32, (V7X_LANES, V7X_LANES), 0)
    c_i = lax.broadcasted_iota(jnp.int32, (V7X_LANES, V7X_LANES), 1)
    before = jnp.where(r_i < c_i, 1.0, 0.0).astype(BF16)
    run = cnt_ref[...]
    rank0, rank1 = [], []
    for blk in range(t // V7X_LANES):
        lanes = slice(blk * V7X_LANES, (blk + 1) * V7X_LANES)
        b = both[:, lanes]
        pre = jnp.dot(b.astype(BF16), before, preferred_element_type=F32) + run
        rank0.append(jnp.sum(jnp.where(oh0[:, lanes], pre, 0.0), axis=0, keepdims=True))
        rank1.append(jnp.sum(jnp.where(oh1[:, lanes], pre, 0.0), axis=0, keepdims=True))
        run = run + jnp.sum(b, axis=1, keepdims=True)
    cnt_ref[...] = run
    rank_ref[0:1, :] = jnp.concatenate(rank0, axis=1).astype(jnp.int32)
    rank_ref[1:2, :] = jnp.concatenate(rank1, axis=1).astype(jnp.int32)
    counts_ref[...] = jnp.broadcast_to(run, counts_ref.shape).astype(jnp.int32)


def _finish_mixer(x, m, mod, n2g_ref, rwt_ref, rbias_ref,
                  x1_ref, h2_ref, eidx_ref, wsel_ref, rank_ref, counts_ref, cnt_ref):
    @pl.when(pl.program_id(0) == 0)
    def _():
        cnt_ref[...] = jnp.zeros_like(cnt_ref)

    x1 = x + mod[2:3] * m
    x1_ref[...] = x1
    h2 = _rms_mod(x1, n2g_ref[...], mod[3:4], mod[4:5])
    h2_bf = h2.astype(BF16)
    _store_rows(h2_ref, h2_bf.astype(F32))
    _route(h2_bf, rwt_ref, rbias_ref, eidx_ref, wsel_ref, rank_ref, counts_ref, cnt_ref)


def _moe_residual(x_ref, y0_ref, y1_ref, wt_ref, g2_row):
    tm = x_ref.shape[0]
    wt = wt_ref[...]
    y = wt[:, 0:1] * _load_rows(y0_ref, tm) + wt[:, 1:2] * _load_rows(y1_ref, tm)
    return x_ref[...] + g2_row * y


def _mixer_ab_kernel(tiles_per_seq,
                     x_ref, mod_ref, n1g_ref, n2g_ref, win_ref, poolw_ref, pscale_ref,
                     convw_ref, convb_ref, lng_ref, lnb_ref, wout_ref, rwt_ref, rbias_ref,
                     x1_ref, h2_ref, eidx_ref, wsel_ref, rank_ref, counts_ref,
                     pool_ext, conv_ext, cnt_ref):
    i = pl.program_id(0)
    tm = x_ref.shape[0]
    seq_tile = i % tiles_per_seq

    @pl.when(seq_tile == 0)
    def _():
        pool_ext[0:POOL_HIST, :] = jnp.zeros((POOL_HIST, D_HALF), F32)
        conv_ext[0:CONV_HIST, :] = jnp.zeros((CONV_HIST, D_HALF), F32)

    x = x_ref[...]
    mod = mod_ref[0, 0]
    h = _rms_mod(x, n1g_ref[...], mod[0:1], mod[1:2]).astype(BF16)
    z = jnp.dot(h, win_ref[...], preferred_element_type=F32)
    zp = z[:, :D_HALF]
    glu = z[:, D_HALF:2 * D_HALF] * _sigmoid(z[:, 2 * D_HALF:])
    pool_ext[POOL_HIST:POOL_HIST + tm, :] = zp
    conv_ext[CONV_HIST:CONV_HIST + tm, :] = glu

    row = lax.broadcasted_iota(jnp.int32, (tm, 1), 0)
    pos1 = (seq_tile * tm + row + 1).astype(F32)
    pool_out = []
    for g, w in enumerate(POOL_WINDOWS):
        cols = slice(g * POOL_GROUP, (g + 1) * POOL_GROUP)
        u = pool_ext[:, cols]
        acc = u
        span = 1
        while span < w:
            acc = acc + pltpu.roll(acc, span, axis=0)
            span *= 2
        wsum = acc[POOL_HIST:POOL_HIST + tm]
        inv_cnt = 1.0 / jnp.minimum(pos1, float(w))
        diff = wsum * inv_cnt - zp[:, cols]
        po = jnp.dot(diff.astype(BF16), poolw_ref[g], preferred_element_type=F32)
        pool_out.append(po * pscale_ref[:, cols])

    convw = convw_ref[...]
    ext_rows = tm + V7X_SUBLANES
    conv = None
    for r in range(V7X_SUBLANES):
        vr = None
        for a in range(CONV_HIST // V7X_SUBLANES):
            lag = V7X_SUBLANES * a + r
            if lag >= CONF_KERNEL:
                continue
            k = CONF_KERNEL - 1 - lag
            start = CONV_HIST - V7X_SUBLANES - V7X_SUBLANES * a
            term = convw[k:k + 1, :] * conv_ext[start:start + ext_rows, :]
            vr = term if vr is None else vr + term
        if r:
            vr = pltpu.roll(vr, r, axis=0)
        conv = vr if conv is None else conv + vr
    conv = conv[V7X_SUBLANES:V7X_SUBLANES + tm] + convb_ref[...]
    conf = _silu(_layer_norm(conv, lng_ref[...], lnb_ref[...]))

    pool_ext[0:POOL_HIST, :] = zp[tm - POOL_HIST:tm]
    conv_ext[0:CONV_HIST, :] = glu[tm - CONV_HIST:tm]

    m = jnp.dot(conf.astype(BF16), wout_ref[D_HALF:, :], preferred_element_type=F32)
    for g in range(len(POOL_WINDOWS)):
        rows = slice(g * POOL_GROUP, (g + 1) * POOL_GROUP)
        m = m + jnp.dot(pool_out[g].astype(BF16), wout_ref[rows, :], preferred_element_type=F32)
    _finish_mixer(x, m, mod, n2g_ref, rwt_ref, rbias_ref,
                  x1_ref, h2_ref, eidx_ref, wsel_ref, rank_ref, counts_ref, cnt_ref)


def _mixer_cd_kernel(tiles_per_seq,
                     x_ref, y0_ref, y1_ref, wt_ref, modp_ref,
                     mod_ref, n1g_ref, n2g_ref, win_ref, sconvw_ref, lng_ref, lnb_ref,
                     ws_ref, bsf_ref, wout_ref, rwt_ref, rbias_ref,
                     x1_ref, h2_ref, eidx_ref, wsel_ref, rank_ref, counts_ref,
                     sconv_ext, cnt_ref):
    i = pl.program_id(0)
    tm = x_ref.shape[0]
    seq_tile = i % tiles_per_seq

    @pl.when(seq_tile == 0)
    def _():
        sconv_ext[0:SCONV_HIST, :] = jnp.zeros((SCONV_HIST, D_HALF), F32)

    x = _moe_residual(x_ref, y0_ref, y1_ref, wt_ref, modp_ref[0, 0][5:6])
    mod = mod_ref[0, 0]
    h = _rms_mod(x, n1g_ref[...], mod[0:1], mod[1:2]).astype(BF16)
    z = jnp.dot(h, win_ref[...], preferred_element_type=F32)
    bg = z[:, :D_HALF]
    ch = z[:, D_HALF:2 * D_HALF] * z[:, 2 * D_HALF:3 * D_HALF]
    zd = _gelu_tanh(z[:, 3 * D_HALF:])
    u = zd[:, :D_HALF]
    v = _layer_norm(zd[:, D_HALF:], lng_ref[...], lnb_ref[...])

    sconv_ext[SCONV_HIST:SCONV_HIST + tm, :] = ch
    sw = sconvw_ref[...]
    ext = sconv_ext[...]
    conv = sw[2:3, :] * ext
    conv = conv + sw[1:2, :] * pltpu.roll(ext, 1, axis=0)
    conv = conv + sw[0:1, :] * pltpu.roll(ext, 2, axis=0)
    sc_out = bg * conv[SCONV_HIST:SCONV_HIST + tm]
    sconv_ext[0:SCONV_HIST, :] = ch[tm - SCONV_HIST:tm]

    r_i = lax.broadcasted_iota(jnp.int32, (CHUNK, CHUNK), 0)
    c_i = lax.broadcasted_iota(jnp.int32, (CHUNK, CHUNK), 1)
    tril = c_i <= r_i
    wm = [jnp.where(tril, ws_ref[hd], 0.0).astype(BF16) for hd in range(GMLP_HEADS)]
    v_bf = v.astype(BF16)
    bsf = bsf_ref[...]
    gm_rows = []
    for n in range(tm // CHUNK):
        rows = slice(n * CHUNK, (n + 1) * CHUNK)
        heads = []
        for hd in range(GMLP_HEADS):
            cols = slice(hd * POOL_GROUP, (hd + 1) * POOL_GROUP)
            heads.append(jnp.dot(wm[hd], v_bf[rows, cols], preferred_element_type=F32))
        mixed = jnp.concatenate(heads, axis=1) + bsf
        gm_rows.append(u[rows] * mixed)
    gm_out = jnp.concatenate(gm_rows, axis=0)

    m = jnp.dot(sc_out.astype(BF16), wout_ref[:D_HALF, :], preferred_element_type=F32)
    m = m + jnp.dot(gm_out.astype(BF16), wout_ref[D_HALF:, :], preferred_element_type=F32)
    _finish_mixer(x, m, mod, n2g_ref, rwt_ref, rbias_ref,
                  x1_ref, h2_ref, eidx_ref, wsel_ref, rank_ref, counts_ref, cnt_ref)


def _const_spec(shape):
    nd = len(shape)
    return pl.BlockSpec(shape, lambda i: (0,) * nd)


def _mixer_call(kernel_fn, layer, batch, stream_inputs, stream_specs, mod4, n1g, n2g, weights,
                rwt, rbias, scratch, seq_len, name):
    n_tok = seq_len
    d = D_MODEL
    tm = MIX_TILE
    tiles_per_seq = seq_len // tm
    in_specs = stream_specs + [
        pl.BlockSpec((1, 1, 6, d), lambda i: (layer, batch, 0, 0)),
        _const_spec(n1g.shape),
        _const_spec(n2g.shape),
    ] + [_const_spec(w.shape) for w in weights] + [_const_spec(rwt.shape), _const_spec(rbias.shape)]
    out_specs = [
        pl.BlockSpec((tm, d), lambda i: (i, 0)),
        pl.BlockSpec((tm * ROW_CHUNKS, V7X_LANES), lambda i: (i, 0)),
        pl.BlockSpec((TOP_K, tm), lambda i: (0, i)),
        pl.BlockSpec((TOP_K, tm), lambda i: (0, i)),
        pl.BlockSpec((TOP_K, tm), lambda i: (0, i)),
        pl.BlockSpec((N_EXPERTS, V7X_LANES), lambda i: (0, 0)),
    ]
    out_shape = [
        jax.ShapeDtypeStruct((n_tok, d), F32),
        jax.ShapeDtypeStruct((n_tok * ROW_CHUNKS, V7X_LANES), F32),
        jax.ShapeDtypeStruct((TOP_K, n_tok), jnp.int32),
        jax.ShapeDtypeStruct((TOP_K, n_tok), F32),
        jax.ShapeDtypeStruct((TOP_K, n_tok), jnp.int32),
        jax.ShapeDtypeStruct((N_EXPERTS, V7X_LANES), jnp.int32),
    ]
    return pl.pallas_call(
        functools.partial(kernel_fn, tiles_per_seq),
        grid=(n_tok // tm,),
        in_specs=in_specs,
        out_specs=out_specs,
        out_shape=out_shape,
        scratch_shapes=scratch + [pltpu.VMEM((N_EXPERTS, 1), F32)],
        compiler_params=pltpu.CompilerParams(
            dimension_semantics=("arbitrary",),
            vmem_limit_bytes=V7X_VMEM_LIMIT_BYTES),
        name=name,
    )(*stream_inputs, mod4, n1g, n2g, *weights, rwt, rbias)


def _combine_specs(n_tok):
    tm = MIX_TILE
    n_tiles = n_tok // tm
    return [
        pl.BlockSpec((tm, D_MODEL), lambda i: (i, 0)),
        pl.BlockSpec((tm * ROW_CHUNKS, V7X_LANES), lambda i: (i, 0)),
        pl.BlockSpec((tm * ROW_CHUNKS, V7X_LANES), lambda i: (n_tiles + i, 0)),
        pl.BlockSpec((tm, TOP_K), lambda i: (i, 0)),
    ]


def _sorted_positions(eidx, rank, counts):
    n_pairs = eidx.shape[0] * eidx.shape[1]
    n_rows = n_pairs + N_EXPERTS * MOE_TILE
    n_tiles = n_rows // MOE_TILE
    padded = (counts + MOE_TILE - 1) // MOE_TILE * MOE_TILE
    seg_end = jnp.cumsum(padded)
    seg_start = seg_end - padded
    expert_ids = jnp.arange(N_EXPERTS, dtype=jnp.int32)
    pos = jnp.sum(jnp.where(eidx[..., None] == expert_ids, seg_start, 0), axis=-1) + rank
    tile_row0 = jnp.arange(n_tiles, dtype=jnp.int32) * MOE_TILE
    tile_expert = jnp.minimum(
        jnp.sum((tile_row0[:, None] >= seg_end[None, :]).astype(jnp.int32), axis=1), N_EXPERTS - 1)
    n_used = (seg_end[N_EXPERTS - 1] // MOE_TILE).reshape(1)
    valid_end = seg_start + counts
    tile_valid_end = jnp.sum(
        jnp.where(tile_expert[:, None] == expert_ids, valid_end, 0), axis=-1)
    tile_valid = jnp.clip(tile_valid_end - tile_row0, 0, MOE_TILE)
    return (pos.reshape(n_pairs).astype(jnp.int32), tile_expert.astype(jnp.int32),
            tile_valid.astype(jnp.int32), n_used.astype(jnp.int32), n_rows)


def _sc_move_rows(scatter, src, pos_flat, n_out_rows, name):
    info = plsc.get_sparse_core_info()
    n_workers = info.num_cores * info.num_subcores
    n_pairs = pos_flat.shape[0]
    n_src = src.shape[0]
    per_worker = n_pairs // n_workers
    n_chunks = per_worker // SC_CHUNK
    assert per_worker * n_workers == n_pairs and n_chunks * SC_CHUNK == per_worker
    assert n_src % per_worker == 0
    idx = pos_flat.reshape(n_workers, n_chunks, SC_CHUNK)
    mesh = plsc.VectorSubcoreMesh(core_axis_name="core", subcore_axis_name="subcore")

    @functools.partial(
        pl.kernel,
        out_type=jax.ShapeDtypeStruct((n_out_rows, ROW_CHUNKS, V7X_LANES), F32),
        mesh=mesh,
        scratch_types=[
            pltpu.VMEM((n_chunks, SC_CHUNK), jnp.int32),
            pltpu.VMEM((2, SC_CHUNK, ROW_CHUNKS, V7X_LANES), F32),
            pltpu.SemaphoreType.DMA((2,)),
            pltpu.SemaphoreType.DMA((2,)),
        ],
        name=name)
    def move(src_hbm, i_hbm, o_hbm, idx_v, buf, in_sem, out_sem):
        wid = lax.axis_index("subcore") * info.num_cores + lax.axis_index("core")
        base = wid * per_worker
        src_base = lax.rem(base, n_src)
        pltpu.sync_copy(i_hbm.at[wid], idx_v)

        def fetch(s, slot):
            if scatter:
                rows = src_hbm.at[pl.ds(src_base + s * SC_CHUNK, SC_CHUNK)]
            else:
                rows = src_hbm.at[idx_v.at[s]]
            return pltpu.make_async_copy(rows, buf.at[slot], in_sem.at[slot])

        def flush(s, slot):
            if scatter:
                rows = o_hbm.at[idx_v.at[s]]
            else:
                rows = o_hbm.at[pl.ds(base + s * SC_CHUNK, SC_CHUNK)]
            return pltpu.make_async_copy(buf.at[slot], rows, out_sem.at[slot])

        fetch(0, 0).start()
        for s in range(n_chunks):
            slot = s % 2
            fetch(s, slot).wait()
            flush(s, slot).start()
            if s + 1 < n_chunks:
                if s >= 1:
                    flush(s - 1, 1 - slot).wait()
                fetch(s + 1, 1 - slot).start()
        flush(n_chunks - 2, n_chunks % 2).wait()
        flush(n_chunks - 1, (n_chunks - 1) % 2).wait()

    return move(src, idx)


def _experts_kernel(te_ref, tv_ref, nu_ref, x_ref, wg_ref, wu_ref, wd_ref, y_ref,
                    wg_bf, wu_bf, wd_bf):
    j = pl.program_id(0)
    tm = x_ref.shape[0] // ROW_CHUNKS

    @pl.when((j == 0) | (te_ref[j] != te_ref[jnp.maximum(j - 1, 0)]))
    def _():
        wg_bf[...] = wg_ref[0, 0].astype(BF16)
        wu_bf[...] = wu_ref[0, 0].astype(BF16)
        wd_bf[...] = wd_ref[0, 0].astype(BF16)

    @pl.when(j < nu_ref[0])
    def _():
        row = lax.broadcasted_iota(jnp.int32, (tm, 1), 0)
        h = jnp.where(row < tv_ref[j], _load_rows(x_ref, tm), 0.0).astype(BF16)
        a = jnp.dot(h, wg_bf[...], preferred_element_type=F32)
        b = jnp.dot(h, wu_bf[...], preferred_element_type=F32)
        t = (_silu(a) * b).astype(BF16)
        _store_rows(y_ref, jnp.dot(t, wd_bf[...], preferred_element_type=F32))

    @pl.when(j >= nu_ref[0])
    def _():
        y_ref[...] = jnp.zeros_like(y_ref)


def _experts_call(layer, rows, tile_expert, tile_valid, n_used, w_gate, w_up, w_down):
    n_rows = rows.shape[0]
    tm = MOE_TILE
    d = D_MODEL
    rows2 = rows.reshape(n_rows * ROW_CHUNKS, V7X_LANES)
    grid_spec = pltpu.PrefetchScalarGridSpec(
        num_scalar_prefetch=3,
        grid=(n_rows // tm,),
        in_specs=[
            pl.BlockSpec((tm * ROW_CHUNKS, V7X_LANES),
                         lambda j, te, tv, nu: (jnp.minimum(j, nu[0] - 1), 0)),
            pl.BlockSpec((1, 1, d, D_EXPERT), lambda j, te, tv, nu: (layer, te[j], 0, 0)),
            pl.BlockSpec((1, 1, d, D_EXPERT), lambda j, te, tv, nu: (layer, te[j], 0, 0)),
            pl.BlockSpec((1, 1, D_EXPERT, d), lambda j, te, tv, nu: (layer, te[j], 0, 0)),
        ],
        out_specs=pl.BlockSpec((tm * ROW_CHUNKS, V7X_LANES), lambda j, te, tv, nu: (j, 0)),
        scratch_shapes=[
            pltpu.VMEM((d, D_EXPERT), BF16),
            pltpu.VMEM((d, D_EXPERT), BF16),
            pltpu.VMEM((D_EXPERT, d), BF16),
        ],
    )
    y = pl.pallas_call(
        _experts_kernel,
        grid_spec=grid_spec,
        out_shape=jax.ShapeDtypeStruct((n_rows * ROW_CHUNKS, V7X_LANES), F32),
        compiler_params=pltpu.CompilerParams(
            dimension_semantics=("arbitrary",),
            vmem_limit_bytes=V7X_VMEM_LIMIT_BYTES),
        name="experts_l%d" % layer,
    )(tile_expert, tile_valid, n_used, rows2, w_gate, w_up, w_down)
    return y.reshape(n_rows, ROW_CHUNKS, V7X_LANES)


def _moe_rows(layer, mixer_outs, w_gate, w_up, w_down):
    plans = [_sorted_positions(eidx, rank, counts[:, 0])
             for (_, _, eidx, _, rank, counts) in mixer_outs]
    sorted_rows = []
    for (_, h_rows, eidx, _, _, _), (pos_flat, _, _, _, n_rows) in zip(mixer_outs, plans):
        n_tok = eidx.shape[1]
        sorted_rows.append(_sc_move_rows(
            True, h_rows.reshape(n_tok, ROW_CHUNKS, V7X_LANES), pos_flat, n_rows, "sc_scatter_rows"))
    y_sorted = [_experts_call(layer, rows, tile_expert, tile_valid, n_used, w_gate, w_up, w_down)
                for rows, (_, tile_expert, tile_valid, n_used, _) in zip(sorted_rows, plans)]
    y_pairs = []
    for y, (pos_flat, _, _, _, _) in zip(y_sorted, plans):
        n_pairs = pos_flat.shape[0]
        moved = _sc_move_rows(False, y, pos_flat, n_pairs, "sc_gather_rows")
        y_pairs.append(moved.reshape(n_pairs * ROW_CHUNKS, V7X_LANES))
    return y_pairs


def _final_kernel(x_ref, y0_ref, y1_ref, wt_ref, modp_ref, fg_ref, *rest):
    o_ref = rest[-1]
    y = _moe_residual(x_ref, y0_ref, y1_ref, wt_ref, modp_ref[0, 0][5:6])
    ms = jnp.mean(y * y, axis=-1, keepdims=True)
    o_ref[...] = y * lax.rsqrt(ms + EPS) * fg_ref[...]


def _final_call(layer, batch, n_batch, x, y_pairs, wt, mod4, fg, out_prev):
    seq_len, d = x.shape
    tm = MIX_TILE
    tiles_per_seq = seq_len // tm
    in_specs = _combine_specs(seq_len) + [
        pl.BlockSpec((1, 1, 6, d), lambda i: (layer, batch, 0, 0)),
        _const_spec(fg.shape),
    ]
    args = [x, y_pairs, y_pairs, wt, mod4, fg]
    aliases = {}
    if out_prev is not None:
        in_specs.append(pl.BlockSpec(memory_space=pl.ANY))
        aliases = {len(args): 0}
        args.append(out_prev)
    return pl.pallas_call(
        _final_kernel,
        grid=(tiles_per_seq,),
        in_specs=in_specs,
        out_specs=pl.BlockSpec((tm, d), lambda i: (batch * tiles_per_seq + i, 0)),
        out_shape=jax.ShapeDtypeStruct((n_batch * seq_len, d), F32),
        input_output_aliases=aliases,
        compiler_params=pltpu.CompilerParams(
            dimension_semantics=("arbitrary",),
            vmem_limit_bytes=V7X_VMEM_LIMIT_BYTES),
        name="final_norm",
    )(*args)


def kernel(x, c, norm1_g, norm2_g, ada_w, ada_b, ab_w_in, pool_w, pool_scale, conf_conv_w, conf_conv_b, conf_ln_g, conf_ln_b, ab_w_out, cd_w_in, sconv_w, gmlp_ln_g, gmlp_ln_b, gmlp_ws, gmlp_bs, cd_w_out, router_w, router_bias, exp_w_gate, exp_w_up, exp_w_down, final_g):
    bsz, seq_len, d = x.shape
    n_tok = bsz * seq_len
    tm = MIX_TILE
    tiles_per_seq = seq_len // tm
    xf = x.reshape(n_tok, d)

    mod = _ada_mod(c, ada_w, ada_b)
    mod4 = mod.reshape(mod.shape[0], bsz, 6, d)

    rw_hi = router_w.astype(BF16)
    rw_lo = (router_w - rw_hi.astype(F32)).astype(BF16)
    rwt = jnp.concatenate([rw_hi.T, rw_lo.T], axis=0)
    rbias = router_bias.reshape(N_EXPERTS, 1)
    fg = final_g.reshape(1, d)

    weights_ab = [
        ab_w_in[0].astype(BF16), pool_w[0].astype(BF16), pool_scale[0].reshape(1, D_HALF),
        conf_conv_w[0], conf_conv_b[0].reshape(1, D_HALF), conf_ln_g[0].reshape(1, D_HALF),
        conf_ln_b[0].reshape(1, D_HALF), ab_w_out[0].astype(BF16),
    ]
    scratch_ab = [pltpu.VMEM((POOL_HIST + tm, D_HALF), F32),
                  pltpu.VMEM((CONV_HIST + tm, D_HALF), F32)]
    bsf = jnp.repeat(gmlp_bs[0].T, POOL_GROUP, axis=1)
    weights_cd = [
        cd_w_in[0].astype(BF16), sconv_w[0], gmlp_ln_g[0].reshape(1, D_HALF),
        gmlp_ln_b[0].reshape(1, D_HALF), gmlp_ws[0], bsf, cd_w_out[0].astype(BF16),
    ]
    scratch_cd = [pltpu.VMEM((SCONV_HIST + tm, D_HALF), F32)]
    experts = (exp_w_gate, exp_w_up, exp_w_down)

    batches = range(bsz)
    stage_ab = []
    for b in batches:
        x_spec = pl.BlockSpec((tm, d), lambda i, b=b: (b * tiles_per_seq + i, 0))
        stage_ab.append(_mixer_call(
            _mixer_ab_kernel, 0, b, [xf], [x_spec], mod4, norm1_g[0:1], norm2_g[0:1],
            weights_ab, rwt, rbias, scratch_ab, seq_len, "mixer_ab"))
    y_pairs0 = _moe_rows(0, stage_ab, *experts)

    stage_cd = []
    for b in batches:
        x1, _, _, wsel0, _, _ = stage_ab[b]
        prev_mod_spec = pl.BlockSpec((1, 1, 6, d), lambda i, b=b: (0, b, 0, 0))
        stage_cd.append(_mixer_call(
            _mixer_cd_kernel, 1, b, [x1, y_pairs0[b], y_pairs0[b], wsel0.T, mod4],
            _combine_specs(seq_len) + [prev_mod_spec], mod4, norm1_g[1:2], norm2_g[1:2],
            weights_cd, rwt, rbias, scratch_cd, seq_len, "mixer_cd"))
    y_pairs1 = _moe_rows(1, stage_cd, *experts)

    out = None
    for b in batches:
        x3, _, _, wsel1, _, _ = stage_cd[b]
        out = _final_call(1, b, bsz, x3, y_pairs1[b], wsel1.T, mod4, fg, out)
    return out.reshape(bsz, seq_len, d)
```

```python
import functools

import jax
import jax.numpy as jnp
from jax import lax
from jax.experimental import pallas as pl
from jax.experimental.pallas import tpu as pltpu
from jax.experimental.pallas import tpu_sc as plsc

D_MODEL = 1024
EPS = 1e-6
POOL_WINDOWS = (2, 4, 8, 16)
POOL_GROUP = 128
D_HALF = 512
CONF_KERNEL = 31
SCONV_KERNEL = 3
CHUNK = 128
GMLP_HEADS = 4
N_EXPERTS = 16
N_GROUPS = 4
EXPERTS_PER_GROUP = 4
TOP_K = 2
D_EXPERT = 512

V7X_LANES = 128
V7X_SUBLANES = 8
V7X_VMEM_LIMIT_BYTES = 56 * 1024 * 1024

MIX_TILE = 512
MOE_TILE = 512
SC_CHUNK = 64
ROW_CHUNKS = D_MODEL // (2 * V7X_LANES)
CONV_HIST = 32
POOL_HIST = 16
SCONV_HIST = 8

BF16 = jnp.bfloat16
F32 = jnp.float32
U32 = jnp.uint32


def _rms_mod(x, g_row, shift_row, scale_row):
    ms = jnp.mean(x * x, axis=-1, keepdims=True)
    y = x * lax.rsqrt(ms + EPS)
    return (y * g_row) * (1.0 + scale_row) + shift_row


def _layer_norm(x, g_row, b_row):
    mu = jnp.mean(x, axis=-1, keepdims=True)
    xc = x - mu
    var = jnp.mean(xc * xc, axis=-1, keepdims=True)
    return xc * lax.rsqrt(var + EPS) * g_row + b_row


def _sigmoid(x):
    return 1.0 / (1.0 + jnp.exp(-x))


def _silu(x):
    return x * _sigmoid(x)


def _gelu_tanh(x):
    c = 0.7978845608028654
    return 0.5 * x * (1.0 + jnp.tanh(c * (x + 0.044715 * (x * x * x))))


def _load_words(ref, n_rows):
    return jnp.concatenate(
        [ref[pl.ds(c, n_rows, stride=ROW_CHUNKS), :] for c in range(ROW_CHUNKS)], axis=1)


def _store_words(ref, words):
    n_rows = words.shape[0]
    for c in range(ROW_CHUNKS):
        ref[pl.ds(c, n_rows, stride=ROW_CHUNKS), :] = words[:, c * V7X_LANES:(c + 1) * V7X_LANES]


def _pack_rows(val):
    bits = lax.bitcast_convert_type(val.astype(BF16).astype(F32), U32)
    half = val.shape[1] // 2
    return (bits[:, :half] & jnp.uint32(0xFFFF0000)) | (bits[:, half:] >> 16)


def _unpack_rows(words):
    hi = lax.bitcast_convert_type(words & jnp.uint32(0xFFFF0000), F32)
    lo = lax.bitcast_convert_type(words << 16, F32)
    return jnp.concatenate([hi, lo], axis=1)


def _ada_kernel(ct_ref, w_ref, b_ref, o_ref):
    ct = ct_ref[...]
    cond = _silu(ct)
    w = w_ref[0]
    nb = ct.shape[1]
    for b in range(nb):
        col = cond[:, b:b + 1]
        o_ref[0, b:b + 1, :] = jnp.sum(col * w, axis=0, keepdims=True) + b_ref[0]


def _ada_mod(c, ada_w, ada_b):
    depth, d, six_d = ada_w.shape
    bsz = c.shape[0]
    nb = D_MODEL
    return pl.pallas_call(
        _ada_kernel,
        grid=(depth, six_d // nb),
        in_specs=[
            pl.BlockSpec((d, bsz), lambda l, j: (0, 0)),
            pl.BlockSpec((1, d, nb), lambda l, j: (l, 0, j)),
            pl.BlockSpec((1, 1, nb), lambda l, j: (l, 0, j)),
        ],
        out_specs=pl.BlockSpec((1, bsz, nb), lambda l, j: (l, 0, j)),
        out_shape=jax.ShapeDtypeStruct((depth, bsz, six_d), F32),
        compiler_params=pltpu.CompilerParams(
            dimension_semantics=("arbitrary", "arbitrary"),
            vmem_limit_bytes=V7X_VMEM_LIMIT_BYTES),
        name="ada_mod",
    )(c.T, ada_w, ada_b.reshape(depth, 1, six_d))


def _route(h2_bf, rwt_ref, rbias_ref, eidx_ref, wsel_ref, rank_ref, counts_ref, cnt_ref):
    nt = (((1,), (1,)), ((), ()))
    r = lax.dot_general(rwt_ref[...], h2_bf, nt, preferred_element_type=F32)
    logits = r[:N_EXPERTS] + r[N_EXPERTS:]
    m = jnp.max(logits, axis=0, keepdims=True)
    ex = jnp.exp(logits - m)
    probs = ex / jnp.sum(ex, axis=0, keepdims=True)
    sel = probs + rbias_ref[...]
    s = [sel[e:e + 1] for e in range(N_EXPERTS)]
    p = [probs[e:e + 1] for e in range(N_EXPERTS)]
    best = None
    gi = None
    for g in range(N_GROUPS):
        a, b, c, d = s[4 * g:4 * g + 4]
        hi1, lo1 = jnp.maximum(a, b), jnp.minimum(a, b)
        hi2, lo2 = jnp.maximum(c, d), jnp.minimum(c, d)
        top1 = jnp.maximum(hi1, hi2)
        top2 = jnp.maximum(jnp.minimum(hi1, hi2), jnp.maximum(lo1, lo2))
        score = top1 + top2
        if g == 0:
            best, gi = score, jnp.zeros(score.shape, jnp.int32)
        else:
            upd = score > best
            gi = jnp.where(upd, g, gi)
            best = jnp.where(upd, score, best)
    v, q = [], []
    for j in range(EXPERTS_PER_GROUP):
        vj, qj = s[j], p[j]
        for g in range(1, N_GROUPS):
            pick = gi == g
            vj = jnp.where(pick, s[4 * g + j], vj)
            qj = jnp.where(pick, p[4 * g + j], qj)
        v.append(vj)
        q.append(qj)
    i1 = jnp.zeros(gi.shape, jnp.int32)
    m1 = v[0]
    for j in range(1, EXPERTS_PER_GROUP):
        upd = v[j] > m1
        i1 = jnp.where(upd, j, i1)
        m1 = jnp.where(upd, v[j], m1)
    i2 = jnp.zeros(gi.shape, jnp.int32)
    m2 = jnp.full(m1.shape, -jnp.inf, F32)
    for j in range(EXPERTS_PER_GROUP):
        cand = (i1 != j) & (v[j] > m2)
        i2 = jnp.where(cand, j, i2)
        m2 = jnp.where(cand, v[j], m2)
    pa = q[0]
    pb = q[0]
    for j in range(1, EXPERTS_PER_GROUP):
        pa = jnp.where(i1 == j, q[j], pa)
        pb = jnp.where(i2 == j, q[j], pb)
    tot = pa + pb
    e0 = gi * EXPERTS_PER_GROUP + i1
    e1 = gi * EXPERTS_PER_GROUP + i2
    eidx_ref[0:1, :] = e0
    eidx_ref[1:2, :] = e1
    wsel_ref[0:1, :] = pa / tot
    wsel_ref[1:2, :] = pb / tot

    t = h2_bf.shape[0]
    e_iota = lax.broadcasted_iota(jnp.int32, (N_EXPERTS, t), 0)
    oh0 = e_iota == e0
    oh1 = e_iota == e1
    both = jnp.where(oh0 | oh1, 1.0, 0.0)
    r_i = lax.broadcasted_iota(jnp.int32, (V7X_LANES, V7X_LANES), 0)
    c_i = lax.broadcasted_iota(jnp.int32, (V7X_LANES, V7X_LANES), 1)
    before = jnp.where(r_i < c_i, 1.0, 0.0).astype(BF16)
    run = cnt_ref[...]
    rank0, rank1 = [], []
    for blk in range(t // V7X_LANES):
        lanes = slice(blk * V7X_LANES, (blk + 1) * V7X_LANES)
        b = both[:, lanes]
        pre = jnp.dot(b.astype(BF16), before, preferred_element_type=F32) + run
        rank0.append(jnp.sum(jnp.where(oh0[:, lanes], pre, 0.0), axis=0, keepdims=True))
        rank1.append(jnp.sum(jnp.where(oh1[:, lanes], pre, 0.0), axis=0, keepdims=True))
        run = run + jnp.sum(b, axis=1, keepdims=True)
    cnt_ref[...] = run
    rank_ref[0:1, :] = jnp.concatenate(rank0, axis=1).astype(jnp.int32)
    rank_ref[1:2, :] = jnp.concatenate(rank1, axis=1).astype(jnp.int32)
    counts_ref[...] = jnp.broadcast_to(run, counts_ref.shape).astype(jnp.int32)


def _finish_mixer(x, m, mod, n2g_ref, rwt_ref, rbias_ref,
                  x1_ref, h2_ref, eidx_ref, wsel_ref, rank_ref, counts_ref, cnt_ref):
    @pl.when(pl.program_id(0) == 0)
    def _():
        cnt_ref[...] = jnp.zeros_like(cnt_ref)

    x1 = x + mod[2:3] * m
    x1_ref[...] = x1
    h2 = _rms_mod(x1, n2g_ref[...], mod[3:4], mod[4:5])
    h2_bf = h2.astype(BF16)
    _store_words(h2_ref, _pack_rows(h2))
    _route(h2_bf, rwt_ref, rbias_ref, eidx_ref, wsel_ref, rank_ref, counts_ref, cnt_ref)


def _moe_residual(x_ref, y0_ref, y1_ref, wt_ref, g2_row):
    tm = x_ref.shape[0]
    wt = wt_ref[...]
    y0 = _unpack_rows(_load_words(y0_ref, tm))
    y1 = _unpack_rows(_load_words(y1_ref, tm))
    y = wt[:, 0:1] * y0 + wt[:, 1:2] * y1
    return x_ref[...] + g2_row * y


def _mixer_ab_kernel(tiles_per_seq,
                     x_ref, mod_ref, n1g_ref, n2g_ref, win_ref, poolw_ref, pscale_ref,
                     convw_ref, convb_ref, lng_ref, lnb_ref, wout_ref, rwt_ref, rbias_ref,
                     x1_ref, h2_ref, eidx_ref, wsel_ref, rank_ref, counts_ref,
                     pool_ext, conv_ext, cnt_ref):
    i = pl.program_id(0)
    tm = x_ref.shape[0]
    seq_tile = i % tiles_per_seq

    @pl.when(seq_tile == 0)
    def _():
        pool_ext[0:POOL_HIST, :] = jnp.zeros((POOL_HIST, D_HALF), F32)
        conv_ext[0:CONV_HIST, :] = jnp.zeros((CONV_HIST, D_HALF), F32)

    x = x_ref[...]
    mod = mod_ref[0, 0]
    h = _rms_mod(x, n1g_ref[...], mod[0:1], mod[1:2]).astype(BF16)
    z = jnp.dot(h, win_ref[...], preferred_element_type=F32)
    zp = z[:, :D_HALF]
    glu = z[:, D_HALF:2 * D_HALF] * _sigmoid(z[:, 2 * D_HALF:])
    pool_ext[POOL_HIST:POOL_HIST + tm, :] = zp
    conv_ext[CONV_HIST:CONV_HIST + tm, :] = glu

    row = lax.broadcasted_iota(jnp.int32, (tm, 1), 0)
    pos1 = (seq_tile * tm + row + 1).astype(F32)
    pool_out = []
    for g, w in enumerate(POOL_WINDOWS):
        cols = slice(g * POOL_GROUP, (g + 1) * POOL_GROUP)
        u = pool_ext[:, cols]
        acc = u
        span = 1
        while span < w:
            acc = acc + pltpu.roll(acc, span, axis=0)
            span *= 2
        wsum = acc[POOL_HIST:POOL_HIST + tm]
        inv_cnt = 1.0 / jnp.minimum(pos1, float(w))
        diff = wsum * inv_cnt - zp[:, cols]
        po = jnp.dot(diff.astype(BF16), poolw_ref[g], preferred_element_type=F32)
        pool_out.append(po * pscale_ref[:, cols])

    convw = convw_ref[...]
    ext_rows = tm + V7X_SUBLANES
    conv = None
    for r in range(V7X_SUBLANES):
        vr = None
        for a in range(CONV_HIST // V7X_SUBLANES):
            lag = V7X_SUBLANES * a + r
            if lag >= CONF_KERNEL:
                continue
            k = CONF_KERNEL - 1 - lag
            start = CONV_HIST - V7X_SUBLANES - V7X_SUBLANES * a
            term = convw[k:k + 1, :] * conv_ext[start:start + ext_rows, :]
            vr = term if vr is None else vr + term
        if r:
            vr = pltpu.roll(vr, r, axis=0)
        conv = vr if conv is None else conv + vr
    conv = conv[V7X_SUBLANES:V7X_SUBLANES + tm] + convb_ref[...]
    conf = _silu(_layer_norm(conv, lng_ref[...], lnb_ref[...]))

    pool_ext[0:POOL_HIST, :] = zp[tm - POOL_HIST:tm]
    conv_ext[0:CONV_HIST, :] = glu[tm - CONV_HIST:tm]

    m = jnp.dot(conf.astype(BF16), wout_ref[D_HALF:, :], preferred_element_type=F32)
    for g in range(len(POOL_WINDOWS)):
        rows = slice(g * POOL_GROUP, (g + 1) * POOL_GROUP)
        m = m + jnp.dot(pool_out[g].astype(BF16), wout_ref[rows, :], preferred_element_type=F32)
    _finish_mixer(x, m, mod, n2g_ref, rwt_ref, rbias_ref,
                  x1_ref, h2_ref, eidx_ref, wsel_ref, rank_ref, counts_ref, cnt_ref)


def _mixer_cd_kernel(tiles_per_seq,
                     x_ref, y0_ref, y1_ref, wt_ref, modp_ref,
                     mod_ref, n1g_ref, n2g_ref, win_ref, sconvw_ref, lng_ref, lnb_ref,
                     ws_ref, bsf_ref, wout_ref, rwt_ref, rbias_ref,
                     x1_ref, h2_ref, eidx_ref, wsel_ref, rank_ref, counts_ref,
                     sconv_ext, cnt_ref):
    i = pl.program_id(0)
    tm = x_ref.shape[0]
    seq_tile = i % tiles_per_seq

    @pl.when(seq_tile == 0)
    def _():
        sconv_ext[0:SCONV_HIST, :] = jnp.zeros((SCONV_HIST, D_HALF), F32)

    x = _moe_residual(x_ref, y0_ref, y1_ref, wt_ref, modp_ref[0, 0][5:6])
    mod = mod_ref[0, 0]
    h = _rms_mod(x, n1g_ref[...], mod[0:1], mod[1:2]).astype(BF16)
    z = jnp.dot(h, win_ref[...], preferred_element_type=F32)
    bg = z[:, :D_HALF]
    ch = z[:, D_HALF:2 * D_HALF] * z[:, 2 * D_HALF:3 * D_HALF]
    zd = _gelu_tanh(z[:, 3 * D_HALF:])
    u = zd[:, :D_HALF]
    v = _layer_norm(zd[:, D_HALF:], lng_ref[...], lnb_ref[...])

    sconv_ext[SCONV_HIST:SCONV_HIST + tm, :] = ch
    sw = sconvw_ref[...]
    ext = sconv_ext[...]
    conv = sw[2:3, :] * ext
    conv = conv + sw[1:2, :] * pltpu.roll(ext, 1, axis=0)
    conv = conv + sw[0:1, :] * pltpu.roll(ext, 2, axis=0)
    sc_out = bg * conv[SCONV_HIST:SCONV_HIST + tm]
    sconv_ext[0:SCONV_HIST, :] = ch[tm - SCONV_HIST:tm]

    r_i = lax.broadcasted_iota(jnp.int32, (CHUNK, CHUNK), 0)
    c_i = lax.broadcasted_iota(jnp.int32, (CHUNK, CHUNK), 1)
    tril = c_i <= r_i
    wm = [jnp.where(tril, ws_ref[hd], 0.0).astype(BF16) for hd in range(GMLP_HEADS)]
    v_bf = v.astype(BF16)
    bsf = bsf_ref[...]
    gm_rows = []
    for n in range(tm // CHUNK):
        rows = slice(n * CHUNK, (n + 1) * CHUNK)
        heads = []
        for hd in range(GMLP_HEADS):
            cols = slice(hd * POOL_GROUP, (hd + 1) * POOL_GROUP)
            heads.append(jnp.dot(wm[hd], v_bf[rows, cols], preferred_element_type=F32))
        mixed = jnp.concatenate(heads, axis=1) + bsf
        gm_rows.append(u[rows] * mixed)
    gm_out = jnp.concatenate(gm_rows, axis=0)

    m = jnp.dot(sc_out.astype(BF16), wout_ref[:D_HALF, :], preferred_element_type=F32)
    m = m + jnp.dot(gm_out.astype(BF16), wout_ref[D_HALF:, :], preferred_element_type=F32)
    _finish_mixer(x, m, mod, n2g_ref, rwt_ref, rbias_ref,
                  x1_ref, h2_ref, eidx_ref, wsel_ref, rank_ref, counts_ref, cnt_ref)


def _const_spec(shape):
    nd = len(shape)
    return pl.BlockSpec(shape, lambda i: (0,) * nd)


def _mixer_call(kernel_fn, layer, batch, stream_inputs, stream_specs, mod4, n1g, n2g, weights,
                rwt, rbias, scratch, seq_len, name):
    n_tok = seq_len
    d = D_MODEL
    tm = MIX_TILE
    tiles_per_seq = seq_len // tm
    in_specs = stream_specs + [
        pl.BlockSpec((1, 1, 6, d), lambda i: (layer, batch, 0, 0)),
        _const_spec(n1g.shape),
        _const_spec(n2g.shape),
    ] + [_const_spec(w.shape) for w in weights] + [_const_spec(rwt.shape), _const_spec(rbias.shape)]
    out_specs = [
        pl.BlockSpec((tm, d), lambda i: (i, 0)),
        pl.BlockSpec((tm * ROW_CHUNKS, V7X_LANES), lambda i: (i, 0)),
        pl.BlockSpec((TOP_K, tm), lambda i: (0, i)),
        pl.BlockSpec((TOP_K, tm), lambda i: (0, i)),
        pl.BlockSpec((TOP_K, tm), lambda i: (0, i)),
        pl.BlockSpec((N_EXPERTS, V7X_LANES), lambda i: (0, 0)),
    ]
    out_shape = [
        jax.ShapeDtypeStruct((n_tok, d), F32),
        jax.ShapeDtypeStruct((n_tok * ROW_CHUNKS, V7X_LANES), U32),
        jax.ShapeDtypeStruct((TOP_K, n_tok), jnp.int32),
        jax.ShapeDtypeStruct((TOP_K, n_tok), F32),
        jax.ShapeDtypeStruct((TOP_K, n_tok), jnp.int32),
        jax.ShapeDtypeStruct((N_EXPERTS, V7X_LANES), jnp.int32),
    ]
    return pl.pallas_call(
        functools.partial(kernel_fn, tiles_per_seq),
        grid=(n_tok // tm,),
        in_specs=in_specs,
        out_specs=out_specs,
        out_shape=out_shape,
        scratch_shapes=scratch + [pltpu.VMEM((N_EXPERTS, 1), F32)],
        compiler_params=pltpu.CompilerParams(
            dimension_semantics=("arbitrary",),
            vmem_limit_bytes=V7X_VMEM_LIMIT_BYTES),
        name=name,
    )(*stream_inputs, mod4, n1g, n2g, *weights, rwt, rbias)


def _combine_specs(n_tok):
    tm = MIX_TILE
    n_tiles = n_tok // tm
    return [
        pl.BlockSpec((tm, D_MODEL), lambda i: (i, 0)),
        pl.BlockSpec((tm * ROW_CHUNKS, V7X_LANES), lambda i: (i, 0)),
        pl.BlockSpec((tm * ROW_CHUNKS, V7X_LANES), lambda i: (n_tiles + i, 0)),
        pl.BlockSpec((tm, TOP_K), lambda i: (i, 0)),
    ]


def _sorted_positions(eidx, rank, counts):
    n_pairs = eidx.shape[0] * eidx.shape[1]
    n_rows = n_pairs + N_EXPERTS * MOE_TILE
    n_tiles = n_rows // MOE_TILE
    padded = (counts + MOE_TILE - 1) // MOE_TILE * MOE_TILE
    seg_end = jnp.cumsum(padded)
    seg_start = seg_end - padded
    expert_ids = jnp.arange(N_EXPERTS, dtype=jnp.int32)
    pos = jnp.sum(jnp.where(eidx[..., None] == expert_ids, seg_start, 0), axis=-1) + rank
    tile_row0 = jnp.arange(n_tiles, dtype=jnp.int32) * MOE_TILE
    tile_expert = jnp.minimum(
        jnp.sum((tile_row0[:, None] >= seg_end[None, :]).astype(jnp.int32), axis=1), N_EXPERTS - 1)
    n_used = (seg_end[N_EXPERTS - 1] // MOE_TILE).reshape(1)
    valid_end = seg_start + counts
    tile_valid_end = jnp.sum(
        jnp.where(tile_expert[:, None] == expert_ids, valid_end, 0), axis=-1)
    tile_valid = jnp.clip(tile_valid_end - tile_row0, 0, MOE_TILE)
    return (pos.reshape(n_pairs).astype(jnp.int32), tile_expert.astype(jnp.int32),
            tile_valid.astype(jnp.int32), n_used.astype(jnp.int32), n_rows)


def _sc_move_rows(scatter, src, pos_flat, n_out_rows, name):
    info = plsc.get_sparse_core_info()
    n_workers = info.num_cores * info.num_subcores
    n_pairs = pos_flat.shape[0]
    n_src = src.shape[0]
    per_worker = n_pairs // n_workers
    n_chunks = per_worker // SC_CHUNK
    assert per_worker * n_workers == n_pairs and n_chunks * SC_CHUNK == per_worker
    assert n_src % per_worker == 0
    idx = pos_flat.reshape(n_workers, n_chunks, SC_CHUNK)
    mesh = plsc.VectorSubcoreMesh(core_axis_name="core", subcore_axis_name="subcore")

    @functools.partial(
        pl.kernel,
        out_type=jax.ShapeDtypeStruct((n_out_rows, ROW_CHUNKS, V7X_LANES), U32),
        mesh=mesh,
        scratch_types=[
            pltpu.VMEM((n_chunks, SC_CHUNK), jnp.int32),
            pltpu.VMEM((2, SC_CHUNK, ROW_CHUNKS, V7X_LANES), U32),
            pltpu.SemaphoreType.DMA((2,)),
            pltpu.SemaphoreType.DMA((2,)),
        ],
        name=name)
    def move(src_hbm, i_hbm, o_hbm, idx_v, buf, in_sem, out_sem):
        wid = lax.axis_index("subcore") * info.num_cores + lax.axis_index("core")
        base = wid * per_worker
        src_base = lax.rem(base, n_src)
        pltpu.sync_copy(i_hbm.at[wid], idx_v)

        def fetch(s, slot):
            if scatter:
                rows = src_hbm.at[pl.ds(src_base + s * SC_CHUNK, SC_CHUNK)]
            else:
                rows = src_hbm.at[idx_v.at[s]]
            return pltpu.make_async_copy(rows, buf.at[slot], in_sem.at[slot])

        def flush(s, slot):
            if scatter:
                rows = o_hbm.at[idx_v.at[s]]
            else:
                rows = o_hbm.at[pl.ds(base + s * SC_CHUNK, SC_CHUNK)]
            return pltpu.make_async_copy(buf.at[slot], rows, out_sem.at[slot])

        fetch(0, 0).start()
        for s in range(n_chunks):
            slot = s % 2
            fetch(s, slot).wait()
            flush(s, slot).start()
            if s + 1 < n_chunks:
                if s >= 1:
                    flush(s - 1, 1 - slot).wait()
                fetch(s + 1, 1 - slot).start()
        flush(n_chunks - 2, n_chunks % 2).wait()
        flush(n_chunks - 1, (n_chunks - 1) % 2).wait()

    return move(src, idx)


def _experts_kernel(te_ref, tv_ref, nu_ref, x_ref, wg_ref, wu_ref, wd_ref, y_ref,
                    wg_bf, wu_bf, wd_bf):
    j = pl.program_id(0)
    tm = x_ref.shape[0] // ROW_CHUNKS

    @pl.when((j == 0) | (te_ref[j] != te_ref[jnp.maximum(j - 1, 0)]))
    def _():
        wg_bf[...] = wg_ref[0, 0].astype(BF16)
        wu_bf[...] = wu_ref[0, 0].astype(BF16)
        wd_bf[...] = wd_ref[0, 0].astype(BF16)

    @pl.when(j < nu_ref[0])
    def _():
        row = lax.broadcasted_iota(jnp.int32, (tm, 1), 0)
        words = jnp.where(row < tv_ref[j], _load_words(x_ref, tm), jnp.uint32(0))
        h = _unpack_rows(words).astype(BF16)
        a = jnp.dot(h, wg_bf[...], preferred_element_type=F32)
        b = jnp.dot(h, wu_bf[...], preferred_element_type=F32)
        t = (_silu(a) * b).astype(BF16)
        _store_words(y_ref, _pack_rows(jnp.dot(t, wd_bf[...], preferred_element_type=F32)))

    @pl.when(j >= nu_ref[0])
    def _():
        y_ref[...] = jnp.zeros_like(y_ref)


def _experts_call(layer, rows, tile_expert, tile_valid, n_used, w_gate, w_up, w_down):
    n_rows = rows.shape[0]
    tm = MOE_TILE
    d = D_MODEL
    rows2 = rows.reshape(n_rows * ROW_CHUNKS, V7X_LANES)
    grid_spec = pltpu.PrefetchScalarGridSpec(
        num_scalar_prefetch=3,
        grid=(n_rows // tm,),
        in_specs=[
            pl.BlockSpec((tm * ROW_CHUNKS, V7X_LANES),
                         lambda j, te, tv, nu: (jnp.minimum(j, nu[0] - 1), 0)),
            pl.BlockSpec((1, 1, d, D_EXPERT), lambda j, te, tv, nu: (layer, te[j], 0, 0)),
            pl.BlockSpec((1, 1, d, D_EXPERT), lambda j, te, tv, nu: (layer, te[j], 0, 0)),
            pl.BlockSpec((1, 1, D_EXPERT, d), lambda j, te, tv, nu: (layer, te[j], 0, 0)),
        ],
        out_specs=pl.BlockSpec((tm * ROW_CHUNKS, V7X_LANES), lambda j, te, tv, nu: (j, 0)),
        scratch_shapes=[
            pltpu.VMEM((d, D_EXPERT), BF16),
            pltpu.VMEM((d, D_EXPERT), BF16),
            pltpu.VMEM((D_EXPERT, d), BF16),
        ],
    )
    y = pl.pallas_call(
        _experts_kernel,
        grid_spec=grid_spec,
        out_shape=jax.ShapeDtypeStruct((n_rows * ROW_CHUNKS, V7X_LANES), U32),
        compiler_params=pltpu.CompilerParams(
            dimension_semantics=("arbitrary",),
            vmem_limit_bytes=V7X_VMEM_LIMIT_BYTES),
        name="experts_l%d" % layer,
    )(tile_expert, tile_valid, n_used, rows2, w_gate, w_up, w_down)
    return y.reshape(n_rows, ROW_CHUNKS, V7X_LANES)


def _moe_rows(layer, mixer_outs, w_gate, w_up, w_down):
    plans = [_sorted_positions(eidx, rank, counts[:, 0])
             for (_, _, eidx, _, rank, counts) in mixer_outs]
    sorted_rows = []
    for (_, h_rows, eidx, _, _, _), (pos_flat, _, _, _, n_rows) in zip(mixer_outs, plans):
        n_tok = eidx.shape[1]
        sorted_rows.append(_sc_move_rows(
            True, h_rows.reshape(n_tok, ROW_CHUNKS, V7X_LANES), pos_flat, n_rows, "sc_scatter_rows"))
    y_sorted = [_experts_call(layer, rows, tile_expert, tile_valid, n_used, w_gate, w_up, w_down)
                for rows, (_, tile_expert, tile_valid, n_used, _) in zip(sorted_rows, plans)]
    y_pairs = []
    for y, (pos_flat, _, _, _, _) in zip(y_sorted, plans):
        n_pairs = pos_flat.shape[0]
        moved = _sc_move_rows(False, y, pos_flat, n_pairs, "sc_gather_rows")
        y_pairs.append(moved.reshape(n_pairs * ROW_CHUNKS, V7X_LANES))
    return y_pairs


def _final_kernel(x_ref, y0_ref, y1_ref, wt_ref, modp_ref, fg_ref, *rest):
    o_ref = rest[-1]
    y = _moe_residual(x_ref, y0_ref, y1_ref, wt_ref, modp_ref[0, 0][5:6])
    ms = jnp.mean(y * y, axis=-1, keepdims=True)
    o_ref[...] = y * lax.rsqrt(ms + EPS) * fg_ref[...]


def _final_call(layer, batch, n_batch, x, y_pairs, wt, mod4, fg, out_prev):
    seq_len, d = x.shape
    tm = MIX_TILE
    tiles_per_seq = seq_len // tm
    in_specs = _combine_specs(seq_len) + [
        pl.BlockSpec((1, 1, 6, d), lambda i: (layer, batch, 0, 0)),
        _const_spec(fg.shape),
    ]
    args = [x, y_pairs, y_pairs, wt, mod4, fg]
    aliases = {}
    if out_prev is not None:
        in_specs.append(pl.BlockSpec(memory_space=pl.ANY))
        aliases = {len(args): 0}
        args.append(out_prev)
    return pl.pallas_call(
        _final_kernel,
        grid=(tiles_per_seq,),
        in_specs=in_specs,
        out_specs=pl.BlockSpec((tm, d), lambda i: (batch * tiles_per_seq + i, 0)),
        out_shape=jax.ShapeDtypeStruct((n_batch * seq_len, d), F32),
        input_output_aliases=aliases,
        compiler_params=pltpu.CompilerParams(
            dimension_semantics=("arbitrary",),
            vmem_limit_bytes=V7X_VMEM_LIMIT_BYTES),
        name="final_norm",
    )(*args)


def kernel(x, c, norm1_g, norm2_g, ada_w, ada_b, ab_w_in, pool_w, pool_scale, conf_conv_w, conf_conv_b, conf_ln_g, conf_ln_b, ab_w_out, cd_w_in, sconv_w, gmlp_ln_g, gmlp_ln_b, gmlp_ws, gmlp_bs, cd_w_out, router_w, router_bias, exp_w_gate, exp_w_up, exp_w_down, final_g):
    bsz, seq_len, d = x.shape
    n_tok = bsz * seq_len
    tm = MIX_TILE
    tiles_per_seq = seq_len // tm
    xf = x.reshape(n_tok, d)

    mod = _ada_mod(c, ada_w, ada_b)
    mod4 = mod.reshape(mod.shape[0], bsz, 6, d)

    rw_hi = router_w.astype(BF16)
    rw_lo = (router_w - rw_hi.astype(F32)).astype(BF16)
    rwt = jnp.concatenate([rw_hi.T, rw_lo.T], axis=0)
    rbias = router_bias.reshape(N_EXPERTS, 1)
    fg = final_g.reshape(1, d)

    weights_ab = [
        ab_w_in[0].astype(BF16), pool_w[0].astype(BF16), pool_scale[0].reshape(1, D_HALF),
        conf_conv_w[0], conf_conv_b[0].reshape(1, D_HALF), conf_ln_g[0].reshape(1, D_HALF),
        conf_ln_b[0].reshape(1, D_HALF), ab_w_out[0].astype(BF16),
    ]
    scratch_ab = [pltpu.VMEM((POOL_HIST + tm, D_HALF), F32),
                  pltpu.VMEM((CONV_HIST + tm, D_HALF), F32)]
    bsf = jnp.repeat(gmlp_bs[0].T, POOL_GROUP, axis=1)
    weights_cd = [
        cd_w_in[0].astype(BF16), sconv_w[0], gmlp_ln_g[0].reshape(1, D_HALF),
        gmlp_ln_b[0].reshape(1, D_HALF), gmlp_ws[0], bsf, cd_w_out[0].astype(BF16),
    ]
    scratch_cd = [pltpu.VMEM((SCONV_HIST + tm, D_HALF), F32)]
    experts = (exp_w_gate, exp_w_up, exp_w_down)

    batches = range(bsz)
    stage_ab = []
    for b in batches:
        x_spec = pl.BlockSpec((tm, d), lambda i, b=b: (b * tiles_per_seq + i, 0))
        stage_ab.append(_mixer_call(
            _mixer_ab_kernel, 0, b, [xf], [x_spec], mod4, norm1_g[0:1], norm2_g[0:1],
            weights_ab, rwt, rbias, scratch_ab, seq_len, "mixer_ab"))
    y_pairs0 = _moe_rows(0, stage_ab, *experts)

    stage_cd = []
    for b in batches:
        x1, _, _, wsel0, _, _ = stage_ab[b]
        prev_mod_spec = pl.BlockSpec((1, 1, 6, d), lambda i, b=b: (0, b, 0, 0))
        stage_cd.append(_mixer_call(
            _mixer_cd_kernel, 1, b, [x1, y_pairs0[b], y_pairs0[b], wsel0.T, mod4],
            _combine_specs(seq_len) + [prev_mod_spec], mod4, norm1_g[1:2], norm2_g[1:2],
            weights_cd, rwt, rbias, scratch_cd, seq_len, "mixer_cd"))
    y_pairs1 = _moe_rows(1, stage_cd, *experts)

    out = None
    for b in batches:
        x3, _, _, wsel1, _, _ = stage_cd[b]
        out = _final_call(1, b, bsz, x3, y_pairs1[b], wsel1.T, mod4, fg, out)
    return out.reshape(bsz, seq_len, d)
```

```python
import functools

import jax
import jax.numpy as jnp
from jax import lax
from jax.experimental import pallas as pl
from jax.experimental.pallas import tpu as pltpu
from jax.experimental.pallas import tpu_sc as plsc

D_MODEL = 1024
EPS = 1e-6
POOL_WINDOWS = (2, 4, 8, 16)
POOL_GROUP = 128
D_HALF = 512
CONF_KERNEL = 31
SCONV_KERNEL = 3
CHUNK = 128
GMLP_HEADS = 4
N_EXPERTS = 16
N_GROUPS = 4
EXPERTS_PER_GROUP = 4
TOP_K = 2
D_EXPERT = 512

V7X_LANES = 128
V7X_SUBLANES = 8
V7X_VMEM_LIMIT_BYTES = 56 * 1024 * 1024

MIX_TILE = 512
MOE_TILE = 512
SC_CHUNK = 64
ROW_CHUNKS = D_MODEL // (2 * V7X_LANES)
CONV_HIST = 32
POOL_HIST = 16
SCONV_HIST = 8

BF16 = jnp.bfloat16
F32 = jnp.float32
U32 = jnp.uint32


def _rms_mod(x, g_row, shift_row, scale_row):
    ms = jnp.mean(x * x, axis=-1, keepdims=True)
    gain = g_row * (1.0 + scale_row)
    return (x * lax.rsqrt(ms + EPS)) * gain + shift_row


def _layer_norm(x, g_row, b_row):
    mu = jnp.mean(x, axis=-1, keepdims=True)
    xc = x - mu
    var = jnp.mean(xc * xc, axis=-1, keepdims=True)
    return xc * lax.rsqrt(var + EPS) * g_row + b_row


def _sigmoid(x):
    return 1.0 / (1.0 + jnp.exp(-x))


def _silu(x):
    return x * _sigmoid(x)


def _gelu_tanh(x):
    c = 0.7978845608028654
    return 0.5 * x * (1.0 + jnp.tanh(c * (x + 0.044715 * (x * x * x))))


def _load_words(ref, n_rows):
    return jnp.concatenate(
        [ref[pl.ds(c, n_rows, stride=ROW_CHUNKS), :] for c in range(ROW_CHUNKS)], axis=1)


def _store_words(ref, words):
    n_rows = words.shape[0]
    for c in range(ROW_CHUNKS):
        ref[pl.ds(c, n_rows, stride=ROW_CHUNKS), :] = words[:, c * V7X_LANES:(c + 1) * V7X_LANES]


def _pack_rows(val):
    bits = lax.bitcast_convert_type(val.astype(BF16).astype(F32), U32)
    half = val.shape[1] // 2
    return (bits[:, :half] & jnp.uint32(0xFFFF0000)) | (bits[:, half:] >> 16)


def _unpack_rows(words):
    hi = lax.bitcast_convert_type(words & jnp.uint32(0xFFFF0000), F32)
    lo = lax.bitcast_convert_type(words << 16, F32)
    return jnp.concatenate([hi, lo], axis=1)


def _ada_kernel(ct_ref, w_ref, b_ref, o_ref):
    ct = ct_ref[...]
    cond = _silu(ct)
    w = w_ref[0]
    nb = ct.shape[1]
    for b in range(nb):
        col = cond[:, b:b + 1]
        o_ref[0, b:b + 1, :] = jnp.sum(col * w, axis=0, keepdims=True) + b_ref[0]


def _ada_mod(c, ada_w, ada_b):
    depth, d, six_d = ada_w.shape
    bsz = c.shape[0]
    nb = D_MODEL
    return pl.pallas_call(
        _ada_kernel,
        grid=(depth, six_d // nb),
        in_specs=[
            pl.BlockSpec((d, bsz), lambda l, j: (0, 0)),
            pl.BlockSpec((1, d, nb), lambda l, j: (l, 0, j)),
            pl.BlockSpec((1, 1, nb), lambda l, j: (l, 0, j)),
        ],
        out_specs=pl.BlockSpec((1, bsz, nb), lambda l, j: (l, 0, j)),
        out_shape=jax.ShapeDtypeStruct((depth, bsz, six_d), F32),
        compiler_params=pltpu.CompilerParams(
            dimension_semantics=("arbitrary", "arbitrary"),
            vmem_limit_bytes=V7X_VMEM_LIMIT_BYTES),
        name="ada_mod",
    )(c.T, ada_w, ada_b.reshape(depth, 1, six_d))


def _route(h2_bf, rwt_ref, rbias_ref, eidx_ref, wsel_ref, rank_ref, counts_ref, cnt_ref):
    nt = (((1,), (1,)), ((), ()))
    r = lax.dot_general(rwt_ref[...], h2_bf, nt, preferred_element_type=F32)
    logits = r[:N_EXPERTS] + r[N_EXPERTS:]
    m = jnp.max(logits, axis=0, keepdims=True)
    ex = jnp.exp(logits - m)
    probs = ex / jnp.sum(ex, axis=0, keepdims=True)
    sel = probs + rbias_ref[...]
    s = [sel[e:e + 1] for e in range(N_EXPERTS)]
    p = [probs[e:e + 1] for e in range(N_EXPERTS)]
    best = None
    gi = None
    for g in range(N_GROUPS):
        a, b, c, d = s[4 * g:4 * g + 4]
        hi1, lo1 = jnp.maximum(a, b), jnp.minimum(a, b)
        hi2, lo2 = jnp.maximum(c, d), jnp.minimum(c, d)
        top1 = jnp.maximum(hi1, hi2)
        top2 = jnp.maximum(jnp.minimum(hi1, hi2), jnp.maximum(lo1, lo2))
        score = top1 + top2
        if g == 0:
            best, gi = score, jnp.zeros(score.shape, jnp.int32)
        else:
            upd = score > best
            gi = jnp.where(upd, g, gi)
            best = jnp.where(upd, score, best)
    v, q = [], []
    for j in range(EXPERTS_PER_GROUP):
        vj, qj = s[j], p[j]
        for g in range(1, N_GROUPS):
            pick = gi == g
            vj = jnp.where(pick, s[4 * g + j], vj)
            qj = jnp.where(pick, p[4 * g + j], qj)
        v.append(vj)
        q.append(qj)
    i1 = jnp.zeros(gi.shape, jnp.int32)
    m1 = v[0]
    for j in range(1, EXPERTS_PER_GROUP):
        upd = v[j] > m1
        i1 = jnp.where(upd, j, i1)
        m1 = jnp.where(upd, v[j], m1)
    i2 = jnp.zeros(gi.shape, jnp.int32)
    m2 = jnp.full(m1.shape, -jnp.inf, F32)
    for j in range(EXPERTS_PER_GROUP):
        cand = (i1 != j) & (v[j] > m2)
        i2 = jnp.where(cand, j, i2)
        m2 = jnp.where(cand, v[j], m2)
    pa = q[0]
    pb = q[0]
    for j in range(1, EXPERTS_PER_GROUP):
        pa = jnp.where(i1 == j, q[j], pa)
        pb = jnp.where(i2 == j, q[j], pb)
    tot = pa + pb
    e0 = gi * EXPERTS_PER_GROUP + i1
    e1 = gi * EXPERTS_PER_GROUP + i2
    t = h2_bf.shape[0]
    eidx_ref[0:1, :] = e0
    eidx_ref[1:2, :] = e1
    wsel_ref[0:1, :] = pa / tot
    wsel_ref[1:2, :] = pb / tot

    e_iota = lax.broadcasted_iota(jnp.int32, (N_EXPERTS, t), 0)
    oh0 = e_iota == e0
    oh1 = e_iota == e1
    both = jnp.where(oh0 | oh1, 1.0, 0.0)
    r_i = lax.broadcasted_iota(jnp.int32, (V7X_LANES, V7X_LANES), 0)
    c_i = lax.broadcasted_iota(jnp.int32, (V7X_LANES, V7X_LANES), 1)
    before = jnp.where(r_i < c_i, 1.0, 0.0).astype(BF16)
    run = cnt_ref[...]
    rank0, rank1 = [], []
    for blk in range(t // V7X_LANES):
        lanes = slice(blk * V7X_LANES, (blk + 1) * V7X_LANES)
        b = both[:, lanes]
        pre = jnp.dot(b.astype(BF16), before, preferred_element_type=F32) + run
        rank0.append(jnp.sum(jnp.where(oh0[:, lanes], pre, 0.0), axis=0, keepdims=True))
        rank1.append(jnp.sum(jnp.where(oh1[:, lanes], pre, 0.0), axis=0, keepdims=True))
        run = run + jnp.sum(b, axis=1, keepdims=True)
    cnt_ref[...] = run
    rank_ref[0:1, :] = jnp.concatenate(rank0, axis=1).astype(jnp.int32)
    rank_ref[1:2, :] = jnp.concatenate(rank1, axis=1).astype(jnp.int32)
    counts_ref[...] = jnp.broadcast_to(run, counts_ref.shape).astype(jnp.int32)


def _finish_mixer(x, m, mod, n2g_ref, rwt_ref, rbias_ref,
                  x1_ref, h2_ref, eidx_ref, wsel_ref, rank_ref, counts_ref, cnt_ref):
    x1 = x + mod[2:3] * m
    x1_ref[...] = x1
    h2 = _rms_mod(x1, n2g_ref[...], mod[3:4], mod[4:5])
    h2_bf = h2.astype(BF16)
    _store_words(h2_ref, _pack_rows(h2))
    _route(h2_bf, rwt_ref, rbias_ref, eidx_ref, wsel_ref, rank_ref, counts_ref, cnt_ref)


def _moe_residual(x_ref, y0_ref, y1_ref, wt_ref, g2_row):
    tm = x_ref.shape[0]
    wt = wt_ref[...]
    y0 = _unpack_rows(_load_words(y0_ref, tm))
    y1 = _unpack_rows(_load_words(y1_ref, tm))
    y = wt[:, 0:1] * y0 + wt[:, 1:2] * y1
    return x_ref[...] + g2_row * y


def _mixer_ab_kernel(x_ref, mod_ref, n1g_ref, n2g_ref, win_ref, poolw_ref, pscale_ref,
                     convw_ref, convb_ref, lng_ref, lnb_ref, wout_ref, rwt_ref, rbias_ref,
                     x1_ref, h2_ref, eidx_ref, wsel_ref, rank_ref, counts_ref,
                     pool_ext, conv_ext, cnt_ref):
    seq_tile = pl.program_id(0)
    tm = x_ref.shape[0]

    @pl.when(seq_tile == 0)
    def _():
        pool_ext[0:POOL_HIST, :] = jnp.zeros((POOL_HIST, D_HALF), F32)
        conv_ext[0:CONV_HIST, :] = jnp.zeros((CONV_HIST, D_HALF), F32)
        cnt_ref[...] = jnp.zeros_like(cnt_ref)

    x = x_ref[...]
    mod = mod_ref[0, 0]
    h = _rms_mod(x, n1g_ref[...], mod[0:1], mod[1:2]).astype(BF16)
    z = jnp.dot(h, win_ref[...], preferred_element_type=F32)
    zp = z[:, :D_HALF]
    glu = z[:, D_HALF:2 * D_HALF] * _sigmoid(z[:, 2 * D_HALF:])
    pool_ext[POOL_HIST:POOL_HIST + tm, :] = zp
    conv_ext[CONV_HIST:CONV_HIST + tm, :] = glu

    row = lax.broadcasted_iota(jnp.int32, (tm, 1), 0)
    pos1 = (seq_tile * tm + row + 1).astype(F32)
    pool_out = []
    for g, w in enumerate(POOL_WINDOWS):
        cols = slice(g * POOL_GROUP, (g + 1) * POOL_GROUP)
        acc = pool_ext[:, cols]
        span = 1
        while span < w:
            acc = acc + pltpu.roll(acc, span, axis=0)
            span *= 2
        wsum = acc[POOL_HIST:POOL_HIST + tm]
        inv_cnt = 1.0 / jnp.minimum(pos1, float(w))
        diff = wsum * inv_cnt - zp[:, cols]
        po = jnp.dot(diff.astype(BF16), poolw_ref[g], preferred_element_type=F32)
        pool_out.append(po * pscale_ref[:, cols])

    convw = convw_ref[...]
    ext_rows = tm + V7X_SUBLANES
    conv = None
    for r in range(V7X_SUBLANES):
        vr = None
        for a in range(CONV_HIST // V7X_SUBLANES):
            lag = V7X_SUBLANES * a + r
            if lag >= CONF_KERNEL:
                continue
            k = CONF_KERNEL - 1 - lag
            start = CONV_HIST - V7X_SUBLANES - V7X_SUBLANES * a
            term = convw[k:k + 1, :] * conv_ext[start:start + ext_rows, :]
            vr = term if vr is None else vr + term
        if r:
            vr = pltpu.roll(vr, r, axis=0)
        conv = vr if conv is None else conv + vr
    conv = conv[V7X_SUBLANES:V7X_SUBLANES + tm] + convb_ref[...]
    conf = _silu(_layer_norm(conv, lng_ref[...], lnb_ref[...]))

    pool_ext[0:POOL_HIST, :] = zp[tm - POOL_HIST:tm]
    conv_ext[0:CONV_HIST, :] = glu[tm - CONV_HIST:tm]

    m = jnp.dot(conf.astype(BF16), wout_ref[D_HALF:, :], preferred_element_type=F32)
    for g in range(len(POOL_WINDOWS)):
        rows = slice(g * POOL_GROUP, (g + 1) * POOL_GROUP)
        m = m + jnp.dot(pool_out[g].astype(BF16), wout_ref[rows, :], preferred_element_type=F32)
    _finish_mixer(x, m, mod, n2g_ref, rwt_ref, rbias_ref,
                  x1_ref, h2_ref, eidx_ref, wsel_ref, rank_ref, counts_ref, cnt_ref)


def _mixer_cd_kernel(x_ref, y0_ref, y1_ref, wt_ref, modp_ref,
                     mod_ref, n1g_ref, n2g_ref, win_ref, sconvw_ref, lng_ref, lnb_ref,
                     ws_ref, bsf_ref, wout_ref, rwt_ref, rbias_ref,
                     x1_ref, h2_ref, eidx_ref, wsel_ref, rank_ref, counts_ref,
                     sconv_ext, cnt_ref):
    tm = x_ref.shape[0]

    @pl.when(pl.program_id(0) == 0)
    def _():
        sconv_ext[0:SCONV_HIST, :] = jnp.zeros((SCONV_HIST, D_HALF), F32)
        cnt_ref[...] = jnp.zeros_like(cnt_ref)

    x = _moe_residual(x_ref, y0_ref, y1_ref, wt_ref, modp_ref[0, 0][5:6])
    mod = mod_ref[0, 0]
    h = _rms_mod(x, n1g_ref[...], mod[0:1], mod[1:2]).astype(BF16)
    z = jnp.dot(h, win_ref[...], preferred_element_type=F32)
    bg = z[:, :D_HALF]
    ch = z[:, D_HALF:2 * D_HALF] * z[:, 2 * D_HALF:3 * D_HALF]
    zd = _gelu_tanh(z[:, 3 * D_HALF:])
    u = zd[:, :D_HALF]
    v = _layer_norm(zd[:, D_HALF:], lng_ref[...], lnb_ref[...])

    sconv_ext[SCONV_HIST:SCONV_HIST + tm, :] = ch
    sw = sconvw_ref[...]
    ext = sconv_ext[...]
    conv = sw[2:3, :] * ext
    conv = conv + sw[1:2, :] * pltpu.roll(ext, 1, axis=0)
    conv = conv + sw[0:1, :] * pltpu.roll(ext, 2, axis=0)
    sc_out = bg * conv[SCONV_HIST:SCONV_HIST + tm]
    sconv_ext[0:SCONV_HIST, :] = ch[tm - SCONV_HIST:tm]

    r_i = lax.broadcasted_iota(jnp.int32, (CHUNK, CHUNK), 0)
    c_i = lax.broadcasted_iota(jnp.int32, (CHUNK, CHUNK), 1)
    tril = c_i <= r_i
    wm = [jnp.where(tril, ws_ref[hd], 0.0).astype(BF16) for hd in range(GMLP_HEADS)]
    v_bf = v.astype(BF16)
    bsf = bsf_ref[...]
    gm_rows = []
    for n in range(tm // CHUNK):
        rows = slice(n * CHUNK, (n + 1) * CHUNK)
        heads = []
        for hd in range(GMLP_HEADS):
            cols = slice(hd * POOL_GROUP, (hd + 1) * POOL_GROUP)
            heads.append(jnp.dot(wm[hd], v_bf[rows, cols], preferred_element_type=F32))
        mixed = jnp.concatenate(heads, axis=1) + bsf
        gm_rows.append(u[rows] * mixed)
    gm_out = jnp.concatenate(gm_rows, axis=0)

    m = jnp.dot(sc_out.astype(BF16), wout_ref[:D_HALF, :], preferred_element_type=F32)
    m = m + jnp.dot(gm_out.astype(BF16), wout_ref[D_HALF:, :], preferred_element_type=F32)
    _finish_mixer(x, m, mod, n2g_ref, rwt_ref, rbias_ref,
                  x1_ref, h2_ref, eidx_ref, wsel_ref, rank_ref, counts_ref, cnt_ref)


def _const_spec(shape):
    nd = len(shape)
    return pl.BlockSpec(shape, lambda i: (0,) * nd)


def _mixer_call(kernel_fn, layer, batch, stream_inputs, stream_specs, mod4, n1g, n2g, weights,
                rwt, rbias, scratch, seq_len, name):
    n_tok = seq_len
    d = D_MODEL
    tm = MIX_TILE
    in_specs = stream_specs + [
        pl.BlockSpec((1, 1, 6, d), lambda i: (layer, batch, 0, 0)),
        _const_spec(n1g.shape),
        _const_spec(n2g.shape),
    ] + [_const_spec(w.shape) for w in weights] + [_const_spec(rwt.shape), _const_spec(rbias.shape)]
    out_specs = [
        pl.BlockSpec((tm, d), lambda i: (i, 0)),
        pl.BlockSpec((tm * ROW_CHUNKS, V7X_LANES), lambda i: (i, 0)),
        pl.BlockSpec((TOP_K, tm), lambda i: (0, i)),
        pl.BlockSpec((TOP_K, tm), lambda i: (0, i)),
        pl.BlockSpec((TOP_K, tm), lambda i: (0, i)),
        pl.BlockSpec((N_EXPERTS, V7X_LANES), lambda i: (0, 0)),
    ]
    out_shape = [
        jax.ShapeDtypeStruct((n_tok, d), F32),
        jax.ShapeDtypeStruct((n_tok * ROW_CHUNKS, V7X_LANES), U32),
        jax.ShapeDtypeStruct((TOP_K, n_tok), jnp.int32),
        jax.ShapeDtypeStruct((TOP_K, n_tok), F32),
        jax.ShapeDtypeStruct((TOP_K, n_tok), jnp.int32),
        jax.ShapeDtypeStruct((N_EXPERTS, V7X_LANES), jnp.int32),
    ]
    return pl.pallas_call(
        kernel_fn,
        grid=(n_tok // tm,),
        in_specs=in_specs,
        out_specs=out_specs,
        out_shape=out_shape,
        scratch_shapes=scratch + [pltpu.VMEM((N_EXPERTS, 1), F32)],
        compiler_params=pltpu.CompilerParams(
            dimension_semantics=("arbitrary",),
            vmem_limit_bytes=V7X_VMEM_LIMIT_BYTES),
        name=name,
    )(*stream_inputs, mod4, n1g, n2g, *weights, rwt, rbias)


def _combine_specs(n_tok):
    tm = MIX_TILE
    n_tiles = n_tok // tm
    return [
        pl.BlockSpec((tm, D_MODEL), lambda i: (i, 0)),
        pl.BlockSpec((tm * ROW_CHUNKS, V7X_LANES), lambda i: (i, 0)),
        pl.BlockSpec((tm * ROW_CHUNKS, V7X_LANES), lambda i: (n_tiles + i, 0)),
        pl.BlockSpec((tm, TOP_K), lambda i: (i, 0)),
    ]


def _sorted_positions(eidx, rank, counts):
    n_pairs = eidx.shape[0] * eidx.shape[1]
    n_rows = n_pairs + N_EXPERTS * MOE_TILE
    n_tiles = n_rows // MOE_TILE
    padded = (counts + MOE_TILE - 1) // MOE_TILE * MOE_TILE
    seg_end = jnp.cumsum(padded)
    seg_start = seg_end - padded
    expert_ids = jnp.arange(N_EXPERTS, dtype=jnp.int32)
    pos = jnp.sum(jnp.where(eidx[..., None] == expert_ids, seg_start, 0), axis=-1) + rank
    tile_row0 = jnp.arange(n_tiles, dtype=jnp.int32) * MOE_TILE
    tile_expert = jnp.minimum(
        jnp.sum((tile_row0[:, None] >= seg_end[None, :]).astype(jnp.int32), axis=1), N_EXPERTS - 1)
    n_used = (seg_end[N_EXPERTS - 1] // MOE_TILE).reshape(1)
    valid_end = seg_start + counts
    tile_valid_end = jnp.sum(
        jnp.where(tile_expert[:, None] == expert_ids, valid_end, 0), axis=-1)
    tile_valid = jnp.clip(tile_valid_end - tile_row0, 0, MOE_TILE)
    return (pos.reshape(n_pairs).astype(jnp.int32), tile_expert.astype(jnp.int32),
            tile_valid.astype(jnp.int32), n_used.astype(jnp.int32), n_rows)


def _sc_move_rows(scatter, src, pos_flat, n_out_rows, name):
    info = plsc.get_sparse_core_info()
    n_workers = info.num_cores * info.num_subcores
    n_pairs = pos_flat.shape[0]
    n_src = src.shape[0]
    per_worker = n_pairs // n_workers
    n_chunks = per_worker // SC_CHUNK
    assert per_worker * n_workers == n_pairs and n_chunks * SC_CHUNK == per_worker
    assert n_src % per_worker == 0
    idx = pos_flat.reshape(n_workers, n_chunks, SC_CHUNK)
    mesh = plsc.VectorSubcoreMesh(core_axis_name="core", subcore_axis_name="subcore")

    @functools.partial(
        pl.kernel,
        out_type=jax.ShapeDtypeStruct((n_out_rows, ROW_CHUNKS, V7X_LANES), U32),
        mesh=mesh,
        scratch_types=[
            pltpu.VMEM((n_chunks, SC_CHUNK), jnp.int32),
            pltpu.VMEM((2, SC_CHUNK, ROW_CHUNKS, V7X_LANES), U32),
            pltpu.SemaphoreType.DMA((2,)),
            pltpu.SemaphoreType.DMA((2,)),
        ],
        name=name)
    def move(src_hbm, i_hbm, o_hbm, idx_v, buf, in_sem, out_sem):
        wid = lax.axis_index("subcore") * info.num_cores + lax.axis_index("core")
        base = wid * per_worker
        src_base = lax.rem(base, n_src)
        pltpu.sync_copy(i_hbm.at[wid], idx_v)

        def fetch(s, slot):
            if scatter:
                rows = src_hbm.at[pl.ds(src_base + s * SC_CHUNK, SC_CHUNK)]
            else:
                rows = src_hbm.at[idx_v.at[s]]
            return pltpu.make_async_copy(rows, buf.at[slot], in_sem.at[slot])

        def flush(s, slot):
            if scatter:
                rows = o_hbm.at[idx_v.at[s]]
            else:
                rows = o_hbm.at[pl.ds(base + s * SC_CHUNK, SC_CHUNK)]
            return pltpu.make_async_copy(buf.at[slot], rows, out_sem.at[slot])

        fetch(0, 0).start()
        for s in range(n_chunks):
            slot = s % 2
            fetch(s, slot).wait()
            flush(s, slot).start()
            if s + 1 < n_chunks:
                if s >= 1:
                    flush(s - 1, 1 - slot).wait()
                fetch(s + 1, 1 - slot).start()
        flush(n_chunks - 2, n_chunks % 2).wait()
        flush(n_chunks - 1, (n_chunks - 1) % 2).wait()

    return move(src, idx)


def _experts_kernel(te_ref, tv_ref, nu_ref, x_ref, wg_ref, wu_ref, wd_ref, y_ref,
                    wg_bf, wu_bf, wd_bf):
    j = pl.program_id(0)
    tm = x_ref.shape[0] // ROW_CHUNKS

    @pl.when((j == 0) | (te_ref[j] != te_ref[jnp.maximum(j - 1, 0)]))
    def _():
        wg_bf[...] = wg_ref[0, 0].astype(BF16)
        wu_bf[...] = wu_ref[0, 0].astype(BF16)
        wd_bf[...] = wd_ref[0, 0].astype(BF16)

    def ffn(n_rows):
        row = lax.broadcasted_iota(jnp.int32, (n_rows, 1), 0)
        words = jnp.where(row < tv_ref[j], _load_words(x_ref, n_rows), jnp.uint32(0))
        h = _unpack_rows(words).astype(BF16)
        a = jnp.dot(h, wg_bf[...], preferred_element_type=F32)
        b = jnp.dot(h, wu_bf[...], preferred_element_type=F32)
        t = (_silu(a) * b).astype(BF16)
        _store_words(y_ref, _pack_rows(jnp.dot(t, wd_bf[...], preferred_element_type=F32)))

    used = j < nu_ref[0]
    half = tm // 2
    pl.when(used & (tv_ref[j] > half))(functools.partial(ffn, tm))

    @pl.when(used & (tv_ref[j] <= half))
    def _():
        ffn(half)
        y_ref[half * ROW_CHUNKS:, :] = jnp.zeros((half * ROW_CHUNKS, V7X_LANES), U32)

    @pl.when(jnp.logical_not(used))
    def _():
        y_ref[...] = jnp.zeros_like(y_ref)


def _experts_call(layer, rows, tile_expert, tile_valid, n_used, w_gate, w_up, w_down):
    n_rows = rows.shape[0]
    tm = MOE_TILE
    d = D_MODEL
    rows2 = rows.reshape(n_rows * ROW_CHUNKS, V7X_LANES)
    grid_spec = pltpu.PrefetchScalarGridSpec(
        num_scalar_prefetch=3,
        grid=(n_rows // tm,),
        in_specs=[
            pl.BlockSpec((tm * ROW_CHUNKS, V7X_LANES),
                         lambda j, te, tv, nu: (jnp.minimum(j, nu[0] - 1), 0)),
            pl.BlockSpec((1, 1, d, D_EXPERT), lambda j, te, tv, nu: (layer, te[j], 0, 0)),
            pl.BlockSpec((1, 1, d, D_EXPERT), lambda j, te, tv, nu: (layer, te[j], 0, 0)),
            pl.BlockSpec((1, 1, D_EXPERT, d), lambda j, te, tv, nu: (layer, te[j], 0, 0)),
        ],
        out_specs=pl.BlockSpec((tm * ROW_CHUNKS, V7X_LANES), lambda j, te, tv, nu: (j, 0)),
        scratch_shapes=[
            pltpu.VMEM((d, D_EXPERT), BF16),
            pltpu.VMEM((d, D_EXPERT), BF16),
            pltpu.VMEM((D_EXPERT, d), BF16),
        ],
    )
    y = pl.pallas_call(
        _experts_kernel,
        grid_spec=grid_spec,
        out_shape=jax.ShapeDtypeStruct((n_rows * ROW_CHUNKS, V7X_LANES), U32),
        compiler_params=pltpu.CompilerParams(
            dimension_semantics=("arbitrary",),
            vmem_limit_bytes=V7X_VMEM_LIMIT_BYTES),
        name="experts_l%d" % layer,
    )(tile_expert, tile_valid, n_used, rows2, w_gate, w_up, w_down)
    return y.reshape(n_rows, ROW_CHUNKS, V7X_LANES)


def _moe_rows(layer, mixer_outs, w_gate, w_up, w_down):
    plans = [_sorted_positions(eidx, rank, counts[:, 0])
             for (_, _, eidx, _, rank, counts) in mixer_outs]
    sorted_rows = []
    for (_, h_rows, eidx, _, _, _), (pos_flat, _, _, _, n_rows) in zip(mixer_outs, plans):
        n_tok = eidx.shape[1]
        sorted_rows.append(_sc_move_rows(
            True, h_rows.reshape(n_tok, ROW_CHUNKS, V7X_LANES), pos_flat, n_rows, "sc_scatter_rows"))
    y_sorted = [_experts_call(layer, rows, tile_expert, tile_valid, n_used, w_gate, w_up, w_down)
                for rows, (_, tile_expert, tile_valid, n_used, _) in zip(sorted_rows, plans)]
    y_pairs = []
    for y, (pos_flat, _, _, _, _) in zip(y_sorted, plans):
        n_pairs = pos_flat.shape[0]
        moved = _sc_move_rows(False, y, pos_flat, n_pairs, "sc_gather_rows")
        y_pairs.append(moved.reshape(n_pairs * ROW_CHUNKS, V7X_LANES))
    return y_pairs


def _final_kernel(x_ref, y0_ref, y1_ref, wt_ref, modp_ref, fg_ref, *rest):
    o_ref = rest[-1]
    y = _moe_residual(x_ref, y0_ref, y1_ref, wt_ref, modp_ref[0, 0][5:6])
    ms = jnp.mean(y * y, axis=-1, keepdims=True)
    o_ref[...] = y * lax.rsqrt(ms + EPS) * fg_ref[...]


def _final_call(layer, batch, n_batch, x, y_pairs, wt, mod4, fg, out_prev):
    seq_len, d = x.shape
    tm = MIX_TILE
    tiles_per_seq = seq_len // tm
    in_specs = _combine_specs(seq_len) + [
        pl.BlockSpec((1, 1, 6, d), lambda i: (layer, batch, 0, 0)),
        _const_spec(fg.shape),
    ]
    args = [x, y_pairs, y_pairs, wt, mod4, fg]
    aliases = {}
    if out_prev is not None:
        in_specs.append(pl.BlockSpec(memory_space=pl.ANY))
        aliases = {len(args): 0}
        args.append(out_prev)
    return pl.pallas_call(
        _final_kernel,
        grid=(tiles_per_seq,),
        in_specs=in_specs,
        out_specs=pl.BlockSpec((tm, d), lambda i: (batch * tiles_per_seq + i, 0)),
        out_shape=jax.ShapeDtypeStruct((n_batch * seq_len, d), F32),
        input_output_aliases=aliases,
        compiler_params=pltpu.CompilerParams(
            dimension_semantics=("arbitrary",),
            vmem_limit_bytes=V7X_VMEM_LIMIT_BYTES),
        name="final_norm",
    )(*args)


def kernel(x, c, norm1_g, norm2_g, ada_w, ada_b, ab_w_in, pool_w, pool_scale, conf_conv_w, conf_conv_b, conf_ln_g, conf_ln_b, ab_w_out, cd_w_in, sconv_w, gmlp_ln_g, gmlp_ln_b, gmlp_ws, gmlp_bs, cd_w_out, router_w, router_bias, exp_w_gate, exp_w_up, exp_w_down, final_g):
    bsz, seq_len, d = x.shape
    n_tok = bsz * seq_len
    tm = MIX_TILE
    tiles_per_seq = seq_len // tm
    xf = x.reshape(n_tok, d)

    mod = _ada_mod(c, ada_w, ada_b)
    mod4 = mod.reshape(mod.shape[0], bsz, 6, d)

    rw_hi = router_w.astype(BF16)
    rw_lo = (router_w - rw_hi.astype(F32)).astype(BF16)
    rwt = jnp.concatenate([rw_hi.T, rw_lo.T], axis=0)
    rbias = router_bias.reshape(N_EXPERTS, 1)
    fg = final_g.reshape(1, d)

    weights_ab = [
        ab_w_in[0].astype(BF16), pool_w[0].astype(BF16), pool_scale[0].reshape(1, D_HALF),
        conf_conv_w[0], conf_conv_b[0].reshape(1, D_HALF), conf_ln_g[0].reshape(1, D_HALF),
        conf_ln_b[0].reshape(1, D_HALF), ab_w_out[0].astype(BF16),
    ]
    scratch_ab = [pltpu.VMEM((POOL_HIST + tm, D_HALF), F32),
                  pltpu.VMEM((CONV_HIST + tm, D_HALF), F32)]
    bsf = jnp.repeat(gmlp_bs[0].T, POOL_GROUP, axis=1)
    weights_cd = [
        cd_w_in[0].astype(BF16), sconv_w[0], gmlp_ln_g[0].reshape(1, D_HALF),
        gmlp_ln_b[0].reshape(1, D_HALF), gmlp_ws[0], bsf, cd_w_out[0].astype(BF16),
    ]
    scratch_cd = [pltpu.VMEM((SCONV_HIST + tm, D_HALF), F32)]
    experts = (exp_w_gate, exp_w_up, exp_w_down)

    batches = range(bsz)
    stage_ab = []
    for b in batches:
        x_spec = pl.BlockSpec((tm, d), lambda i, b=b: (b * tiles_per_seq + i, 0))
        stage_ab.append(_mixer_call(
            _mixer_ab_kernel, 0, b, [xf], [x_spec], mod4, norm1_g[0:1], norm2_g[0:1],
            weights_ab, rwt, rbias, scratch_ab, seq_len, "mixer_ab"))
    y_pairs0 = _moe_rows(0, stage_ab, *experts)

    stage_cd = []
    for b in batches:
        x1, _, _, wsel0, _, _ = stage_ab[b]
        prev_mod_spec = pl.BlockSpec((1, 1, 6, d), lambda i, b=b: (0, b, 0, 0))
        stage_cd.append(_mixer_call(
            _mixer_cd_kernel, 1, b, [x1, y_pairs0[b], y_pairs0[b], wsel0.T, mod4],
            _combine_specs(seq_len) + [prev_mod_spec], mod4, norm1_g[1:2], norm2_g[1:2],
            weights_cd, rwt, rbias, scratch_cd, seq_len, "mixer_cd"))
    y_pairs1 = _moe_rows(1, stage_cd, *experts)

    out = None
    for b in batches:
        x3, _, _, wsel1, _, _ = stage_cd[b]
        out = _final_call(1, b, bsz, x3, y_pairs1[b], wsel1.T, mod4, fg, out)
    return out.reshape(bsz, seq_len, d)
```

```python
import functools

import jax
import jax.numpy as jnp
from jax import lax
from jax.experimental import pallas as pl
from jax.experimental.pallas import tpu as pltpu
from jax.experimental.pallas import tpu_sc as plsc

D_MODEL = 1024
EPS = 1e-6
POOL_WINDOWS = (2, 4, 8, 16)
POOL_GROUP = 128
D_HALF = 512
CONF_KERNEL = 31
SCONV_KERNEL = 3
CHUNK = 128
GMLP_HEADS = 4
N_EXPERTS = 16
N_GROUPS = 4
EXPERTS_PER_GROUP = 4
TOP_K = 2
D_EXPERT = 512

V7X_LANES = 128
V7X_SUBLANES = 8
V7X_VMEM_LIMIT_BYTES = 56 * 1024 * 1024

MIX_TILE = 512
MOE_TILE = 512
SC_CHUNK = 64
ROW_CHUNKS = D_MODEL // (2 * V7X_LANES)
CONV_HIST = 32
POOL_HIST = 16
SCONV_HIST = 8

BF16 = jnp.bfloat16
F32 = jnp.float32
U32 = jnp.uint32


def _rms_mod(x, g_row, shift_row, scale_row):
    ms = jnp.mean(x * x, axis=-1, keepdims=True)
    gain = g_row * (1.0 + scale_row)
    return (x * lax.rsqrt(ms + EPS)) * gain + shift_row


def _layer_norm(x, g_row, b_row):
    mu = jnp.mean(x, axis=-1, keepdims=True)
    xc = x - mu
    var = jnp.mean(xc * xc, axis=-1, keepdims=True)
    return xc * lax.rsqrt(var + EPS) * g_row + b_row


def _sigmoid(x):
    return 1.0 / (1.0 + jnp.exp(-x))


def _silu(x):
    return x * _sigmoid(x)


def _gelu_tanh(x):
    c = 0.7978845608028654
    return 0.5 * x * (1.0 + jnp.tanh(c * (x + 0.044715 * (x * x * x))))


def _load_words(ref, n_rows):
    return jnp.concatenate(
        [ref[pl.ds(c, n_rows, stride=ROW_CHUNKS), :] for c in range(ROW_CHUNKS)], axis=1)


def _store_words(ref, words):
    n_rows = words.shape[0]
    for c in range(ROW_CHUNKS):
        ref[pl.ds(c, n_rows, stride=ROW_CHUNKS), :] = words[:, c * V7X_LANES:(c + 1) * V7X_LANES]


def _pack_rows(val):
    bits = lax.bitcast_convert_type(val.astype(BF16).astype(F32), U32)
    half = val.shape[1] // 2
    return (bits[:, :half] & jnp.uint32(0xFFFF0000)) | (bits[:, half:] >> 16)


def _unpack_rows(words):
    hi = lax.bitcast_convert_type(words & jnp.uint32(0xFFFF0000), F32)
    lo = lax.bitcast_convert_type(words << 16, F32)
    return jnp.concatenate([hi, lo], axis=1)


def _ada_kernel(ct_ref, w_ref, b_ref, o_ref):
    ct = ct_ref[...]
    cond = _silu(ct)
    w = w_ref[0]
    nb = ct.shape[1]
    for b in range(nb):
        col = cond[:, b:b + 1]
        o_ref[0, b:b + 1, :] = jnp.sum(col * w, axis=0, keepdims=True) + b_ref[0]


def _ada_mod(c, ada_w, ada_b):
    depth, d, six_d = ada_w.shape
    bsz = c.shape[0]
    nb = D_MODEL
    return pl.pallas_call(
        _ada_kernel,
        grid=(depth, six_d // nb),
        in_specs=[
            pl.BlockSpec((d, bsz), lambda l, j: (0, 0)),
            pl.BlockSpec((1, d, nb), lambda l, j: (l, 0, j)),
            pl.BlockSpec((1, 1, nb), lambda l, j: (l, 0, j)),
        ],
        out_specs=pl.BlockSpec((1, bsz, nb), lambda l, j: (l, 0, j)),
        out_shape=jax.ShapeDtypeStruct((depth, bsz, six_d), F32),
        compiler_params=pltpu.CompilerParams(
            dimension_semantics=("arbitrary", "arbitrary"),
            vmem_limit_bytes=V7X_VMEM_LIMIT_BYTES),
        name="ada_mod",
    )(c.T, ada_w, ada_b.reshape(depth, 1, six_d))


def _route(h2_bf, rwt_ref, rbias_ref, eidx_ref, wsel_ref, rank_ref, counts_ref, cnt_ref):
    nt = (((1,), (1,)), ((), ()))
    r = lax.dot_general(rwt_ref[...], h2_bf, nt, preferred_element_type=F32)
    logits = r[:N_EXPERTS] + r[N_EXPERTS:]
    m = jnp.max(logits, axis=0, keepdims=True)
    ex = jnp.exp(logits - m)
    probs = ex / jnp.sum(ex, axis=0, keepdims=True)
    sel = probs + rbias_ref[...]
    s = [sel[e:e + 1] for e in range(N_EXPERTS)]
    p = [probs[e:e + 1] for e in range(N_EXPERTS)]
    best = None
    gi = None
    for g in range(N_GROUPS):
        a, b, c, d = s[4 * g:4 * g + 4]
        hi1, lo1 = jnp.maximum(a, b), jnp.minimum(a, b)
        hi2, lo2 = jnp.maximum(c, d), jnp.minimum(c, d)
        top1 = jnp.maximum(hi1, hi2)
        top2 = jnp.maximum(jnp.minimum(hi1, hi2), jnp.maximum(lo1, lo2))
        score = top1 + top2
        if g == 0:
            best, gi = score, jnp.zeros(score.shape, jnp.int32)
        else:
            upd = score > best
            gi = jnp.where(upd, g, gi)
            best = jnp.where(upd, score, best)
    v, q = [], []
    for j in range(EXPERTS_PER_GROUP):
        vj, qj = s[j], p[j]
        for g in range(1, N_GROUPS):
            pick = gi == g
            vj = jnp.where(pick, s[4 * g + j], vj)
            qj = jnp.where(pick, p[4 * g + j], qj)
        v.append(vj)
        q.append(qj)
    i1 = jnp.zeros(gi.shape, jnp.int32)
    m1 = v[0]
    for j in range(1, EXPERTS_PER_GROUP):
        upd = v[j] > m1
        i1 = jnp.where(upd, j, i1)
        m1 = jnp.where(upd, v[j], m1)
    i2 = jnp.zeros(gi.shape, jnp.int32)
    m2 = jnp.full(m1.shape, -jnp.inf, F32)
    for j in range(EXPERTS_PER_GROUP):
        cand = (i1 != j) & (v[j] > m2)
        i2 = jnp.where(cand, j, i2)
        m2 = jnp.where(cand, v[j], m2)
    pa = q[0]
    pb = q[0]
    for j in range(1, EXPERTS_PER_GROUP):
        pa = jnp.where(i1 == j, q[j], pa)
        pb = jnp.where(i2 == j, q[j], pb)
    tot = pa + pb
    e0 = gi * EXPERTS_PER_GROUP + i1
    e1 = gi * EXPERTS_PER_GROUP + i2
    t = h2_bf.shape[0]
    eidx_ref[0:1, :] = e0
    eidx_ref[1:2, :] = e1
    wsel_ref[0:1, :] = pa / tot
    wsel_ref[1:2, :] = pb / tot

    e_iota = lax.broadcasted_iota(jnp.int32, (N_EXPERTS, t), 0)
    oh0 = e_iota == e0
    oh1 = e_iota == e1
    both = jnp.where(oh0 | oh1, 1.0, 0.0)
    r_i = lax.broadcasted_iota(jnp.int32, (V7X_LANES, V7X_LANES), 0)
    c_i = lax.broadcasted_iota(jnp.int32, (V7X_LANES, V7X_LANES), 1)
    before = jnp.where(r_i < c_i, 1.0, 0.0).astype(BF16)
    run = cnt_ref[...]
    rank0, rank1 = [], []
    for blk in range(t // V7X_LANES):
        lanes = slice(blk * V7X_LANES, (blk + 1) * V7X_LANES)
        b = both[:, lanes]
        pre = jnp.dot(b.astype(BF16), before, preferred_element_type=F32) + run
        rank0.append(jnp.sum(jnp.where(oh0[:, lanes], pre, 0.0), axis=0, keepdims=True))
        rank1.append(jnp.sum(jnp.where(oh1[:, lanes], pre, 0.0), axis=0, keepdims=True))
        run = run + jnp.sum(b, axis=1, keepdims=True)
    cnt_ref[...] = run
    rank_ref[0:1, :] = jnp.concatenate(rank0, axis=1).astype(jnp.int32)
    rank_ref[1:2, :] = jnp.concatenate(rank1, axis=1).astype(jnp.int32)
    counts_ref[...] = jnp.broadcast_to(run, counts_ref.shape).astype(jnp.int32)


def _finish_mixer(x, m, mod, n2g_ref, rwt_ref, rbias_ref,
                  x1_ref, h2_ref, eidx_ref, wsel_ref, rank_ref, counts_ref, cnt_ref):
    x1 = x + mod[2:3] * m
    x1_ref[...] = x1
    h2 = _rms_mod(x1, n2g_ref[...], mod[3:4], mod[4:5])
    h2_bf = h2.astype(BF16)
    _store_words(h2_ref, _pack_rows(h2))
    _route(h2_bf, rwt_ref, rbias_ref, eidx_ref, wsel_ref, rank_ref, counts_ref, cnt_ref)


def _moe_residual(x_ref, y0_ref, y1_ref, wt_ref, g2_row):
    tm = x_ref.shape[0]
    wt = wt_ref[...]
    y0 = _unpack_rows(_load_words(y0_ref, tm))
    y1 = _unpack_rows(_load_words(y1_ref, tm))
    y = wt[:, 0:1] * y0 + wt[:, 1:2] * y1
    return x_ref[...] + g2_row * y


def _mixer_ab_kernel(x_ref, mod_ref, n1g_ref, n2g_ref, win_ref, poolw_ref, pscale_ref,
                     convw_ref, convb_ref, lng_ref, lnb_ref, wout_ref, rwt_ref, rbias_ref,
                     x1_ref, h2_ref, eidx_ref, wsel_ref, rank_ref, counts_ref,
                     pool_ext, conv_ext, cnt_ref):
    seq_tile = pl.program_id(0)
    tm = x_ref.shape[0]

    @pl.when(seq_tile == 0)
    def _():
        pool_ext[0:POOL_HIST, :] = jnp.zeros((POOL_HIST, D_HALF), F32)
        conv_ext[0:CONV_HIST, :] = jnp.zeros((CONV_HIST, D_HALF), F32)
        cnt_ref[...] = jnp.zeros_like(cnt_ref)

    x = x_ref[...]
    mod = mod_ref[0, 0]
    h = _rms_mod(x, n1g_ref[...], mod[0:1], mod[1:2]).astype(BF16)
    z = jnp.dot(h, win_ref[...], preferred_element_type=F32)
    zp = z[:, :D_HALF]
    glu = z[:, D_HALF:2 * D_HALF] * _sigmoid(z[:, 2 * D_HALF:])
    pool_ext[POOL_HIST:POOL_HIST + tm, :] = zp
    conv_ext[CONV_HIST:CONV_HIST + tm, :] = glu

    row = lax.broadcasted_iota(jnp.int32, (tm, 1), 0)
    pos1 = (seq_tile * tm + row + 1).astype(F32)
    pool_out = []
    for g, w in enumerate(POOL_WINDOWS):
        cols = slice(g * POOL_GROUP, (g + 1) * POOL_GROUP)
        acc = pool_ext[:, cols]
        span = 1
        while span < w:
            acc = acc + pltpu.roll(acc, span, axis=0)
            span *= 2
        wsum = acc[POOL_HIST:POOL_HIST + tm]
        inv_cnt = 1.0 / jnp.minimum(pos1, float(w))
        diff = wsum * inv_cnt - zp[:, cols]
        po = jnp.dot(diff.astype(BF16), poolw_ref[g], preferred_element_type=F32)
        pool_out.append(po * pscale_ref[:, cols])

    convw = convw_ref[...]
    ext_rows = tm + V7X_SUBLANES
    conv = None
    for r in range(V7X_SUBLANES):
        vr = None
        for a in range(CONV_HIST // V7X_SUBLANES):
            lag = V7X_SUBLANES * a + r
            if lag >= CONF_KERNEL:
                continue
            k = CONF_KERNEL - 1 - lag
            start = CONV_HIST - V7X_SUBLANES - V7X_SUBLANES * a
            term = convw[k:k + 1, :] * conv_ext[start:start + ext_rows, :]
            vr = term if vr is None else vr + term
        if r:
            vr = pltpu.roll(vr, r, axis=0)
        conv = vr if conv is None else conv + vr
    conv = conv[V7X_SUBLANES:V7X_SUBLANES + tm] + convb_ref[...]
    conf = _silu(_layer_norm(conv, lng_ref[...], lnb_ref[...]))

    pool_ext[0:POOL_HIST, :] = zp[tm - POOL_HIST:tm]
    conv_ext[0:CONV_HIST, :] = glu[tm - CONV_HIST:tm]

    m = jnp.dot(conf.astype(BF16), wout_ref[D_HALF:, :], preferred_element_type=F32)
    for g in range(len(POOL_WINDOWS)):
        rows = slice(g * POOL_GROUP, (g + 1) * POOL_GROUP)
        m = m + jnp.dot(pool_out[g].astype(BF16), wout_ref[rows, :], preferred_element_type=F32)
    _finish_mixer(x, m, mod, n2g_ref, rwt_ref, rbias_ref,
                  x1_ref, h2_ref, eidx_ref, wsel_ref, rank_ref, counts_ref, cnt_ref)


def _mixer_cd_kernel(x_ref, y0_ref, y1_ref, wt_ref, modp_ref,
                     mod_ref, n1g_ref, n2g_ref, win_ref, sconvw_ref, lng_ref, lnb_ref,
                     ws_ref, bsf_ref, wout_ref, rwt_ref, rbias_ref,
                     x1_ref, h2_ref, eidx_ref, wsel_ref, rank_ref, counts_ref,
                     sconv_ext, cnt_ref):
    tm = x_ref.shape[0]

    @pl.when(pl.program_id(0) == 0)
    def _():
        sconv_ext[0:SCONV_HIST, :] = jnp.zeros((SCONV_HIST, D_HALF), F32)
        cnt_ref[...] = jnp.zeros_like(cnt_ref)

    x = _moe_residual(x_ref, y0_ref, y1_ref, wt_ref, modp_ref[0, 0][5:6])
    mod = mod_ref[0, 0]
    h = _rms_mod(x, n1g_ref[...], mod[0:1], mod[1:2]).astype(BF16)
    z = jnp.dot(h, win_ref[...], preferred_element_type=F32)
    bg = z[:, :D_HALF]
    ch = z[:, D_HALF:2 * D_HALF] * z[:, 2 * D_HALF:3 * D_HALF]
    zd = _gelu_tanh(z[:, 3 * D_HALF:])
    u = zd[:, :D_HALF]
    v = _layer_norm(zd[:, D_HALF:], lng_ref[...], lnb_ref[...])

    sconv_ext[SCONV_HIST:SCONV_HIST + tm, :] = ch
    sw = sconvw_ref[...]
    ext = sconv_ext[...]
    conv = sw[2:3, :] * ext
    conv = conv + sw[1:2, :] * pltpu.roll(ext, 1, axis=0)
    conv = conv + sw[0:1, :] * pltpu.roll(ext, 2, axis=0)
    sc_out = bg * conv[SCONV_HIST:SCONV_HIST + tm]
    sconv_ext[0:SCONV_HIST, :] = ch[tm - SCONV_HIST:tm]

    r_i = lax.broadcasted_iota(jnp.int32, (CHUNK, CHUNK), 0)
    c_i = lax.broadcasted_iota(jnp.int32, (CHUNK, CHUNK), 1)
    tril = c_i <= r_i
    wm = [jnp.where(tril, ws_ref[hd], 0.0).astype(BF16) for hd in range(GMLP_HEADS)]
    v_bf = v.astype(BF16)
    bsf = bsf_ref[...]
    gm_rows = []
    for n in range(tm // CHUNK):
        rows = slice(n * CHUNK, (n + 1) * CHUNK)
        heads = []
        for hd in range(GMLP_HEADS):
            cols = slice(hd * POOL_GROUP, (hd + 1) * POOL_GROUP)
            heads.append(jnp.dot(wm[hd], v_bf[rows, cols], preferred_element_type=F32))
        mixed = jnp.concatenate(heads, axis=1) + bsf
        gm_rows.append(u[rows] * mixed)
    gm_out = jnp.concatenate(gm_rows, axis=0)

    m = jnp.dot(sc_out.astype(BF16), wout_ref[:D_HALF, :], preferred_element_type=F32)
    m = m + jnp.dot(gm_out.astype(BF16), wout_ref[D_HALF:, :], preferred_element_type=F32)
    _finish_mixer(x, m, mod, n2g_ref, rwt_ref, rbias_ref,
                  x1_ref, h2_ref, eidx_ref, wsel_ref, rank_ref, counts_ref, cnt_ref)


def _const_spec(shape):
    nd = len(shape)
    return pl.BlockSpec(shape, lambda i: (0,) * nd)


def _mixer_call(kernel_fn, layer, batch, stream_inputs, stream_specs, mod4, n1g, n2g, weights,
                rwt, rbias, scratch, seq_len, name):
    n_tok = seq_len
    d = D_MODEL
    tm = MIX_TILE
    in_specs = stream_specs + [
        pl.BlockSpec((1, 1, 6, d), lambda i: (layer, batch, 0, 0)),
        _const_spec(n1g.shape),
        _const_spec(n2g.shape),
    ] + [_const_spec(w.shape) for w in weights] + [_const_spec(rwt.shape), _const_spec(rbias.shape)]
    out_specs = [
        pl.BlockSpec((tm, d), lambda i: (i, 0)),
        pl.BlockSpec((tm * ROW_CHUNKS, V7X_LANES), lambda i: (i, 0)),
        pl.BlockSpec((TOP_K, tm), lambda i: (0, i)),
        pl.BlockSpec((TOP_K, tm), lambda i: (0, i)),
        pl.BlockSpec((TOP_K, tm), lambda i: (0, i)),
        pl.BlockSpec((N_EXPERTS, V7X_LANES), lambda i: (0, 0)),
    ]
    out_shape = [
        jax.ShapeDtypeStruct((n_tok, d), F32),
        jax.ShapeDtypeStruct((n_tok * ROW_CHUNKS, V7X_LANES), U32),
        jax.ShapeDtypeStruct((TOP_K, n_tok), jnp.int32),
        jax.ShapeDtypeStruct((TOP_K, n_tok), F32),
        jax.ShapeDtypeStruct((TOP_K, n_tok), jnp.int32),
        jax.ShapeDtypeStruct((N_EXPERTS, V7X_LANES), jnp.int32),
    ]
    return pl.pallas_call(
        kernel_fn,
        grid=(n_tok // tm,),
        in_specs=in_specs,
        out_specs=out_specs,
        out_shape=out_shape,
        scratch_shapes=scratch + [pltpu.VMEM((N_EXPERTS, 1), F32)],
        compiler_params=pltpu.CompilerParams(
            dimension_semantics=("arbitrary",),
            vmem_limit_bytes=V7X_VMEM_LIMIT_BYTES),
        name=name,
    )(*stream_inputs, mod4, n1g, n2g, *weights, rwt, rbias)


def _combine_specs(n_tok):
    tm = MIX_TILE
    n_tiles = n_tok // tm
    return [
        pl.BlockSpec((tm, D_MODEL), lambda i: (i, 0)),
        pl.BlockSpec((tm * ROW_CHUNKS, V7X_LANES), lambda i: (i, 0)),
        pl.BlockSpec((tm * ROW_CHUNKS, V7X_LANES), lambda i: (n_tiles + i, 0)),
        pl.BlockSpec((tm, TOP_K), lambda i: (i, 0)),
    ]


def _sorted_positions(eidx, rank, counts):
    n_pairs = eidx.shape[0] * eidx.shape[1]
    n_rows = n_pairs + N_EXPERTS * MOE_TILE
    n_tiles = n_rows // MOE_TILE
    padded = (counts + MOE_TILE - 1) // MOE_TILE * MOE_TILE
    seg_end = jnp.cumsum(padded)
    seg_start = seg_end - padded
    expert_ids = jnp.arange(N_EXPERTS, dtype=jnp.int32)
    pos = jnp.sum(jnp.where(eidx[..., None] == expert_ids, seg_start, 0), axis=-1) + rank
    tile_row0 = jnp.arange(n_tiles, dtype=jnp.int32) * MOE_TILE
    tile_expert = jnp.minimum(
        jnp.sum((tile_row0[:, None] >= seg_end[None, :]).astype(jnp.int32), axis=1), N_EXPERTS - 1)
    n_used = (seg_end[N_EXPERTS - 1] // MOE_TILE).reshape(1)
    valid_end = seg_start + counts
    tile_valid_end = jnp.sum(
        jnp.where(tile_expert[:, None] == expert_ids, valid_end, 0), axis=-1)
    tile_valid = jnp.clip(tile_valid_end - tile_row0, 0, MOE_TILE)

    def of_tile(per_expert):
        return jnp.sum(jnp.where(tile_expert[:, None] == expert_ids, per_expert, 0), axis=-1)

    nonempty = (padded > 0).astype(jnp.int32)
    later_nonempty = (expert_ids[None, :] > expert_ids[:, None]) & (padded[None, :] > 0)
    next_expert = jnp.min(jnp.where(later_nonempty, expert_ids[None, :], N_EXPERTS), axis=1)
    next_expert = jnp.where(next_expert < N_EXPERTS, next_expert, -1)
    tile_used = tile_row0 < seg_end[N_EXPERTS - 1]
    tile_first = (tile_used & (of_tile(seg_start) == tile_row0)).astype(jnp.int32)
    tile_next = of_tile(next_expert)
    tile_slot = of_tile(jnp.cumsum(nonempty) - nonempty) % 2
    sched = tuple(a.astype(jnp.int32) for a in
                  (tile_expert, tile_valid, n_used, tile_first, tile_next, tile_slot))
    return pos.reshape(n_pairs).astype(jnp.int32), sched, n_rows


def _sc_move_rows(scatter, src, pos_flat, n_out_rows, name):
    info = plsc.get_sparse_core_info()
    n_workers = info.num_cores * info.num_subcores
    n_pairs = pos_flat.shape[0]
    n_src = src.shape[0]
    per_worker = n_pairs // n_workers
    n_chunks = per_worker // SC_CHUNK
    assert per_worker * n_workers == n_pairs and n_chunks * SC_CHUNK == per_worker
    assert n_src % per_worker == 0
    idx = pos_flat.reshape(n_workers, n_chunks, SC_CHUNK)
    mesh = plsc.VectorSubcoreMesh(core_axis_name="core", subcore_axis_name="subcore")

    @functools.partial(
        pl.kernel,
        out_type=jax.ShapeDtypeStruct((n_out_rows, ROW_CHUNKS, V7X_LANES), U32),
        mesh=mesh,
        scratch_types=[
            pltpu.VMEM((n_chunks, SC_CHUNK), jnp.int32),
            pltpu.VMEM((2, SC_CHUNK, ROW_CHUNKS, V7X_LANES), U32),
            pltpu.SemaphoreType.DMA((2,)),
            pltpu.SemaphoreType.DMA((2,)),
        ],
        name=name)
    def move(src_hbm, i_hbm, o_hbm, idx_v, buf, in_sem, out_sem):
        wid = lax.axis_index("subcore") * info.num_cores + lax.axis_index("core")
        base = wid * per_worker
        src_base = lax.rem(base, n_src)
        pltpu.sync_copy(i_hbm.at[wid], idx_v)

        def fetch(s, slot):
            if scatter:
                rows = src_hbm.at[pl.ds(src_base + s * SC_CHUNK, SC_CHUNK)]
            else:
                rows = src_hbm.at[idx_v.at[s]]
            return pltpu.make_async_copy(rows, buf.at[slot], in_sem.at[slot])

        def flush(s, slot):
            if scatter:
                rows = o_hbm.at[idx_v.at[s]]
            else:
                rows = o_hbm.at[pl.ds(base + s * SC_CHUNK, SC_CHUNK)]
            return pltpu.make_async_copy(buf.at[slot], rows, out_sem.at[slot])

        fetch(0, 0).start()
        for s in range(n_chunks):
            slot = s % 2
            fetch(s, slot).wait()
            flush(s, slot).start()
            if s + 1 < n_chunks:
                if s >= 1:
                    flush(s - 1, 1 - slot).wait()
                fetch(s + 1, 1 - slot).start()
        flush(n_chunks - 2, n_chunks % 2).wait()
        flush(n_chunks - 1, (n_chunks - 1) % 2).wait()

    return move(src, idx)


def _experts_kernel(layer, te_ref, tv_ref, nu_ref, first_ref, next_ref, slot_ref,
                    x_ref, wg_hbm, wu_hbm, wd_hbm, y_ref,
                    wg_f32, wu_f32, wd_f32, wg_bf, wu_bf, wd_bf, sems):
    j = pl.program_id(0)
    tm = x_ref.shape[0] // ROW_CHUNKS
    used = j < nu_ref[0]

    def weight_copies(expert, slot):
        pairs = ((wg_hbm, wg_f32), (wu_hbm, wu_f32), (wd_hbm, wd_f32))
        return [pltpu.make_async_copy(src.at[layer, expert], dst.at[slot], sems.at[slot, m])
                for m, (src, dst) in enumerate(pairs)]

    @pl.when(used & (first_ref[j] == 1))
    def _():
        slot = slot_ref[j]

        @pl.when(j == 0)
        def _():
            for cp in weight_copies(te_ref[j], slot):
                cp.start()

        for cp in weight_copies(te_ref[j], slot):
            cp.wait()
        wg_bf[...] = wg_f32[slot].astype(BF16)
        wu_bf[...] = wu_f32[slot].astype(BF16)
        wd_bf[...] = wd_f32[slot].astype(BF16)

        @pl.when(next_ref[j] >= 0)
        def _():
            for cp in weight_copies(next_ref[j], 1 - slot):
                cp.start()

    @pl.when(used)
    def _():
        row = lax.broadcasted_iota(jnp.int32, (tm, 1), 0)
        words = jnp.where(row < tv_ref[j], _load_words(x_ref, tm), jnp.uint32(0))
        h = _unpack_rows(words).astype(BF16)
        a = jnp.dot(h, wg_bf[...], preferred_element_type=F32)
        b = jnp.dot(h, wu_bf[...], preferred_element_type=F32)
        t = (_silu(a) * b).astype(BF16)
        _store_words(y_ref, _pack_rows(jnp.dot(t, wd_bf[...], preferred_element_type=F32)))

    @pl.when(jnp.logical_not(used))
    def _():
        y_ref[...] = jnp.zeros_like(y_ref)


def _experts_call(layer, rows, sched, w_gate, w_up, w_down):
    n_rows = rows.shape[0]
    tm = MOE_TILE
    d = D_MODEL
    rows2 = rows.reshape(n_rows * ROW_CHUNKS, V7X_LANES)
    n_sched = len(sched)
    grid_spec = pltpu.PrefetchScalarGridSpec(
        num_scalar_prefetch=n_sched,
        grid=(n_rows // tm,),
        in_specs=[
            pl.BlockSpec((tm * ROW_CHUNKS, V7X_LANES),
                         lambda j, te, tv, nu, *_: (jnp.minimum(j, nu[0] - 1), 0)),
            pl.BlockSpec(memory_space=pl.ANY),
            pl.BlockSpec(memory_space=pl.ANY),
            pl.BlockSpec(memory_space=pl.ANY),
        ],
        out_specs=pl.BlockSpec((tm * ROW_CHUNKS, V7X_LANES), lambda j, *_: (j, 0)),
        scratch_shapes=[
            pltpu.VMEM((2, d, D_EXPERT), F32),
            pltpu.VMEM((2, d, D_EXPERT), F32),
            pltpu.VMEM((2, D_EXPERT, d), F32),
            pltpu.VMEM((d, D_EXPERT), BF16),
            pltpu.VMEM((d, D_EXPERT), BF16),
            pltpu.VMEM((D_EXPERT, d), BF16),
            pltpu.SemaphoreType.DMA((2, 3)),
        ],
    )
    y = pl.pallas_call(
        functools.partial(_experts_kernel, layer),
        grid_spec=grid_spec,
        out_shape=jax.ShapeDtypeStruct((n_rows * ROW_CHUNKS, V7X_LANES), U32),
        compiler_params=pltpu.CompilerParams(
            dimension_semantics=("arbitrary",),
            vmem_limit_bytes=V7X_VMEM_LIMIT_BYTES),
        name="experts_l%d" % layer,
    )(*sched, rows2, w_gate, w_up, w_down)
    return y.reshape(n_rows, ROW_CHUNKS, V7X_LANES)


def _moe_rows(layer, mixer_outs, w_gate, w_up, w_down):
    plans = [_sorted_positions(eidx, rank, counts[:, 0])
             for (_, _, eidx, _, rank, counts) in mixer_outs]
    sorted_rows = []
    for (_, h_rows, eidx, _, _, _), (pos_flat, _, n_rows) in zip(mixer_outs, plans):
        n_tok = eidx.shape[1]
        sorted_rows.append(_sc_move_rows(
            True, h_rows.reshape(n_tok, ROW_CHUNKS, V7X_LANES), pos_flat, n_rows, "sc_scatter_rows"))
    y_sorted = [_experts_call(layer, rows, sched, w_gate, w_up, w_down)
                for rows, (_, sched, _) in zip(sorted_rows, plans)]
    y_pairs = []
    for y, (pos_flat, _, _) in zip(y_sorted, plans):
        n_pairs = pos_flat.shape[0]
        moved = _sc_move_rows(False, y, pos_flat, n_pairs, "sc_gather_rows")
        y_pairs.append(moved.reshape(n_pairs * ROW_CHUNKS, V7X_LANES))
    return y_pairs


def _final_kernel(x_ref, y0_ref, y1_ref, wt_ref, modp_ref, fg_ref, *rest):
    o_ref = rest[-1]
    y = _moe_residual(x_ref, y0_ref, y1_ref, wt_ref, modp_ref[0, 0][5:6])
    ms = jnp.mean(y * y, axis=-1, keepdims=True)
    o_ref[...] = y * lax.rsqrt(ms + EPS) * fg_ref[...]


def _final_call(layer, batch, n_batch, x, y_pairs, wt, mod4, fg, out_prev):
    seq_len, d = x.shape
    tm = MIX_TILE
    tiles_per_seq = seq_len // tm
    in_specs = _combine_specs(seq_len) + [
        pl.BlockSpec((1, 1, 6, d), lambda i: (layer, batch, 0, 0)),
        _const_spec(fg.shape),
    ]
    args = [x, y_pairs, y_pairs, wt, mod4, fg]
    aliases = {}
    if out_prev is not None:
        in_specs.append(pl.BlockSpec(memory_space=pl.ANY))
        aliases = {len(args): 0}
        args.append(out_prev)
    return pl.pallas_call(
        _final_kernel,
        grid=(tiles_per_seq,),
        in_specs=in_specs,
        out_specs=pl.BlockSpec((tm, d), lambda i: (batch * tiles_per_seq + i, 0)),
        out_shape=jax.ShapeDtypeStruct((n_batch * seq_len, d), F32),
        input_output_aliases=aliases,
        compiler_params=pltpu.CompilerParams(
            dimension_semantics=("arbitrary",),
            vmem_limit_bytes=V7X_VMEM_LIMIT_BYTES),
        name="final_norm",
    )(*args)


def kernel(x, c, norm1_g, norm2_g, ada_w, ada_b, ab_w_in, pool_w, pool_scale, conf_conv_w, conf_conv_b, conf_ln_g, conf_ln_b, ab_w_out, cd_w_in, sconv_w, gmlp_ln_g, gmlp_ln_b, gmlp_ws, gmlp_bs, cd_w_out, router_w, router_bias, exp_w_gate, exp_w_up, exp_w_down, final_g):
    bsz, seq_len, d = x.shape
    n_tok = bsz * seq_len
    tm = MIX_TILE
    tiles_per_seq = seq_len // tm
    xf = x.reshape(n_tok, d)

    mod = _ada_mod(c, ada_w, ada_b)
    mod4 = mod.reshape(mod.shape[0], bsz, 6, d)

    rw_hi = router_w.astype(BF16)
    rw_lo = (router_w - rw_hi.astype(F32)).astype(BF16)
    rwt = jnp.concatenate([rw_hi.T, rw_lo.T], axis=0)
    rbias = router_bias.reshape(N_EXPERTS, 1)
    fg = final_g.reshape(1, d)

    weights_ab = [
        ab_w_in[0].astype(BF16), pool_w[0].astype(BF16), pool_scale[0].reshape(1, D_HALF),
        conf_conv_w[0], conf_conv_b[0].reshape(1, D_HALF), conf_ln_g[0].reshape(1, D_HALF),
        conf_ln_b[0].reshape(1, D_HALF), ab_w_out[0].astype(BF16),
    ]
    scratch_ab = [pltpu.VMEM((POOL_HIST + tm, D_HALF), F32),
                  pltpu.VMEM((CONV_HIST + tm, D_HALF), F32)]
    bsf = jnp.repeat(gmlp_bs[0].T, POOL_GROUP, axis=1)
    weights_cd = [
        cd_w_in[0].astype(BF16), sconv_w[0], gmlp_ln_g[0].reshape(1, D_HALF),
        gmlp_ln_b[0].reshape(1, D_HALF), gmlp_ws[0], bsf, cd_w_out[0].astype(BF16),
    ]
    scratch_cd = [pltpu.VMEM((SCONV_HIST + tm, D_HALF), F32)]
    experts = (exp_w_gate, exp_w_up, exp_w_down)

    batches = range(bsz)
    stage_ab = []
    for b in batches:
        x_spec = pl.BlockSpec((tm, d), lambda i, b=b: (b * tiles_per_seq + i, 0))
        stage_ab.append(_mixer_call(
            _mixer_ab_kernel, 0, b, [xf], [x_spec], mod4, norm1_g[0:1], norm2_g[0:1],
            weights_ab, rwt, rbias, scratch_ab, seq_len, "mixer_ab"))
    y_pairs0 = _moe_rows(0, stage_ab, *experts)

    stage_cd = []
    for b in batches:
        x1, _, _, wsel0, _, _ = stage_ab[b]
        prev_mod_spec = pl.BlockSpec((1, 1, 6, d), lambda i, b=b: (0, b, 0, 0))
        stage_cd.append(_mixer_call(
            _mixer_cd_kernel, 1, b, [x1, y_pairs0[b], y_pairs0[b], wsel0.T, mod4],
            _combine_specs(seq_len) + [prev_mod_spec], mod4, norm1_g[1:2], norm2_g[1:2],
            weights_cd, rwt, rbias, scratch_cd, seq_len, "mixer_cd"))
    y_pairs1 = _moe_rows(1, stage_cd, *experts)

    out = None
    for b in batches:
        x3, _, _, wsel1, _, _ = stage_cd[b]
        out = _final_call(1, b, bsz, x3, y_pairs1[b], wsel1.T, mod4, fg, out)
    return out.reshape(bsz, seq_len, d)
```

```python
import functools

import jax
import jax.numpy as jnp
from jax import lax
from jax.experimental import pallas as pl
from jax.experimental.pallas import tpu as pltpu
from jax.experimental.pallas import tpu_sc as plsc

D_MODEL = 1024
EPS = 1e-6
POOL_WINDOWS = (2, 4, 8, 16)
POOL_GROUP = 128
D_HALF = 512
CONF_KERNEL = 31
SCONV_KERNEL = 3
CHUNK = 128
GMLP_HEADS = 4
N_EXPERTS = 16
N_GROUPS = 4
EXPERTS_PER_GROUP = 4
TOP_K = 2
D_EXPERT = 512

V7X_LANES = 128
V7X_SUBLANES = 8
V7X_VMEM_LIMIT_BYTES = 56 * 1024 * 1024

MIX_TILE = 512
MOE_TILE = 512
SC_CHUNK = 64
ROW_CHUNKS = D_MODEL // (2 * V7X_LANES)
CONV_HIST = 32
POOL_HIST = 16
SCONV_HIST = 8

BF16 = jnp.bfloat16
F32 = jnp.float32
U32 = jnp.uint32


def _rms_mod(x, g_row, shift_row, scale_row):
    ms = jnp.mean(x * x, axis=-1, keepdims=True)
    gain = g_row * (1.0 + scale_row)
    return (x * lax.rsqrt(ms + EPS)) * gain + shift_row


def _layer_norm(x, g_row, b_row):
    mu = jnp.mean(x, axis=-1, keepdims=True)
    xc = x - mu
    var = jnp.mean(xc * xc, axis=-1, keepdims=True)
    return xc * lax.rsqrt(var + EPS) * g_row + b_row


def _sigmoid(x):
    return 1.0 / (1.0 + jnp.exp(-x))


def _silu(x):
    return x * _sigmoid(x)


def _gelu_tanh(x):
    c = 0.7978845608028654
    return 0.5 * x * (1.0 + jnp.tanh(c * (x + 0.044715 * (x * x * x))))


def _shift_rows(x, r):
    n, c = x.shape
    if r == V7X_SUBLANES:
        return jnp.concatenate([x[:r], x[:n - r]], axis=0)
    g = x.reshape(n // V7X_SUBLANES, V7X_SUBLANES, c)
    rot = pltpu.roll(g, r, axis=1)
    prev = jnp.concatenate([rot[:1], rot[:-1]], axis=0)
    sub = lax.broadcasted_iota(jnp.int32, g.shape, 1)
    return jnp.where(sub < r, prev, rot).reshape(n, c)


def _load_words(ref, n_rows):
    return jnp.concatenate(
        [ref[pl.ds(c, n_rows, stride=ROW_CHUNKS), :] for c in range(ROW_CHUNKS)], axis=1)


def _store_words(ref, words):
    n_rows = words.shape[0]
    for c in range(ROW_CHUNKS):
        ref[pl.ds(c, n_rows, stride=ROW_CHUNKS), :] = words[:, c * V7X_LANES:(c + 1) * V7X_LANES]


def _pack_rows(val):
    bits = lax.bitcast_convert_type(val.astype(BF16).astype(F32), U32)
    half = val.shape[1] // 2
    return (bits[:, :half] & jnp.uint32(0xFFFF0000)) | (bits[:, half:] >> 16)


def _unpack_rows(words):
    hi = lax.bitcast_convert_type(words & jnp.uint32(0xFFFF0000), F32)
    lo = lax.bitcast_convert_type(words << 16, F32)
    return jnp.concatenate([hi, lo], axis=1)


def _ada_kernel(ct_ref, w_ref, b_ref, o_ref):
    ct = ct_ref[...]
    cond = _silu(ct)
    w = w_ref[0]
    nb = ct.shape[1]
    for b in range(nb):
        col = cond[:, b:b + 1]
        o_ref[0, b:b + 1, :] = jnp.sum(col * w, axis=0, keepdims=True) + b_ref[0]


def _ada_mod(c, ada_w, ada_b):
    depth, d, six_d = ada_w.shape
    bsz = c.shape[0]
    nb = D_MODEL
    return pl.pallas_call(
        _ada_kernel,
        grid=(depth, six_d // nb),
        in_specs=[
            pl.BlockSpec((d, bsz), lambda l, j: (0, 0)),
            pl.BlockSpec((1, d, nb), lambda l, j: (l, 0, j)),
            pl.BlockSpec((1, 1, nb), lambda l, j: (l, 0, j)),
        ],
        out_specs=pl.BlockSpec((1, bsz, nb), lambda l, j: (l, 0, j)),
        out_shape=jax.ShapeDtypeStruct((depth, bsz, six_d), F32),
        compiler_params=pltpu.CompilerParams(
            dimension_semantics=("arbitrary", "arbitrary"),
            vmem_limit_bytes=V7X_VMEM_LIMIT_BYTES),
        name="ada_mod",
    )(c.T, ada_w, ada_b.reshape(depth, 1, six_d))


def _route(h2_bf, rwt_ref, rbias_ref, eidx_ref, wsel_ref, rank_ref, counts_ref, cnt_ref):
    nt = (((1,), (1,)), ((), ()))
    r = lax.dot_general(rwt_ref[...], h2_bf, nt, preferred_element_type=F32)
    logits = r[:N_EXPERTS] + r[N_EXPERTS:]
    m = jnp.max(logits, axis=0, keepdims=True)
    ex = jnp.exp(logits - m)
    probs = ex / jnp.sum(ex, axis=0, keepdims=True)
    sel = probs + rbias_ref[...]
    s = [sel[e:e + 1] for e in range(N_EXPERTS)]
    p = [probs[e:e + 1] for e in range(N_EXPERTS)]
    best = None
    gi = None
    for g in range(N_GROUPS):
        a, b, c, d = s[4 * g:4 * g + 4]
        hi1, lo1 = jnp.maximum(a, b), jnp.minimum(a, b)
        hi2, lo2 = jnp.maximum(c, d), jnp.minimum(c, d)
        top1 = jnp.maximum(hi1, hi2)
        top2 = jnp.maximum(jnp.minimum(hi1, hi2), jnp.maximum(lo1, lo2))
        score = top1 + top2
        if g == 0:
            best, gi = score, jnp.zeros(score.shape, jnp.int32)
        else:
            upd = score > best
            gi = jnp.where(upd, g, gi)
            best = jnp.where(upd, score, best)
    v, q = [], []
    for j in range(EXPERTS_PER_GROUP):
        vj, qj = s[j], p[j]
        for g in range(1, N_GROUPS):
            pick = gi == g
            vj = jnp.where(pick, s[4 * g + j], vj)
            qj = jnp.where(pick, p[4 * g + j], qj)
        v.append(vj)
        q.append(qj)
    i1 = jnp.zeros(gi.shape, jnp.int32)
    m1 = v[0]
    for j in range(1, EXPERTS_PER_GROUP):
        upd = v[j] > m1
        i1 = jnp.where(upd, j, i1)
        m1 = jnp.where(upd, v[j], m1)
    i2 = jnp.zeros(gi.shape, jnp.int32)
    m2 = jnp.full(m1.shape, -jnp.inf, F32)
    for j in range(EXPERTS_PER_GROUP):
        cand = (i1 != j) & (v[j] > m2)
        i2 = jnp.where(cand, j, i2)
        m2 = jnp.where(cand, v[j], m2)
    pa = q[0]
    pb = q[0]
    for j in range(1, EXPERTS_PER_GROUP):
        pa = jnp.where(i1 == j, q[j], pa)
        pb = jnp.where(i2 == j, q[j], pb)
    tot = pa + pb
    e0 = gi * EXPERTS_PER_GROUP + i1
    e1 = gi * EXPERTS_PER_GROUP + i2
    t = h2_bf.shape[0]
    eidx_ref[0:1, :] = e0
    eidx_ref[1:2, :] = e1
    wsel_ref[0:1, :] = pa / tot
    wsel_ref[1:2, :] = pb / tot

    e_iota = lax.broadcasted_iota(jnp.int32, (N_EXPERTS, t), 0)
    oh0 = e_iota == e0
    oh1 = e_iota == e1
    both = jnp.where(oh0 | oh1, 1.0, 0.0)
    r_i = lax.broadcasted_iota(jnp.int32, (V7X_LANES, V7X_LANES), 0)
    c_i = lax.broadcasted_iota(jnp.int32, (V7X_LANES, V7X_LANES), 1)
    before = jnp.where(r_i < c_i, 1.0, 0.0).astype(BF16)
    run = cnt_ref[...]
    rank0, rank1 = [], []
    for blk in range(t // V7X_LANES):
        lanes = slice(blk * V7X_LANES, (blk + 1) * V7X_LANES)
        b = both[:, lanes]
        pre = jnp.dot(b.astype(BF16), before, preferred_element_type=F32) + run
        rank0.append(jnp.sum(jnp.where(oh0[:, lanes], pre, 0.0), axis=0, keepdims=True))
        rank1.append(jnp.sum(jnp.where(oh1[:, lanes], pre, 0.0), axis=0, keepdims=True))
        run = run + jnp.sum(b, axis=1, keepdims=True)
    cnt_ref[...] = run
    rank_ref[0:1, :] = jnp.concatenate(rank0, axis=1).astype(jnp.int32)
    rank_ref[1:2, :] = jnp.concatenate(rank1, axis=1).astype(jnp.int32)
    counts_ref[...] = jnp.broadcast_to(run, counts_ref.shape).astype(jnp.int32)


def _finish_mixer(x, m, mod, n2g_ref, rwt_ref, rbias_ref,
                  x1_ref, h2_ref, eidx_ref, wsel_ref, rank_ref, counts_ref, cnt_ref):
    x1 = x + mod[2:3] * m
    x1_ref[...] = x1
    h2 = _rms_mod(x1, n2g_ref[...], mod[3:4], mod[4:5])
    h2_bf = h2.astype(BF16)
    _store_words(h2_ref, _pack_rows(h2))
    _route(h2_bf, rwt_ref, rbias_ref, eidx_ref, wsel_ref, rank_ref, counts_ref, cnt_ref)


def _moe_residual(x_ref, y0_ref, y1_ref, wt_ref, g2_row):
    tm = x_ref.shape[0]
    wt = wt_ref[...]
    y0 = _unpack_rows(_load_words(y0_ref, tm))
    y1 = _unpack_rows(_load_words(y1_ref, tm))
    y = wt[:, 0:1] * y0 + wt[:, 1:2] * y1
    return x_ref[...] + g2_row * y


def _mixer_ab_kernel(x_ref, mod_ref, n1g_ref, n2g_ref, win_ref, poolw_ref, pscale_ref,
                     convw_ref, convb_ref, lng_ref, lnb_ref, wout_ref, rwt_ref, rbias_ref,
                     x1_ref, h2_ref, eidx_ref, wsel_ref, rank_ref, counts_ref,
                     pool_ext, conv_ext, cnt_ref):
    seq_tile = pl.program_id(0)
    tm = x_ref.shape[0]

    @pl.when(seq_tile == 0)
    def _():
        pool_ext[0:POOL_HIST, :] = jnp.zeros((POOL_HIST, D_HALF), F32)
        conv_ext[0:CONV_HIST, :] = jnp.zeros((CONV_HIST, D_HALF), F32)
        cnt_ref[...] = jnp.zeros_like(cnt_ref)

    x = x_ref[...]
    mod = mod_ref[0, 0]
    h = _rms_mod(x, n1g_ref[...], mod[0:1], mod[1:2]).astype(BF16)
    z = jnp.dot(h, win_ref[...], preferred_element_type=F32)
    zp = z[:, :D_HALF]
    glu = z[:, D_HALF:2 * D_HALF] * _sigmoid(z[:, 2 * D_HALF:])
    pool_ext[POOL_HIST:POOL_HIST + tm, :] = zp
    conv_ext[CONV_HIST:CONV_HIST + tm, :] = glu

    row = lax.broadcasted_iota(jnp.int32, (tm, 1), 0)
    pos1 = (seq_tile * tm + row + 1).astype(F32)
    pool_out = []
    for g, w in enumerate(POOL_WINDOWS):
        cols = slice(g * POOL_GROUP, (g + 1) * POOL_GROUP)
        acc = pool_ext[:, cols]
        span = 1
        while span < w:
            acc = acc + _shift_rows(acc, span)
            span *= 2
        wsum = acc[POOL_HIST:POOL_HIST + tm]
        inv_cnt = 1.0 / jnp.minimum(pos1, float(w))
        diff = wsum * inv_cnt - zp[:, cols]
        po = jnp.dot(diff.astype(BF16), poolw_ref[g], preferred_element_type=F32)
        pool_out.append(po * pscale_ref[:, cols])

    convw = convw_ref[...]
    ext_rows = tm + V7X_SUBLANES
    conv = None
    for r in range(V7X_SUBLANES):
        vr = None
        for a in range(CONV_HIST // V7X_SUBLANES):
            lag = V7X_SUBLANES * a + r
            if lag >= CONF_KERNEL:
                continue
            k = CONF_KERNEL - 1 - lag
            start = CONV_HIST - V7X_SUBLANES - V7X_SUBLANES * a
            term = convw[k:k + 1, :] * conv_ext[start:start + ext_rows, :]
            vr = term if vr is None else vr + term
        if r:
            vr = _shift_rows(vr, r)
        conv = vr if conv is None else conv + vr
    conv = conv[V7X_SUBLANES:V7X_SUBLANES + tm] + convb_ref[...]
    conf = _silu(_layer_norm(conv, lng_ref[...], lnb_ref[...]))

    pool_ext[0:POOL_HIST, :] = zp[tm - POOL_HIST:tm]
    conv_ext[0:CONV_HIST, :] = glu[tm - CONV_HIST:tm]

    m = jnp.dot(conf.astype(BF16), wout_ref[D_HALF:, :], preferred_element_type=F32)
    for g in range(len(POOL_WINDOWS)):
        rows = slice(g * POOL_GROUP, (g + 1) * POOL_GROUP)
        m = m + jnp.dot(pool_out[g].astype(BF16), wout_ref[rows, :], preferred_element_type=F32)
    _finish_mixer(x, m, mod, n2g_ref, rwt_ref, rbias_ref,
                  x1_ref, h2_ref, eidx_ref, wsel_ref, rank_ref, counts_ref, cnt_ref)


def _mixer_cd_kernel(x_ref, y0_ref, y1_ref, wt_ref, modp_ref,
                     mod_ref, n1g_ref, n2g_ref, win_ref, sconvw_ref, lng_ref, lnb_ref,
                     ws_ref, bsf_ref, wout_ref, rwt_ref, rbias_ref,
                     x1_ref, h2_ref, eidx_ref, wsel_ref, rank_ref, counts_ref,
                     sconv_ext, cnt_ref):
    tm = x_ref.shape[0]

    @pl.when(pl.program_id(0) == 0)
    def _():
        sconv_ext[0:SCONV_HIST, :] = jnp.zeros((SCONV_HIST, D_HALF), F32)
        cnt_ref[...] = jnp.zeros_like(cnt_ref)

    x = _moe_residual(x_ref, y0_ref, y1_ref, wt_ref, modp_ref[0, 0][5:6])
    mod = mod_ref[0, 0]
    h = _rms_mod(x, n1g_ref[...], mod[0:1], mod[1:2]).astype(BF16)
    def proj(lo, hi):
        return jnp.dot(h, win_ref[:, lo:hi], preferred_element_type=F32)

    v = _layer_norm(_gelu_tanh(proj(4 * D_HALF, 5 * D_HALF)), lng_ref[...], lnb_ref[...])
    u = _gelu_tanh(proj(3 * D_HALF, 4 * D_HALF))
    ch = proj(D_HALF, 2 * D_HALF) * proj(2 * D_HALF, 3 * D_HALF)
    bg = proj(0, D_HALF)

    sconv_ext[SCONV_HIST:SCONV_HIST + tm, :] = ch
    sw = sconvw_ref[...]
    ext = sconv_ext[...]
    conv = sw[2:3, :] * ext
    conv = conv + sw[1:2, :] * _shift_rows(ext, 1)
    conv = conv + sw[0:1, :] * _shift_rows(ext, 2)
    sc_out = bg * conv[SCONV_HIST:SCONV_HIST + tm]
    sconv_ext[0:SCONV_HIST, :] = ch[tm - SCONV_HIST:tm]

    r_i = lax.broadcasted_iota(jnp.int32, (CHUNK, CHUNK), 0)
    c_i = lax.broadcasted_iota(jnp.int32, (CHUNK, CHUNK), 1)
    tril = c_i <= r_i
    wm = [jnp.where(tril, ws_ref[hd], 0.0).astype(BF16) for hd in range(GMLP_HEADS)]
    v_bf = v.astype(BF16)
    bsf = bsf_ref[...]
    gm_rows = []
    for n in range(tm // CHUNK):
        rows = slice(n * CHUNK, (n + 1) * CHUNK)
        heads = []
        for hd in range(GMLP_HEADS):
            cols = slice(hd * POOL_GROUP, (hd + 1) * POOL_GROUP)
            heads.append(jnp.dot(wm[hd], v_bf[rows, cols], preferred_element_type=F32))
        mixed = jnp.concatenate(heads, axis=1) + bsf
        gm_rows.append(u[rows] * mixed)
    gm_out = jnp.concatenate(gm_rows, axis=0)

    m = jnp.dot(sc_out.astype(BF16), wout_ref[:D_HALF, :], preferred_element_type=F32)
    m = m + jnp.dot(gm_out.astype(BF16), wout_ref[D_HALF:, :], preferred_element_type=F32)
    _finish_mixer(x, m, mod, n2g_ref, rwt_ref, rbias_ref,
                  x1_ref, h2_ref, eidx_ref, wsel_ref, rank_ref, counts_ref, cnt_ref)


def _const_spec(shape):
    nd = len(shape)
    return pl.BlockSpec(shape, lambda i: (0,) * nd)


def _mixer_call(kernel_fn, layer, batch, stream_inputs, stream_specs, mod4, n1g, n2g, weights,
                rwt, rbias, scratch, seq_len, name):
    n_tok = seq_len
    d = D_MODEL
    tm = MIX_TILE
    in_specs = stream_specs + [
        pl.BlockSpec((1, 1, 6, d), lambda i: (layer, batch, 0, 0)),
        _const_spec(n1g.shape),
        _const_spec(n2g.shape),
    ] + [_const_spec(w.shape) for w in weights] + [_const_spec(rwt.shape), _const_spec(rbias.shape)]
    out_specs = [
        pl.BlockSpec((tm, d), lambda i: (i, 0)),
        pl.BlockSpec((tm * ROW_CHUNKS, V7X_LANES), lambda i: (i, 0)),
        pl.BlockSpec((TOP_K, tm), lambda i: (0, i)),
        pl.BlockSpec((TOP_K, tm), lambda i: (0, i)),
        pl.BlockSpec((TOP_K, tm), lambda i: (0, i)),
        pl.BlockSpec((N_EXPERTS, V7X_LANES), lambda i: (0, 0)),
    ]
    out_shape = [
        jax.ShapeDtypeStruct((n_tok, d), F32),
        jax.ShapeDtypeStruct((n_tok * ROW_CHUNKS, V7X_LANES), U32),
        jax.ShapeDtypeStruct((TOP_K, n_tok), jnp.int32),
        jax.ShapeDtypeStruct((TOP_K, n_tok), F32),
        jax.ShapeDtypeStruct((TOP_K, n_tok), jnp.int32),
        jax.ShapeDtypeStruct((N_EXPERTS, V7X_LANES), jnp.int32),
    ]
    return pl.pallas_call(
        kernel_fn,
        grid=(n_tok // tm,),
        in_specs=in_specs,
        out_specs=out_specs,
        out_shape=out_shape,
        scratch_shapes=scratch + [pltpu.VMEM((N_EXPERTS, 1), F32)],
        compiler_params=pltpu.CompilerParams(
            dimension_semantics=("arbitrary",),
            vmem_limit_bytes=V7X_VMEM_LIMIT_BYTES),
        name=name,
    )(*stream_inputs, mod4, n1g, n2g, *weights, rwt, rbias)


def _combine_specs(n_tok):
    tm = MIX_TILE
    n_tiles = n_tok // tm
    return [
        pl.BlockSpec((tm, D_MODEL), lambda i: (i, 0)),
        pl.BlockSpec((tm * ROW_CHUNKS, V7X_LANES), lambda i: (i, 0)),
        pl.BlockSpec((tm * ROW_CHUNKS, V7X_LANES), lambda i: (n_tiles + i, 0)),
        pl.BlockSpec((tm, TOP_K), lambda i: (i, 0)),
    ]


def _sorted_positions(eidx, rank, counts):
    n_pairs = eidx.shape[0] * eidx.shape[1]
    n_rows = n_pairs + N_EXPERTS * MOE_TILE
    n_tiles = n_rows // MOE_TILE
    padded = (counts + MOE_TILE - 1) // MOE_TILE * MOE_TILE
    seg_end = jnp.cumsum(padded)
    seg_start = seg_end - padded
    expert_ids = jnp.arange(N_EXPERTS, dtype=jnp.int32)
    pos = jnp.sum(jnp.where(eidx[..., None] == expert_ids, seg_start, 0), axis=-1) + rank
    tile_row0 = jnp.arange(n_tiles, dtype=jnp.int32) * MOE_TILE
    tile_expert = jnp.minimum(
        jnp.sum((tile_row0[:, None] >= seg_end[None, :]).astype(jnp.int32), axis=1), N_EXPERTS - 1)
    n_used = (seg_end[N_EXPERTS - 1] // MOE_TILE).reshape(1)
    valid_end = seg_start + counts
    tile_valid_end = jnp.sum(
        jnp.where(tile_expert[:, None] == expert_ids, valid_end, 0), axis=-1)
    tile_valid = jnp.clip(tile_valid_end - tile_row0, 0, MOE_TILE)

    def of_tile(per_expert):
        return jnp.sum(jnp.where(tile_expert[:, None] == expert_ids, per_expert, 0), axis=-1)

    nonempty = (padded > 0).astype(jnp.int32)
    later_nonempty = (expert_ids[None, :] > expert_ids[:, None]) & (padded[None, :] > 0)
    next_expert = jnp.min(jnp.where(later_nonempty, expert_ids[None, :], N_EXPERTS), axis=1)
    next_expert = jnp.where(next_expert < N_EXPERTS, next_expert, -1)
    tile_used = tile_row0 < seg_end[N_EXPERTS - 1]
    tile_first = (tile_used & (of_tile(seg_start) == tile_row0)).astype(jnp.int32)
    tile_next = of_tile(next_expert)
    tile_slot = of_tile(jnp.cumsum(nonempty) - nonempty) % 2
    sched = tuple(a.astype(jnp.int32) for a in
                  (tile_expert, tile_valid, n_used, tile_first, tile_next, tile_slot))
    return pos.reshape(n_pairs).astype(jnp.int32), sched, n_rows


def _sc_move_rows(scatter, src, pos_flat, n_out_rows, name):
    info = plsc.get_sparse_core_info()
    n_workers = info.num_cores * info.num_subcores
    n_pairs = pos_flat.shape[0]
    n_src = src.shape[0]
    per_worker = n_pairs // n_workers
    n_chunks = per_worker // SC_CHUNK
    assert per_worker * n_workers == n_pairs and n_chunks * SC_CHUNK == per_worker
    assert n_src % per_worker == 0
    idx = pos_flat.reshape(n_workers, n_chunks, SC_CHUNK)
    mesh = plsc.VectorSubcoreMesh(core_axis_name="core", subcore_axis_name="subcore")

    @functools.partial(
        pl.kernel,
        out_type=jax.ShapeDtypeStruct((n_out_rows, ROW_CHUNKS, V7X_LANES), U32),
        mesh=mesh,
        scratch_types=[
            pltpu.VMEM((n_chunks, SC_CHUNK), jnp.int32),
            pltpu.VMEM((2, SC_CHUNK, ROW_CHUNKS, V7X_LANES), U32),
            pltpu.SemaphoreType.DMA((2,)),
            pltpu.SemaphoreType.DMA((2,)),
        ],
        name=name)
    def move(src_hbm, i_hbm, o_hbm, idx_v, buf, in_sem, out_sem):
        wid = lax.axis_index("subcore") * info.num_cores + lax.axis_index("core")
        base = wid * per_worker
        src_base = lax.rem(base, n_src)
        pltpu.sync_copy(i_hbm.at[wid], idx_v)

        def fetch(s, slot):
            if scatter:
                rows = src_hbm.at[pl.ds(src_base + s * SC_CHUNK, SC_CHUNK)]
            else:
                rows = src_hbm.at[idx_v.at[s]]
            return pltpu.make_async_copy(rows, buf.at[slot], in_sem.at[slot])

        def flush(s, slot):
            if scatter:
                rows = o_hbm.at[idx_v.at[s]]
            else:
                rows = o_hbm.at[pl.ds(base + s * SC_CHUNK, SC_CHUNK)]
            return pltpu.make_async_copy(buf.at[slot], rows, out_sem.at[slot])

        fetch(0, 0).start()
        for s in range(n_chunks):
            slot = s % 2
            fetch(s, slot).wait()
            flush(s, slot).start()
            if s + 1 < n_chunks:
                if s >= 1:
                    flush(s - 1, 1 - slot).wait()
                fetch(s + 1, 1 - slot).start()
        flush(n_chunks - 2, n_chunks % 2).wait()
        flush(n_chunks - 1, (n_chunks - 1) % 2).wait()

    return move(src, idx)


def _experts_kernel(layer, te_ref, tv_ref, nu_ref, first_ref, next_ref, slot_ref,
                    x_ref, wg_hbm, wu_hbm, wd_hbm, y_ref,
                    wg_f32, wu_f32, wd_f32, wg_bf, wu_bf, wd_bf, sems):
    j = pl.program_id(0)
    tm = x_ref.shape[0] // ROW_CHUNKS
    used = j < nu_ref[0]

    def weight_copies(expert, slot):
        pairs = ((wg_hbm, wg_f32), (wu_hbm, wu_f32), (wd_hbm, wd_f32))
        return [pltpu.make_async_copy(src.at[layer, expert], dst.at[slot], sems.at[slot, m])
                for m, (src, dst) in enumerate(pairs)]

    @pl.when(used & (first_ref[j] == 1))
    def _():
        slot = slot_ref[j]

        @pl.when(j == 0)
        def _():
            for cp in weight_copies(te_ref[j], slot):
                cp.start()

        for cp in weight_copies(te_ref[j], slot):
            cp.wait()
        wg_bf[...] = wg_f32[slot].astype(BF16)
        wu_bf[...] = wu_f32[slot].astype(BF16)
        wd_bf[...] = wd_f32[slot].astype(BF16)

        @pl.when(next_ref[j] >= 0)
        def _():
            for cp in weight_copies(next_ref[j], 1 - slot):
                cp.start()

    @pl.when(used)
    def _():
        row = lax.broadcasted_iota(jnp.int32, (tm, 1), 0)
        words = jnp.where(row < tv_ref[j], _load_words(x_ref, tm), jnp.uint32(0))
        h = _unpack_rows(words).astype(BF16)
        a = jnp.dot(h, wg_bf[...], preferred_element_type=F32)
        b = jnp.dot(h, wu_bf[...], preferred_element_type=F32)
        t = (_silu(a) * b).astype(BF16)
        _store_words(y_ref, _pack_rows(jnp.dot(t, wd_bf[...], preferred_element_type=F32)))

    @pl.when(jnp.logical_not(used))
    def _():
        y_ref[...] = jnp.zeros_like(y_ref)


def _experts_call(layer, rows, sched, w_gate, w_up, w_down):
    n_rows = rows.shape[0]
    tm = MOE_TILE
    d = D_MODEL
    rows2 = rows.reshape(n_rows * ROW_CHUNKS, V7X_LANES)
    n_sched = len(sched)
    grid_spec = pltpu.PrefetchScalarGridSpec(
        num_scalar_prefetch=n_sched,
        grid=(n_rows // tm,),
        in_specs=[
            pl.BlockSpec((tm * ROW_CHUNKS, V7X_LANES),
                         lambda j, te, tv, nu, *_: (jnp.minimum(j, nu[0] - 1), 0)),
            pl.BlockSpec(memory_space=pl.ANY),
            pl.BlockSpec(memory_space=pl.ANY),
            pl.BlockSpec(memory_space=pl.ANY),
        ],
        out_specs=pl.BlockSpec((tm * ROW_CHUNKS, V7X_LANES), lambda j, *_: (j, 0)),
        scratch_shapes=[
            pltpu.VMEM((2, d, D_EXPERT), F32),
            pltpu.VMEM((2, d, D_EXPERT), F32),
            pltpu.VMEM((2, D_EXPERT, d), F32),
            pltpu.VMEM((d, D_EXPERT), BF16),
            pltpu.VMEM((d, D_EXPERT), BF16),
            pltpu.VMEM((D_EXPERT, d), BF16),
            pltpu.SemaphoreType.DMA((2, 3)),
        ],
    )
    y = pl.pallas_call(
        functools.partial(_experts_kernel, layer),
        grid_spec=grid_spec,
        out_shape=jax.ShapeDtypeStruct((n_rows * ROW_CHUNKS, V7X_LANES), U32),
        compiler_params=pltpu.CompilerParams(
            dimension_semantics=("arbitrary",),
            vmem_limit_bytes=V7X_VMEM_LIMIT_BYTES),
        name="experts_l%d" % layer,
    )(*sched, rows2, w_gate, w_up, w_down)
    return y.reshape(n_rows, ROW_CHUNKS, V7X_LANES)


def _moe_rows(layer, mixer_outs, w_gate, w_up, w_down):
    plans = [_sorted_positions(eidx, rank, counts[:, 0])
             for (_, _, eidx, _, rank, counts) in mixer_outs]
    sorted_rows = []
    for (_, h_rows, eidx, _, _, _), (pos_flat, _, n_rows) in zip(mixer_outs, plans):
        n_tok = eidx.shape[1]
        sorted_rows.append(_sc_move_rows(
            True, h_rows.reshape(n_tok, ROW_CHUNKS, V7X_LANES), pos_flat, n_rows, "sc_scatter_rows"))
    y_sorted = [_experts_call(layer, rows, sched, w_gate, w_up, w_down)
                for rows, (_, sched, _) in zip(sorted_rows, plans)]
    y_pairs = []
    for y, (pos_flat, _, _) in zip(y_sorted, plans):
        n_pairs = pos_flat.shape[0]
        moved = _sc_move_rows(False, y, pos_flat, n_pairs, "sc_gather_rows")
        y_pairs.append(moved.reshape(n_pairs * ROW_CHUNKS, V7X_LANES))
    return y_pairs


def _final_kernel(x_ref, y0_ref, y1_ref, wt_ref, modp_ref, fg_ref, *rest):
    o_ref = rest[-1]
    y = _moe_residual(x_ref, y0_ref, y1_ref, wt_ref, modp_ref[0, 0][5:6])
    ms = jnp.mean(y * y, axis=-1, keepdims=True)
    o_ref[...] = y * lax.rsqrt(ms + EPS) * fg_ref[...]


def _final_call(layer, batch, n_batch, x, y_pairs, wt, mod4, fg, out_prev):
    seq_len, d = x.shape
    tm = MIX_TILE
    tiles_per_seq = seq_len // tm
    in_specs = _combine_specs(seq_len) + [
        pl.BlockSpec((1, 1, 6, d), lambda i: (layer, batch, 0, 0)),
        _const_spec(fg.shape),
    ]
    args = [x, y_pairs, y_pairs, wt, mod4, fg]
    aliases = {}
    if out_prev is not None:
        in_specs.append(pl.BlockSpec(memory_space=pl.ANY))
        aliases = {len(args): 0}
        args.append(out_prev)
    return pl.pallas_call(
        _final_kernel,
        grid=(tiles_per_seq,),
        in_specs=in_specs,
        out_specs=pl.BlockSpec((tm, d), lambda i: (batch * tiles_per_seq + i, 0)),
        out_shape=jax.ShapeDtypeStruct((n_batch * seq_len, d), F32),
        input_output_aliases=aliases,
        compiler_params=pltpu.CompilerParams(
            dimension_semantics=("arbitrary",),
            vmem_limit_bytes=V7X_VMEM_LIMIT_BYTES),
        name="final_norm",
    )(*args)


def kernel(x, c, norm1_g, norm2_g, ada_w, ada_b, ab_w_in, pool_w, pool_scale, conf_conv_w, conf_conv_b, conf_ln_g, conf_ln_b, ab_w_out, cd_w_in, sconv_w, gmlp_ln_g, gmlp_ln_b, gmlp_ws, gmlp_bs, cd_w_out, router_w, router_bias, exp_w_gate, exp_w_up, exp_w_down, final_g):
    bsz, seq_len, d = x.shape
    n_tok = bsz * seq_len
    tm = MIX_TILE
    tiles_per_seq = seq_len // tm
    xf = x.reshape(n_tok, d)

    mod = _ada_mod(c, ada_w, ada_b)
    mod4 = mod.reshape(mod.shape[0], bsz, 6, d)

    rw_hi = router_w.astype(BF16)
    rw_lo = (router_w - rw_hi.astype(F32)).astype(BF16)
    rwt = jnp.concatenate([rw_hi.T, rw_lo.T], axis=0)
    rbias = router_bias.reshape(N_EXPERTS, 1)
    fg = final_g.reshape(1, d)

    weights_ab = [
        ab_w_in[0].astype(BF16), pool_w[0].astype(BF16), pool_scale[0].reshape(1, D_HALF),
        conf_conv_w[0], conf_conv_b[0].reshape(1, D_HALF), conf_ln_g[0].reshape(1, D_HALF),
        conf_ln_b[0].reshape(1, D_HALF), ab_w_out[0].astype(BF16),
    ]
    scratch_ab = [pltpu.VMEM((POOL_HIST + tm, D_HALF), F32),
                  pltpu.VMEM((CONV_HIST + tm, D_HALF), F32)]
    bsf = jnp.repeat(gmlp_bs[0].T, POOL_GROUP, axis=1)
    weights_cd = [
        cd_w_in[0].astype(BF16), sconv_w[0], gmlp_ln_g[0].reshape(1, D_HALF),
        gmlp_ln_b[0].reshape(1, D_HALF), gmlp_ws[0], bsf, cd_w_out[0].astype(BF16),
    ]
    scratch_cd = [pltpu.VMEM((SCONV_HIST + tm, D_HALF), F32)]
    experts = (exp_w_gate, exp_w_up, exp_w_down)

    batches = range(bsz)
    stage_ab = []
    for b in batches:
        x_spec = pl.BlockSpec((tm, d), lambda i, b=b: (b * tiles_per_seq + i, 0))
        stage_ab.append(_mixer_call(
            _mixer_ab_kernel, 0, b, [xf], [x_spec], mod4, norm1_g[0:1], norm2_g[0:1],
            weights_ab, rwt, rbias, scratch_ab, seq_len, "mixer_ab"))
    y_pairs0 = _moe_rows(0, stage_ab, *experts)

    stage_cd = []
    for b in batches:
        x1, _, _, wsel0, _, _ = stage_ab[b]
        prev_mod_spec = pl.BlockSpec((1, 1, 6, d), lambda i, b=b: (0, b, 0, 0))
        stage_cd.append(_mixer_call(
            _mixer_cd_kernel, 1, b, [x1, y_pairs0[b], y_pairs0[b], wsel0.T, mod4],
            _combine_specs(seq_len) + [prev_mod_spec], mod4, norm1_g[1:2], norm2_g[1:2],
            weights_cd, rwt, rbias, scratch_cd, seq_len, "mixer_cd"))
    y_pairs1 = _moe_rows(1, stage_cd, *experts)

    out = None
    for b in batches:
        x3, _, _, wsel1, _, _ = stage_cd[b]
        out = _final_call(1, b, bsz, x3, y_pairs1[b], wsel1.T, mod4, fg, out)
    return out.reshape(bsz, seq_len, d)
```

```python
import functools

import jax
import jax.numpy as jnp
from jax import lax
from jax.experimental import pallas as pl
from jax.experimental.pallas import tpu as pltpu
from jax.experimental.pallas import tpu_sc as plsc

D_MODEL = 1024
EPS = 1e-6
POOL_WINDOWS = (2, 4, 8, 16)
POOL_GROUP = 128
D_HALF = 512
CONF_KERNEL = 31
SCONV_KERNEL = 3
CHUNK = 128
GMLP_HEADS = 4
N_EXPERTS = 16
N_GROUPS = 4
EXPERTS_PER_GROUP = 4
TOP_K = 2
D_EXPERT = 512

V7X_LANES = 128
V7X_SUBLANES = 8
V7X_VMEM_LIMIT_BYTES = 56 * 1024 * 1024

MIX_TILE = 512
MOE_TILE = 512
SC_CHUNK = 64
ROW_CHUNKS = D_MODEL // (2 * V7X_LANES)
CONV_HIST = 32
POOL_HIST = 16
SCONV_HIST = 8

BF16 = jnp.bfloat16
F32 = jnp.float32
U32 = jnp.uint32


def _rms_mod(x, g_row, shift_row, scale_row):
    ms = jnp.mean(x * x, axis=-1, keepdims=True)
    gain = g_row * (1.0 + scale_row)
    return (x * lax.rsqrt(ms + EPS)) * gain + shift_row


def _layer_norm(x, g_row, b_row):
    mu = jnp.mean(x, axis=-1, keepdims=True)
    xc = x - mu
    var = jnp.mean(xc * xc, axis=-1, keepdims=True)
    return xc * lax.rsqrt(var + EPS) * g_row + b_row


def _sigmoid(x):
    return 1.0 / (1.0 + jnp.exp(-x))


def _silu(x):
    return x * _sigmoid(x)


def _gelu_tanh(x):
    c = 0.7978845608028654
    return 0.5 * x * (1.0 + jnp.tanh(c * (x + 0.044715 * (x * x * x))))


def _shift_rows(x, r):
    n, c = x.shape
    if r == V7X_SUBLANES:
        return jnp.concatenate([x[:r], x[:n - r]], axis=0)
    g = x.reshape(n // V7X_SUBLANES, V7X_SUBLANES, c)
    rot = pltpu.roll(g, r, axis=1)
    prev = jnp.concatenate([rot[:1], rot[:-1]], axis=0)
    sub = lax.broadcasted_iota(jnp.int32, g.shape, 1)
    return jnp.where(sub < r, prev, rot).reshape(n, c)


def _load_words(ref, n_rows):
    return jnp.concatenate(
        [ref[pl.ds(c, n_rows, stride=ROW_CHUNKS), :] for c in range(ROW_CHUNKS)], axis=1)


def _store_words(ref, words):
    n_rows = words.shape[0]
    for c in range(ROW_CHUNKS):
        ref[pl.ds(c, n_rows, stride=ROW_CHUNKS), :] = words[:, c * V7X_LANES:(c + 1) * V7X_LANES]


def _pack_rows(val):
    half = val.shape[1] // 2
    return pltpu.pack_elementwise([val[:, :half], val[:, half:]], packed_dtype=BF16)


def _unpack_rows(words):
    halves = [pltpu.unpack_elementwise(words, index=i, packed_dtype=BF16, unpacked_dtype=F32)
              for i in range(2)]
    return jnp.concatenate(halves, axis=1)


def _ada_kernel(ct_ref, w_ref, b_ref, o_ref):
    ct = ct_ref[...]
    cond = _silu(ct)
    w = w_ref[0]
    nb = ct.shape[1]
    for b in range(nb):
        col = cond[:, b:b + 1]
        o_ref[0, b:b + 1, :] = jnp.sum(col * w, axis=0, keepdims=True) + b_ref[0]


def _ada_mod(c, ada_w, ada_b):
    depth, d, six_d = ada_w.shape
    bsz = c.shape[0]
    nb = D_MODEL
    return pl.pallas_call(
        _ada_kernel,
        grid=(depth, six_d // nb),
        in_specs=[
            pl.BlockSpec((d, bsz), lambda l, j: (0, 0)),
            pl.BlockSpec((1, d, nb), lambda l, j: (l, 0, j)),
            pl.BlockSpec((1, 1, nb), lambda l, j: (l, 0, j)),
        ],
        out_specs=pl.BlockSpec((1, bsz, nb), lambda l, j: (l, 0, j)),
        out_shape=jax.ShapeDtypeStruct((depth, bsz, six_d), F32),
        compiler_params=pltpu.CompilerParams(
            dimension_semantics=("arbitrary", "arbitrary"),
            vmem_limit_bytes=V7X_VMEM_LIMIT_BYTES),
        name="ada_mod",
    )(c.T, ada_w, ada_b.reshape(depth, 1, six_d))


def _route(h2_bf, rwt_ref, rbias_ref, eidx_ref, wsel_ref, rank_ref, counts_ref, cnt_ref):
    nt = (((1,), (1,)), ((), ()))
    r = lax.dot_general(rwt_ref[...], h2_bf, nt, preferred_element_type=F32)
    logits = r[:N_EXPERTS] + r[N_EXPERTS:]
    m = jnp.max(logits, axis=0, keepdims=True)
    ex = jnp.exp(logits - m)
    probs = ex / jnp.sum(ex, axis=0, keepdims=True)
    sel = probs + rbias_ref[...]
    s = [sel[e:e + 1] for e in range(N_EXPERTS)]
    p = [probs[e:e + 1] for e in range(N_EXPERTS)]
    best = None
    gi = None
    for g in range(N_GROUPS):
        a, b, c, d = s[4 * g:4 * g + 4]
        hi1, lo1 = jnp.maximum(a, b), jnp.minimum(a, b)
        hi2, lo2 = jnp.maximum(c, d), jnp.minimum(c, d)
        top1 = jnp.maximum(hi1, hi2)
        top2 = jnp.maximum(jnp.minimum(hi1, hi2), jnp.maximum(lo1, lo2))
        score = top1 + top2
        if g == 0:
            best, gi = score, jnp.zeros(score.shape, jnp.int32)
        else:
            upd = score > best
            gi = jnp.where(upd, g, gi)
            best = jnp.where(upd, score, best)
    v, q = [], []
    for j in range(EXPERTS_PER_GROUP):
        vj, qj = s[j], p[j]
        for g in range(1, N_GROUPS):
            pick = gi == g
            vj = jnp.where(pick, s[4 * g + j], vj)
            qj = jnp.where(pick, p[4 * g + j], qj)
        v.append(vj)
        q.append(qj)
    i1 = jnp.zeros(gi.shape, jnp.int32)
    m1 = v[0]
    for j in range(1, EXPERTS_PER_GROUP):
        upd = v[j] > m1
        i1 = jnp.where(upd, j, i1)
        m1 = jnp.where(upd, v[j], m1)
    i2 = jnp.zeros(gi.shape, jnp.int32)
    m2 = jnp.full(m1.shape, -jnp.inf, F32)
    for j in range(EXPERTS_PER_GROUP):
        cand = (i1 != j) & (v[j] > m2)
        i2 = jnp.where(cand, j, i2)
        m2 = jnp.where(cand, v[j], m2)
    pa = q[0]
    pb = q[0]
    for j in range(1, EXPERTS_PER_GROUP):
        pa = jnp.where(i1 == j, q[j], pa)
        pb = jnp.where(i2 == j, q[j], pb)
    tot = pa + pb
    e0 = gi * EXPERTS_PER_GROUP + i1
    e1 = gi * EXPERTS_PER_GROUP + i2
    t = h2_bf.shape[0]
    eidx_ref[0:1, :] = e0
    eidx_ref[1:2, :] = e1
    wsel_ref[0:1, :] = pa / tot
    wsel_ref[1:2, :] = pb / tot

    e_iota = lax.broadcasted_iota(jnp.int32, (N_EXPERTS, t), 0)
    oh0 = e_iota == e0
    oh1 = e_iota == e1
    both = jnp.where(oh0 | oh1, 1.0, 0.0)
    r_i = lax.broadcasted_iota(jnp.int32, (V7X_LANES, V7X_LANES), 0)
    c_i = lax.broadcasted_iota(jnp.int32, (V7X_LANES, V7X_LANES), 1)
    before = jnp.where(r_i < c_i, 1.0, 0.0).astype(BF16)
    run = cnt_ref[...]
    rank0, rank1 = [], []
    for blk in range(t // V7X_LANES):
        lanes = slice(blk * V7X_LANES, (blk + 1) * V7X_LANES)
        b = both[:, lanes]
        pre = jnp.dot(b.astype(BF16), before, preferred_element_type=F32) + run
        rank0.append(jnp.sum(jnp.where(oh0[:, lanes], pre, 0.0), axis=0, keepdims=True))
        rank1.append(jnp.sum(jnp.where(oh1[:, lanes], pre, 0.0), axis=0, keepdims=True))
        run = run + jnp.sum(b, axis=1, keepdims=True)
    cnt_ref[...] = run
    rank_ref[0:1, :] = jnp.concatenate(rank0, axis=1).astype(jnp.int32)
    rank_ref[1:2, :] = jnp.concatenate(rank1, axis=1).astype(jnp.int32)
    counts_ref[...] = jnp.broadcast_to(run, counts_ref.shape).astype(jnp.int32)


def _finish_mixer(x, m, mod, n2g_ref, rwt_ref, rbias_ref,
                  x1_ref, h2_ref, eidx_ref, wsel_ref, rank_ref, counts_ref, cnt_ref):
    x1 = x + mod[2:3] * m
    x1_ref[...] = x1
    h2 = _rms_mod(x1, n2g_ref[...], mod[3:4], mod[4:5])
    h2_bf = h2.astype(BF16)
    _store_words(h2_ref, _pack_rows(h2))
    _route(h2_bf, rwt_ref, rbias_ref, eidx_ref, wsel_ref, rank_ref, counts_ref, cnt_ref)


def _moe_residual(x_ref, y0_ref, y1_ref, wt_ref, g2_row):
    tm = x_ref.shape[0]
    wt = wt_ref[...]
    y0 = _unpack_rows(_load_words(y0_ref, tm))
    y1 = _unpack_rows(_load_words(y1_ref, tm))
    y = wt[:, 0:1] * y0 + wt[:, 1:2] * y1
    return x_ref[...] + g2_row * y


def _mixer_ab_kernel(x_ref, mod_ref, n1g_ref, n2g_ref, win_ref, poolw_ref, pscale_ref,
                     convw_ref, convb_ref, lng_ref, lnb_ref, wout_ref, rwt_ref, rbias_ref,
                     x1_ref, h2_ref, eidx_ref, wsel_ref, rank_ref, counts_ref,
                     pool_ext, conv_ext, cnt_ref):
    seq_tile = pl.program_id(0)
    tm = x_ref.shape[0]

    @pl.when(seq_tile == 0)
    def _():
        pool_ext[0:POOL_HIST, :] = jnp.zeros((POOL_HIST, D_HALF), F32)
        conv_ext[0:CONV_HIST, :] = jnp.zeros((CONV_HIST, D_HALF), F32)
        cnt_ref[...] = jnp.zeros_like(cnt_ref)

    x = x_ref[...]
    mod = mod_ref[0, 0]
    h = _rms_mod(x, n1g_ref[...], mod[0:1], mod[1:2]).astype(BF16)
    z = jnp.dot(h, win_ref[...], preferred_element_type=F32)
    zp = z[:, :D_HALF]
    glu = z[:, D_HALF:2 * D_HALF] * _sigmoid(z[:, 2 * D_HALF:])
    pool_ext[POOL_HIST:POOL_HIST + tm, :] = zp
    conv_ext[CONV_HIST:CONV_HIST + tm, :] = glu

    row = lax.broadcasted_iota(jnp.int32, (tm, 1), 0)
    pos1 = (seq_tile * tm + row + 1).astype(F32)
    pool_out = []
    for g, w in enumerate(POOL_WINDOWS):
        cols = slice(g * POOL_GROUP, (g + 1) * POOL_GROUP)
        acc = pool_ext[:, cols]
        span = 1
        while span < w:
            acc = acc + _shift_rows(acc, span)
            span *= 2
        wsum = acc[POOL_HIST:POOL_HIST + tm]
        inv_cnt = 1.0 / jnp.minimum(pos1, float(w))
        diff = wsum * inv_cnt - zp[:, cols]
        po = jnp.dot(diff.astype(BF16), poolw_ref[g], preferred_element_type=F32)
        pool_out.append(po * pscale_ref[:, cols])

    convw = convw_ref[...]
    ext_rows = tm + V7X_SUBLANES
    conv = None
    for r in range(V7X_SUBLANES):
        vr = None
        for a in range(CONV_HIST // V7X_SUBLANES):
            lag = V7X_SUBLANES * a + r
            if lag >= CONF_KERNEL:
                continue
            k = CONF_KERNEL - 1 - lag
            start = CONV_HIST - V7X_SUBLANES - V7X_SUBLANES * a
            term = convw[k:k + 1, :] * conv_ext[start:start + ext_rows, :]
            vr = term if vr is None else vr + term
        if r:
            vr = _shift_rows(vr, r)
        conv = vr if conv is None else conv + vr
    conv = conv[V7X_SUBLANES:V7X_SUBLANES + tm] + convb_ref[...]
    conf = _silu(_layer_norm(conv, lng_ref[...], lnb_ref[...]))

    pool_ext[0:POOL_HIST, :] = zp[tm - POOL_HIST:tm]
    conv_ext[0:CONV_HIST, :] = glu[tm - CONV_HIST:tm]

    m = jnp.dot(conf.astype(BF16), wout_ref[D_HALF:, :], preferred_element_type=F32)
    for g in range(len(POOL_WINDOWS)):
        rows = slice(g * POOL_GROUP, (g + 1) * POOL_GROUP)
        m = m + jnp.dot(pool_out[g].astype(BF16), wout_ref[rows, :], preferred_element_type=F32)
    _finish_mixer(x, m, mod, n2g_ref, rwt_ref, rbias_ref,
                  x1_ref, h2_ref, eidx_ref, wsel_ref, rank_ref, counts_ref, cnt_ref)


def _mixer_cd_kernel(x_ref, y0_ref, y1_ref, wt_ref, modp_ref,
                     mod_ref, n1g_ref, n2g_ref, win_ref, sconvw_ref, lng_ref, lnb_ref,
                     ws_ref, bsf_ref, wout_ref, rwt_ref, rbias_ref,
                     x1_ref, h2_ref, eidx_ref, wsel_ref, rank_ref, counts_ref,
                     sconv_ext, cnt_ref):
    tm = x_ref.shape[0]

    @pl.when(pl.program_id(0) == 0)
    def _():
        sconv_ext[0:SCONV_HIST, :] = jnp.zeros((SCONV_HIST, D_HALF), F32)
        cnt_ref[...] = jnp.zeros_like(cnt_ref)

    x = _moe_residual(x_ref, y0_ref, y1_ref, wt_ref, modp_ref[0, 0][5:6])
    mod = mod_ref[0, 0]
    h = _rms_mod(x, n1g_ref[...], mod[0:1], mod[1:2]).astype(BF16)
    def proj(lo, hi):
        return jnp.dot(h, win_ref[:, lo:hi], preferred_element_type=F32)

    v = _layer_norm(_gelu_tanh(proj(4 * D_HALF, 5 * D_HALF)), lng_ref[...], lnb_ref[...])
    u = _gelu_tanh(proj(3 * D_HALF, 4 * D_HALF))
    ch = proj(D_HALF, 2 * D_HALF) * proj(2 * D_HALF, 3 * D_HALF)
    bg = proj(0, D_HALF)

    sconv_ext[SCONV_HIST:SCONV_HIST + tm, :] = ch
    sw = sconvw_ref[...]
    ext = sconv_ext[...]
    conv = sw[2:3, :] * ext
    conv = conv + sw[1:2, :] * _shift_rows(ext, 1)
    conv = conv + sw[0:1, :] * _shift_rows(ext, 2)
    sc_out = bg * conv[SCONV_HIST:SCONV_HIST + tm]
    sconv_ext[0:SCONV_HIST, :] = ch[tm - SCONV_HIST:tm]

    r_i = lax.broadcasted_iota(jnp.int32, (CHUNK, CHUNK), 0)
    c_i = lax.broadcasted_iota(jnp.int32, (CHUNK, CHUNK), 1)
    tril = c_i <= r_i
    wm = [jnp.where(tril, ws_ref[hd], 0.0).astype(BF16) for hd in range(GMLP_HEADS)]
    v_bf = v.astype(BF16)
    bsf = bsf_ref[...]
    gm_rows = []
    for n in range(tm // CHUNK):
        rows = slice(n * CHUNK, (n + 1) * CHUNK)
        heads = []
        for hd in range(GMLP_HEADS):
            cols = slice(hd * POOL_GROUP, (hd + 1) * POOL_GROUP)
            heads.append(jnp.dot(wm[hd], v_bf[rows, cols], preferred_element_type=F32))
        mixed = jnp.concatenate(heads, axis=1) + bsf
        gm_rows.append(u[rows] * mixed)
    gm_out = jnp.concatenate(gm_rows, axis=0)

    m = jnp.dot(sc_out.astype(BF16), wout_ref[:D_HALF, :], preferred_element_type=F32)
    m = m + jnp.dot(gm_out.astype(BF16), wout_ref[D_HALF:, :], preferred_element_type=F32)
    _finish_mixer(x, m, mod, n2g_ref, rwt_ref, rbias_ref,
                  x1_ref, h2_ref, eidx_ref, wsel_ref, rank_ref, counts_ref, cnt_ref)


def _const_spec(shape):
    nd = len(shape)
    return pl.BlockSpec(shape, lambda i: (0,) * nd)


def _mixer_call(kernel_fn, layer, batch, stream_inputs, stream_specs, mod4, n1g, n2g, weights,
                rwt, rbias, scratch, seq_len, name):
    n_tok = seq_len
    d = D_MODEL
    tm = MIX_TILE
    in_specs = stream_specs + [
        pl.BlockSpec((1, 1, 6, d), lambda i: (layer, batch, 0, 0)),
        _const_spec(n1g.shape),
        _const_spec(n2g.shape),
    ] + [_const_spec(w.shape) for w in weights] + [_const_spec(rwt.shape), _const_spec(rbias.shape)]
    out_specs = [
        pl.BlockSpec((tm, d), lambda i: (i, 0)),
        pl.BlockSpec((tm * ROW_CHUNKS, V7X_LANES), lambda i: (i, 0)),
        pl.BlockSpec((TOP_K, tm), lambda i: (0, i)),
        pl.BlockSpec((TOP_K, tm), lambda i: (0, i)),
        pl.BlockSpec((TOP_K, tm), lambda i: (0, i)),
        pl.BlockSpec((N_EXPERTS, V7X_LANES), lambda i: (0, 0)),
    ]
    out_shape = [
        jax.ShapeDtypeStruct((n_tok, d), F32),
        jax.ShapeDtypeStruct((n_tok * ROW_CHUNKS, V7X_LANES), U32),
        jax.ShapeDtypeStruct((TOP_K, n_tok), jnp.int32),
        jax.ShapeDtypeStruct((TOP_K, n_tok), F32),
        jax.ShapeDtypeStruct((TOP_K, n_tok), jnp.int32),
        jax.ShapeDtypeStruct((N_EXPERTS, V7X_LANES), jnp.int32),
    ]
    return pl.pallas_call(
        kernel_fn,
        grid=(n_tok // tm,),
        in_specs=in_specs,
        out_specs=out_specs,
        out_shape=out_shape,
        scratch_shapes=scratch + [pltpu.VMEM((N_EXPERTS, 1), F32)],
        compiler_params=pltpu.CompilerParams(
            dimension_semantics=("arbitrary",),
            vmem_limit_bytes=V7X_VMEM_LIMIT_BYTES),
        name=name,
    )(*stream_inputs, mod4, n1g, n2g, *weights, rwt, rbias)


def _combine_specs(n_tok):
    tm = MIX_TILE
    n_tiles = n_tok // tm
    return [
        pl.BlockSpec((tm, D_MODEL), lambda i: (i, 0)),
        pl.BlockSpec((tm * ROW_CHUNKS, V7X_LANES), lambda i: (i, 0)),
        pl.BlockSpec((tm * ROW_CHUNKS, V7X_LANES), lambda i: (n_tiles + i, 0)),
        pl.BlockSpec((tm, TOP_K), lambda i: (i, 0)),
    ]


def _sorted_positions(eidx, rank, counts):
    n_pairs = eidx.shape[0] * eidx.shape[1]
    n_rows = n_pairs + N_EXPERTS * MOE_TILE
    n_tiles = n_rows // MOE_TILE
    padded = (counts + MOE_TILE - 1) // MOE_TILE * MOE_TILE
    seg_end = jnp.cumsum(padded)
    seg_start = seg_end - padded
    expert_ids = jnp.arange(N_EXPERTS, dtype=jnp.int32)
    pos = jnp.sum(jnp.where(eidx[..., None] == expert_ids, seg_start, 0), axis=-1) + rank
    tile_row0 = jnp.arange(n_tiles, dtype=jnp.int32) * MOE_TILE
    tile_expert = jnp.minimum(
        jnp.sum((tile_row0[:, None] >= seg_end[None, :]).astype(jnp.int32), axis=1), N_EXPERTS - 1)
    n_used = (seg_end[N_EXPERTS - 1] // MOE_TILE).reshape(1)
    valid_end = seg_start + counts
    tile_valid_end = jnp.sum(
        jnp.where(tile_expert[:, None] == expert_ids, valid_end, 0), axis=-1)
    tile_valid = jnp.clip(tile_valid_end - tile_row0, 0, MOE_TILE)

    def of_tile(per_expert):
        return jnp.sum(jnp.where(tile_expert[:, None] == expert_ids, per_expert, 0), axis=-1)

    nonempty = (padded > 0).astype(jnp.int32)
    later_nonempty = (expert_ids[None, :] > expert_ids[:, None]) & (padded[None, :] > 0)
    next_expert = jnp.min(jnp.where(later_nonempty, expert_ids[None, :], N_EXPERTS), axis=1)
    next_expert = jnp.where(next_expert < N_EXPERTS, next_expert, -1)
    tile_used = tile_row0 < seg_end[N_EXPERTS - 1]
    tile_first = (tile_used & (of_tile(seg_start) == tile_row0)).astype(jnp.int32)
    tile_next = of_tile(next_expert)
    tile_slot = of_tile(jnp.cumsum(nonempty) - nonempty) % 2
    sched = tuple(a.astype(jnp.int32) for a in
                  (tile_expert, tile_valid, n_used, tile_first, tile_next, tile_slot))
    return pos.reshape(n_pairs).astype(jnp.int32), sched, n_rows


def _sc_move_rows(scatter, src, pos_flat, n_out_rows, name):
    info = plsc.get_sparse_core_info()
    n_workers = info.num_cores * info.num_subcores
    n_pairs = pos_flat.shape[0]
    n_src = src.shape[0]
    per_worker = n_pairs // n_workers
    n_chunks = per_worker // SC_CHUNK
    assert per_worker * n_workers == n_pairs and n_chunks * SC_CHUNK == per_worker
    assert n_src % per_worker == 0
    idx = pos_flat.reshape(n_workers, n_chunks, SC_CHUNK)
    mesh = plsc.VectorSubcoreMesh(core_axis_name="core", subcore_axis_name="subcore")

    @functools.partial(
        pl.kernel,
        out_type=jax.ShapeDtypeStruct((n_out_rows, ROW_CHUNKS, V7X_LANES), U32),
        mesh=mesh,
        scratch_types=[
            pltpu.VMEM((n_chunks, SC_CHUNK), jnp.int32),
            pltpu.VMEM((2, SC_CHUNK, ROW_CHUNKS, V7X_LANES), U32),
            pltpu.SemaphoreType.DMA((2,)),
            pltpu.SemaphoreType.DMA((2,)),
        ],
        name=name)
    def move(src_hbm, i_hbm, o_hbm, idx_v, buf, in_sem, out_sem):
        wid = lax.axis_index("subcore") * info.num_cores + lax.axis_index("core")
        base = wid * per_worker
        src_base = lax.rem(base, n_src)
        pltpu.sync_copy(i_hbm.at[wid], idx_v)

        def fetch(s, slot):
            if scatter:
                rows = src_hbm.at[pl.ds(src_base + s * SC_CHUNK, SC_CHUNK)]
            else:
                rows = src_hbm.at[idx_v.at[s]]
            return pltpu.make_async_copy(rows, buf.at[slot], in_sem.at[slot])

        def flush(s, slot):
            if scatter:
                rows = o_hbm.at[idx_v.at[s]]
            else:
                rows = o_hbm.at[pl.ds(base + s * SC_CHUNK, SC_CHUNK)]
            return pltpu.make_async_copy(buf.at[slot], rows, out_sem.at[slot])

        fetch(0, 0).start()
        for s in range(n_chunks):
            slot = s % 2
            fetch(s, slot).wait()
            flush(s, slot).start()
            if s + 1 < n_chunks:
                if s >= 1:
                    flush(s - 1, 1 - slot).wait()
                fetch(s + 1, 1 - slot).start()
        flush(n_chunks - 2, n_chunks % 2).wait()
        flush(n_chunks - 1, (n_chunks - 1) % 2).wait()

    return move(src, idx)


def _experts_kernel(layer, te_ref, tv_ref, nu_ref, first_ref, next_ref, slot_ref,
                    x_ref, wg_hbm, wu_hbm, wd_hbm, y_ref,
                    wg_f32, wu_f32, wd_f32, wg_bf, wu_bf, wd_bf, sems):
    j = pl.program_id(0)
    tm = x_ref.shape[0] // ROW_CHUNKS
    used = j < nu_ref[0]

    def weight_copies(expert, slot):
        pairs = ((wg_hbm, wg_f32), (wu_hbm, wu_f32), (wd_hbm, wd_f32))
        return [pltpu.make_async_copy(src.at[layer, expert], dst.at[slot], sems.at[slot, m])
                for m, (src, dst) in enumerate(pairs)]

    @pl.when(used & (first_ref[j] == 1))
    def _():
        slot = slot_ref[j]

        @pl.when(j == 0)
        def _():
            for cp in weight_copies(te_ref[j], slot):
                cp.start()

        for cp in weight_copies(te_ref[j], slot):
            cp.wait()
        wg_bf[...] = wg_f32[slot].astype(BF16)
        wu_bf[...] = wu_f32[slot].astype(BF16)
        wd_bf[...] = wd_f32[slot].astype(BF16)

        @pl.when(next_ref[j] >= 0)
        def _():
            for cp in weight_copies(next_ref[j], 1 - slot):
                cp.start()

    @pl.when(used)
    def _():
        row = lax.broadcasted_iota(jnp.int32, (tm, 1), 0)
        words = jnp.where(row < tv_ref[j], _load_words(x_ref, tm), jnp.uint32(0))
        h = _unpack_rows(words).astype(BF16)
        a = jnp.dot(h, wg_bf[...], preferred_element_type=F32)
        b = jnp.dot(h, wu_bf[...], preferred_element_type=F32)
        t = (_silu(a) * b).astype(BF16)
        _store_words(y_ref, _pack_rows(jnp.dot(t, wd_bf[...], preferred_element_type=F32)))

    @pl.when(jnp.logical_not(used))
    def _():
        y_ref[...] = jnp.zeros_like(y_ref)


def _experts_call(layer, rows, sched, w_gate, w_up, w_down):
    n_rows = rows.shape[0]
    tm = MOE_TILE
    d = D_MODEL
    rows2 = rows.reshape(n_rows * ROW_CHUNKS, V7X_LANES)
    n_sched = len(sched)
    grid_spec = pltpu.PrefetchScalarGridSpec(
        num_scalar_prefetch=n_sched,
        grid=(n_rows // tm,),
        in_specs=[
            pl.BlockSpec((tm * ROW_CHUNKS, V7X_LANES),
                         lambda j, te, tv, nu, *_: (jnp.minimum(j, nu[0] - 1), 0)),
            pl.BlockSpec(memory_space=pl.ANY),
            pl.BlockSpec(memory_space=pl.ANY),
            pl.BlockSpec(memory_space=pl.ANY),
        ],
        out_specs=pl.BlockSpec((tm * ROW_CHUNKS, V7X_LANES), lambda j, *_: (j, 0)),
        scratch_shapes=[
            pltpu.VMEM((2, d, D_EXPERT), F32),
            pltpu.VMEM((2, d, D_EXPERT), F32),
            pltpu.VMEM((2, D_EXPERT, d), F32),
            pltpu.VMEM((d, D_EXPERT), BF16),
            pltpu.VMEM((d, D_EXPERT), BF16),
            pltpu.VMEM((D_EXPERT, d), BF16),
            pltpu.SemaphoreType.DMA((2, 3)),
        ],
    )
    y = pl.pallas_call(
        functools.partial(_experts_kernel, layer),
        grid_spec=grid_spec,
        out_shape=jax.ShapeDtypeStruct((n_rows * ROW_CHUNKS, V7X_LANES), U32),
        compiler_params=pltpu.CompilerParams(
            dimension_semantics=("arbitrary",),
            vmem_limit_bytes=V7X_VMEM_LIMIT_BYTES),
        name="experts_l%d" % layer,
    )(*sched, rows2, w_gate, w_up, w_down)
    return y.reshape(n_rows, ROW_CHUNKS, V7X_LANES)


def _moe_rows(layer, mixer_outs, w_gate, w_up, w_down):
    plans = [_sorted_positions(eidx, rank, counts[:, 0])
             for (_, _, eidx, _, rank, counts) in mixer_outs]
    sorted_rows = []
    for (_, h_rows, eidx, _, _, _), (pos_flat, _, n_rows) in zip(mixer_outs, plans):
        n_tok = eidx.shape[1]
        sorted_rows.append(_sc_move_rows(
            True, h_rows.reshape(n_tok, ROW_CHUNKS, V7X_LANES), pos_flat, n_rows, "sc_scatter_rows"))
    y_sorted = [_experts_call(layer, rows, sched, w_gate, w_up, w_down)
                for rows, (_, sched, _) in zip(sorted_rows, plans)]
    y_pairs = []
    for y, (pos_flat, _, _) in zip(y_sorted, plans):
        n_pairs = pos_flat.shape[0]
        moved = _sc_move_rows(False, y, pos_flat, n_pairs, "sc_gather_rows")
        y_pairs.append(moved.reshape(n_pairs * ROW_CHUNKS, V7X_LANES))
    return y_pairs


def _final_kernel(x_ref, y0_ref, y1_ref, wt_ref, modp_ref, fg_ref, *rest):
    o_ref = rest[-1]
    y = _moe_residual(x_ref, y0_ref, y1_ref, wt_ref, modp_ref[0, 0][5:6])
    ms = jnp.mean(y * y, axis=-1, keepdims=True)
    o_ref[...] = y * lax.rsqrt(ms + EPS) * fg_ref[...]


def _final_call(layer, batch, n_batch, x, y_pairs, wt, mod4, fg, out_prev):
    seq_len, d = x.shape
    tm = MIX_TILE
    tiles_per_seq = seq_len // tm
    in_specs = _combine_specs(seq_len) + [
        pl.BlockSpec((1, 1, 6, d), lambda i: (layer, batch, 0, 0)),
        _const_spec(fg.shape),
    ]
    args = [x, y_pairs, y_pairs, wt, mod4, fg]
    aliases = {}
    if out_prev is not None:
        in_specs.append(pl.BlockSpec(memory_space=pl.ANY))
        aliases = {len(args): 0}
        args.append(out_prev)
    return pl.pallas_call(
        _final_kernel,
        grid=(tiles_per_seq,),
        in_specs=in_specs,
        out_specs=pl.BlockSpec((tm, d), lambda i: (batch * tiles_per_seq + i, 0)),
        out_shape=jax.ShapeDtypeStruct((n_batch * seq_len, d), F32),
        input_output_aliases=aliases,
        compiler_params=pltpu.CompilerParams(
            dimension_semantics=("arbitrary",),
            vmem_limit_bytes=V7X_VMEM_LIMIT_BYTES),
        name="final_norm",
    )(*args)


def kernel(x, c, norm1_g, norm2_g, ada_w, ada_b, ab_w_in, pool_w, pool_scale, conf_conv_w, conf_conv_b, conf_ln_g, conf_ln_b, ab_w_out, cd_w_in, sconv_w, gmlp_ln_g, gmlp_ln_b, gmlp_ws, gmlp_bs, cd_w_out, router_w, router_bias, exp_w_gate, exp_w_up, exp_w_down, final_g):
    bsz, seq_len, d = x.shape
    n_tok = bsz * seq_len
    tm = MIX_TILE
    tiles_per_seq = seq_len // tm
    xf = x.reshape(n_tok, d)

    mod = _ada_mod(c, ada_w, ada_b)
    mod4 = mod.reshape(mod.shape[0], bsz, 6, d)

    rw_hi = router_w.astype(BF16)
    rw_lo = (router_w - rw_hi.astype(F32)).astype(BF16)
    rwt = jnp.concatenate([rw_hi.T, rw_lo.T], axis=0)
    rbias = router_bias.reshape(N_EXPERTS, 1)
    fg = final_g.reshape(1, d)

    weights_ab = [
        ab_w_in[0].astype(BF16), pool_w[0].astype(BF16), pool_scale[0].reshape(1, D_HALF),
        conf_conv_w[0], conf_conv_b[0].reshape(1, D_HALF), conf_ln_g[0].reshape(1, D_HALF),
        conf_ln_b[0].reshape(1, D_HALF), ab_w_out[0].astype(BF16),
    ]
    scratch_ab = [pltpu.VMEM((POOL_HIST + tm, D_HALF), F32),
                  pltpu.VMEM((CONV_HIST + tm, D_HALF), F32)]
    bsf = jnp.repeat(gmlp_bs[0].T, POOL_GROUP, axis=1)
    weights_cd = [
        cd_w_in[0].astype(BF16), sconv_w[0], gmlp_ln_g[0].reshape(1, D_HALF),
        gmlp_ln_b[0].reshape(1, D_HALF), gmlp_ws[0], bsf, cd_w_out[0].astype(BF16),
    ]
    scratch_cd = [pltpu.VMEM((SCONV_HIST + tm, D_HALF), F32)]
    experts = (exp_w_gate, exp_w_up, exp_w_down)

    batches = range(bsz)
    stage_ab = []
    for b in batches:
        x_spec = pl.BlockSpec((tm, d), lambda i, b=b: (b * tiles_per_seq + i, 0))
        stage_ab.append(_mixer_call(
            _mixer_ab_kernel, 0, b, [xf], [x_spec], mod4, norm1_g[0:1], norm2_g[0:1],
            weights_ab, rwt, rbias, scratch_ab, seq_len, "mixer_ab"))
    y_pairs0 = _moe_rows(0, stage_ab, *experts)

    stage_cd = []
    for b in batches:
        x1, _, _, wsel0, _, _ = stage_ab[b]
        prev_mod_spec = pl.BlockSpec((1, 1, 6, d), lambda i, b=b: (0, b, 0, 0))
        stage_cd.append(_mixer_call(
            _mixer_cd_kernel, 1, b, [x1, y_pairs0[b], y_pairs0[b], wsel0.T, mod4],
            _combine_specs(seq_len) + [prev_mod_spec], mod4, norm1_g[1:2], norm2_g[1:2],
            weights_cd, rwt, rbias, scratch_cd, seq_len, "mixer_cd"))
    y_pairs1 = _moe_rows(1, stage_cd, *experts)

    out = None
    for b in batches:
        x3, _, _, wsel1, _, _ = stage_cd[b]
        out = _final_call(1, b, bsz, x3, y_pairs1[b], wsel1.T, mod4, fg, out)
    return out.reshape(bsz, seq_len, d)
```

```python
import functools

import jax
import jax.numpy as jnp
from jax import lax
from jax.experimental import pallas as pl
from jax.experimental.pallas import tpu as pltpu
from jax.experimental.pallas import tpu_sc as plsc

D_MODEL = 1024
EPS = 1e-6
POOL_WINDOWS = (2, 4, 8, 16)
POOL_GROUP = 128
D_HALF = 512
CONF_KERNEL = 31
SCONV_KERNEL = 3
CHUNK = 128
GMLP_HEADS = 4
N_EXPERTS = 16
N_GROUPS = 4
EXPERTS_PER_GROUP = 4
TOP_K = 2
D_EXPERT = 512

V7X_LANES = 128
V7X_SUBLANES = 8
V7X_VMEM_LIMIT_BYTES = 56 * 1024 * 1024

MIX_TILE = 512
MOE_TILE = 512
SC_CHUNK = 64
ROW_CHUNKS = D_MODEL // (2 * V7X_LANES)
CONV_HIST = 32
POOL_HIST = 16
SCONV_HIST = 8

BF16 = jnp.bfloat16
F32 = jnp.float32
U32 = jnp.uint32


def _rms_mod(x, g_row, shift_row, scale_row):
    ms = jnp.mean(x * x, axis=-1, keepdims=True)
    gain = g_row * (1.0 + scale_row)
    return (x * lax.rsqrt(ms + EPS)) * gain + shift_row


def _layer_norm(x, g_row, b_row):
    mu = jnp.mean(x, axis=-1, keepdims=True)
    xc = x - mu
    var = jnp.mean(xc * xc, axis=-1, keepdims=True)
    return xc * lax.rsqrt(var + EPS) * g_row + b_row


def _sigmoid(x):
    return 1.0 / (1.0 + jnp.exp(-x))


def _silu(x):
    return x * _sigmoid(x)


def _gelu_tanh(x):
    c = 0.7978845608028654
    return 0.5 * x * (1.0 + jnp.tanh(c * (x + 0.044715 * (x * x * x))))


def _shift_rows(x, r):
    n, c = x.shape
    if r == V7X_SUBLANES:
        return jnp.concatenate([x[:r], x[:n - r]], axis=0)
    g = x.reshape(n // V7X_SUBLANES, V7X_SUBLANES, c)
    rot = pltpu.roll(g, r, axis=1)
    prev = jnp.concatenate([rot[:1], rot[:-1]], axis=0)
    sub = lax.broadcasted_iota(jnp.int32, g.shape, 1)
    return jnp.where(sub < r, prev, rot).reshape(n, c)


def _load_words(ref, n_rows):
    return jnp.concatenate(
        [ref[pl.ds(c, n_rows, stride=ROW_CHUNKS), :] for c in range(ROW_CHUNKS)], axis=1)


def _store_words(ref, words):
    n_rows = words.shape[0]
    for c in range(ROW_CHUNKS):
        ref[pl.ds(c, n_rows, stride=ROW_CHUNKS), :] = words[:, c * V7X_LANES:(c + 1) * V7X_LANES]


def _pack_rows(val):
    half = val.shape[1] // 2
    return pltpu.pack_elementwise([val[:, :half], val[:, half:]], packed_dtype=BF16)


def _unpack_rows(words):
    halves = [pltpu.unpack_elementwise(words, index=i, packed_dtype=BF16, unpacked_dtype=F32)
              for i in range(2)]
    return jnp.concatenate(halves, axis=1)


def _ada_kernel(ct_ref, w_ref, b_ref, o_ref):
    ct = ct_ref[...]
    cond = _silu(ct)
    w = w_ref[0]
    nb = ct.shape[1]
    for b in range(nb):
        col = cond[:, b:b + 1]
        o_ref[0, b:b + 1, :] = jnp.sum(col * w, axis=0, keepdims=True) + b_ref[0]


def _ada_mod(c, ada_w, ada_b):
    depth, d, six_d = ada_w.shape
    bsz = c.shape[0]
    nb = D_MODEL
    return pl.pallas_call(
        _ada_kernel,
        grid=(depth, six_d // nb),
        in_specs=[
            pl.BlockSpec((d, bsz), lambda l, j: (0, 0)),
            pl.BlockSpec((1, d, nb), lambda l, j: (l, 0, j)),
            pl.BlockSpec((1, 1, nb), lambda l, j: (l, 0, j)),
        ],
        out_specs=pl.BlockSpec((1, bsz, nb), lambda l, j: (l, 0, j)),
        out_shape=jax.ShapeDtypeStruct((depth, bsz, six_d), F32),
        compiler_params=pltpu.CompilerParams(
            dimension_semantics=("arbitrary", "arbitrary"),
            vmem_limit_bytes=V7X_VMEM_LIMIT_BYTES),
        name="ada_mod",
    )(c.T, ada_w, ada_b.reshape(depth, 1, six_d))


def _route(h2_bf, rwt_ref, rbias_ref, eidx_ref, wsel_ref, rank_ref, counts_ref, cnt_ref):
    nt = (((1,), (1,)), ((), ()))
    r = lax.dot_general(rwt_ref[...], h2_bf, nt, preferred_element_type=F32)
    logits = r[:N_EXPERTS] + r[N_EXPERTS:]
    m = jnp.max(logits, axis=0, keepdims=True)
    ex = jnp.exp(logits - m)
    probs = ex / jnp.sum(ex, axis=0, keepdims=True)
    sel = probs + rbias_ref[...]
    s = [sel[e:e + 1] for e in range(N_EXPERTS)]
    p = [probs[e:e + 1] for e in range(N_EXPERTS)]
    best = None
    gi = None
    for g in range(N_GROUPS):
        a, b, c, d = s[4 * g:4 * g + 4]
        hi1, lo1 = jnp.maximum(a, b), jnp.minimum(a, b)
        hi2, lo2 = jnp.maximum(c, d), jnp.minimum(c, d)
        top1 = jnp.maximum(hi1, hi2)
        top2 = jnp.maximum(jnp.minimum(hi1, hi2), jnp.maximum(lo1, lo2))
        score = top1 + top2
        if g == 0:
            best, gi = score, jnp.zeros(score.shape, jnp.int32)
        else:
            upd = score > best
            gi = jnp.where(upd, g, gi)
            best = jnp.where(upd, score, best)
    v, q = [], []
    for j in range(EXPERTS_PER_GROUP):
        vj, qj = s[j], p[j]
        for g in range(1, N_GROUPS):
            pick = gi == g
            vj = jnp.where(pick, s[4 * g + j], vj)
            qj = jnp.where(pick, p[4 * g + j], qj)
        v.append(vj)
        q.append(qj)
    i1 = jnp.zeros(gi.shape, jnp.int32)
    m1 = v[0]
    for j in range(1, EXPERTS_PER_GROUP):
        upd = v[j] > m1
        i1 = jnp.where(upd, j, i1)
        m1 = jnp.where(upd, v[j], m1)
    i2 = jnp.zeros(gi.shape, jnp.int32)
    m2 = jnp.full(m1.shape, -jnp.inf, F32)
    for j in range(EXPERTS_PER_GROUP):
        cand = (i1 != j) & (v[j] > m2)
        i2 = jnp.where(cand, j, i2)
        m2 = jnp.where(cand, v[j], m2)
    pa = q[0]
    pb = q[0]
    for j in range(1, EXPERTS_PER_GROUP):
        pa = jnp.where(i1 == j, q[j], pa)
        pb = jnp.where(i2 == j, q[j], pb)
    tot = pa + pb
    e0 = gi * EXPERTS_PER_GROUP + i1
    e1 = gi * EXPERTS_PER_GROUP + i2
    t = h2_bf.shape[0]
    eidx_ref[0:1, :] = e0
    eidx_ref[1:2, :] = e1
    wsel_ref[0:1, :] = pa / tot
    wsel_ref[1:2, :] = pb / tot

    e_iota = lax.broadcasted_iota(jnp.int32, (N_EXPERTS, t), 0)
    oh0 = e_iota == e0
    oh1 = e_iota == e1
    both = jnp.where(oh0 | oh1, 1.0, 0.0)
    r_i = lax.broadcasted_iota(jnp.int32, (V7X_LANES, V7X_LANES), 0)
    c_i = lax.broadcasted_iota(jnp.int32, (V7X_LANES, V7X_LANES), 1)
    before = jnp.where(r_i < c_i, 1.0, 0.0).astype(BF16)
    run = cnt_ref[...]
    rank0, rank1 = [], []
    for blk in range(t // V7X_LANES):
        lanes = slice(blk * V7X_LANES, (blk + 1) * V7X_LANES)
        b = both[:, lanes]
        pre = jnp.dot(b.astype(BF16), before, preferred_element_type=F32) + run
        rank0.append(jnp.sum(jnp.where(oh0[:, lanes], pre, 0.0), axis=0, keepdims=True))
        rank1.append(jnp.sum(jnp.where(oh1[:, lanes], pre, 0.0), axis=0, keepdims=True))
        run = run + jnp.sum(b, axis=1, keepdims=True)
    cnt_ref[...] = run
    rank_ref[0:1, :] = jnp.concatenate(rank0, axis=1).astype(jnp.int32)
    rank_ref[1:2, :] = jnp.concatenate(rank1, axis=1).astype(jnp.int32)
    counts_ref[...] = jnp.broadcast_to(run, counts_ref.shape).astype(jnp.int32)


def _finish_mixer(x, m, mod, n2g_ref, rwt_ref, rbias_ref,
                  x1_ref, h2_ref, eidx_ref, wsel_ref, rank_ref, counts_ref, cnt_ref):
    x1 = x + mod[2:3] * m
    x1_ref[...] = x1
    h2 = _rms_mod(x1, n2g_ref[...], mod[3:4], mod[4:5])
    h2_bf = h2.astype(BF16)
    _store_words(h2_ref, _pack_rows(h2))
    _route(h2_bf, rwt_ref, rbias_ref, eidx_ref, wsel_ref, rank_ref, counts_ref, cnt_ref)


def _moe_residual(x_ref, y0_ref, y1_ref, wt_ref, g2_row):
    tm = x_ref.shape[0]
    wt = wt_ref[...]
    y0 = _unpack_rows(_load_words(y0_ref, tm))
    y1 = _unpack_rows(_load_words(y1_ref, tm))
    y = wt[:, 0:1] * y0 + wt[:, 1:2] * y1
    return x_ref[...] + g2_row * y


def _mixer_ab_kernel(x_ref, mod_ref, n1g_ref, n2g_ref, win_ref, poolw_ref, pscale_ref,
                     convw_ref, convb_ref, lng_ref, lnb_ref, wout_ref, rwt_ref, rbias_ref,
                     x1_ref, h2_ref, eidx_ref, wsel_ref, rank_ref, counts_ref,
                     pool_ext, conv_ext, cnt_ref):
    seq_tile = pl.program_id(0)
    tm = x_ref.shape[0]

    @pl.when(seq_tile == 0)
    def _():
        pool_ext[0:POOL_HIST, :] = jnp.zeros((POOL_HIST, D_HALF), F32)
        conv_ext[0:CONV_HIST, :] = jnp.zeros((CONV_HIST, D_HALF), F32)
        cnt_ref[...] = jnp.zeros_like(cnt_ref)

    x = x_ref[...]
    mod = mod_ref[0, 0]
    h = _rms_mod(x, n1g_ref[...], mod[0:1], mod[1:2]).astype(BF16)
    z = jnp.dot(h, win_ref[...], preferred_element_type=F32)
    zp = z[:, :D_HALF]
    glu = z[:, D_HALF:2 * D_HALF] * _sigmoid(z[:, 2 * D_HALF:])
    pool_ext[POOL_HIST:POOL_HIST + tm, :] = zp
    conv_ext[CONV_HIST:CONV_HIST + tm, :] = glu

    row = lax.broadcasted_iota(jnp.int32, (tm, 1), 0)
    pos1 = (seq_tile * tm + row + 1).astype(F32)
    pool_out = []
    for g, w in enumerate(POOL_WINDOWS):
        cols = slice(g * POOL_GROUP, (g + 1) * POOL_GROUP)
        acc = pool_ext[:, cols]
        span = 1
        while span < w:
            acc = acc + _shift_rows(acc, span)
            span *= 2
        wsum = acc[POOL_HIST:POOL_HIST + tm]
        inv_cnt = 1.0 / jnp.minimum(pos1, float(w))
        diff = wsum * inv_cnt - zp[:, cols]
        po = jnp.dot(diff.astype(BF16), poolw_ref[g], preferred_element_type=F32)
        pool_out.append(po * pscale_ref[:, cols])

    convw = convw_ref[...]
    ext_rows = tm + V7X_SUBLANES
    conv = None
    for r in range(V7X_SUBLANES):
        vr = None
        for a in range(CONV_HIST // V7X_SUBLANES):
            lag = V7X_SUBLANES * a + r
            if lag >= CONF_KERNEL:
                continue
            k = CONF_KERNEL - 1 - lag
            start = CONV_HIST - V7X_SUBLANES - V7X_SUBLANES * a
            term = convw[k:k + 1, :] * conv_ext[start:start + ext_rows, :]
            vr = term if vr is None else vr + term
        if r:
            vr = _shift_rows(vr, r)
        conv = vr if conv is None else conv + vr
    conv = conv[V7X_SUBLANES:V7X_SUBLANES + tm] + convb_ref[...]
    conf = _silu(_layer_norm(conv, lng_ref[...], lnb_ref[...]))

    pool_ext[0:POOL_HIST, :] = zp[tm - POOL_HIST:tm]
    conv_ext[0:CONV_HIST, :] = glu[tm - CONV_HIST:tm]

    m = jnp.dot(conf.astype(BF16), wout_ref[D_HALF:, :], preferred_element_type=F32)
    for g in range(len(POOL_WINDOWS)):
        rows = slice(g * POOL_GROUP, (g + 1) * POOL_GROUP)
        m = m + jnp.dot(pool_out[g].astype(BF16), wout_ref[rows, :], preferred_element_type=F32)
    _finish_mixer(x, m, mod, n2g_ref, rwt_ref, rbias_ref,
                  x1_ref, h2_ref, eidx_ref, wsel_ref, rank_ref, counts_ref, cnt_ref)


def _mixer_cd_kernel(x_ref, y0_ref, y1_ref, wt_ref, modp_ref,
                     mod_ref, n1g_ref, n2g_ref, win_ref, sconvw_ref, lng_ref, lnb_ref,
                     ws_ref, bsf_ref, wout_ref, rwt_ref, rbias_ref,
                     x1_ref, h2_ref, eidx_ref, wsel_ref, rank_ref, counts_ref,
                     sconv_ext, cnt_ref):
    tm = x_ref.shape[0]

    @pl.when(pl.program_id(0) == 0)
    def _():
        sconv_ext[0:SCONV_HIST, :] = jnp.zeros((SCONV_HIST, D_HALF), F32)
        cnt_ref[...] = jnp.zeros_like(cnt_ref)

    x = _moe_residual(x_ref, y0_ref, y1_ref, wt_ref, modp_ref[0, 0][5:6])
    mod = mod_ref[0, 0]
    h = _rms_mod(x, n1g_ref[...], mod[0:1], mod[1:2]).astype(BF16)
    def proj(lo, hi):
        return jnp.dot(h, win_ref[:, lo:hi], preferred_element_type=F32)

    v = _layer_norm(_gelu_tanh(proj(4 * D_HALF, 5 * D_HALF)), lng_ref[...], lnb_ref[...])
    u = _gelu_tanh(proj(3 * D_HALF, 4 * D_HALF))
    ch = proj(D_HALF, 2 * D_HALF) * proj(2 * D_HALF, 3 * D_HALF)
    bg = proj(0, D_HALF)

    sconv_ext[SCONV_HIST:SCONV_HIST + tm, :] = ch
    sw = sconvw_ref[...]
    ext = sconv_ext[...]
    conv = sw[2:3, :] * ext
    conv = conv + sw[1:2, :] * _shift_rows(ext, 1)
    conv = conv + sw[0:1, :] * _shift_rows(ext, 2)
    sc_out = bg * conv[SCONV_HIST:SCONV_HIST + tm]
    sconv_ext[0:SCONV_HIST, :] = ch[tm - SCONV_HIST:tm]

    r_i = lax.broadcasted_iota(jnp.int32, (CHUNK, CHUNK), 0)
    c_i = lax.broadcasted_iota(jnp.int32, (CHUNK, CHUNK), 1)
    tril = c_i <= r_i
    wm = [jnp.where(tril, ws_ref[hd], 0.0).astype(BF16) for hd in range(GMLP_HEADS)]
    v_bf = v.astype(BF16)
    bsf = bsf_ref[...]
    gm_rows = []
    for n in range(tm // CHUNK):
        rows = slice(n * CHUNK, (n + 1) * CHUNK)
        heads = []
        for hd in range(GMLP_HEADS):
            cols = slice(hd * POOL_GROUP, (hd + 1) * POOL_GROUP)
            heads.append(jnp.dot(wm[hd], v_bf[rows, cols], preferred_element_type=F32))
        mixed = jnp.concatenate(heads, axis=1) + bsf
        gm_rows.append(u[rows] * mixed)
    gm_out = jnp.concatenate(gm_rows, axis=0)

    m = jnp.dot(sc_out.astype(BF16), wout_ref[:D_HALF, :], preferred_element_type=F32)
    m = m + jnp.dot(gm_out.astype(BF16), wout_ref[D_HALF:, :], preferred_element_type=F32)
    _finish_mixer(x, m, mod, n2g_ref, rwt_ref, rbias_ref,
                  x1_ref, h2_ref, eidx_ref, wsel_ref, rank_ref, counts_ref, cnt_ref)


def _const_spec(shape):
    nd = len(shape)
    return pl.BlockSpec(shape, lambda i: (0,) * nd)


def _mixer_call(kernel_fn, layer, batch, stream_inputs, stream_specs, mod4, n1g, n2g, weights,
                rwt, rbias, scratch, seq_len, name):
    n_tok = seq_len
    d = D_MODEL
    tm = MIX_TILE
    in_specs = stream_specs + [
        pl.BlockSpec((1, 1, 6, d), lambda i: (layer, batch, 0, 0)),
        _const_spec(n1g.shape),
        _const_spec(n2g.shape),
    ] + [_const_spec(w.shape) for w in weights] + [_const_spec(rwt.shape), _const_spec(rbias.shape)]
    out_specs = [
        pl.BlockSpec((tm, d), lambda i: (i, 0)),
        pl.BlockSpec((tm * ROW_CHUNKS, V7X_LANES), lambda i: (i, 0)),
        pl.BlockSpec((TOP_K, tm), lambda i: (0, i)),
        pl.BlockSpec((TOP_K, tm), lambda i: (0, i)),
        pl.BlockSpec((TOP_K, tm), lambda i: (0, i)),
        pl.BlockSpec((N_EXPERTS, V7X_LANES), lambda i: (0, 0)),
    ]
    out_shape = [
        jax.ShapeDtypeStruct((n_tok, d), F32),
        jax.ShapeDtypeStruct((n_tok * ROW_CHUNKS, V7X_LANES), U32),
        jax.ShapeDtypeStruct((TOP_K, n_tok), jnp.int32),
        jax.ShapeDtypeStruct((TOP_K, n_tok), F32),
        jax.ShapeDtypeStruct((TOP_K, n_tok), jnp.int32),
        jax.ShapeDtypeStruct((N_EXPERTS, V7X_LANES), jnp.int32),
    ]
    return pl.pallas_call(
        kernel_fn,
        grid=(n_tok // tm,),
        in_specs=in_specs,
        out_specs=out_specs,
        out_shape=out_shape,
        scratch_shapes=scratch + [pltpu.VMEM((N_EXPERTS, 1), F32)],
        compiler_params=pltpu.CompilerParams(
            dimension_semantics=("arbitrary",),
            vmem_limit_bytes=V7X_VMEM_LIMIT_BYTES),
        name=name,
    )(*stream_inputs, mod4, n1g, n2g, *weights, rwt, rbias)


def _combine_specs(n_tok):
    tm = MIX_TILE
    n_tiles = n_tok // tm
    return [
        pl.BlockSpec((tm, D_MODEL), lambda i: (i, 0)),
        pl.BlockSpec((tm * ROW_CHUNKS, V7X_LANES), lambda i: (i, 0)),
        pl.BlockSpec((tm * ROW_CHUNKS, V7X_LANES), lambda i: (n_tiles + i, 0)),
        pl.BlockSpec((tm, TOP_K), lambda i: (i, 0)),
    ]


SCHED_EXPERT, SCHED_VALID, SCHED_USED, SCHED_FIRST, SCHED_NEXT, SCHED_SLOT = range(6)
SCHED_ROWS = V7X_SUBLANES


def _plan_kernel(eidx_ref, rank_ref, counts_ref, pos_ref, sched_ref):
    i32 = jnp.int32
    shift = MOE_TILE.bit_length() - 1
    cnt = counts_ref[:, 0:1]
    padded = ((cnt + (MOE_TILE - 1)) >> shift) << shift
    seg = [padded[e:e + 1] for e in range(N_EXPERTS)]
    starts, ends = [], []
    run = jnp.zeros((1, 1), i32)
    for e in range(N_EXPERTS):
        starts.append(run)
        run = run + seg[e]
        ends.append(run)
    total = run
    nexts = [None] * N_EXPERTS
    nxt = jnp.full((1, 1), -1, i32)
    for e in reversed(range(N_EXPERTS)):
        nexts[e] = nxt
        nxt = jnp.where(seg[e] > 0, e, nxt)
    slots = []
    seen = jnp.zeros((1, 1), i32)
    for e in range(N_EXPERTS):
        slots.append(seen & 1)
        seen = seen + (seg[e] > 0).astype(i32)

    eidx = eidx_ref[...]
    pos = rank_ref[...]
    tile_row0 = lax.broadcasted_iota(i32, (1, V7X_LANES), 1) * MOE_TILE
    te = jnp.zeros((1, V7X_LANES), i32)
    for e in range(N_EXPERTS):
        pos = pos + jnp.where(eidx == e, starts[e], 0)
        te = te + (tile_row0 >= ends[e]).astype(i32)
    pos_ref[...] = pos
    te = jnp.minimum(te, N_EXPERTS - 1)

    def of_tile(per_expert):
        acc = jnp.zeros((1, V7X_LANES), i32)
        for e in range(N_EXPERTS):
            acc = acc + jnp.where(te == e, per_expert[e], 0)
        return acc

    valid_end = of_tile([starts[e] + cnt[e:e + 1] for e in range(N_EXPERTS)])
    used = tile_row0 < total
    rows = {
        SCHED_EXPERT: te,
        SCHED_VALID: jnp.clip(valid_end - tile_row0, 0, MOE_TILE),
        SCHED_USED: jnp.broadcast_to(total >> shift, (1, V7X_LANES)),
        SCHED_FIRST: (used & (of_tile(starts) == tile_row0)).astype(i32),
        SCHED_NEXT: of_tile(nexts),
        SCHED_SLOT: of_tile(slots),
    }
    for r in range(SCHED_ROWS):
        sched_ref[r:r + 1, :] = rows.get(r, jnp.zeros((1, V7X_LANES), i32))


def _sorted_positions(eidx, rank, counts):
    n_pairs = eidx.shape[0] * eidx.shape[1]
    n_rows = n_pairs + N_EXPERTS * MOE_TILE
    assert n_rows // MOE_TILE <= V7X_LANES
    pos, sched = pl.pallas_call(
        _plan_kernel,
        out_shape=[jax.ShapeDtypeStruct(eidx.shape, jnp.int32),
                   jax.ShapeDtypeStruct((SCHED_ROWS, V7X_LANES), jnp.int32)],
        name="moe_plan",
    )(eidx, rank, counts)
    return pos.reshape(n_pairs), sched, n_rows


def _sc_move_rows(scatter, src, pos_flat, n_out_rows, name):
    info = plsc.get_sparse_core_info()
    n_workers = info.num_cores * info.num_subcores
    n_pairs = pos_flat.shape[0]
    n_src = src.shape[0]
    per_worker = n_pairs // n_workers
    n_chunks = per_worker // SC_CHUNK
    assert per_worker * n_workers == n_pairs and n_chunks * SC_CHUNK == per_worker
    assert n_src % per_worker == 0
    idx = pos_flat.reshape(n_workers, n_chunks, SC_CHUNK)
    mesh = plsc.VectorSubcoreMesh(core_axis_name="core", subcore_axis_name="subcore")

    @functools.partial(
        pl.kernel,
        out_type=jax.ShapeDtypeStruct((n_out_rows, ROW_CHUNKS, V7X_LANES), U32),
        mesh=mesh,
        scratch_types=[
            pltpu.VMEM((n_chunks, SC_CHUNK), jnp.int32),
            pltpu.VMEM((2, SC_CHUNK, ROW_CHUNKS, V7X_LANES), U32),
            pltpu.SemaphoreType.DMA((2,)),
            pltpu.SemaphoreType.DMA((2,)),
        ],
        name=name)
    def move(src_hbm, i_hbm, o_hbm, idx_v, buf, in_sem, out_sem):
        wid = lax.axis_index("subcore") * info.num_cores + lax.axis_index("core")
        base = wid * per_worker
        src_base = lax.rem(base, n_src)
        pltpu.sync_copy(i_hbm.at[wid], idx_v)

        def fetch(s, slot):
            if scatter:
                rows = src_hbm.at[pl.ds(src_base + s * SC_CHUNK, SC_CHUNK)]
            else:
                rows = src_hbm.at[idx_v.at[s]]
            return pltpu.make_async_copy(rows, buf.at[slot], in_sem.at[slot])

        def flush(s, slot):
            if scatter:
                rows = o_hbm.at[idx_v.at[s]]
            else:
                rows = o_hbm.at[pl.ds(base + s * SC_CHUNK, SC_CHUNK)]
            return pltpu.make_async_copy(buf.at[slot], rows, out_sem.at[slot])

        fetch(0, 0).start()
        for s in range(n_chunks):
            slot = s % 2
            fetch(s, slot).wait()
            flush(s, slot).start()
            if s + 1 < n_chunks:
                if s >= 1:
                    flush(s - 1, 1 - slot).wait()
                fetch(s + 1, 1 - slot).start()
        flush(n_chunks - 2, n_chunks % 2).wait()
        flush(n_chunks - 1, (n_chunks - 1) % 2).wait()

    return move(src, idx)


def _experts_kernel(layer, sched_ref, x_ref, wg_hbm, wu_hbm, wd_hbm, y_ref,
                    wg_f32, wu_f32, wd_f32, wg_bf, wu_bf, wd_bf, sems):
    j = pl.program_id(0)
    tm = x_ref.shape[0] // ROW_CHUNKS
    used = j < sched_ref[SCHED_USED, j]
    expert = sched_ref[SCHED_EXPERT, j]

    def weight_copies(expert, slot):
        pairs = ((wg_hbm, wg_f32), (wu_hbm, wu_f32), (wd_hbm, wd_f32))
        return [pltpu.make_async_copy(src.at[layer, expert], dst.at[slot], sems.at[slot, m])
                for m, (src, dst) in enumerate(pairs)]

    @pl.when(used & (sched_ref[SCHED_FIRST, j] == 1))
    def _():
        slot = sched_ref[SCHED_SLOT, j]
        next_expert = sched_ref[SCHED_NEXT, j]

        @pl.when(j == 0)
        def _():
            for cp in weight_copies(expert, slot):
                cp.start()

        for cp in weight_copies(expert, slot):
            cp.wait()
        wg_bf[...] = wg_f32[slot].astype(BF16)
        wu_bf[...] = wu_f32[slot].astype(BF16)
        wd_bf[...] = wd_f32[slot].astype(BF16)

        @pl.when(next_expert >= 0)
        def _():
            for cp in weight_copies(next_expert, 1 - slot):
                cp.start()

    @pl.when(used)
    def _():
        row = lax.broadcasted_iota(jnp.int32, (tm, 1), 0)
        words = jnp.where(row < sched_ref[SCHED_VALID, j], _load_words(x_ref, tm), jnp.uint32(0))
        h = _unpack_rows(words).astype(BF16)
        a = jnp.dot(h, wg_bf[...], preferred_element_type=F32)
        b = jnp.dot(h, wu_bf[...], preferred_element_type=F32)
        t = (_silu(a) * b).astype(BF16)
        _store_words(y_ref, _pack_rows(jnp.dot(t, wd_bf[...], preferred_element_type=F32)))

    @pl.when(jnp.logical_not(used))
    def _():
        y_ref[...] = jnp.zeros_like(y_ref)


def _experts_call(layer, rows, sched, w_gate, w_up, w_down):
    n_rows = rows.shape[0]
    tm = MOE_TILE
    d = D_MODEL
    rows2 = rows.reshape(n_rows * ROW_CHUNKS, V7X_LANES)
    grid_spec = pltpu.PrefetchScalarGridSpec(
        num_scalar_prefetch=1,
        grid=(n_rows // tm,),
        in_specs=[
            pl.BlockSpec((tm * ROW_CHUNKS, V7X_LANES),
                         lambda j, sc: (jnp.minimum(j, sc[SCHED_USED, 0] - 1), 0)),
            pl.BlockSpec(memory_space=pl.ANY),
            pl.BlockSpec(memory_space=pl.ANY),
            pl.BlockSpec(memory_space=pl.ANY),
        ],
        out_specs=pl.BlockSpec((tm * ROW_CHUNKS, V7X_LANES), lambda j, sc: (j, 0)),
        scratch_shapes=[
            pltpu.VMEM((2, d, D_EXPERT), F32),
            pltpu.VMEM((2, d, D_EXPERT), F32),
            pltpu.VMEM((2, D_EXPERT, d), F32),
            pltpu.VMEM((d, D_EXPERT), BF16),
            pltpu.VMEM((d, D_EXPERT), BF16),
            pltpu.VMEM((D_EXPERT, d), BF16),
            pltpu.SemaphoreType.DMA((2, 3)),
        ],
    )
    y = pl.pallas_call(
        functools.partial(_experts_kernel, layer),
        grid_spec=grid_spec,
        out_shape=jax.ShapeDtypeStruct((n_rows * ROW_CHUNKS, V7X_LANES), U32),
        compiler_params=pltpu.CompilerParams(
            dimension_semantics=("arbitrary",),
            vmem_limit_bytes=V7X_VMEM_LIMIT_BYTES),
        name="experts_l%d" % layer,
    )(sched, rows2, w_gate, w_up, w_down)
    return y.reshape(n_rows, ROW_CHUNKS, V7X_LANES)


def _moe_rows(layer, mixer_outs, w_gate, w_up, w_down):
    plans = [_sorted_positions(eidx, rank, counts)
             for (_, _, eidx, _, rank, counts) in mixer_outs]
    sorted_rows = []
    for (_, h_rows, eidx, _, _, _), (pos_flat, _, n_rows) in zip(mixer_outs, plans):
        n_tok = eidx.shape[1]
        sorted_rows.append(_sc_move_rows(
            True, h_rows.reshape(n_tok, ROW_CHUNKS, V7X_LANES), pos_flat, n_rows, "sc_scatter_rows"))
    y_sorted = [_experts_call(layer, rows, sched, w_gate, w_up, w_down)
                for rows, (_, sched, _) in zip(sorted_rows, plans)]
    y_pairs = []
    for y, (pos_flat, _, _) in zip(y_sorted, plans):
        n_pairs = pos_flat.shape[0]
        moved = _sc_move_rows(False, y, pos_flat, n_pairs, "sc_gather_rows")
        y_pairs.append(moved.reshape(n_pairs * ROW_CHUNKS, V7X_LANES))
    return y_pairs


def _final_kernel(x_ref, y0_ref, y1_ref, wt_ref, modp_ref, fg_ref, *rest):
    o_ref = rest[-1]
    y = _moe_residual(x_ref, y0_ref, y1_ref, wt_ref, modp_ref[0, 0][5:6])
    ms = jnp.mean(y * y, axis=-1, keepdims=True)
    o_ref[...] = y * lax.rsqrt(ms + EPS) * fg_ref[...]


def _final_call(layer, batch, n_batch, x, y_pairs, wt, mod4, fg, out_prev):
    seq_len, d = x.shape
    tm = MIX_TILE
    tiles_per_seq = seq_len // tm
    in_specs = _combine_specs(seq_len) + [
        pl.BlockSpec((1, 1, 6, d), lambda i: (layer, batch, 0, 0)),
        _const_spec(fg.shape),
    ]
    args = [x, y_pairs, y_pairs, wt, mod4, fg]
    aliases = {}
    if out_prev is not None:
        in_specs.append(pl.BlockSpec(memory_space=pl.ANY))
        aliases = {len(args): 0}
        args.append(out_prev)
    return pl.pallas_call(
        _final_kernel,
        grid=(tiles_per_seq,),
        in_specs=in_specs,
        out_specs=pl.BlockSpec((tm, d), lambda i: (batch * tiles_per_seq + i, 0)),
        out_shape=jax.ShapeDtypeStruct((n_batch * seq_len, d), F32),
        input_output_aliases=aliases,
        compiler_params=pltpu.CompilerParams(
            dimension_semantics=("arbitrary",),
            vmem_limit_bytes=V7X_VMEM_LIMIT_BYTES),
        name="final_norm",
    )(*args)


def kernel(x, c, norm1_g, norm2_g, ada_w, ada_b, ab_w_in, pool_w, pool_scale, conf_conv_w, conf_conv_b, conf_ln_g, conf_ln_b, ab_w_out, cd_w_in, sconv_w, gmlp_ln_g, gmlp_ln_b, gmlp_ws, gmlp_bs, cd_w_out, router_w, router_bias, exp_w_gate, exp_w_up, exp_w_down, final_g):
    bsz, seq_len, d = x.shape
    n_tok = bsz * seq_len
    tm = MIX_TILE
    tiles_per_seq = seq_len // tm
    xf = x.reshape(n_tok, d)

    mod = _ada_mod(c, ada_w, ada_b)
    mod4 = mod.reshape(mod.shape[0], bsz, 6, d)

    rw_hi = router_w.astype(BF16)
    rw_lo = (router_w - rw_hi.astype(F32)).astype(BF16)
    rwt = jnp.concatenate([rw_hi.T, rw_lo.T], axis=0)
    rbias = router_bias.reshape(N_EXPERTS, 1)
    fg = final_g.reshape(1, d)

    weights_ab = [
        ab_w_in[0].astype(BF16), pool_w[0].astype(BF16), pool_scale[0].reshape(1, D_HALF),
        conf_conv_w[0], conf_conv_b[0].reshape(1, D_HALF), conf_ln_g[0].reshape(1, D_HALF),
        conf_ln_b[0].reshape(1, D_HALF), ab_w_out[0].astype(BF16),
    ]
    scratch_ab = [pltpu.VMEM((POOL_HIST + tm, D_HALF), F32),
                  pltpu.VMEM((CONV_HIST + tm, D_HALF), F32)]
    bsf = jnp.repeat(gmlp_bs[0].T, POOL_GROUP, axis=1)
    weights_cd = [
        cd_w_in[0].astype(BF16), sconv_w[0], gmlp_ln_g[0].reshape(1, D_HALF),
        gmlp_ln_b[0].reshape(1, D_HALF), gmlp_ws[0], bsf, cd_w_out[0].astype(BF16),
    ]
    scratch_cd = [pltpu.VMEM((SCONV_HIST + tm, D_HALF), F32)]
    experts = (exp_w_gate, exp_w_up, exp_w_down)

    batches = range(bsz)
    stage_ab = []
    for b in batches:
        x_spec = pl.BlockSpec((tm, d), lambda i, b=b: (b * tiles_per_seq + i, 0))
        stage_ab.append(_mixer_call(
            _mixer_ab_kernel, 0, b, [xf], [x_spec], mod4, norm1_g[0:1], norm2_g[0:1],
            weights_ab, rwt, rbias, scratch_ab, seq_len, "mixer_ab"))
    y_pairs0 = _moe_rows(0, stage_ab, *experts)

    stage_cd = []
    for b in batches:
        x1, _, _, wsel0, _, _ = stage_ab[b]
        prev_mod_spec = pl.BlockSpec((1, 1, 6, d), lambda i, b=b: (0, b, 0, 0))
        stage_cd.append(_mixer_call(
            _mixer_cd_kernel, 1, b, [x1, y_pairs0[b], y_pairs0[b], wsel0.T, mod4],
            _combine_specs(seq_len) + [prev_mod_spec], mod4, norm1_g[1:2], norm2_g[1:2],
            weights_cd, rwt, rbias, scratch_cd, seq_len, "mixer_cd"))
    y_pairs1 = _moe_rows(1, stage_cd, *experts)

    out = None
    for b in batches:
        x3, _, _, wsel1, _, _ = stage_cd[b]
        out = _final_call(1, b, bsz, x3, y_pairs1[b], wsel1.T, mod4, fg, out)
    return out.reshape(bsz, seq_len, d)
```

```python
import functools

import jax
import jax.numpy as jnp
from jax import lax
from jax.experimental import pallas as pl
from jax.experimental.pallas import tpu as pltpu
from jax.experimental.pallas import tpu_sc as plsc

D_MODEL = 1024
EPS = 1e-6
POOL_WINDOWS = (2, 4, 8, 16)
POOL_GROUP = 128
D_HALF = 512
CONF_KERNEL = 31
SCONV_KERNEL = 3
CHUNK = 128
GMLP_HEADS = 4
N_EXPERTS = 16
N_GROUPS = 4
EXPERTS_PER_GROUP = 4
TOP_K = 2
D_EXPERT = 512

V7X_LANES = 128
V7X_SUBLANES = 8
V7X_VMEM_LIMIT_BYTES = 56 * 1024 * 1024

MIX_TILE = 512
MOE_TILE = 512
SC_CHUNK = 64
ROW_CHUNKS = D_MODEL // (2 * V7X_LANES)
CONV_HIST = 32
POOL_HIST = 16
SCONV_HIST = 8

BF16 = jnp.bfloat16
F32 = jnp.float32
U32 = jnp.uint32


def _rms_mod(x, g_row, shift_row, scale_row):
    ms = jnp.mean(x * x, axis=-1, keepdims=True)
    gain = g_row * (1.0 + scale_row)
    return (x * lax.rsqrt(ms + EPS)) * gain + shift_row


def _layer_norm(x, g_row, b_row):
    mu = jnp.mean(x, axis=-1, keepdims=True)
    xc = x - mu
    var = jnp.mean(xc * xc, axis=-1, keepdims=True)
    return xc * lax.rsqrt(var + EPS) * g_row + b_row


def _sigmoid(x):
    return 1.0 / (1.0 + jnp.exp(-x))


def _silu(x):
    return x * _sigmoid(x)


def _gelu_tanh(x):
    c = 0.7978845608028654
    return 0.5 * x * (1.0 + jnp.tanh(c * (x + 0.044715 * (x * x * x))))


def _shift_rows(x, r):
    n, c = x.shape
    if r == V7X_SUBLANES:
        return jnp.concatenate([x[:r], x[:n - r]], axis=0)
    g = x.reshape(n // V7X_SUBLANES, V7X_SUBLANES, c)
    rot = pltpu.roll(g, r, axis=1)
    prev = jnp.concatenate([rot[:1], rot[:-1]], axis=0)
    sub = lax.broadcasted_iota(jnp.int32, g.shape, 1)
    return jnp.where(sub < r, prev, rot).reshape(n, c)


def _load_words(ref, n_rows):
    return jnp.concatenate(
        [ref[pl.ds(c, n_rows, stride=ROW_CHUNKS), :] for c in range(ROW_CHUNKS)], axis=1)


def _store_words(ref, words):
    n_rows = words.shape[0]
    for c in range(ROW_CHUNKS):
        ref[pl.ds(c, n_rows, stride=ROW_CHUNKS), :] = words[:, c * V7X_LANES:(c + 1) * V7X_LANES]


def _pack_rows(val):
    half = val.shape[1] // 2
    return pltpu.pack_elementwise([val[:, :half], val[:, half:]], packed_dtype=BF16)


def _unpack_rows(words):
    halves = [pltpu.unpack_elementwise(words, index=i, packed_dtype=BF16, unpacked_dtype=F32)
              for i in range(2)]
    return jnp.concatenate(halves, axis=1)


def _ada_kernel(ct_ref, w_ref, b_ref, o_ref):
    ct = ct_ref[...]
    cond = _silu(ct)
    w = w_ref[0]
    nb = ct.shape[1]
    for b in range(nb):
        col = cond[:, b:b + 1]
        o_ref[0, b:b + 1, :] = jnp.sum(col * w, axis=0, keepdims=True) + b_ref[0]


def _ada_mod(c, ada_w, ada_b):
    depth, d, six_d = ada_w.shape
    bsz = c.shape[0]
    nb = D_MODEL
    return pl.pallas_call(
        _ada_kernel,
        grid=(depth, six_d // nb),
        in_specs=[
            pl.BlockSpec((d, bsz), lambda l, j: (0, 0)),
            pl.BlockSpec((1, d, nb), lambda l, j: (l, 0, j)),
            pl.BlockSpec((1, 1, nb), lambda l, j: (l, 0, j)),
        ],
        out_specs=pl.BlockSpec((1, bsz, nb), lambda l, j: (l, 0, j)),
        out_shape=jax.ShapeDtypeStruct((depth, bsz, six_d), F32),
        compiler_params=pltpu.CompilerParams(
            dimension_semantics=("arbitrary", "arbitrary"),
            vmem_limit_bytes=V7X_VMEM_LIMIT_BYTES),
        name="ada_mod",
    )(c.T, ada_w, ada_b.reshape(depth, 1, six_d))


def _route(h2_bf, rwt_ref, rbias_ref, eidx_ref, wsel_ref, rank_ref, counts_ref, cnt_ref):
    nt = (((1,), (1,)), ((), ()))
    r = lax.dot_general(rwt_ref[...], h2_bf, nt, preferred_element_type=F32)
    logits = r[:N_EXPERTS] + r[N_EXPERTS:]
    m = jnp.max(logits, axis=0, keepdims=True)
    ex = jnp.exp(logits - m)
    probs = ex / jnp.sum(ex, axis=0, keepdims=True)
    sel = probs + rbias_ref[...]
    s = [sel[e:e + 1] for e in range(N_EXPERTS)]
    p = [probs[e:e + 1] for e in range(N_EXPERTS)]
    best = None
    gi = None
    for g in range(N_GROUPS):
        a, b, c, d = s[4 * g:4 * g + 4]
        hi1, lo1 = jnp.maximum(a, b), jnp.minimum(a, b)
        hi2, lo2 = jnp.maximum(c, d), jnp.minimum(c, d)
        top1 = jnp.maximum(hi1, hi2)
        top2 = jnp.maximum(jnp.minimum(hi1, hi2), jnp.maximum(lo1, lo2))
        score = top1 + top2
        if g == 0:
            best, gi = score, jnp.zeros(score.shape, jnp.int32)
        else:
            upd = score > best
            gi = jnp.where(upd, g, gi)
            best = jnp.where(upd, score, best)
    v, q = [], []
    for j in range(EXPERTS_PER_GROUP):
        vj, qj = s[j], p[j]
        for g in range(1, N_GROUPS):
            pick = gi == g
            vj = jnp.where(pick, s[4 * g + j], vj)
            qj = jnp.where(pick, p[4 * g + j], qj)
        v.append(vj)
        q.append(qj)
    i1 = jnp.zeros(gi.shape, jnp.int32)
    m1 = v[0]
    for j in range(1, EXPERTS_PER_GROUP):
        upd = v[j] > m1
        i1 = jnp.where(upd, j, i1)
        m1 = jnp.where(upd, v[j], m1)
    i2 = jnp.zeros(gi.shape, jnp.int32)
    m2 = jnp.full(m1.shape, -jnp.inf, F32)
    for j in range(EXPERTS_PER_GROUP):
        cand = (i1 != j) & (v[j] > m2)
        i2 = jnp.where(cand, j, i2)
        m2 = jnp.where(cand, v[j], m2)
    pa = q[0]
    pb = q[0]
    for j in range(1, EXPERTS_PER_GROUP):
        pa = jnp.where(i1 == j, q[j], pa)
        pb = jnp.where(i2 == j, q[j], pb)
    tot = pa + pb
    e0 = gi * EXPERTS_PER_GROUP + i1
    e1 = gi * EXPERTS_PER_GROUP + i2
    t = h2_bf.shape[0]
    eidx_ref[0:1, :] = e0
    eidx_ref[1:2, :] = e1
    w_rows = jnp.concatenate(
        [pa / tot, pb / tot, jnp.zeros((V7X_LANES - TOP_K, t), F32)], axis=0)
    wsel_ref[...] = w_rows.T

    e_iota = lax.broadcasted_iota(jnp.int32, (N_EXPERTS, t), 0)
    oh0 = e_iota == e0
    oh1 = e_iota == e1
    both = jnp.where(oh0 | oh1, 1.0, 0.0)
    r_i = lax.broadcasted_iota(jnp.int32, (V7X_LANES, V7X_LANES), 0)
    c_i = lax.broadcasted_iota(jnp.int32, (V7X_LANES, V7X_LANES), 1)
    before = jnp.where(r_i < c_i, 1.0, 0.0).astype(BF16)
    run = cnt_ref[...]
    rank0, rank1 = [], []
    for blk in range(t // V7X_LANES):
        lanes = slice(blk * V7X_LANES, (blk + 1) * V7X_LANES)
        b = both[:, lanes]
        pre = jnp.dot(b.astype(BF16), before, preferred_element_type=F32) + run
        rank0.append(jnp.sum(jnp.where(oh0[:, lanes], pre, 0.0), axis=0, keepdims=True))
        rank1.append(jnp.sum(jnp.where(oh1[:, lanes], pre, 0.0), axis=0, keepdims=True))
        run = run + jnp.sum(b, axis=1, keepdims=True)
    cnt_ref[...] = run
    rank_ref[0:1, :] = jnp.concatenate(rank0, axis=1).astype(jnp.int32)
    rank_ref[1:2, :] = jnp.concatenate(rank1, axis=1).astype(jnp.int32)
    counts_ref[...] = jnp.broadcast_to(run, counts_ref.shape).astype(jnp.int32)


def _finish_mixer(x, m, mod, n2g_ref, rwt_ref, rbias_ref,
                  x1_ref, h2_ref, eidx_ref, wsel_ref, rank_ref, counts_ref, cnt_ref):
    x1 = x + mod[2:3] * m
    x1_ref[...] = x1
    h2 = _rms_mod(x1, n2g_ref[...], mod[3:4], mod[4:5])
    h2_bf = h2.astype(BF16)
    _store_words(h2_ref, _pack_rows(h2))
    _route(h2_bf, rwt_ref, rbias_ref, eidx_ref, wsel_ref, rank_ref, counts_ref, cnt_ref)


def _moe_residual(x_ref, y0_ref, y1_ref, wt_ref, g2_row):
    tm = x_ref.shape[0]
    wt = wt_ref[...]
    y0 = _unpack_rows(_load_words(y0_ref, tm))
    y1 = _unpack_rows(_load_words(y1_ref, tm))
    y = wt[:, 0:1] * y0 + wt[:, 1:2] * y1
    return x_ref[...] + g2_row * y


def _mixer_ab_kernel(x_ref, mod_ref, n1g_ref, n2g_ref, win_ref, poolw_ref, pscale_ref,
                     convw_ref, convb_ref, lng_ref, lnb_ref, wout_ref, rwt_ref, rbias_ref,
                     x1_ref, h2_ref, eidx_ref, wsel_ref, rank_ref, counts_ref,
                     pool_ext, conv_ext, cnt_ref):
    seq_tile = pl.program_id(0)
    tm = x_ref.shape[0]

    @pl.when(seq_tile == 0)
    def _():
        pool_ext[0:POOL_HIST, :] = jnp.zeros((POOL_HIST, D_HALF), F32)
        conv_ext[0:CONV_HIST, :] = jnp.zeros((CONV_HIST, D_HALF), F32)
        cnt_ref[...] = jnp.zeros_like(cnt_ref)

    x = x_ref[...]
    mod = mod_ref[0, 0]
    h = _rms_mod(x, n1g_ref[...], mod[0:1], mod[1:2]).astype(BF16)
    z = jnp.dot(h, win_ref[...], preferred_element_type=F32)
    zp = z[:, :D_HALF]
    glu = z[:, D_HALF:2 * D_HALF] * _sigmoid(z[:, 2 * D_HALF:])
    pool_ext[POOL_HIST:POOL_HIST + tm, :] = zp
    conv_ext[CONV_HIST:CONV_HIST + tm, :] = glu

    row = lax.broadcasted_iota(jnp.int32, (tm, 1), 0)
    pos1 = (seq_tile * tm + row + 1).astype(F32)
    pool_out = []
    for g, w in enumerate(POOL_WINDOWS):
        cols = slice(g * POOL_GROUP, (g + 1) * POOL_GROUP)
        acc = pool_ext[:, cols]
        span = 1
        while span < w:
            acc = acc + _shift_rows(acc, span)
            span *= 2
        wsum = acc[POOL_HIST:POOL_HIST + tm]
        inv_cnt = 1.0 / jnp.minimum(pos1, float(w))
        diff = wsum * inv_cnt - zp[:, cols]
        po = jnp.dot(diff.astype(BF16), poolw_ref[g], preferred_element_type=F32)
        pool_out.append(po * pscale_ref[:, cols])

    convw = convw_ref[...]
    ext_rows = tm + V7X_SUBLANES
    conv = None
    for r in range(V7X_SUBLANES):
        vr = None
        for a in range(CONV_HIST // V7X_SUBLANES):
            lag = V7X_SUBLANES * a + r
            if lag >= CONF_KERNEL:
                continue
            k = CONF_KERNEL - 1 - lag
            start = CONV_HIST - V7X_SUBLANES - V7X_SUBLANES * a
            term = convw[k:k + 1, :] * conv_ext[start:start + ext_rows, :]
            vr = term if vr is None else vr + term
        if r:
            vr = _shift_rows(vr, r)
        conv = vr if conv is None else conv + vr
    conv = conv[V7X_SUBLANES:V7X_SUBLANES + tm] + convb_ref[...]
    conf = _silu(_layer_norm(conv, lng_ref[...], lnb_ref[...]))

    pool_ext[0:POOL_HIST, :] = zp[tm - POOL_HIST:tm]
    conv_ext[0:CONV_HIST, :] = glu[tm - CONV_HIST:tm]

    m = jnp.dot(conf.astype(BF16), wout_ref[D_HALF:, :], preferred_element_type=F32)
    for g in range(len(POOL_WINDOWS)):
        rows = slice(g * POOL_GROUP, (g + 1) * POOL_GROUP)
        m = m + jnp.dot(pool_out[g].astype(BF16), wout_ref[rows, :], preferred_element_type=F32)
    _finish_mixer(x, m, mod, n2g_ref, rwt_ref, rbias_ref,
                  x1_ref, h2_ref, eidx_ref, wsel_ref, rank_ref, counts_ref, cnt_ref)


def _mixer_cd_kernel(x_ref, y0_ref, y1_ref, wt_ref, modp_ref,
                     mod_ref, n1g_ref, n2g_ref, win_ref, sconvw_ref, lng_ref, lnb_ref,
                     ws_ref, bsf_ref, wout_ref, rwt_ref, rbias_ref,
                     x1_ref, h2_ref, eidx_ref, wsel_ref, rank_ref, counts_ref,
                     sconv_ext, cnt_ref):
    tm = x_ref.shape[0]

    @pl.when(pl.program_id(0) == 0)
    def _():
        sconv_ext[0:SCONV_HIST, :] = jnp.zeros((SCONV_HIST, D_HALF), F32)
        cnt_ref[...] = jnp.zeros_like(cnt_ref)

    x = _moe_residual(x_ref, y0_ref, y1_ref, wt_ref, modp_ref[0, 0][5:6])
    mod = mod_ref[0, 0]
    h = _rms_mod(x, n1g_ref[...], mod[0:1], mod[1:2]).astype(BF16)
    def proj(lo, hi):
        return jnp.dot(h, win_ref[:, lo:hi], preferred_element_type=F32)

    v = _layer_norm(_gelu_tanh(proj(4 * D_HALF, 5 * D_HALF)), lng_ref[...], lnb_ref[...])
    u = _gelu_tanh(proj(3 * D_HALF, 4 * D_HALF))
    ch = proj(D_HALF, 2 * D_HALF) * proj(2 * D_HALF, 3 * D_HALF)
    bg = proj(0, D_HALF)

    sconv_ext[SCONV_HIST:SCONV_HIST + tm, :] = ch
    sw = sconvw_ref[...]
    ext = sconv_ext[...]
    conv = sw[2:3, :] * ext
    conv = conv + sw[1:2, :] * _shift_rows(ext, 1)
    conv = conv + sw[0:1, :] * _shift_rows(ext, 2)
    sc_out = bg * conv[SCONV_HIST:SCONV_HIST + tm]
    sconv_ext[0:SCONV_HIST, :] = ch[tm - SCONV_HIST:tm]

    r_i = lax.broadcasted_iota(jnp.int32, (CHUNK, CHUNK), 0)
    c_i = lax.broadcasted_iota(jnp.int32, (CHUNK, CHUNK), 1)
    tril = c_i <= r_i
    wm = [jnp.where(tril, ws_ref[hd], 0.0).astype(BF16) for hd in range(GMLP_HEADS)]
    v_bf = v.astype(BF16)
    bsf = bsf_ref[...]
    gm_rows = []
    for n in range(tm // CHUNK):
        rows = slice(n * CHUNK, (n + 1) * CHUNK)
        heads = []
        for hd in range(GMLP_HEADS):
            cols = slice(hd * POOL_GROUP, (hd + 1) * POOL_GROUP)
            heads.append(jnp.dot(wm[hd], v_bf[rows, cols], preferred_element_type=F32))
        mixed = jnp.concatenate(heads, axis=1) + bsf
        gm_rows.append(u[rows] * mixed)
    gm_out = jnp.concatenate(gm_rows, axis=0)

    m = jnp.dot(sc_out.astype(BF16), wout_ref[:D_HALF, :], preferred_element_type=F32)
    m = m + jnp.dot(gm_out.astype(BF16), wout_ref[D_HALF:, :], preferred_element_type=F32)
    _finish_mixer(x, m, mod, n2g_ref, rwt_ref, rbias_ref,
                  x1_ref, h2_ref, eidx_ref, wsel_ref, rank_ref, counts_ref, cnt_ref)


def _const_spec(shape):
    nd = len(shape)
    return pl.BlockSpec(shape, lambda i: (0,) * nd)


def _mixer_call(kernel_fn, layer, batch, stream_inputs, stream_specs, mod4, n1g, n2g, weights,
                rwt, rbias, scratch, seq_len, name):
    n_tok = seq_len
    d = D_MODEL
    tm = MIX_TILE
    in_specs = stream_specs + [
        pl.BlockSpec((1, 1, 6, d), lambda i: (layer, batch, 0, 0)),
        _const_spec(n1g.shape),
        _const_spec(n2g.shape),
    ] + [_const_spec(w.shape) for w in weights] + [_const_spec(rwt.shape), _const_spec(rbias.shape)]
    out_specs = [
        pl.BlockSpec((tm, d), lambda i: (i, 0)),
        pl.BlockSpec((tm * ROW_CHUNKS, V7X_LANES), lambda i: (i, 0)),
        pl.BlockSpec((TOP_K, tm), lambda i: (0, i)),
        pl.BlockSpec((tm, V7X_LANES), lambda i: (i, 0)),
        pl.BlockSpec((TOP_K, tm), lambda i: (0, i)),
        pl.BlockSpec((N_EXPERTS, V7X_LANES), lambda i: (0, 0)),
    ]
    out_shape = [
        jax.ShapeDtypeStruct((n_tok, d), F32),
        jax.ShapeDtypeStruct((n_tok * ROW_CHUNKS, V7X_LANES), U32),
        jax.ShapeDtypeStruct((TOP_K, n_tok), jnp.int32),
        jax.ShapeDtypeStruct((n_tok, V7X_LANES), F32),
        jax.ShapeDtypeStruct((TOP_K, n_tok), jnp.int32),
        jax.ShapeDtypeStruct((N_EXPERTS, V7X_LANES), jnp.int32),
    ]
    return pl.pallas_call(
        kernel_fn,
        grid=(n_tok // tm,),
        in_specs=in_specs,
        out_specs=out_specs,
        out_shape=out_shape,
        scratch_shapes=scratch + [pltpu.VMEM((N_EXPERTS, 1), F32)],
        compiler_params=pltpu.CompilerParams(
            dimension_semantics=("arbitrary",),
            vmem_limit_bytes=V7X_VMEM_LIMIT_BYTES),
        name=name,
    )(*stream_inputs, mod4, n1g, n2g, *weights, rwt, rbias)


def _combine_specs(n_tok):
    tm = MIX_TILE
    n_tiles = n_tok // tm
    return [
        pl.BlockSpec((tm, D_MODEL), lambda i: (i, 0)),
        pl.BlockSpec((tm * ROW_CHUNKS, V7X_LANES), lambda i: (i, 0)),
        pl.BlockSpec((tm * ROW_CHUNKS, V7X_LANES), lambda i: (n_tiles + i, 0)),
        pl.BlockSpec((tm, V7X_LANES), lambda i: (i, 0)),
    ]


SCHED_EXPERT, SCHED_VALID, SCHED_USED, SCHED_FIRST, SCHED_NEXT, SCHED_SLOT = range(6)
SCHED_ROWS = V7X_SUBLANES


def _plan_kernel(eidx_ref, rank_ref, counts_ref, pos_ref, sched_ref):
    i32 = jnp.int32
    shift = MOE_TILE.bit_length() - 1
    cnt = counts_ref[:, 0:1]
    padded = ((cnt + (MOE_TILE - 1)) >> shift) << shift
    seg = [padded[e:e + 1] for e in range(N_EXPERTS)]
    starts, ends = [], []
    run = jnp.zeros((1, 1), i32)
    for e in range(N_EXPERTS):
        starts.append(run)
        run = run + seg[e]
        ends.append(run)
    total = run
    nexts = [None] * N_EXPERTS
    nxt = jnp.full((1, 1), -1, i32)
    for e in reversed(range(N_EXPERTS)):
        nexts[e] = nxt
        nxt = jnp.where(seg[e] > 0, e, nxt)
    slots = []
    seen = jnp.zeros((1, 1), i32)
    for e in range(N_EXPERTS):
        slots.append(seen & 1)
        seen = seen + (seg[e] > 0).astype(i32)

    eidx = eidx_ref[...]
    pos = rank_ref[...]
    tile_row0 = lax.broadcasted_iota(i32, (1, V7X_LANES), 1) * MOE_TILE
    te = jnp.zeros((1, V7X_LANES), i32)
    for e in range(N_EXPERTS):
        pos = pos + jnp.where(eidx == e, starts[e], 0)
        te = te + (tile_row0 >= ends[e]).astype(i32)
    pos_ref[...] = pos
    te = jnp.minimum(te, N_EXPERTS - 1)

    def of_tile(per_expert):
        acc = jnp.zeros((1, V7X_LANES), i32)
        for e in range(N_EXPERTS):
            acc = acc + jnp.where(te == e, per_expert[e], 0)
        return acc

    valid_end = of_tile([starts[e] + cnt[e:e + 1] for e in range(N_EXPERTS)])
    used = tile_row0 < total
    rows = {
        SCHED_EXPERT: te,
        SCHED_VALID: jnp.clip(valid_end - tile_row0, 0, MOE_TILE),
        SCHED_USED: jnp.broadcast_to(total >> shift, (1, V7X_LANES)),
        SCHED_FIRST: (used & (of_tile(starts) == tile_row0)).astype(i32),
        SCHED_NEXT: of_tile(nexts),
        SCHED_SLOT: of_tile(slots),
    }
    for r in range(SCHED_ROWS):
        sched_ref[r:r + 1, :] = rows.get(r, jnp.zeros((1, V7X_LANES), i32))


def _sorted_positions(eidx, rank, counts):
    n_pairs = eidx.shape[0] * eidx.shape[1]
    n_rows = n_pairs + N_EXPERTS * MOE_TILE
    assert n_rows // MOE_TILE <= V7X_LANES
    pos, sched = pl.pallas_call(
        _plan_kernel,
        out_shape=[jax.ShapeDtypeStruct(eidx.shape, jnp.int32),
                   jax.ShapeDtypeStruct((SCHED_ROWS, V7X_LANES), jnp.int32)],
        name="moe_plan",
    )(eidx, rank, counts)
    return pos.reshape(n_pairs), sched, n_rows


def _sc_move_rows(scatter, src, pos_flat, n_out_rows, name):
    info = plsc.get_sparse_core_info()
    n_workers = info.num_cores * info.num_subcores
    n_pairs = pos_flat.shape[0]
    n_src = src.shape[0]
    per_worker = n_pairs // n_workers
    n_chunks = per_worker // SC_CHUNK
    assert per_worker * n_workers == n_pairs and n_chunks * SC_CHUNK == per_worker
    assert n_src % per_worker == 0
    idx = pos_flat.reshape(n_workers, n_chunks, SC_CHUNK)
    mesh = plsc.VectorSubcoreMesh(core_axis_name="core", subcore_axis_name="subcore")

    @functools.partial(
        pl.kernel,
        out_type=jax.ShapeDtypeStruct((n_out_rows, ROW_CHUNKS, V7X_LANES), U32),
        mesh=mesh,
        scratch_types=[
            pltpu.VMEM((n_chunks, SC_CHUNK), jnp.int32),
            pltpu.VMEM((2, SC_CHUNK, ROW_CHUNKS, V7X_LANES), U32),
            pltpu.SemaphoreType.DMA((2,)),
            pltpu.SemaphoreType.DMA((2,)),
        ],
        name=name)
    def move(src_hbm, i_hbm, o_hbm, idx_v, buf, in_sem, out_sem):
        wid = lax.axis_index("subcore") * info.num_cores + lax.axis_index("core")
        base = wid * per_worker
        src_base = lax.rem(base, n_src)
        pltpu.sync_copy(i_hbm.at[wid], idx_v)

        def fetch(s, slot):
            if scatter:
                rows = src_hbm.at[pl.ds(src_base + s * SC_CHUNK, SC_CHUNK)]
            else:
                rows = src_hbm.at[idx_v.at[s]]
            return pltpu.make_async_copy(rows, buf.at[slot], in_sem.at[slot])

        def flush(s, slot):
            if scatter:
                rows = o_hbm.at[idx_v.at[s]]
            else:
                rows = o_hbm.at[pl.ds(base + s * SC_CHUNK, SC_CHUNK)]
            return pltpu.make_async_copy(buf.at[slot], rows, out_sem.at[slot])

        fetch(0, 0).start()
        for s in range(n_chunks):
            slot = s % 2
            fetch(s, slot).wait()
            flush(s, slot).start()
            if s + 1 < n_chunks:
                if s >= 1:
                    flush(s - 1, 1 - slot).wait()
                fetch(s + 1, 1 - slot).start()
        flush(n_chunks - 2, n_chunks % 2).wait()
        flush(n_chunks - 1, (n_chunks - 1) % 2).wait()

    return move(src, idx)


def _experts_kernel(layer, sched_ref, x_ref, wg_hbm, wu_hbm, wd_hbm, y_ref,
                    wg_f32, wu_f32, wd_f32, wg_bf, wu_bf, wd_bf, sems):
    j = pl.program_id(0)
    tm = x_ref.shape[0] // ROW_CHUNKS
    used = j < sched_ref[SCHED_USED, j]
    expert = sched_ref[SCHED_EXPERT, j]

    def weight_copies(expert, slot):
        pairs = ((wg_hbm, wg_f32), (wu_hbm, wu_f32), (wd_hbm, wd_f32))
        return [pltpu.make_async_copy(src.at[layer, expert], dst.at[slot], sems.at[slot, m])
                for m, (src, dst) in enumerate(pairs)]

    @pl.when(used & (sched_ref[SCHED_FIRST, j] == 1))
    def _():
        slot = sched_ref[SCHED_SLOT, j]
        next_expert = sched_ref[SCHED_NEXT, j]

        @pl.when(j == 0)
        def _():
            for cp in weight_copies(expert, slot):
                cp.start()

        for cp in weight_copies(expert, slot):
            cp.wait()
        wg_bf[...] = wg_f32[slot].astype(BF16)
        wu_bf[...] = wu_f32[slot].astype(BF16)
        wd_bf[...] = wd_f32[slot].astype(BF16)

        @pl.when(next_expert >= 0)
        def _():
            for cp in weight_copies(next_expert, 1 - slot):
                cp.start()

    @pl.when(used)
    def _():
        row = lax.broadcasted_iota(jnp.int32, (tm, 1), 0)
        words = jnp.where(row < sched_ref[SCHED_VALID, j], _load_words(x_ref, tm), jnp.uint32(0))
        h = _unpack_rows(words).astype(BF16)
        a = jnp.dot(h, wg_bf[...], preferred_element_type=F32)
        b = jnp.dot(h, wu_bf[...], preferred_element_type=F32)
        t = (_silu(a) * b).astype(BF16)
        _store_words(y_ref, _pack_rows(jnp.dot(t, wd_bf[...], preferred_element_type=F32)))

    @pl.when(jnp.logical_not(used))
    def _():
        y_ref[...] = jnp.zeros_like(y_ref)


def _experts_call(layer, rows, sched, w_gate, w_up, w_down):
    n_rows = rows.shape[0]
    tm = MOE_TILE
    d = D_MODEL
    rows2 = rows.reshape(n_rows * ROW_CHUNKS, V7X_LANES)
    grid_spec = pltpu.PrefetchScalarGridSpec(
        num_scalar_prefetch=1,
        grid=(n_rows // tm,),
        in_specs=[
            pl.BlockSpec((tm * ROW_CHUNKS, V7X_LANES),
                         lambda j, sc: (jnp.minimum(j, sc[SCHED_USED, 0] - 1), 0)),
            pl.BlockSpec(memory_space=pl.ANY),
            pl.BlockSpec(memory_space=pl.ANY),
            pl.BlockSpec(memory_space=pl.ANY),
        ],
        out_specs=pl.BlockSpec((tm * ROW_CHUNKS, V7X_LANES), lambda j, sc: (j, 0)),
        scratch_shapes=[
            pltpu.VMEM((2, d, D_EXPERT), F32),
            pltpu.VMEM((2, d, D_EXPERT), F32),
            pltpu.VMEM((2, D_EXPERT, d), F32),
            pltpu.VMEM((d, D_EXPERT), BF16),
            pltpu.VMEM((d, D_EXPERT), BF16),
            pltpu.VMEM((D_EXPERT, d), BF16),
            pltpu.SemaphoreType.DMA((2, 3)),
        ],
    )
    y = pl.pallas_call(
        functools.partial(_experts_kernel, layer),
        grid_spec=grid_spec,
        out_shape=jax.ShapeDtypeStruct((n_rows * ROW_CHUNKS, V7X_LANES), U32),
        compiler_params=pltpu.CompilerParams(
            dimension_semantics=("arbitrary",),
            vmem_limit_bytes=V7X_VMEM_LIMIT_BYTES),
        name="experts_l%d" % layer,
    )(sched, rows2, w_gate, w_up, w_down)
    return y.reshape(n_rows, ROW_CHUNKS, V7X_LANES)


def _moe_rows(layer, mixer_outs, w_gate, w_up, w_down):
    plans = [_sorted_positions(eidx, rank, counts)
             for (_, _, eidx, _, rank, counts) in mixer_outs]
    sorted_rows = []
    for (_, h_rows, eidx, _, _, _), (pos_flat, _, n_rows) in zip(mixer_outs, plans):
        n_tok = eidx.shape[1]
        sorted_rows.append(_sc_move_rows(
            True, h_rows.reshape(n_tok, ROW_CHUNKS, V7X_LANES), pos_flat, n_rows, "sc_scatter_rows"))
    y_sorted = [_experts_call(layer, rows, sched, w_gate, w_up, w_down)
                for rows, (_, sched, _) in zip(sorted_rows, plans)]
    y_pairs = []
    for y, (pos_flat, _, _) in zip(y_sorted, plans):
        n_pairs = pos_flat.shape[0]
        moved = _sc_move_rows(False, y, pos_flat, n_pairs, "sc_gather_rows")
        y_pairs.append(moved.reshape(n_pairs * ROW_CHUNKS, V7X_LANES))
    return y_pairs


def _final_kernel(x_ref, y0_ref, y1_ref, wt_ref, modp_ref, fg_ref, *rest):
    o_ref = rest[-1]
    y = _moe_residual(x_ref, y0_ref, y1_ref, wt_ref, modp_ref[0, 0][5:6])
    ms = jnp.mean(y * y, axis=-1, keepdims=True)
    o_ref[...] = y * lax.rsqrt(ms + EPS) * fg_ref[...]


def _final_call(layer, batch, n_batch, x, y_pairs, wt, mod4, fg, out_prev):
    seq_len, d = x.shape
    tm = MIX_TILE
    tiles_per_seq = seq_len // tm
    in_specs = _combine_specs(seq_len) + [
        pl.BlockSpec((1, 1, 6, d), lambda i: (layer, batch, 0, 0)),
        _const_spec(fg.shape),
    ]
    args = [x, y_pairs, y_pairs, wt, mod4, fg]
    aliases = {}
    if out_prev is not None:
        in_specs.append(pl.BlockSpec(memory_space=pl.ANY))
        aliases = {len(args): 0}
        args.append(out_prev)
    return pl.pallas_call(
        _final_kernel,
        grid=(tiles_per_seq,),
        in_specs=in_specs,
        out_specs=pl.BlockSpec((tm, d), lambda i: (batch * tiles_per_seq + i, 0)),
        out_shape=jax.ShapeDtypeStruct((n_batch * seq_len, d), F32),
        input_output_aliases=aliases,
        compiler_params=pltpu.CompilerParams(
            dimension_semantics=("arbitrary",),
            vmem_limit_bytes=V7X_VMEM_LIMIT_BYTES),
        name="final_norm",
    )(*args)


def kernel(x, c, norm1_g, norm2_g, ada_w, ada_b, ab_w_in, pool_w, pool_scale, conf_conv_w, conf_conv_b, conf_ln_g, conf_ln_b, ab_w_out, cd_w_in, sconv_w, gmlp_ln_g, gmlp_ln_b, gmlp_ws, gmlp_bs, cd_w_out, router_w, router_bias, exp_w_gate, exp_w_up, exp_w_down, final_g):
    bsz, seq_len, d = x.shape
    n_tok = bsz * seq_len
    tm = MIX_TILE
    tiles_per_seq = seq_len // tm
    xf = x.reshape(n_tok, d)

    mod = _ada_mod(c, ada_w, ada_b)
    mod4 = mod.reshape(mod.shape[0], bsz, 6, d)

    rw_hi = router_w.astype(BF16)
    rw_lo = (router_w - rw_hi.astype(F32)).astype(BF16)
    rwt = jnp.concatenate([rw_hi.T, rw_lo.T], axis=0)
    rbias = router_bias.reshape(N_EXPERTS, 1)
    fg = final_g.reshape(1, d)

    weights_ab = [
        ab_w_in[0].astype(BF16), pool_w[0].astype(BF16), pool_scale[0].reshape(1, D_HALF),
        conf_conv_w[0], conf_conv_b[0].reshape(1, D_HALF), conf_ln_g[0].reshape(1, D_HALF),
        conf_ln_b[0].reshape(1, D_HALF), ab_w_out[0].astype(BF16),
    ]
    scratch_ab = [pltpu.VMEM((POOL_HIST + tm, D_HALF), F32),
                  pltpu.VMEM((CONV_HIST + tm, D_HALF), F32)]
    bsf = jnp.repeat(gmlp_bs[0].T, POOL_GROUP, axis=1)
    weights_cd = [
        cd_w_in[0].astype(BF16), sconv_w[0], gmlp_ln_g[0].reshape(1, D_HALF),
        gmlp_ln_b[0].reshape(1, D_HALF), gmlp_ws[0], bsf, cd_w_out[0].astype(BF16),
    ]
    scratch_cd = [pltpu.VMEM((SCONV_HIST + tm, D_HALF), F32)]
    experts = (exp_w_gate, exp_w_up, exp_w_down)

    batches = range(bsz)
    stage_ab = []
    for b in batches:
        x_spec = pl.BlockSpec((tm, d), lambda i, b=b: (b * tiles_per_seq + i, 0))
        stage_ab.append(_mixer_call(
            _mixer_ab_kernel, 0, b, [xf], [x_spec], mod4, norm1_g[0:1], norm2_g[0:1],
            weights_ab, rwt, rbias, scratch_ab, seq_len, "mixer_ab"))
    y_pairs0 = _moe_rows(0, stage_ab, *experts)

    stage_cd = []
    for b in batches:
        x1, _, _, wsel0, _, _ = stage_ab[b]
        prev_mod_spec = pl.BlockSpec((1, 1, 6, d), lambda i, b=b: (0, b, 0, 0))
        stage_cd.append(_mixer_call(
            _mixer_cd_kernel, 1, b, [x1, y_pairs0[b], y_pairs0[b], wsel0, mod4],
            _combine_specs(seq_len) + [prev_mod_spec], mod4, norm1_g[1:2], norm2_g[1:2],
            weights_cd, rwt, rbias, scratch_cd, seq_len, "mixer_cd"))
    y_pairs1 = _moe_rows(1, stage_cd, *experts)

    out = None
    for b in batches:
        x3, _, _, wsel1, _, _ = stage_cd[b]
        out = _final_call(1, b, bsz, x3, y_pairs1[b], wsel1, mod4, fg, out)
    return out.reshape(bsz, seq_len, d)
```

```python
import functools

import jax
import jax.numpy as jnp
from jax import lax
from jax.experimental import pallas as pl
from jax.experimental.pallas import tpu as pltpu
from jax.experimental.pallas import tpu_sc as plsc

D_MODEL = 1024
EPS = 1e-6
POOL_WINDOWS = (2, 4, 8, 16)
POOL_GROUP = 128
D_HALF = 512
CONF_KERNEL = 31
SCONV_KERNEL = 3
CHUNK = 128
GMLP_HEADS = 4
N_EXPERTS = 16
N_GROUPS = 4
EXPERTS_PER_GROUP = 4
TOP_K = 2
D_EXPERT = 512

V7X_LANES = 128
V7X_SUBLANES = 8
V7X_VMEM_LIMIT_BYTES = 56 * 1024 * 1024

MIX_TILE = 512
MOE_TILE = 512
SC_CHUNK = 64
CAST_ROWS = 256
ROW_CHUNKS = D_MODEL // (2 * V7X_LANES)
CONV_HIST = 32
POOL_HIST = 16
SCONV_HIST = 8

BF16 = jnp.bfloat16
F32 = jnp.float32
U32 = jnp.uint32


def _rms_mod(x, g_row, shift_row, scale_row):
    ms = jnp.mean(x * x, axis=-1, keepdims=True)
    gain = g_row * (1.0 + scale_row)
    return (x * lax.rsqrt(ms + EPS)) * gain + shift_row


def _layer_norm(x, g_row, b_row):
    mu = jnp.mean(x, axis=-1, keepdims=True)
    xc = x - mu
    var = jnp.mean(xc * xc, axis=-1, keepdims=True)
    return xc * lax.rsqrt(var + EPS) * g_row + b_row


def _sigmoid(x):
    return 1.0 / (1.0 + jnp.exp(-x))


def _silu(x):
    return x * _sigmoid(x)


def _gelu_tanh(x):
    c = 0.7978845608028654
    return 0.5 * x * (1.0 + jnp.tanh(c * (x + 0.044715 * (x * x * x))))


def _shift_rows(x, r):
    n, c = x.shape
    if r == V7X_SUBLANES:
        return jnp.concatenate([x[:r], x[:n - r]], axis=0)
    g = x.reshape(n // V7X_SUBLANES, V7X_SUBLANES, c)
    rot = pltpu.roll(g, r, axis=1)
    prev = jnp.concatenate([rot[:1], rot[:-1]], axis=0)
    sub = lax.broadcasted_iota(jnp.int32, g.shape, 1)
    return jnp.where(sub < r, prev, rot).reshape(n, c)


def _load_words(ref, n_rows):
    return jnp.concatenate(
        [ref[pl.ds(c, n_rows, stride=ROW_CHUNKS), :] for c in range(ROW_CHUNKS)], axis=1)


def _store_words(ref, words):
    n_rows = words.shape[0]
    for c in range(ROW_CHUNKS):
        ref[pl.ds(c, n_rows, stride=ROW_CHUNKS), :] = words[:, c * V7X_LANES:(c + 1) * V7X_LANES]


def _pack_rows(val):
    half = val.shape[1] // 2
    return pltpu.pack_elementwise([val[:, :half], val[:, half:]], packed_dtype=BF16)


def _unpack_rows(words):
    halves = [pltpu.unpack_elementwise(words, index=i, packed_dtype=BF16, unpacked_dtype=F32)
              for i in range(2)]
    return jnp.concatenate(halves, axis=1)


def _ada_kernel(ct_ref, w_ref, b_ref, o_ref):
    ct = ct_ref[...]
    cond = _silu(ct)
    w = w_ref[0]
    nb = ct.shape[1]
    for b in range(nb):
        col = cond[:, b:b + 1]
        o_ref[0, b:b + 1, :] = jnp.sum(col * w, axis=0, keepdims=True) + b_ref[0]


def _ada_mod(c, ada_w, ada_b):
    depth, d, six_d = ada_w.shape
    bsz = c.shape[0]
    nb = D_MODEL
    return pl.pallas_call(
        _ada_kernel,
        grid=(depth, six_d // nb),
        in_specs=[
            pl.BlockSpec((d, bsz), lambda l, j: (0, 0)),
            pl.BlockSpec((1, d, nb), lambda l, j: (l, 0, j)),
            pl.BlockSpec((1, 1, nb), lambda l, j: (l, 0, j)),
        ],
        out_specs=pl.BlockSpec((1, bsz, nb), lambda l, j: (l, 0, j)),
        out_shape=jax.ShapeDtypeStruct((depth, bsz, six_d), F32),
        compiler_params=pltpu.CompilerParams(
            dimension_semantics=("arbitrary", "arbitrary"),
            vmem_limit_bytes=V7X_VMEM_LIMIT_BYTES),
        name="ada_mod",
    )(c.T, ada_w, ada_b.reshape(depth, 1, six_d))


def _cast_kernel(w_ref, o_ref):
    o_ref[...] = w_ref[0].astype(BF16)


def _to_bf16(w):
    _, k, n = w.shape
    rows = CAST_ROWS
    return pl.pallas_call(
        _cast_kernel,
        grid=(k // rows,),
        in_specs=[pl.BlockSpec((1, rows, n), lambda i: (0, i, 0))],
        out_specs=pl.BlockSpec((rows, n), lambda i: (i, 0)),
        out_shape=jax.ShapeDtypeStruct((k, n), BF16),
        name="cast_bf16",
    )(w)


def _route(h2_bf, rwt_ref, rbias_ref, eidx_ref, wsel_ref, rank_ref, counts_ref, cnt_ref):
    nt = (((1,), (1,)), ((), ()))
    r = lax.dot_general(rwt_ref[...], h2_bf, nt, preferred_element_type=F32)
    logits = r[:N_EXPERTS] + r[N_EXPERTS:]
    m = jnp.max(logits, axis=0, keepdims=True)
    ex = jnp.exp(logits - m)
    probs = ex / jnp.sum(ex, axis=0, keepdims=True)
    sel = probs + rbias_ref[...]
    s = [sel[e:e + 1] for e in range(N_EXPERTS)]
    p = [probs[e:e + 1] for e in range(N_EXPERTS)]
    best = None
    gi = None
    for g in range(N_GROUPS):
        a, b, c, d = s[4 * g:4 * g + 4]
        hi1, lo1 = jnp.maximum(a, b), jnp.minimum(a, b)
        hi2, lo2 = jnp.maximum(c, d), jnp.minimum(c, d)
        top1 = jnp.maximum(hi1, hi2)
        top2 = jnp.maximum(jnp.minimum(hi1, hi2), jnp.maximum(lo1, lo2))
        score = top1 + top2
        if g == 0:
            best, gi = score, jnp.zeros(score.shape, jnp.int32)
        else:
            upd = score > best
            gi = jnp.where(upd, g, gi)
            best = jnp.where(upd, score, best)
    v, q = [], []
    for j in range(EXPERTS_PER_GROUP):
        vj, qj = s[j], p[j]
        for g in range(1, N_GROUPS):
            pick = gi == g
            vj = jnp.where(pick, s[4 * g + j], vj)
            qj = jnp.where(pick, p[4 * g + j], qj)
        v.append(vj)
        q.append(qj)
    i1 = jnp.zeros(gi.shape, jnp.int32)
    m1 = v[0]
    for j in range(1, EXPERTS_PER_GROUP):
        upd = v[j] > m1
        i1 = jnp.where(upd, j, i1)
        m1 = jnp.where(upd, v[j], m1)
    i2 = jnp.zeros(gi.shape, jnp.int32)
    m2 = jnp.full(m1.shape, -jnp.inf, F32)
    for j in range(EXPERTS_PER_GROUP):
        cand = (i1 != j) & (v[j] > m2)
        i2 = jnp.where(cand, j, i2)
        m2 = jnp.where(cand, v[j], m2)
    pa = q[0]
    pb = q[0]
    for j in range(1, EXPERTS_PER_GROUP):
        pa = jnp.where(i1 == j, q[j], pa)
        pb = jnp.where(i2 == j, q[j], pb)
    tot = pa + pb
    e0 = gi * EXPERTS_PER_GROUP + i1
    e1 = gi * EXPERTS_PER_GROUP + i2
    t = h2_bf.shape[0]
    eidx_ref[0:1, :] = e0
    eidx_ref[1:2, :] = e1
    w_rows = jnp.concatenate(
        [pa / tot, pb / tot, jnp.zeros((V7X_LANES - TOP_K, t), F32)], axis=0)
    wsel_ref[...] = w_rows.T

    e_iota = lax.broadcasted_iota(jnp.int32, (N_EXPERTS, t), 0)
    oh0 = e_iota == e0
    oh1 = e_iota == e1
    both = jnp.where(oh0 | oh1, 1.0, 0.0)
    r_i = lax.broadcasted_iota(jnp.int32, (V7X_LANES, V7X_LANES), 0)
    c_i = lax.broadcasted_iota(jnp.int32, (V7X_LANES, V7X_LANES), 1)
    before = jnp.where(r_i < c_i, 1.0, 0.0).astype(BF16)
    run = cnt_ref[...]
    rank0, rank1 = [], []
    for blk in range(t // V7X_LANES):
        lanes = slice(blk * V7X_LANES, (blk + 1) * V7X_LANES)
        b = both[:, lanes]
        pre = jnp.dot(b.astype(BF16), before, preferred_element_type=F32) + run
        rank0.append(jnp.sum(jnp.where(oh0[:, lanes], pre, 0.0), axis=0, keepdims=True))
        rank1.append(jnp.sum(jnp.where(oh1[:, lanes], pre, 0.0), axis=0, keepdims=True))
        run = run + jnp.sum(b, axis=1, keepdims=True)
    cnt_ref[...] = run
    rank_ref[0:1, :] = jnp.concatenate(rank0, axis=1).astype(jnp.int32)
    rank_ref[1:2, :] = jnp.concatenate(rank1, axis=1).astype(jnp.int32)
    counts_ref[...] = jnp.broadcast_to(run, counts_ref.shape).astype(jnp.int32)


def _finish_mixer(x1, mod, n2g_ref, rwt_ref, rbias_ref,
                  x1_ref, h2_ref, eidx_ref, wsel_ref, rank_ref, counts_ref, cnt_ref):
    x1_ref[...] = x1
    h2 = _rms_mod(x1, n2g_ref[...], mod[3:4], mod[4:5])
    h2_bf = h2.astype(BF16)
    _store_words(h2_ref, _pack_rows(h2))
    _route(h2_bf, rwt_ref, rbias_ref, eidx_ref, wsel_ref, rank_ref, counts_ref, cnt_ref)


def _moe_residual(x_ref, y0_ref, y1_ref, wt_ref, g2_row):
    tm = x_ref.shape[0]
    wt = wt_ref[...]
    y0 = _unpack_rows(_load_words(y0_ref, tm))
    y1 = _unpack_rows(_load_words(y1_ref, tm))
    y = wt[:, 0:1] * y0 + wt[:, 1:2] * y1
    return x_ref[...] + g2_row * y


def _mixer_ab_kernel(x_ref, mod_ref, n1g_ref, n2g_ref, win_ref, poolw_ref, pscale_ref,
                     convw_ref, convb_ref, lng_ref, lnb_ref, wout_ref, rwt_ref, rbias_ref,
                     x1_ref, h2_ref, eidx_ref, wsel_ref, rank_ref, counts_ref,
                     pool_ext, conv_ext, cnt_ref):
    seq_tile = pl.program_id(0)
    tm = x_ref.shape[0]

    @pl.when(seq_tile == 0)
    def _():
        pool_ext[0:POOL_HIST, :] = jnp.zeros((POOL_HIST, D_HALF), F32)
        conv_ext[0:CONV_HIST, :] = jnp.zeros((CONV_HIST, D_HALF), F32)
        cnt_ref[...] = jnp.zeros_like(cnt_ref)

    x = x_ref[...]
    mod = mod_ref[0, 0]
    h = _rms_mod(x, n1g_ref[...], mod[0:1], mod[1:2]).astype(BF16)
    z = jnp.dot(h, win_ref[...], preferred_element_type=F32)
    zp = z[:, :D_HALF]
    glu = z[:, D_HALF:2 * D_HALF] * _sigmoid(z[:, 2 * D_HALF:])
    pool_ext[POOL_HIST:POOL_HIST + tm, :] = zp
    conv_ext[CONV_HIST:CONV_HIST + tm, :] = glu

    row = lax.broadcasted_iota(jnp.int32, (tm, 1), 0)
    pos1 = (seq_tile * tm + row + 1).astype(F32)
    pool_out = []
    for g, w in enumerate(POOL_WINDOWS):
        cols = slice(g * POOL_GROUP, (g + 1) * POOL_GROUP)
        acc = pool_ext[:, cols]
        span = 1
        while span < w:
            acc = acc + _shift_rows(acc, span)
            span *= 2
        wsum = acc[POOL_HIST:POOL_HIST + tm]
        inv_cnt = 1.0 / jnp.minimum(pos1, float(w))
        diff = wsum * inv_cnt - zp[:, cols]
        po = jnp.dot(diff.astype(BF16), poolw_ref[g], preferred_element_type=F32)
        pool_out.append(po * pscale_ref[:, cols])

    convw = convw_ref[...]
    ext_rows = tm + V7X_SUBLANES
    conv = None
    for r in range(V7X_SUBLANES):
        vr = None
        for a in range(CONV_HIST // V7X_SUBLANES):
            lag = V7X_SUBLANES * a + r
            if lag >= CONF_KERNEL:
                continue
            k = CONF_KERNEL - 1 - lag
            start = CONV_HIST - V7X_SUBLANES - V7X_SUBLANES * a
            term = convw[k:k + 1, :] * conv_ext[start:start + ext_rows, :]
            vr = term if vr is None else vr + term
        if r:
            vr = _shift_rows(vr, r)
        conv = vr if conv is None else conv + vr
    conv = conv[V7X_SUBLANES:V7X_SUBLANES + tm] + convb_ref[...]
    conf = _silu(_layer_norm(conv, lng_ref[...], lnb_ref[...]))

    pool_ext[0:POOL_HIST, :] = zp[tm - POOL_HIST:tm]
    conv_ext[0:CONV_HIST, :] = glu[tm - CONV_HIST:tm]

    m = jnp.dot(conf.astype(BF16), wout_ref[D_HALF:, :], preferred_element_type=F32)
    for g in range(len(POOL_WINDOWS)):
        rows = slice(g * POOL_GROUP, (g + 1) * POOL_GROUP)
        m = m + jnp.dot(pool_out[g].astype(BF16), wout_ref[rows, :], preferred_element_type=F32)
    _finish_mixer(x + mod[2:3] * m, mod, n2g_ref, rwt_ref, rbias_ref,
                  x1_ref, h2_ref, eidx_ref, wsel_ref, rank_ref, counts_ref, cnt_ref)


def _mixer_cd_kernel(x_ref, y0_ref, y1_ref, wt_ref, modp_ref,
                     mod_ref, n1g_ref, n2g_ref, win_ref, sconvw_ref, lng_ref, lnb_ref,
                     ws_ref, bsf_ref, wout_ref, rwt_ref, rbias_ref,
                     x1_ref, h2_ref, eidx_ref, wsel_ref, rank_ref, counts_ref,
                     sconv_ext, cnt_ref):
    tm = x_ref.shape[0]

    @pl.when(pl.program_id(0) == 0)
    def _():
        sconv_ext[0:SCONV_HIST, :] = jnp.zeros((SCONV_HIST, D_HALF), F32)
        cnt_ref[...] = jnp.zeros_like(cnt_ref)

    x = _moe_residual(x_ref, y0_ref, y1_ref, wt_ref, modp_ref[0, 0][5:6])
    mod = mod_ref[0, 0]
    h = _rms_mod(x, n1g_ref[...], mod[0:1], mod[1:2]).astype(BF16)
    def proj(lo, hi):
        return jnp.dot(h, win_ref[:, lo:hi], preferred_element_type=F32)

    v = _layer_norm(_gelu_tanh(proj(4 * D_HALF, 5 * D_HALF)), lng_ref[...], lnb_ref[...])
    u = _gelu_tanh(proj(3 * D_HALF, 4 * D_HALF))
    ch = proj(D_HALF, 2 * D_HALF) * proj(2 * D_HALF, 3 * D_HALF)
    bg = proj(0, D_HALF)

    sconv_ext[SCONV_HIST:SCONV_HIST + tm, :] = ch
    sw = sconvw_ref[...]
    ext = sconv_ext[...]
    conv = sw[2:3, :] * ext
    conv = conv + sw[1:2, :] * _shift_rows(ext, 1)
    conv = conv + sw[0:1, :] * _shift_rows(ext, 2)
    sc_out = bg * conv[SCONV_HIST:SCONV_HIST + tm]
    sconv_ext[0:SCONV_HIST, :] = ch[tm - SCONV_HIST:tm]

    r_i = lax.broadcasted_iota(jnp.int32, (CHUNK, CHUNK), 0)
    c_i = lax.broadcasted_iota(jnp.int32, (CHUNK, CHUNK), 1)
    tril = c_i <= r_i
    wm = [jnp.where(tril, ws_ref[hd], 0.0).astype(BF16) for hd in range(GMLP_HEADS)]
    v_bf = v.astype(BF16)
    bsf = bsf_ref[...]
    gm_rows = []
    for n in range(tm // CHUNK):
        rows = slice(n * CHUNK, (n + 1) * CHUNK)
        heads = []
        for hd in range(GMLP_HEADS):
            cols = slice(hd * POOL_GROUP, (hd + 1) * POOL_GROUP)
            heads.append(jnp.dot(wm[hd], v_bf[rows, cols], preferred_element_type=F32))
        mixed = jnp.concatenate(heads, axis=1) + bsf
        gm_rows.append(u[rows] * mixed)
    gm_out = jnp.concatenate(gm_rows, axis=0)

    m = jnp.dot(sc_out.astype(BF16), wout_ref[:D_HALF, :], preferred_element_type=F32)
    m = m + jnp.dot(gm_out.astype(BF16), wout_ref[D_HALF:, :], preferred_element_type=F32)
    _finish_mixer(x + mod[2:3] * m, mod, n2g_ref, rwt_ref, rbias_ref,
                  x1_ref, h2_ref, eidx_ref, wsel_ref, rank_ref, counts_ref, cnt_ref)


def _const_spec(shape):
    nd = len(shape)
    return pl.BlockSpec(shape, lambda i: (0,) * nd)


def _mixer_call(kernel_fn, layer, batch, stream_inputs, stream_specs, mod4, n1g, n2g, weights,
                rwt, rbias, scratch, seq_len, name):
    n_tok = seq_len
    d = D_MODEL
    tm = MIX_TILE
    in_specs = stream_specs + [
        pl.BlockSpec((1, 1, 6, d), lambda i: (layer, batch, 0, 0)),
        _const_spec(n1g.shape),
        _const_spec(n2g.shape),
    ] + [_const_spec(w.shape) for w in weights] + [_const_spec(rwt.shape), _const_spec(rbias.shape)]
    out_specs = [
        pl.BlockSpec((tm, d), lambda i: (i, 0)),
        pl.BlockSpec((tm * ROW_CHUNKS, V7X_LANES), lambda i: (i, 0)),
        pl.BlockSpec((TOP_K, tm), lambda i: (0, i)),
        pl.BlockSpec((tm, V7X_LANES), lambda i: (i, 0)),
        pl.BlockSpec((TOP_K, tm), lambda i: (0, i)),
        pl.BlockSpec((N_EXPERTS, V7X_LANES), lambda i: (0, 0)),
    ]
    out_shape = [
        jax.ShapeDtypeStruct((n_tok, d), F32),
        jax.ShapeDtypeStruct((n_tok * ROW_CHUNKS, V7X_LANES), U32),
        jax.ShapeDtypeStruct((TOP_K, n_tok), jnp.int32),
        jax.ShapeDtypeStruct((n_tok, V7X_LANES), F32),
        jax.ShapeDtypeStruct((TOP_K, n_tok), jnp.int32),
        jax.ShapeDtypeStruct((N_EXPERTS, V7X_LANES), jnp.int32),
    ]
    return pl.pallas_call(
        kernel_fn,
        grid=(n_tok // tm,),
        in_specs=in_specs,
        out_specs=out_specs,
        out_shape=out_shape,
        scratch_shapes=scratch + [pltpu.VMEM((N_EXPERTS, 1), F32)],
        compiler_params=pltpu.CompilerParams(
            dimension_semantics=("arbitrary",),
            vmem_limit_bytes=V7X_VMEM_LIMIT_BYTES),
        name=name,
    )(*stream_inputs, mod4, n1g, n2g, *weights, rwt, rbias)


def _combine_specs(n_tok):
    tm = MIX_TILE
    n_tiles = n_tok // tm
    return [
        pl.BlockSpec((tm, D_MODEL), lambda i: (i, 0)),
        pl.BlockSpec((tm * ROW_CHUNKS, V7X_LANES), lambda i: (i, 0)),
        pl.BlockSpec((tm * ROW_CHUNKS, V7X_LANES), lambda i: (n_tiles + i, 0)),
        pl.BlockSpec((tm, V7X_LANES), lambda i: (i, 0)),
    ]


SCHED_EXPERT, SCHED_VALID, SCHED_USED, SCHED_FIRST, SCHED_NEXT, SCHED_SLOT = range(6)
SCHED_ROWS = V7X_SUBLANES


def _plan_kernel(eidx_ref, rank_ref, counts_ref, pos_ref, sched_ref):
    i32 = jnp.int32
    shift = MOE_TILE.bit_length() - 1
    cnt = counts_ref[:, 0:1]
    padded = ((cnt + (MOE_TILE - 1)) >> shift) << shift
    seg = [padded[e:e + 1] for e in range(N_EXPERTS)]
    starts, ends = [], []
    run = jnp.zeros((1, 1), i32)
    for e in range(N_EXPERTS):
        starts.append(run)
        run = run + seg[e]
        ends.append(run)
    total = run
    nexts = [None] * N_EXPERTS
    nxt = jnp.full((1, 1), -1, i32)
    for e in reversed(range(N_EXPERTS)):
        nexts[e] = nxt
        nxt = jnp.where(seg[e] > 0, e, nxt)
    slots = []
    seen = jnp.zeros((1, 1), i32)
    for e in range(N_EXPERTS):
        slots.append(seen & 1)
        seen = seen + (seg[e] > 0).astype(i32)

    eidx = eidx_ref[...]
    pos = rank_ref[...]
    tile_row0 = lax.broadcasted_iota(i32, (1, V7X_LANES), 1) * MOE_TILE
    te = jnp.zeros((1, V7X_LANES), i32)
    for e in range(N_EXPERTS):
        pos = pos + jnp.where(eidx == e, starts[e], 0)
        te = te + (tile_row0 >= ends[e]).astype(i32)
    pos_ref[...] = pos
    te = jnp.minimum(te, N_EXPERTS - 1)

    def of_tile(per_expert):
        acc = jnp.zeros((1, V7X_LANES), i32)
        for e in range(N_EXPERTS):
            acc = acc + jnp.where(te == e, per_expert[e], 0)
        return acc

    valid_end = of_tile([starts[e] + cnt[e:e + 1] for e in range(N_EXPERTS)])
    used = tile_row0 < total
    rows = {
        SCHED_EXPERT: te,
        SCHED_VALID: jnp.clip(valid_end - tile_row0, 0, MOE_TILE),
        SCHED_USED: jnp.broadcast_to(total >> shift, (1, V7X_LANES)),
        SCHED_FIRST: (used & (of_tile(starts) == tile_row0)).astype(i32),
        SCHED_NEXT: of_tile(nexts),
        SCHED_SLOT: of_tile(slots),
    }
    for r in range(SCHED_ROWS):
        sched_ref[r:r + 1, :] = rows.get(r, jnp.zeros((1, V7X_LANES), i32))


def _sorted_positions(eidx, rank, counts):
    n_pairs = eidx.shape[0] * eidx.shape[1]
    n_rows = n_pairs + N_EXPERTS * MOE_TILE
    assert n_rows // MOE_TILE <= V7X_LANES
    pos, sched = pl.pallas_call(
        _plan_kernel,
        out_shape=[jax.ShapeDtypeStruct(eidx.shape, jnp.int32),
                   jax.ShapeDtypeStruct((SCHED_ROWS, V7X_LANES), jnp.int32)],
        name="moe_plan",
    )(eidx, rank, counts)
    return pos.reshape(n_pairs), sched, n_rows


def _sc_move_rows(scatter, src, pos_flat, n_out_rows, name):
    info = plsc.get_sparse_core_info()
    n_workers = info.num_cores * info.num_subcores
    n_pairs = pos_flat.shape[0]
    n_src = src.shape[0]
    per_worker = n_pairs // n_workers
    n_chunks = per_worker // SC_CHUNK
    assert per_worker * n_workers == n_pairs and n_chunks * SC_CHUNK == per_worker
    assert n_src % per_worker == 0
    idx = pos_flat.reshape(n_workers, n_chunks, SC_CHUNK)
    mesh = plsc.VectorSubcoreMesh(core_axis_name="core", subcore_axis_name="subcore")

    @functools.partial(
        pl.kernel,
        out_type=jax.ShapeDtypeStruct((n_out_rows, ROW_CHUNKS, V7X_LANES), U32),
        mesh=mesh,
        scratch_types=[
            pltpu.VMEM((n_chunks, SC_CHUNK), jnp.int32),
            pltpu.VMEM((2, SC_CHUNK, ROW_CHUNKS, V7X_LANES), U32),
            pltpu.SemaphoreType.DMA((2,)),
            pltpu.SemaphoreType.DMA((2,)),
        ],
        name=name)
    def move(src_hbm, i_hbm, o_hbm, idx_v, buf, in_sem, out_sem):
        wid = lax.axis_index("subcore") * info.num_cores + lax.axis_index("core")
        base = wid * per_worker
        src_base = lax.rem(base, n_src)
        pltpu.sync_copy(i_hbm.at[wid], idx_v)

        def fetch(s, slot):
            if scatter:
                rows = src_hbm.at[pl.ds(src_base + s * SC_CHUNK, SC_CHUNK)]
            else:
                rows = src_hbm.at[idx_v.at[s]]
            return pltpu.make_async_copy(rows, buf.at[slot], in_sem.at[slot])

        def flush(s, slot):
            if scatter:
                rows = o_hbm.at[idx_v.at[s]]
            else:
                rows = o_hbm.at[pl.ds(base + s * SC_CHUNK, SC_CHUNK)]
            return pltpu.make_async_copy(buf.at[slot], rows, out_sem.at[slot])

        fetch(0, 0).start()
        for s in range(n_chunks):
            slot = s % 2
            fetch(s, slot).wait()
            flush(s, slot).start()
            if s + 1 < n_chunks:
                if s >= 1:
                    flush(s - 1, 1 - slot).wait()
                fetch(s + 1, 1 - slot).start()
        flush(n_chunks - 2, n_chunks % 2).wait()
        flush(n_chunks - 1, (n_chunks - 1) % 2).wait()

    return move(src, idx)


def _experts_kernel(layer, sched_ref, x_ref, wg_hbm, wu_hbm, wd_hbm, y_ref,
                    wg_f32, wu_f32, wd_f32, wg_bf, wu_bf, wd_bf, sems):
    j = pl.program_id(0)
    tm = x_ref.shape[0] // ROW_CHUNKS
    used = j < sched_ref[SCHED_USED, j]
    expert = sched_ref[SCHED_EXPERT, j]

    def weight_copies(expert, slot):
        pairs = ((wg_hbm, wg_f32), (wu_hbm, wu_f32), (wd_hbm, wd_f32))
        return [pltpu.make_async_copy(src.at[layer, expert], dst.at[slot], sems.at[slot, m])
                for m, (src, dst) in enumerate(pairs)]

    @pl.when(used & (sched_ref[SCHED_FIRST, j] == 1))
    def _():
        slot = sched_ref[SCHED_SLOT, j]
        next_expert = sched_ref[SCHED_NEXT, j]

        @pl.when(j == 0)
        def _():
            for cp in weight_copies(expert, slot):
                cp.start()

        for cp in weight_copies(expert, slot):
            cp.wait()
        wg_bf[...] = wg_f32[slot].astype(BF16)
        wu_bf[...] = wu_f32[slot].astype(BF16)
        wd_bf[...] = wd_f32[slot].astype(BF16)

        @pl.when(next_expert >= 0)
        def _():
            for cp in weight_copies(next_expert, 1 - slot):
                cp.start()

    @pl.when(used)
    def _():
        row = lax.broadcasted_iota(jnp.int32, (tm, 1), 0)
        words = jnp.where(row < sched_ref[SCHED_VALID, j], _load_words(x_ref, tm), jnp.uint32(0))
        h = _unpack_rows(words).astype(BF16)
        a = jnp.dot(h, wg_bf[...], preferred_element_type=F32)
        b = jnp.dot(h, wu_bf[...], preferred_element_type=F32)
        t = (_silu(a) * b).astype(BF16)
        _store_words(y_ref, _pack_rows(jnp.dot(t, wd_bf[...], preferred_element_type=F32)))

    @pl.when(jnp.logical_not(used))
    def _():
        y_ref[...] = jnp.zeros_like(y_ref)


def _experts_call(layer, rows, sched, w_gate, w_up, w_down):
    n_rows = rows.shape[0]
    tm = MOE_TILE
    d = D_MODEL
    rows2 = rows.reshape(n_rows * ROW_CHUNKS, V7X_LANES)
    grid_spec = pltpu.PrefetchScalarGridSpec(
        num_scalar_prefetch=1,
        grid=(n_rows // tm,),
        in_specs=[
            pl.BlockSpec((tm * ROW_CHUNKS, V7X_LANES),
                         lambda j, sc: (jnp.minimum(j, sc[SCHED_USED, 0] - 1), 0)),
            pl.BlockSpec(memory_space=pl.ANY),
            pl.BlockSpec(memory_space=pl.ANY),
            pl.BlockSpec(memory_space=pl.ANY),
        ],
        out_specs=pl.BlockSpec((tm * ROW_CHUNKS, V7X_LANES), lambda j, sc: (j, 0)),
        scratch_shapes=[
            pltpu.VMEM((2, d, D_EXPERT), F32),
            pltpu.VMEM((2, d, D_EXPERT), F32),
            pltpu.VMEM((2, D_EXPERT, d), F32),
            pltpu.VMEM((d, D_EXPERT), BF16),
            pltpu.VMEM((d, D_EXPERT), BF16),
            pltpu.VMEM((D_EXPERT, d), BF16),
            pltpu.SemaphoreType.DMA((2, 3)),
        ],
    )
    y = pl.pallas_call(
        functools.partial(_experts_kernel, layer),
        grid_spec=grid_spec,
        out_shape=jax.ShapeDtypeStruct((n_rows * ROW_CHUNKS, V7X_LANES), U32),
        compiler_params=pltpu.CompilerParams(
            dimension_semantics=("arbitrary",),
            vmem_limit_bytes=V7X_VMEM_LIMIT_BYTES),
        name="experts_l%d" % layer,
    )(sched, rows2, w_gate, w_up, w_down)
    return y.reshape(n_rows, ROW_CHUNKS, V7X_LANES)


def _moe_rows(layer, mixer_outs, w_gate, w_up, w_down):
    plans = [_sorted_positions(eidx, rank, counts)
             for (_, _, eidx, _, rank, counts) in mixer_outs]
    sorted_rows = []
    for (_, h_rows, eidx, _, _, _), (pos_flat, _, n_rows) in zip(mixer_outs, plans):
        n_tok = eidx.shape[1]
        sorted_rows.append(_sc_move_rows(
            True, h_rows.reshape(n_tok, ROW_CHUNKS, V7X_LANES), pos_flat, n_rows, "sc_scatter_rows"))
    y_sorted = [_experts_call(layer, rows, sched, w_gate, w_up, w_down)
                for rows, (_, sched, _) in zip(sorted_rows, plans)]
    y_pairs = []
    for y, (pos_flat, _, _) in zip(y_sorted, plans):
        n_pairs = pos_flat.shape[0]
        moved = _sc_move_rows(False, y, pos_flat, n_pairs, "sc_gather_rows")
        y_pairs.append(moved.reshape(n_pairs * ROW_CHUNKS, V7X_LANES))
    return y_pairs


def _final_kernel(x_ref, y0_ref, y1_ref, wt_ref, modp_ref, fg_ref, *rest):
    o_ref = rest[-1]
    y = _moe_residual(x_ref, y0_ref, y1_ref, wt_ref, modp_ref[0, 0][5:6])
    ms = jnp.mean(y * y, axis=-1, keepdims=True)
    o_ref[...] = y * lax.rsqrt(ms + EPS) * fg_ref[...]


def _final_call(layer, batch, n_batch, x, y_pairs, wt, mod4, fg, out_prev):
    seq_len, d = x.shape
    tm = MIX_TILE
    tiles_per_seq = seq_len // tm
    in_specs = _combine_specs(seq_len) + [
        pl.BlockSpec((1, 1, 6, d), lambda i: (layer, batch, 0, 0)),
        _const_spec(fg.shape),
    ]
    args = [x, y_pairs, y_pairs, wt, mod4, fg]
    aliases = {}
    if out_prev is not None:
        in_specs.append(pl.BlockSpec(memory_space=pl.ANY))
        aliases = {len(args): 0}
        args.append(out_prev)
    return pl.pallas_call(
        _final_kernel,
        grid=(tiles_per_seq,),
        in_specs=in_specs,
        out_specs=pl.BlockSpec((tm, d), lambda i: (batch * tiles_per_seq + i, 0)),
        out_shape=jax.ShapeDtypeStruct((n_batch * seq_len, d), F32),
        input_output_aliases=aliases,
        compiler_params=pltpu.CompilerParams(
            dimension_semantics=("arbitrary",),
            vmem_limit_bytes=V7X_VMEM_LIMIT_BYTES),
        name="final_norm",
    )(*args)


def kernel(x, c, norm1_g, norm2_g, ada_w, ada_b, ab_w_in, pool_w, pool_scale, conf_conv_w, conf_conv_b, conf_ln_g, conf_ln_b, ab_w_out, cd_w_in, sconv_w, gmlp_ln_g, gmlp_ln_b, gmlp_ws, gmlp_bs, cd_w_out, router_w, router_bias, exp_w_gate, exp_w_up, exp_w_down, final_g):
    bsz, seq_len, d = x.shape
    n_tok = bsz * seq_len
    tm = MIX_TILE
    tiles_per_seq = seq_len // tm
    xf = x.reshape(n_tok, d)

    mod = _ada_mod(c, ada_w, ada_b)
    mod4 = mod.reshape(mod.shape[0], bsz, 6, d)

    rw_hi = router_w.astype(BF16)
    rw_lo = (router_w - rw_hi.astype(F32)).astype(BF16)
    rwt = jnp.concatenate([rw_hi.T, rw_lo.T], axis=0)
    rbias = router_bias.reshape(N_EXPERTS, 1)
    fg = final_g.reshape(1, d)

    weights_ab = [
        _to_bf16(ab_w_in), pool_w[0].astype(BF16), pool_scale[0].reshape(1, D_HALF),
        conf_conv_w[0], conf_conv_b[0].reshape(1, D_HALF), conf_ln_g[0].reshape(1, D_HALF),
        conf_ln_b[0].reshape(1, D_HALF), _to_bf16(ab_w_out),
    ]
    scratch_ab = [pltpu.VMEM((POOL_HIST + tm, D_HALF), F32),
                  pltpu.VMEM((CONV_HIST + tm, D_HALF), F32)]
    bsf = jnp.repeat(gmlp_bs[0].T, POOL_GROUP, axis=1)
    weights_cd = [
        _to_bf16(cd_w_in), sconv_w[0], gmlp_ln_g[0].reshape(1, D_HALF),
        gmlp_ln_b[0].reshape(1, D_HALF), gmlp_ws[0], bsf, _to_bf16(cd_w_out),
    ]
    scratch_cd = [pltpu.VMEM((SCONV_HIST + tm, D_HALF), F32)]
    experts = (exp_w_gate, exp_w_up, exp_w_down)

    batches = range(bsz)
    stage_ab = []
    for b in batches:
        x_spec = pl.BlockSpec((tm, d), lambda i, b=b: (b * tiles_per_seq + i, 0))
        stage_ab.append(_mixer_call(
            _mixer_ab_kernel, 0, b, [xf], [x_spec], mod4, norm1_g[0:1], norm2_g[0:1],
            weights_ab, rwt, rbias, scratch_ab, seq_len, "mixer_ab"))
    y_pairs0 = _moe_rows(0, stage_ab, *experts)

    stage_cd = []
    for b in batches:
        x1, _, _, wsel0, _, _ = stage_ab[b]
        prev_mod_spec = pl.BlockSpec((1, 1, 6, d), lambda i, b=b: (0, b, 0, 0))
        stage_cd.append(_mixer_call(
            _mixer_cd_kernel, 1, b, [x1, y_pairs0[b], y_pairs0[b], wsel0, mod4],
            _combine_specs(seq_len) + [prev_mod_spec], mod4, norm1_g[1:2], norm2_g[1:2],
            weights_cd, rwt, rbias, scratch_cd, seq_len, "mixer_cd"))
    y_pairs1 = _moe_rows(1, stage_cd, *experts)

    out = None
    for b in batches:
        x3, _, _, wsel1, _, _ = stage_cd[b]
        out = _final_call(1, b, bsz, x3, y_pairs1[b], wsel1, mod4, fg, out)
    return out.reshape(bsz, seq_len, d)
```

```python
import functools

import jax
import jax.numpy as jnp
from jax import lax
from jax.experimental import pallas as pl
from jax.experimental.pallas import tpu as pltpu
from jax.experimental.pallas import tpu_sc as plsc

D_MODEL = 1024
EPS = 1e-6
POOL_WINDOWS = (2, 4, 8, 16)
POOL_GROUP = 128
D_HALF = 512
CONF_KERNEL = 31
SCONV_KERNEL = 3
CHUNK = 128
GMLP_HEADS = 4
N_EXPERTS = 16
N_GROUPS = 4
EXPERTS_PER_GROUP = 4
TOP_K = 2
D_EXPERT = 512

V7X_LANES = 128
V7X_SUBLANES = 8
V7X_VMEM_LIMIT_BYTES = 56 * 1024 * 1024

MIX_TILE = 512
FINAL_TILE = 1024
MOE_TILE = 512
SC_CHUNK = 64
ROW_CHUNKS = D_MODEL // (2 * V7X_LANES)
CONV_HIST = 32
POOL_HIST = 16
SCONV_HIST = 8

BF16 = jnp.bfloat16
F32 = jnp.float32
U32 = jnp.uint32


def _rms_mod(x, g_row, shift_row, scale_row):
    ms = jnp.mean(x * x, axis=-1, keepdims=True)
    gain = g_row * (1.0 + scale_row)
    return (x * lax.rsqrt(ms + EPS)) * gain + shift_row


def _layer_norm(x, g_row, b_row):
    mu = jnp.mean(x, axis=-1, keepdims=True)
    xc = x - mu
    var = jnp.mean(xc * xc, axis=-1, keepdims=True)
    return xc * lax.rsqrt(var + EPS) * g_row + b_row


def _sigmoid(x):
    return 1.0 / (1.0 + jnp.exp(-x))


def _silu(x):
    return x * _sigmoid(x)


def _gelu_tanh(x):
    c = 0.7978845608028654
    return 0.5 * x * (1.0 + jnp.tanh(c * (x + 0.044715 * (x * x * x))))


def _shift_rows(x, r):
    n, c = x.shape
    if r == V7X_SUBLANES:
        return jnp.concatenate([x[:r], x[:n - r]], axis=0)
    g = x.reshape(n // V7X_SUBLANES, V7X_SUBLANES, c)
    rot = pltpu.roll(g, r, axis=1)
    prev = jnp.concatenate([rot[:1], rot[:-1]], axis=0)
    sub = lax.broadcasted_iota(jnp.int32, g.shape, 1)
    return jnp.where(sub < r, prev, rot).reshape(n, c)


def _load_words(ref, n_rows):
    return jnp.concatenate(
        [ref[pl.ds(c, n_rows, stride=ROW_CHUNKS), :] for c in range(ROW_CHUNKS)], axis=1)


def _store_words(ref, words):
    n_rows = words.shape[0]
    for c in range(ROW_CHUNKS):
        ref[pl.ds(c, n_rows, stride=ROW_CHUNKS), :] = words[:, c * V7X_LANES:(c + 1) * V7X_LANES]


def _pack_rows(val):
    half = val.shape[1] // 2
    return pltpu.pack_elementwise([val[:, :half], val[:, half:]], packed_dtype=BF16)


def _unpack_rows(words):
    halves = [pltpu.unpack_elementwise(words, index=i, packed_dtype=BF16, unpacked_dtype=F32)
              for i in range(2)]
    return jnp.concatenate(halves, axis=1)


def _ada_kernel(ct_ref, w_ref, b_ref, o_ref):
    ct = ct_ref[...]
    cond = _silu(ct)
    w = w_ref[0]
    nb = ct.shape[1]
    for b in range(nb):
        col = cond[:, b:b + 1]
        o_ref[0, b:b + 1, :] = jnp.sum(col * w, axis=0, keepdims=True) + b_ref[0]


def _ada_mod(c, ada_w, ada_b):
    depth, d, six_d = ada_w.shape
    bsz = c.shape[0]
    nb = D_MODEL
    return pl.pallas_call(
        _ada_kernel,
        grid=(depth, six_d // nb),
        in_specs=[
            pl.BlockSpec((d, bsz), lambda l, j: (0, 0)),
            pl.BlockSpec((1, d, nb), lambda l, j: (l, 0, j)),
            pl.BlockSpec((1, 1, nb), lambda l, j: (l, 0, j)),
        ],
        out_specs=pl.BlockSpec((1, bsz, nb), lambda l, j: (l, 0, j)),
        out_shape=jax.ShapeDtypeStruct((depth, bsz, six_d), F32),
        compiler_params=pltpu.CompilerParams(
            dimension_semantics=("arbitrary", "arbitrary"),
            vmem_limit_bytes=V7X_VMEM_LIMIT_BYTES),
        name="ada_mod",
    )(c.T, ada_w, ada_b.reshape(depth, 1, six_d))


def _route(h2_bf, rwt_ref, rbias_ref, eidx_ref, wsel_ref, rank_ref, counts_ref, cnt_ref):
    nt = (((1,), (1,)), ((), ()))
    r = lax.dot_general(rwt_ref[...], h2_bf, nt, preferred_element_type=F32)
    logits = r[:N_EXPERTS] + r[N_EXPERTS:]
    m = jnp.max(logits, axis=0, keepdims=True)
    ex = jnp.exp(logits - m)
    probs = ex / jnp.sum(ex, axis=0, keepdims=True)
    sel = probs + rbias_ref[...]
    s = [sel[e:e + 1] for e in range(N_EXPERTS)]
    p = [probs[e:e + 1] for e in range(N_EXPERTS)]
    best = None
    gi = None
    for g in range(N_GROUPS):
        a, b, c, d = s[4 * g:4 * g + 4]
        hi1, lo1 = jnp.maximum(a, b), jnp.minimum(a, b)
        hi2, lo2 = jnp.maximum(c, d), jnp.minimum(c, d)
        top1 = jnp.maximum(hi1, hi2)
        top2 = jnp.maximum(jnp.minimum(hi1, hi2), jnp.maximum(lo1, lo2))
        score = top1 + top2
        if g == 0:
            best, gi = score, jnp.zeros(score.shape, jnp.int32)
        else:
            upd = score > best
            gi = jnp.where(upd, g, gi)
            best = jnp.where(upd, score, best)
    v, q = [], []
    for j in range(EXPERTS_PER_GROUP):
        vj, qj = s[j], p[j]
        for g in range(1, N_GROUPS):
            pick = gi == g
            vj = jnp.where(pick, s[4 * g + j], vj)
            qj = jnp.where(pick, p[4 * g + j], qj)
        v.append(vj)
        q.append(qj)
    i1 = jnp.zeros(gi.shape, jnp.int32)
    m1 = v[0]
    for j in range(1, EXPERTS_PER_GROUP):
        upd = v[j] > m1
        i1 = jnp.where(upd, j, i1)
        m1 = jnp.where(upd, v[j], m1)
    i2 = jnp.zeros(gi.shape, jnp.int32)
    m2 = jnp.full(m1.shape, -jnp.inf, F32)
    for j in range(EXPERTS_PER_GROUP):
        cand = (i1 != j) & (v[j] > m2)
        i2 = jnp.where(cand, j, i2)
        m2 = jnp.where(cand, v[j], m2)
    pa = q[0]
    pb = q[0]
    for j in range(1, EXPERTS_PER_GROUP):
        pa = jnp.where(i1 == j, q[j], pa)
        pb = jnp.where(i2 == j, q[j], pb)
    tot = pa + pb
    e0 = gi * EXPERTS_PER_GROUP + i1
    e1 = gi * EXPERTS_PER_GROUP + i2
    t = h2_bf.shape[0]
    eidx_ref[0:1, :] = e0
    eidx_ref[1:2, :] = e1
    w_rows = jnp.concatenate(
        [pa / tot, pb / tot, jnp.zeros((V7X_LANES - TOP_K, t), F32)], axis=0)
    wsel_ref[...] = w_rows.T

    e_iota = lax.broadcasted_iota(jnp.int32, (N_EXPERTS, t), 0)
    oh0 = e_iota == e0
    oh1 = e_iota == e1
    both = jnp.where(oh0 | oh1, 1.0, 0.0)
    r_i = lax.broadcasted_iota(jnp.int32, (V7X_LANES, V7X_LANES), 0)
    c_i = lax.broadcasted_iota(jnp.int32, (V7X_LANES, V7X_LANES), 1)
    before = jnp.where(r_i < c_i, 1.0, 0.0).astype(BF16)
    run = cnt_ref[...]
    rank0, rank1 = [], []
    for blk in range(t // V7X_LANES):
        lanes = slice(blk * V7X_LANES, (blk + 1) * V7X_LANES)
        b = both[:, lanes]
        pre = jnp.dot(b.astype(BF16), before, preferred_element_type=F32) + run
        rank0.append(jnp.sum(jnp.where(oh0[:, lanes], pre, 0.0), axis=0, keepdims=True))
        rank1.append(jnp.sum(jnp.where(oh1[:, lanes], pre, 0.0), axis=0, keepdims=True))
        run = run + jnp.sum(b, axis=1, keepdims=True)
    cnt_ref[...] = run
    rank_ref[0:1, :] = jnp.concatenate(rank0, axis=1).astype(jnp.int32)
    rank_ref[1:2, :] = jnp.concatenate(rank1, axis=1).astype(jnp.int32)
    counts_ref[...] = jnp.broadcast_to(run, counts_ref.shape).astype(jnp.int32)


def _finish_mixer(x, m, mod, n2g_ref, rwt_ref, rbias_ref,
                  x1_ref, h2_ref, eidx_ref, wsel_ref, rank_ref, counts_ref, cnt_ref):
    x1 = x + mod[2:3] * m
    x1_ref[...] = x1
    h2 = _rms_mod(x1, n2g_ref[...], mod[3:4], mod[4:5])
    h2_bf = h2.astype(BF16)
    _store_words(h2_ref, _pack_rows(h2))
    _route(h2_bf, rwt_ref, rbias_ref, eidx_ref, wsel_ref, rank_ref, counts_ref, cnt_ref)


def _moe_residual(x_ref, y0_ref, y1_ref, wt_ref, g2_row):
    tm = x_ref.shape[0]
    wt = wt_ref[...]
    y0 = _unpack_rows(_load_words(y0_ref, tm))
    y1 = _unpack_rows(_load_words(y1_ref, tm))
    y = wt[:, 0:1] * y0 + wt[:, 1:2] * y1
    return x_ref[...] + g2_row * y


def _mixer_ab_kernel(x_ref, mod_ref, n1g_ref, n2g_ref, win_ref, poolw_ref, pscale_ref,
                     convw_ref, convb_ref, lng_ref, lnb_ref, wout_ref, rwt_ref, rbias_ref,
                     x1_ref, h2_ref, eidx_ref, wsel_ref, rank_ref, counts_ref,
                     pool_ext, conv_ext, cnt_ref):
    seq_tile = pl.program_id(0)
    tm = x_ref.shape[0]

    @pl.when(seq_tile == 0)
    def _():
        pool_ext[0:POOL_HIST, :] = jnp.zeros((POOL_HIST, D_HALF), F32)
        conv_ext[0:CONV_HIST, :] = jnp.zeros((CONV_HIST, D_HALF), F32)
        cnt_ref[...] = jnp.zeros_like(cnt_ref)

    x = x_ref[...]
    mod = mod_ref[0, 0]
    h = _rms_mod(x, n1g_ref[...], mod[0:1], mod[1:2]).astype(BF16)
    z = jnp.dot(h, win_ref[...], preferred_element_type=F32)
    zp = z[:, :D_HALF]
    glu = z[:, D_HALF:2 * D_HALF] * _sigmoid(z[:, 2 * D_HALF:])
    pool_ext[POOL_HIST:POOL_HIST + tm, :] = zp
    conv_ext[CONV_HIST:CONV_HIST + tm, :] = glu

    row = lax.broadcasted_iota(jnp.int32, (tm, 1), 0)
    pos1 = (seq_tile * tm + row + 1).astype(F32)
    pool_out = []
    for g, w in enumerate(POOL_WINDOWS):
        cols = slice(g * POOL_GROUP, (g + 1) * POOL_GROUP)
        acc = pool_ext[:, cols]
        span = 1
        while span < w:
            acc = acc + _shift_rows(acc, span)
            span *= 2
        wsum = acc[POOL_HIST:POOL_HIST + tm]
        inv_cnt = 1.0 / jnp.minimum(pos1, float(w))
        diff = wsum * inv_cnt - zp[:, cols]
        po = jnp.dot(diff.astype(BF16), poolw_ref[g], preferred_element_type=F32)
        pool_out.append(po * pscale_ref[:, cols])

    convw = convw_ref[...]
    ext_rows = tm + V7X_SUBLANES
    conv = None
    for r in range(V7X_SUBLANES):
        vr = None
        for a in range(CONV_HIST // V7X_SUBLANES):
            lag = V7X_SUBLANES * a + r
            if lag >= CONF_KERNEL:
                continue
            k = CONF_KERNEL - 1 - lag
            start = CONV_HIST - V7X_SUBLANES - V7X_SUBLANES * a
            term = convw[k:k + 1, :] * conv_ext[start:start + ext_rows, :]
            vr = term if vr is None else vr + term
        if r:
            vr = _shift_rows(vr, r)
        conv = vr if conv is None else conv + vr
    conv = conv[V7X_SUBLANES:V7X_SUBLANES + tm] + convb_ref[...]
    conf = _silu(_layer_norm(conv, lng_ref[...], lnb_ref[...]))

    pool_ext[0:POOL_HIST, :] = zp[tm - POOL_HIST:tm]
    conv_ext[0:CONV_HIST, :] = glu[tm - CONV_HIST:tm]

    m = jnp.dot(conf.astype(BF16), wout_ref[D_HALF:, :], preferred_element_type=F32)
    for g in range(len(POOL_WINDOWS)):
        rows = slice(g * POOL_GROUP, (g + 1) * POOL_GROUP)
        m = m + jnp.dot(pool_out[g].astype(BF16), wout_ref[rows, :], preferred_element_type=F32)
    _finish_mixer(x, m, mod, n2g_ref, rwt_ref, rbias_ref,
                  x1_ref, h2_ref, eidx_ref, wsel_ref, rank_ref, counts_ref, cnt_ref)


def _mixer_cd_kernel(x_ref, y0_ref, y1_ref, wt_ref, modp_ref,
                     mod_ref, n1g_ref, n2g_ref, win_ref, sconvw_ref, lng_ref, lnb_ref,
                     ws_ref, bsf_ref, wout_ref, rwt_ref, rbias_ref,
                     x1_ref, h2_ref, eidx_ref, wsel_ref, rank_ref, counts_ref,
                     sconv_ext, cnt_ref):
    tm = x_ref.shape[0]

    @pl.when(pl.program_id(0) == 0)
    def _():
        sconv_ext[0:SCONV_HIST, :] = jnp.zeros((SCONV_HIST, D_HALF), F32)
        cnt_ref[...] = jnp.zeros_like(cnt_ref)

    x = _moe_residual(x_ref, y0_ref, y1_ref, wt_ref, modp_ref[0, 0][5:6])
    mod = mod_ref[0, 0]
    h = _rms_mod(x, n1g_ref[...], mod[0:1], mod[1:2]).astype(BF16)
    def proj(lo, hi):
        return jnp.dot(h, win_ref[:, lo:hi], preferred_element_type=F32)

    v = _layer_norm(_gelu_tanh(proj(4 * D_HALF, 5 * D_HALF)), lng_ref[...], lnb_ref[...])
    u = _gelu_tanh(proj(3 * D_HALF, 4 * D_HALF))
    ch = proj(D_HALF, 2 * D_HALF) * proj(2 * D_HALF, 3 * D_HALF)
    bg = proj(0, D_HALF)

    sconv_ext[SCONV_HIST:SCONV_HIST + tm, :] = ch
    sw = sconvw_ref[...]
    ext = sconv_ext[...]
    conv = sw[2:3, :] * ext
    conv = conv + sw[1:2, :] * _shift_rows(ext, 1)
    conv = conv + sw[0:1, :] * _shift_rows(ext, 2)
    sc_out = bg * conv[SCONV_HIST:SCONV_HIST + tm]
    sconv_ext[0:SCONV_HIST, :] = ch[tm - SCONV_HIST:tm]

    r_i = lax.broadcasted_iota(jnp.int32, (CHUNK, CHUNK), 0)
    c_i = lax.broadcasted_iota(jnp.int32, (CHUNK, CHUNK), 1)
    tril = c_i <= r_i
    wm = [jnp.where(tril, ws_ref[hd], 0.0).astype(BF16) for hd in range(GMLP_HEADS)]
    v_bf = v.astype(BF16)
    bsf = bsf_ref[...]
    gm_rows = []
    for n in range(tm // CHUNK):
        rows = slice(n * CHUNK, (n + 1) * CHUNK)
        heads = []
        for hd in range(GMLP_HEADS):
            cols = slice(hd * POOL_GROUP, (hd + 1) * POOL_GROUP)
            heads.append(jnp.dot(wm[hd], v_bf[rows, cols], preferred_element_type=F32))
        mixed = jnp.concatenate(heads, axis=1) + bsf
        gm_rows.append(u[rows] * mixed)
    gm_out = jnp.concatenate(gm_rows, axis=0)

    m = jnp.dot(sc_out.astype(BF16), wout_ref[:D_HALF, :], preferred_element_type=F32)
    m = m + jnp.dot(gm_out.astype(BF16), wout_ref[D_HALF:, :], preferred_element_type=F32)
    _finish_mixer(x, m, mod, n2g_ref, rwt_ref, rbias_ref,
                  x1_ref, h2_ref, eidx_ref, wsel_ref, rank_ref, counts_ref, cnt_ref)


def _const_spec(shape):
    nd = len(shape)
    return pl.BlockSpec(shape, lambda i: (0,) * nd)


def _mixer_call(kernel_fn, layer, batch, stream_inputs, stream_specs, mod4, n1g, n2g, weights,
                rwt, rbias, scratch, seq_len, name):
    n_tok = seq_len
    d = D_MODEL
    tm = MIX_TILE
    in_specs = stream_specs + [
        pl.BlockSpec((1, 1, 6, d), lambda i: (layer, batch, 0, 0)),
        _const_spec(n1g.shape),
        _const_spec(n2g.shape),
    ] + [_const_spec(w.shape) for w in weights] + [_const_spec(rwt.shape), _const_spec(rbias.shape)]
    out_specs = [
        pl.BlockSpec((tm, d), lambda i: (i, 0)),
        pl.BlockSpec((tm * ROW_CHUNKS, V7X_LANES), lambda i: (i, 0)),
        pl.BlockSpec((TOP_K, tm), lambda i: (0, i)),
        pl.BlockSpec((tm, V7X_LANES), lambda i: (i, 0)),
        pl.BlockSpec((TOP_K, tm), lambda i: (0, i)),
        pl.BlockSpec((N_EXPERTS, V7X_LANES), lambda i: (0, 0)),
    ]
    out_shape = [
        jax.ShapeDtypeStruct((n_tok, d), F32),
        jax.ShapeDtypeStruct((n_tok * ROW_CHUNKS, V7X_LANES), U32),
        jax.ShapeDtypeStruct((TOP_K, n_tok), jnp.int32),
        jax.ShapeDtypeStruct((n_tok, V7X_LANES), F32),
        jax.ShapeDtypeStruct((TOP_K, n_tok), jnp.int32),
        jax.ShapeDtypeStruct((N_EXPERTS, V7X_LANES), jnp.int32),
    ]
    return pl.pallas_call(
        kernel_fn,
        grid=(n_tok // tm,),
        in_specs=in_specs,
        out_specs=out_specs,
        out_shape=out_shape,
        scratch_shapes=scratch + [pltpu.VMEM((N_EXPERTS, 1), F32)],
        compiler_params=pltpu.CompilerParams(
            dimension_semantics=("arbitrary",),
            vmem_limit_bytes=V7X_VMEM_LIMIT_BYTES),
        name=name,
    )(*stream_inputs, mod4, n1g, n2g, *weights, rwt, rbias)


def _combine_specs(n_tok, tm):
    n_tiles = n_tok // tm
    return [
        pl.BlockSpec((tm, D_MODEL), lambda i: (i, 0)),
        pl.BlockSpec((tm * ROW_CHUNKS, V7X_LANES), lambda i: (i, 0)),
        pl.BlockSpec((tm * ROW_CHUNKS, V7X_LANES), lambda i: (n_tiles + i, 0)),
        pl.BlockSpec((tm, V7X_LANES), lambda i: (i, 0)),
    ]


SCHED_EXPERT, SCHED_VALID, SCHED_USED, SCHED_FIRST, SCHED_NEXT, SCHED_SLOT = range(6)
SCHED_ROWS = V7X_SUBLANES


def _plan_kernel(eidx_ref, rank_ref, counts_ref, pos_ref, sched_ref):
    i32 = jnp.int32
    shift = MOE_TILE.bit_length() - 1
    cnt = counts_ref[:, 0:1]
    padded = ((cnt + (MOE_TILE - 1)) >> shift) << shift
    seg = [padded[e:e + 1] for e in range(N_EXPERTS)]
    starts, ends = [], []
    run = jnp.zeros((1, 1), i32)
    for e in range(N_EXPERTS):
        starts.append(run)
        run = run + seg[e]
        ends.append(run)
    total = run
    nexts = [None] * N_EXPERTS
    nxt = jnp.full((1, 1), -1, i32)
    for e in reversed(range(N_EXPERTS)):
        nexts[e] = nxt
        nxt = jnp.where(seg[e] > 0, e, nxt)
    slots = []
    seen = jnp.zeros((1, 1), i32)
    for e in range(N_EXPERTS):
        slots.append(seen & 1)
        seen = seen + (seg[e] > 0).astype(i32)

    eidx = eidx_ref[...]
    pos = rank_ref[...]
    tile_row0 = lax.broadcasted_iota(i32, (1, V7X_LANES), 1) * MOE_TILE
    te = jnp.zeros((1, V7X_LANES), i32)
    for e in range(N_EXPERTS):
        pos = pos + jnp.where(eidx == e, starts[e], 0)
        te = te + (tile_row0 >= ends[e]).astype(i32)
    pos_ref[...] = pos
    te = jnp.minimum(te, N_EXPERTS - 1)

    def of_tile(per_expert):
        acc = jnp.zeros((1, V7X_LANES), i32)
        for e in range(N_EXPERTS):
            acc = acc + jnp.where(te == e, per_expert[e], 0)
        return acc

    valid_end = of_tile([starts[e] + cnt[e:e + 1] for e in range(N_EXPERTS)])
    used = tile_row0 < total
    rows = {
        SCHED_EXPERT: te,
        SCHED_VALID: jnp.clip(valid_end - tile_row0, 0, MOE_TILE),
        SCHED_USED: jnp.broadcast_to(total >> shift, (1, V7X_LANES)),
        SCHED_FIRST: (used & (of_tile(starts) == tile_row0)).astype(i32),
        SCHED_NEXT: of_tile(nexts),
        SCHED_SLOT: of_tile(slots),
    }
    for r in range(SCHED_ROWS):
        sched_ref[r:r + 1, :] = rows.get(r, jnp.zeros((1, V7X_LANES), i32))


def _sorted_positions(eidx, rank, counts):
    n_pairs = eidx.shape[0] * eidx.shape[1]
    n_rows = n_pairs + N_EXPERTS * MOE_TILE
    assert n_rows // MOE_TILE <= V7X_LANES
    pos, sched = pl.pallas_call(
        _plan_kernel,
        out_shape=[jax.ShapeDtypeStruct(eidx.shape, jnp.int32),
                   jax.ShapeDtypeStruct((SCHED_ROWS, V7X_LANES), jnp.int32)],
        name="moe_plan",
    )(eidx, rank, counts)
    return pos.reshape(n_pairs), sched, n_rows


def _sc_move_rows(scatter, src, pos_flat, n_out_rows, name):
    info = plsc.get_sparse_core_info()
    n_workers = info.num_cores * info.num_subcores
    n_pairs = pos_flat.shape[0]
    n_src = src.shape[0]
    per_worker = n_pairs // n_workers
    n_chunks = per_worker // SC_CHUNK
    assert per_worker * n_workers == n_pairs and n_chunks * SC_CHUNK == per_worker
    assert n_src % per_worker == 0
    idx = pos_flat.reshape(n_workers, n_chunks, SC_CHUNK)
    mesh = plsc.VectorSubcoreMesh(core_axis_name="core", subcore_axis_name="subcore")

    @functools.partial(
        pl.kernel,
        out_type=jax.ShapeDtypeStruct((n_out_rows, ROW_CHUNKS, V7X_LANES), U32),
        mesh=mesh,
        scratch_types=[
            pltpu.VMEM((n_chunks, SC_CHUNK), jnp.int32),
            pltpu.VMEM((2, SC_CHUNK, ROW_CHUNKS, V7X_LANES), U32),
            pltpu.SemaphoreType.DMA((2,)),
            pltpu.SemaphoreType.DMA((2,)),
        ],
        name=name)
    def move(src_hbm, i_hbm, o_hbm, idx_v, buf, in_sem, out_sem):
        wid = lax.axis_index("subcore") * info.num_cores + lax.axis_index("core")
        base = wid * per_worker
        src_base = lax.rem(base, n_src)
        pltpu.sync_copy(i_hbm.at[wid], idx_v)

        def fetch(s, slot):
            if scatter:
                rows = src_hbm.at[pl.ds(src_base + s * SC_CHUNK, SC_CHUNK)]
            else:
                rows = src_hbm.at[idx_v.at[s]]
            return pltpu.make_async_copy(rows, buf.at[slot], in_sem.at[slot])

        def flush(s, slot):
            if scatter:
                rows = o_hbm.at[idx_v.at[s]]
            else:
                rows = o_hbm.at[pl.ds(base + s * SC_CHUNK, SC_CHUNK)]
            return pltpu.make_async_copy(buf.at[slot], rows, out_sem.at[slot])

        fetch(0, 0).start()
        for s in range(n_chunks):
            slot = s % 2
            fetch(s, slot).wait()
            flush(s, slot).start()
            if s + 1 < n_chunks:
                if s >= 1:
                    flush(s - 1, 1 - slot).wait()
                fetch(s + 1, 1 - slot).start()
        flush(n_chunks - 2, n_chunks % 2).wait()
        flush(n_chunks - 1, (n_chunks - 1) % 2).wait()

    return move(src, idx)


def _experts_kernel(layer, sched_ref, x_ref, wg_hbm, wu_hbm, wd_hbm, y_ref,
                    wg_f32, wu_f32, wd_f32, wg_bf, wu_bf, wd_bf, sems):
    j = pl.program_id(0)
    tm = x_ref.shape[0] // ROW_CHUNKS
    used = j < sched_ref[SCHED_USED, j]
    expert = sched_ref[SCHED_EXPERT, j]

    def weight_copies(expert, slot):
        pairs = ((wg_hbm, wg_f32), (wu_hbm, wu_f32), (wd_hbm, wd_f32))
        return [pltpu.make_async_copy(src.at[layer, expert], dst.at[slot], sems.at[slot, m])
                for m, (src, dst) in enumerate(pairs)]

    @pl.when(used & (sched_ref[SCHED_FIRST, j] == 1))
    def _():
        slot = sched_ref[SCHED_SLOT, j]
        next_expert = sched_ref[SCHED_NEXT, j]

        @pl.when(j == 0)
        def _():
            for cp in weight_copies(expert, slot):
                cp.start()

        for cp in weight_copies(expert, slot):
            cp.wait()
        wg_bf[...] = wg_f32[slot].astype(BF16)
        wu_bf[...] = wu_f32[slot].astype(BF16)
        wd_bf[...] = wd_f32[slot].astype(BF16)

        @pl.when(next_expert >= 0)
        def _():
            for cp in weight_copies(next_expert, 1 - slot):
                cp.start()

    @pl.when(used)
    def _():
        row = lax.broadcasted_iota(jnp.int32, (tm, 1), 0)
        words = jnp.where(row < sched_ref[SCHED_VALID, j], _load_words(x_ref, tm), jnp.uint32(0))
        h = _unpack_rows(words).astype(BF16)
        a = jnp.dot(h, wg_bf[...], preferred_element_type=F32)
        b = jnp.dot(h, wu_bf[...], preferred_element_type=F32)
        t = (_silu(a) * b).astype(BF16)
        _store_words(y_ref, _pack_rows(jnp.dot(t, wd_bf[...], preferred_element_type=F32)))

    @pl.when(jnp.logical_not(used))
    def _():
        y_ref[...] = jnp.zeros_like(y_ref)


def _experts_call(layer, rows, sched, w_gate, w_up, w_down):
    n_rows = rows.shape[0]
    tm = MOE_TILE
    d = D_MODEL
    rows2 = rows.reshape(n_rows * ROW_CHUNKS, V7X_LANES)
    grid_spec = pltpu.PrefetchScalarGridSpec(
        num_scalar_prefetch=1,
        grid=(n_rows // tm,),
        in_specs=[
            pl.BlockSpec((tm * ROW_CHUNKS, V7X_LANES),
                         lambda j, sc: (jnp.minimum(j, sc[SCHED_USED, 0] - 1), 0)),
            pl.BlockSpec(memory_space=pl.ANY),
            pl.BlockSpec(memory_space=pl.ANY),
            pl.BlockSpec(memory_space=pl.ANY),
        ],
        out_specs=pl.BlockSpec((tm * ROW_CHUNKS, V7X_LANES), lambda j, sc: (j, 0)),
        scratch_shapes=[
            pltpu.VMEM((2, d, D_EXPERT), F32),
            pltpu.VMEM((2, d, D_EXPERT), F32),
            pltpu.VMEM((2, D_EXPERT, d), F32),
            pltpu.VMEM((d, D_EXPERT), BF16),
            pltpu.VMEM((d, D_EXPERT), BF16),
            pltpu.VMEM((D_EXPERT, d), BF16),
            pltpu.SemaphoreType.DMA((2, 3)),
        ],
    )
    y = pl.pallas_call(
        functools.partial(_experts_kernel, layer),
        grid_spec=grid_spec,
        out_shape=jax.ShapeDtypeStruct((n_rows * ROW_CHUNKS, V7X_LANES), U32),
        compiler_params=pltpu.CompilerParams(
            dimension_semantics=("arbitrary",),
            vmem_limit_bytes=V7X_VMEM_LIMIT_BYTES),
        name="experts_l%d" % layer,
    )(sched, rows2, w_gate, w_up, w_down)
    return y.reshape(n_rows, ROW_CHUNKS, V7X_LANES)


def _moe_rows(layer, mixer_outs, w_gate, w_up, w_down):
    plans = [_sorted_positions(eidx, rank, counts)
             for (_, _, eidx, _, rank, counts) in mixer_outs]
    sorted_rows = []
    for (_, h_rows, eidx, _, _, _), (pos_flat, _, n_rows) in zip(mixer_outs, plans):
        n_tok = eidx.shape[1]
        sorted_rows.append(_sc_move_rows(
            True, h_rows.reshape(n_tok, ROW_CHUNKS, V7X_LANES), pos_flat, n_rows, "sc_scatter_rows"))
    y_sorted = [_experts_call(layer, rows, sched, w_gate, w_up, w_down)
                for rows, (_, sched, _) in zip(sorted_rows, plans)]
    y_pairs = []
    for y, (pos_flat, _, _) in zip(y_sorted, plans):
        n_pairs = pos_flat.shape[0]
        moved = _sc_move_rows(False, y, pos_flat, n_pairs, "sc_gather_rows")
        y_pairs.append(moved.reshape(n_pairs * ROW_CHUNKS, V7X_LANES))
    return y_pairs


def _final_kernel(x_ref, y0_ref, y1_ref, wt_ref, modp_ref, fg_ref, *rest):
    o_ref = rest[-1]
    y = _moe_residual(x_ref, y0_ref, y1_ref, wt_ref, modp_ref[0, 0][5:6])
    ms = jnp.mean(y * y, axis=-1, keepdims=True)
    o_ref[...] = y * lax.rsqrt(ms + EPS) * fg_ref[...]


def _final_call(layer, batch, n_batch, x, y_pairs, wt, mod4, fg, out_prev):
    seq_len, d = x.shape
    tm = FINAL_TILE
    tiles_per_seq = seq_len // tm
    in_specs = _combine_specs(seq_len, tm) + [
        pl.BlockSpec((1, 1, 6, d), lambda i: (layer, batch, 0, 0)),
        _const_spec(fg.shape),
    ]
    args = [x, y_pairs, y_pairs, wt, mod4, fg]
    aliases = {}
    if out_prev is not None:
        in_specs.append(pl.BlockSpec(memory_space=pl.ANY))
        aliases = {len(args): 0}
        args.append(out_prev)
    return pl.pallas_call(
        _final_kernel,
        grid=(tiles_per_seq,),
        in_specs=in_specs,
        out_specs=pl.BlockSpec((tm, d), lambda i: (batch * tiles_per_seq + i, 0)),
        out_shape=jax.ShapeDtypeStruct((n_batch * seq_len, d), F32),
        input_output_aliases=aliases,
        compiler_params=pltpu.CompilerParams(
            dimension_semantics=("arbitrary",),
            vmem_limit_bytes=V7X_VMEM_LIMIT_BYTES),
        name="final_norm",
    )(*args)


def kernel(x, c, norm1_g, norm2_g, ada_w, ada_b, ab_w_in, pool_w, pool_scale, conf_conv_w, conf_conv_b, conf_ln_g, conf_ln_b, ab_w_out, cd_w_in, sconv_w, gmlp_ln_g, gmlp_ln_b, gmlp_ws, gmlp_bs, cd_w_out, router_w, router_bias, exp_w_gate, exp_w_up, exp_w_down, final_g):
    bsz, seq_len, d = x.shape
    n_tok = bsz * seq_len
    tm = MIX_TILE
    tiles_per_seq = seq_len // tm
    xf = x.reshape(n_tok, d)

    mod = _ada_mod(c, ada_w, ada_b)
    mod4 = mod.reshape(mod.shape[0], bsz, 6, d)

    rw_hi = router_w.astype(BF16)
    rw_lo = (router_w - rw_hi.astype(F32)).astype(BF16)
    rwt = jnp.concatenate([rw_hi.T, rw_lo.T], axis=0)
    rbias = router_bias.reshape(N_EXPERTS, 1)
    fg = final_g.reshape(1, d)

    weights_ab = [
        ab_w_in[0].astype(BF16), pool_w[0].astype(BF16), pool_scale[0].reshape(1, D_HALF),
        conf_conv_w[0], conf_conv_b[0].reshape(1, D_HALF), conf_ln_g[0].reshape(1, D_HALF),
        conf_ln_b[0].reshape(1, D_HALF), ab_w_out[0].astype(BF16),
    ]
    scratch_ab = [pltpu.VMEM((POOL_HIST + tm, D_HALF), F32),
                  pltpu.VMEM((CONV_HIST + tm, D_HALF), F32)]
    bsf = jnp.repeat(gmlp_bs[0].T, POOL_GROUP, axis=1)
    weights_cd = [
        cd_w_in[0].astype(BF16), sconv_w[0], gmlp_ln_g[0].reshape(1, D_HALF),
        gmlp_ln_b[0].reshape(1, D_HALF), gmlp_ws[0], bsf, cd_w_out[0].astype(BF16),
    ]
    scratch_cd = [pltpu.VMEM((SCONV_HIST + tm, D_HALF), F32)]
    experts = (exp_w_gate, exp_w_up, exp_w_down)

    batches = range(bsz)
    stage_ab = []
    for b in batches:
        x_spec = pl.BlockSpec((tm, d), lambda i, b=b: (b * tiles_per_seq + i, 0))
        stage_ab.append(_mixer_call(
            _mixer_ab_kernel, 0, b, [xf], [x_spec], mod4, norm1_g[0:1], norm2_g[0:1],
            weights_ab, rwt, rbias, scratch_ab, seq_len, "mixer_ab"))
    y_pairs0 = _moe_rows(0, stage_ab, *experts)

    stage_cd = []
    for b in batches:
        x1, _, _, wsel0, _, _ = stage_ab[b]
        prev_mod_spec = pl.BlockSpec((1, 1, 6, d), lambda i, b=b: (0, b, 0, 0))
        stage_cd.append(_mixer_call(
            _mixer_cd_kernel, 1, b, [x1, y_pairs0[b], y_pairs0[b], wsel0, mod4],
            _combine_specs(seq_len, tm) + [prev_mod_spec], mod4, norm1_g[1:2], norm2_g[1:2],
            weights_cd, rwt, rbias, scratch_cd, seq_len, "mixer_cd"))
    y_pairs1 = _moe_rows(1, stage_cd, *experts)

    out = None
    for b in batches:
        x3, _, _, wsel1, _, _ = stage_cd[b]
        out = _final_call(1, b, bsz, x3, y_pairs1[b], wsel1, mod4, fg, out)
    return out.reshape(bsz, seq_len, d)
```

```python
import functools

import jax
import jax.numpy as jnp
from jax import lax
from jax.experimental import pallas as pl
from jax.experimental.pallas import tpu as pltpu
from jax.experimental.pallas import tpu_sc as plsc

D_MODEL = 1024
EPS = 1e-6
POOL_WINDOWS = (2, 4, 8, 16)
POOL_GROUP = 128
D_HALF = 512
CONF_KERNEL = 31
SCONV_KERNEL = 3
CHUNK = 128
GMLP_HEADS = 4
N_EXPERTS = 16
N_GROUPS = 4
EXPERTS_PER_GROUP = 4
TOP_K = 2
D_EXPERT = 512

V7X_LANES = 128
V7X_SUBLANES = 8
V7X_VMEM_LIMIT_BYTES = 56 * 1024 * 1024

MIX_TILE = 512
FINAL_TILE = 1024
MOE_TILE = 512
MOE_STEP_TILES = 2
SC_CHUNK = 64
ROW_CHUNKS = D_MODEL // (2 * V7X_LANES)
CONV_HIST = 32
POOL_HIST = 16
SCONV_HIST = 8

BF16 = jnp.bfloat16
F32 = jnp.float32
U32 = jnp.uint32


def _rms_mod(x, g_row, shift_row, scale_row):
    ms = jnp.mean(x * x, axis=-1, keepdims=True)
    gain = g_row * (1.0 + scale_row)
    return (x * lax.rsqrt(ms + EPS)) * gain + shift_row


def _layer_norm(x, g_row, b_row):
    mu = jnp.mean(x, axis=-1, keepdims=True)
    xc = x - mu
    var = jnp.mean(xc * xc, axis=-1, keepdims=True)
    return xc * lax.rsqrt(var + EPS) * g_row + b_row


def _sigmoid(x):
    return 1.0 / (1.0 + jnp.exp(-x))


def _silu(x):
    return x * _sigmoid(x)


def _gelu_tanh(x):
    c = 0.7978845608028654
    return 0.5 * x * (1.0 + jnp.tanh(c * (x + 0.044715 * (x * x * x))))


def _shift_rows(x, r):
    n, c = x.shape
    if r == V7X_SUBLANES:
        return jnp.concatenate([x[:r], x[:n - r]], axis=0)
    g = x.reshape(n // V7X_SUBLANES, V7X_SUBLANES, c)
    rot = pltpu.roll(g, r, axis=1)
    prev = jnp.concatenate([rot[:1], rot[:-1]], axis=0)
    sub = lax.broadcasted_iota(jnp.int32, g.shape, 1)
    return jnp.where(sub < r, prev, rot).reshape(n, c)


def _load_words(ref, n_rows, row0=0):
    return jnp.concatenate(
        [ref[pl.ds(row0 * ROW_CHUNKS + c, n_rows, stride=ROW_CHUNKS), :]
         for c in range(ROW_CHUNKS)], axis=1)


def _store_words(ref, words, row0=0):
    n_rows = words.shape[0]
    for c in range(ROW_CHUNKS):
        ref[pl.ds(row0 * ROW_CHUNKS + c, n_rows, stride=ROW_CHUNKS), :] = (
            words[:, c * V7X_LANES:(c + 1) * V7X_LANES])


def _pack_rows(val):
    half = val.shape[1] // 2
    return pltpu.pack_elementwise([val[:, :half], val[:, half:]], packed_dtype=BF16)


def _unpack_rows(words):
    halves = [pltpu.unpack_elementwise(words, index=i, packed_dtype=BF16, unpacked_dtype=F32)
              for i in range(2)]
    return jnp.concatenate(halves, axis=1)


def _ada_kernel(ct_ref, w_ref, b_ref, o_ref):
    ct = ct_ref[...]
    cond = _silu(ct)
    w = w_ref[0]
    nb = ct.shape[1]
    for b in range(nb):
        col = cond[:, b:b + 1]
        o_ref[0, b:b + 1, :] = jnp.sum(col * w, axis=0, keepdims=True) + b_ref[0]


def _ada_mod(c, ada_w, ada_b):
    depth, d, six_d = ada_w.shape
    bsz = c.shape[0]
    nb = D_MODEL
    return pl.pallas_call(
        _ada_kernel,
        grid=(depth, six_d // nb),
        in_specs=[
            pl.BlockSpec((d, bsz), lambda l, j: (0, 0)),
            pl.BlockSpec((1, d, nb), lambda l, j: (l, 0, j)),
            pl.BlockSpec((1, 1, nb), lambda l, j: (l, 0, j)),
        ],
        out_specs=pl.BlockSpec((1, bsz, nb), lambda l, j: (l, 0, j)),
        out_shape=jax.ShapeDtypeStruct((depth, bsz, six_d), F32),
        compiler_params=pltpu.CompilerParams(
            dimension_semantics=("arbitrary", "arbitrary"),
            vmem_limit_bytes=V7X_VMEM_LIMIT_BYTES),
        name="ada_mod",
    )(c.T, ada_w, ada_b.reshape(depth, 1, six_d))


def _route(h2_bf, rwt_ref, rbias_ref, eidx_ref, wsel_ref, rank_ref, counts_ref, cnt_ref):
    nt = (((1,), (1,)), ((), ()))
    r = lax.dot_general(rwt_ref[...], h2_bf, nt, preferred_element_type=F32)
    logits = r[:N_EXPERTS] + r[N_EXPERTS:]
    m = jnp.max(logits, axis=0, keepdims=True)
    ex = jnp.exp(logits - m)
    probs = ex / jnp.sum(ex, axis=0, keepdims=True)
    sel = probs + rbias_ref[...]
    s = [sel[e:e + 1] for e in range(N_EXPERTS)]
    p = [probs[e:e + 1] for e in range(N_EXPERTS)]
    best = None
    gi = None
    for g in range(N_GROUPS):
        a, b, c, d = s[4 * g:4 * g + 4]
        hi1, lo1 = jnp.maximum(a, b), jnp.minimum(a, b)
        hi2, lo2 = jnp.maximum(c, d), jnp.minimum(c, d)
        top1 = jnp.maximum(hi1, hi2)
        top2 = jnp.maximum(jnp.minimum(hi1, hi2), jnp.maximum(lo1, lo2))
        score = top1 + top2
        if g == 0:
            best, gi = score, jnp.zeros(score.shape, jnp.int32)
        else:
            upd = score > best
            gi = jnp.where(upd, g, gi)
            best = jnp.where(upd, score, best)
    v, q = [], []
    for j in range(EXPERTS_PER_GROUP):
        vj, qj = s[j], p[j]
        for g in range(1, N_GROUPS):
            pick = gi == g
            vj = jnp.where(pick, s[4 * g + j], vj)
            qj = jnp.where(pick, p[4 * g + j], qj)
        v.append(vj)
        q.append(qj)
    i1 = jnp.zeros(gi.shape, jnp.int32)
    m1 = v[0]
    for j in range(1, EXPERTS_PER_GROUP):
        upd = v[j] > m1
        i1 = jnp.where(upd, j, i1)
        m1 = jnp.where(upd, v[j], m1)
    i2 = jnp.zeros(gi.shape, jnp.int32)
    m2 = jnp.full(m1.shape, -jnp.inf, F32)
    for j in range(EXPERTS_PER_GROUP):
        cand = (i1 != j) & (v[j] > m2)
        i2 = jnp.where(cand, j, i2)
        m2 = jnp.where(cand, v[j], m2)
    pa = q[0]
    pb = q[0]
    for j in range(1, EXPERTS_PER_GROUP):
        pa = jnp.where(i1 == j, q[j], pa)
        pb = jnp.where(i2 == j, q[j], pb)
    tot = pa + pb
    e0 = gi * EXPERTS_PER_GROUP + i1
    e1 = gi * EXPERTS_PER_GROUP + i2
    t = h2_bf.shape[0]
    eidx_ref[0:1, :] = e0
    eidx_ref[1:2, :] = e1
    w_rows = jnp.concatenate(
        [pa / tot, pb / tot, jnp.zeros((V7X_LANES - TOP_K, t), F32)], axis=0)
    wsel_ref[...] = w_rows.T

    e_iota = lax.broadcasted_iota(jnp.int32, (N_EXPERTS, t), 0)
    oh0 = e_iota == e0
    oh1 = e_iota == e1
    both = jnp.where(oh0 | oh1, 1.0, 0.0)
    r_i = lax.broadcasted_iota(jnp.int32, (V7X_LANES, V7X_LANES), 0)
    c_i = lax.broadcasted_iota(jnp.int32, (V7X_LANES, V7X_LANES), 1)
    before = jnp.where(r_i < c_i, 1.0, 0.0).astype(BF16)
    run = cnt_ref[...]
    rank0, rank1 = [], []
    for blk in range(t // V7X_LANES):
        lanes = slice(blk * V7X_LANES, (blk + 1) * V7X_LANES)
        b = both[:, lanes]
        pre = jnp.dot(b.astype(BF16), before, preferred_element_type=F32) + run
        rank0.append(jnp.sum(jnp.where(oh0[:, lanes], pre, 0.0), axis=0, keepdims=True))
        rank1.append(jnp.sum(jnp.where(oh1[:, lanes], pre, 0.0), axis=0, keepdims=True))
        run = run + jnp.sum(b, axis=1, keepdims=True)
    cnt_ref[...] = run
    rank_ref[0:1, :] = jnp.concatenate(rank0, axis=1).astype(jnp.int32)
    rank_ref[1:2, :] = jnp.concatenate(rank1, axis=1).astype(jnp.int32)
    counts_ref[...] = jnp.broadcast_to(run, counts_ref.shape).astype(jnp.int32)


def _finish_mixer(x, m, mod, n2g_ref, rwt_ref, rbias_ref,
                  x1_ref, h2_ref, eidx_ref, wsel_ref, rank_ref, counts_ref, cnt_ref):
    x1 = x + mod[2:3] * m
    x1_ref[...] = x1
    h2 = _rms_mod(x1, n2g_ref[...], mod[3:4], mod[4:5])
    h2_bf = h2.astype(BF16)
    _store_words(h2_ref, _pack_rows(h2))
    _route(h2_bf, rwt_ref, rbias_ref, eidx_ref, wsel_ref, rank_ref, counts_ref, cnt_ref)


def _moe_residual(x_ref, y0_ref, y1_ref, wt_ref, g2_row):
    tm = x_ref.shape[0]
    wt = wt_ref[...]
    y0 = _unpack_rows(_load_words(y0_ref, tm))
    y1 = _unpack_rows(_load_words(y1_ref, tm))
    y = wt[:, 0:1] * y0 + wt[:, 1:2] * y1
    return x_ref[...] + g2_row * y


def _mixer_ab_kernel(x_ref, mod_ref, n1g_ref, n2g_ref, win_ref, poolw_ref, pscale_ref,
                     convw_ref, convb_ref, lng_ref, lnb_ref, wout_ref, rwt_ref, rbias_ref,
                     x1_ref, h2_ref, eidx_ref, wsel_ref, rank_ref, counts_ref,
                     pool_ext, conv_ext, cnt_ref):
    seq_tile = pl.program_id(0)
    tm = x_ref.shape[0]

    @pl.when(seq_tile == 0)
    def _():
        pool_ext[0:POOL_HIST, :] = jnp.zeros((POOL_HIST, D_HALF), F32)
        conv_ext[0:CONV_HIST, :] = jnp.zeros((CONV_HIST, D_HALF), F32)
        cnt_ref[...] = jnp.zeros_like(cnt_ref)

    x = x_ref[...]
    mod = mod_ref[0, 0]
    h = _rms_mod(x, n1g_ref[...], mod[0:1], mod[1:2]).astype(BF16)
    z = jnp.dot(h, win_ref[...], preferred_element_type=F32)
    zp = z[:, :D_HALF]
    glu = z[:, D_HALF:2 * D_HALF] * _sigmoid(z[:, 2 * D_HALF:])
    pool_ext[POOL_HIST:POOL_HIST + tm, :] = zp
    conv_ext[CONV_HIST:CONV_HIST + tm, :] = glu

    row = lax.broadcasted_iota(jnp.int32, (tm, 1), 0)
    pos1 = (seq_tile * tm + row + 1).astype(F32)
    pool_out = []
    for g, w in enumerate(POOL_WINDOWS):
        cols = slice(g * POOL_GROUP, (g + 1) * POOL_GROUP)
        acc = pool_ext[:, cols]
        span = 1
        while span < w:
            acc = acc + _shift_rows(acc, span)
            span *= 2
        wsum = acc[POOL_HIST:POOL_HIST + tm]
        inv_cnt = 1.0 / jnp.minimum(pos1, float(w))
        diff = wsum * inv_cnt - zp[:, cols]
        po = jnp.dot(diff.astype(BF16), poolw_ref[g], preferred_element_type=F32)
        pool_out.append(po * pscale_ref[:, cols])

    convw = convw_ref[...]
    ext_rows = tm + V7X_SUBLANES
    conv = None
    for r in range(V7X_SUBLANES):
        vr = None
        for a in range(CONV_HIST // V7X_SUBLANES):
            lag = V7X_SUBLANES * a + r
            if lag >= CONF_KERNEL:
                continue
            k = CONF_KERNEL - 1 - lag
            start = CONV_HIST - V7X_SUBLANES - V7X_SUBLANES * a
            term = convw[k:k + 1, :] * conv_ext[start:start + ext_rows, :]
            vr = term if vr is None else vr + term
        if r:
            vr = _shift_rows(vr, r)
        conv = vr if conv is None else conv + vr
    conv = conv[V7X_SUBLANES:V7X_SUBLANES + tm] + convb_ref[...]
    conf = _silu(_layer_norm(conv, lng_ref[...], lnb_ref[...]))

    pool_ext[0:POOL_HIST, :] = zp[tm - POOL_HIST:tm]
    conv_ext[0:CONV_HIST, :] = glu[tm - CONV_HIST:tm]

    m = jnp.dot(conf.astype(BF16), wout_ref[D_HALF:, :], preferred_element_type=F32)
    for g in range(len(POOL_WINDOWS)):
        rows = slice(g * POOL_GROUP, (g + 1) * POOL_GROUP)
        m = m + jnp.dot(pool_out[g].astype(BF16), wout_ref[rows, :], preferred_element_type=F32)
    _finish_mixer(x, m, mod, n2g_ref, rwt_ref, rbias_ref,
                  x1_ref, h2_ref, eidx_ref, wsel_ref, rank_ref, counts_ref, cnt_ref)


def _mixer_cd_kernel(x_ref, y0_ref, y1_ref, wt_ref, modp_ref,
                     mod_ref, n1g_ref, n2g_ref, win_ref, sconvw_ref, lng_ref, lnb_ref,
                     ws_ref, bsf_ref, wout_ref, rwt_ref, rbias_ref,
                     x1_ref, h2_ref, eidx_ref, wsel_ref, rank_ref, counts_ref,
                     sconv_ext, cnt_ref):
    tm = x_ref.shape[0]

    @pl.when(pl.program_id(0) == 0)
    def _():
        sconv_ext[0:SCONV_HIST, :] = jnp.zeros((SCONV_HIST, D_HALF), F32)
        cnt_ref[...] = jnp.zeros_like(cnt_ref)

    x = _moe_residual(x_ref, y0_ref, y1_ref, wt_ref, modp_ref[0, 0][5:6])
    mod = mod_ref[0, 0]
    h = _rms_mod(x, n1g_ref[...], mod[0:1], mod[1:2]).astype(BF16)
    def proj(lo, hi):
        return jnp.dot(h, win_ref[:, lo:hi], preferred_element_type=F32)

    v = _layer_norm(_gelu_tanh(proj(4 * D_HALF, 5 * D_HALF)), lng_ref[...], lnb_ref[...])
    u = _gelu_tanh(proj(3 * D_HALF, 4 * D_HALF))
    ch = proj(D_HALF, 2 * D_HALF) * proj(2 * D_HALF, 3 * D_HALF)
    bg = proj(0, D_HALF)

    sconv_ext[SCONV_HIST:SCONV_HIST + tm, :] = ch
    sw = sconvw_ref[...]
    ext = sconv_ext[...]
    conv = sw[2:3, :] * ext
    conv = conv + sw[1:2, :] * _shift_rows(ext, 1)
    conv = conv + sw[0:1, :] * _shift_rows(ext, 2)
    sc_out = bg * conv[SCONV_HIST:SCONV_HIST + tm]
    sconv_ext[0:SCONV_HIST, :] = ch[tm - SCONV_HIST:tm]

    r_i = lax.broadcasted_iota(jnp.int32, (CHUNK, CHUNK), 0)
    c_i = lax.broadcasted_iota(jnp.int32, (CHUNK, CHUNK), 1)
    tril = c_i <= r_i
    wm = [jnp.where(tril, ws_ref[hd], 0.0).astype(BF16) for hd in range(GMLP_HEADS)]
    v_bf = v.astype(BF16)
    bsf = bsf_ref[...]
    gm_rows = []
    for n in range(tm // CHUNK):
        rows = slice(n * CHUNK, (n + 1) * CHUNK)
        heads = []
        for hd in range(GMLP_HEADS):
            cols = slice(hd * POOL_GROUP, (hd + 1) * POOL_GROUP)
            heads.append(jnp.dot(wm[hd], v_bf[rows, cols], preferred_element_type=F32))
        mixed = jnp.concatenate(heads, axis=1) + bsf
        gm_rows.append(u[rows] * mixed)
    gm_out = jnp.concatenate(gm_rows, axis=0)

    m = jnp.dot(sc_out.astype(BF16), wout_ref[:D_HALF, :], preferred_element_type=F32)
    m = m + jnp.dot(gm_out.astype(BF16), wout_ref[D_HALF:, :], preferred_element_type=F32)
    _finish_mixer(x, m, mod, n2g_ref, rwt_ref, rbias_ref,
                  x1_ref, h2_ref, eidx_ref, wsel_ref, rank_ref, counts_ref, cnt_ref)


def _const_spec(shape):
    nd = len(shape)
    return pl.BlockSpec(shape, lambda i: (0,) * nd)


def _mixer_call(kernel_fn, layer, batch, stream_inputs, stream_specs, mod4, n1g, n2g, weights,
                rwt, rbias, scratch, seq_len, name):
    n_tok = seq_len
    d = D_MODEL
    tm = MIX_TILE
    in_specs = stream_specs + [
        pl.BlockSpec((1, 1, 6, d), lambda i: (layer, batch, 0, 0)),
        _const_spec(n1g.shape),
        _const_spec(n2g.shape),
    ] + [_const_spec(w.shape) for w in weights] + [_const_spec(rwt.shape), _const_spec(rbias.shape)]
    out_specs = [
        pl.BlockSpec((tm, d), lambda i: (i, 0)),
        pl.BlockSpec((tm * ROW_CHUNKS, V7X_LANES), lambda i: (i, 0)),
        pl.BlockSpec((TOP_K, tm), lambda i: (0, i)),
        pl.BlockSpec((tm, V7X_LANES), lambda i: (i, 0)),
        pl.BlockSpec((TOP_K, tm), lambda i: (0, i)),
        pl.BlockSpec((N_EXPERTS, V7X_LANES), lambda i: (0, 0)),
    ]
    out_shape = [
        jax.ShapeDtypeStruct((n_tok, d), F32),
        jax.ShapeDtypeStruct((n_tok * ROW_CHUNKS, V7X_LANES), U32),
        jax.ShapeDtypeStruct((TOP_K, n_tok), jnp.int32),
        jax.ShapeDtypeStruct((n_tok, V7X_LANES), F32),
        jax.ShapeDtypeStruct((TOP_K, n_tok), jnp.int32),
        jax.ShapeDtypeStruct((N_EXPERTS, V7X_LANES), jnp.int32),
    ]
    return pl.pallas_call(
        kernel_fn,
        grid=(n_tok // tm,),
        in_specs=in_specs,
        out_specs=out_specs,
        out_shape=out_shape,
        scratch_shapes=scratch + [pltpu.VMEM((N_EXPERTS, 1), F32)],
        compiler_params=pltpu.CompilerParams(
            dimension_semantics=("arbitrary",),
            vmem_limit_bytes=V7X_VMEM_LIMIT_BYTES),
        name=name,
    )(*stream_inputs, mod4, n1g, n2g, *weights, rwt, rbias)


def _combine_specs(n_tok, tm):
    n_tiles = n_tok // tm
    return [
        pl.BlockSpec((tm, D_MODEL), lambda i: (i, 0)),
        pl.BlockSpec((tm * ROW_CHUNKS, V7X_LANES), lambda i: (i, 0)),
        pl.BlockSpec((tm * ROW_CHUNKS, V7X_LANES), lambda i: (n_tiles + i, 0)),
        pl.BlockSpec((tm, V7X_LANES), lambda i: (i, 0)),
    ]


SCHED_EXPERT, SCHED_VALID, SCHED_USED, SCHED_FIRST, SCHED_NEXT, SCHED_SLOT = range(6)
SCHED_ROWS = V7X_SUBLANES


def _plan_kernel(eidx_ref, rank_ref, counts_ref, pos_ref, sched_ref):
    i32 = jnp.int32
    shift = MOE_TILE.bit_length() - 1
    cnt = counts_ref[:, 0:1]
    padded = ((cnt + (MOE_TILE - 1)) >> shift) << shift
    seg = [padded[e:e + 1] for e in range(N_EXPERTS)]
    starts, ends = [], []
    run = jnp.zeros((1, 1), i32)
    for e in range(N_EXPERTS):
        starts.append(run)
        run = run + seg[e]
        ends.append(run)
    total = run
    nexts = [None] * N_EXPERTS
    nxt = jnp.full((1, 1), -1, i32)
    for e in reversed(range(N_EXPERTS)):
        nexts[e] = nxt
        nxt = jnp.where(seg[e] > 0, e, nxt)
    slots = []
    seen = jnp.zeros((1, 1), i32)
    for e in range(N_EXPERTS):
        slots.append(seen & 1)
        seen = seen + (seg[e] > 0).astype(i32)

    eidx = eidx_ref[...]
    pos = rank_ref[...]
    tile_row0 = lax.broadcasted_iota(i32, (1, V7X_LANES), 1) * MOE_TILE
    te = jnp.zeros((1, V7X_LANES), i32)
    for e in range(N_EXPERTS):
        pos = pos + jnp.where(eidx == e, starts[e], 0)
        te = te + (tile_row0 >= ends[e]).astype(i32)
    pos_ref[...] = pos
    te = jnp.minimum(te, N_EXPERTS - 1)

    def of_tile(per_expert):
        acc = jnp.zeros((1, V7X_LANES), i32)
        for e in range(N_EXPERTS):
            acc = acc + jnp.where(te == e, per_expert[e], 0)
        return acc

    valid_end = of_tile([starts[e] + cnt[e:e + 1] for e in range(N_EXPERTS)])
    used = tile_row0 < total
    rows = {
        SCHED_EXPERT: te,
        SCHED_VALID: jnp.clip(valid_end - tile_row0, 0, MOE_TILE),
        SCHED_USED: jnp.broadcast_to(total >> shift, (1, V7X_LANES)),
        SCHED_FIRST: (used & (of_tile(starts) == tile_row0)).astype(i32),
        SCHED_NEXT: of_tile(nexts),
        SCHED_SLOT: of_tile(slots),
    }
    for r in range(SCHED_ROWS):
        sched_ref[r:r + 1, :] = rows.get(r, jnp.zeros((1, V7X_LANES), i32))


def _sorted_positions(eidx, rank, counts):
    n_pairs = eidx.shape[0] * eidx.shape[1]
    n_rows = n_pairs + N_EXPERTS * MOE_TILE
    assert n_rows // MOE_TILE <= V7X_LANES
    pos, sched = pl.pallas_call(
        _plan_kernel,
        out_shape=[jax.ShapeDtypeStruct(eidx.shape, jnp.int32),
                   jax.ShapeDtypeStruct((SCHED_ROWS, V7X_LANES), jnp.int32)],
        name="moe_plan",
    )(eidx, rank, counts)
    return pos.reshape(n_pairs), sched, n_rows


def _sc_move_rows(scatter, src, pos_flat, n_out_rows, name):
    info = plsc.get_sparse_core_info()
    n_workers = info.num_cores * info.num_subcores
    n_pairs = pos_flat.shape[0]
    n_src = src.shape[0]
    per_worker = n_pairs // n_workers
    n_chunks = per_worker // SC_CHUNK
    assert per_worker * n_workers == n_pairs and n_chunks * SC_CHUNK == per_worker
    assert n_src % per_worker == 0
    idx = pos_flat.reshape(n_workers, n_chunks, SC_CHUNK)
    mesh = plsc.VectorSubcoreMesh(core_axis_name="core", subcore_axis_name="subcore")

    @functools.partial(
        pl.kernel,
        out_type=jax.ShapeDtypeStruct((n_out_rows, ROW_CHUNKS, V7X_LANES), U32),
        mesh=mesh,
        scratch_types=[
            pltpu.VMEM((n_chunks, SC_CHUNK), jnp.int32),
            pltpu.VMEM((2, SC_CHUNK, ROW_CHUNKS, V7X_LANES), U32),
            pltpu.SemaphoreType.DMA((2,)),
            pltpu.SemaphoreType.DMA((2,)),
        ],
        name=name)
    def move(src_hbm, i_hbm, o_hbm, idx_v, buf, in_sem, out_sem):
        wid = lax.axis_index("subcore") * info.num_cores + lax.axis_index("core")
        base = wid * per_worker
        src_base = lax.rem(base, n_src)
        pltpu.sync_copy(i_hbm.at[wid], idx_v)

        def fetch(s, slot):
            if scatter:
                rows = src_hbm.at[pl.ds(src_base + s * SC_CHUNK, SC_CHUNK)]
            else:
                rows = src_hbm.at[idx_v.at[s]]
            return pltpu.make_async_copy(rows, buf.at[slot], in_sem.at[slot])

        def flush(s, slot):
            if scatter:
                rows = o_hbm.at[idx_v.at[s]]
            else:
                rows = o_hbm.at[pl.ds(base + s * SC_CHUNK, SC_CHUNK)]
            return pltpu.make_async_copy(buf.at[slot], rows, out_sem.at[slot])

        fetch(0, 0).start()
        for s in range(n_chunks):
            slot = s % 2
            fetch(s, slot).wait()
            flush(s, slot).start()
            if s + 1 < n_chunks:
                if s >= 1:
                    flush(s - 1, 1 - slot).wait()
                fetch(s + 1, 1 - slot).start()
        flush(n_chunks - 2, n_chunks % 2).wait()
        flush(n_chunks - 1, (n_chunks - 1) % 2).wait()

    return move(src, idx)


def _experts_kernel(layer, sched_ref, x_ref, wg_hbm, wu_hbm, wd_hbm, y_ref,
                    wg_f32, wu_f32, wd_f32, wg_bf, wu_bf, wd_bf, sems):
    tm = MOE_TILE

    def weight_copies(expert, slot):
        pairs = ((wg_hbm, wg_f32), (wu_hbm, wu_f32), (wd_hbm, wd_f32))
        return [pltpu.make_async_copy(src.at[layer, expert], dst.at[slot], sems.at[slot, m])
                for m, (src, dst) in enumerate(pairs)]

    for part in range(MOE_STEP_TILES):
        j = pl.program_id(0) * MOE_STEP_TILES + part
        used = j < sched_ref[SCHED_USED, j]
        expert = sched_ref[SCHED_EXPERT, j]

        @pl.when(used & (sched_ref[SCHED_FIRST, j] == 1))
        def _():
            slot = sched_ref[SCHED_SLOT, j]
            next_expert = sched_ref[SCHED_NEXT, j]

            @pl.when(j == 0)
            def _():
                for cp in weight_copies(expert, slot):
                    cp.start()

            for cp in weight_copies(expert, slot):
                cp.wait()
            wg_bf[...] = wg_f32[slot].astype(BF16)
            wu_bf[...] = wu_f32[slot].astype(BF16)
            wd_bf[...] = wd_f32[slot].astype(BF16)

            @pl.when(next_expert >= 0)
            def _():
                for cp in weight_copies(next_expert, 1 - slot):
                    cp.start()

        @pl.when(used)
        def _():
            row = lax.broadcasted_iota(jnp.int32, (tm, 1), 0)
            words = jnp.where(row < sched_ref[SCHED_VALID, j],
                              _load_words(x_ref, tm, part * tm), jnp.uint32(0))
            h = _unpack_rows(words).astype(BF16)
            a = jnp.dot(h, wg_bf[...], preferred_element_type=F32)
            b = jnp.dot(h, wu_bf[...], preferred_element_type=F32)
            t = (_silu(a) * b).astype(BF16)
            y = jnp.dot(t, wd_bf[...], preferred_element_type=F32)
            _store_words(y_ref, _pack_rows(y), part * tm)

        @pl.when(jnp.logical_not(used))
        def _():
            lo = part * tm * ROW_CHUNKS
            y_ref[lo:lo + tm * ROW_CHUNKS, :] = jnp.zeros((tm * ROW_CHUNKS, V7X_LANES), U32)


def _experts_call(layer, rows, sched, w_gate, w_up, w_down):
    n_rows = rows.shape[0]
    tm = MOE_TILE * MOE_STEP_TILES
    d = D_MODEL
    assert n_rows % tm == 0
    rows2 = rows.reshape(n_rows * ROW_CHUNKS, V7X_LANES)

    def last_used_step(sc):
        return (sc[SCHED_USED, 0] + MOE_STEP_TILES - 1) // MOE_STEP_TILES - 1

    grid_spec = pltpu.PrefetchScalarGridSpec(
        num_scalar_prefetch=1,
        grid=(n_rows // tm,),
        in_specs=[
            pl.BlockSpec((tm * ROW_CHUNKS, V7X_LANES),
                         lambda j, sc: (jnp.minimum(j, last_used_step(sc)), 0)),
            pl.BlockSpec(memory_space=pl.ANY),
            pl.BlockSpec(memory_space=pl.ANY),
            pl.BlockSpec(memory_space=pl.ANY),
        ],
        out_specs=pl.BlockSpec((tm * ROW_CHUNKS, V7X_LANES), lambda j, sc: (j, 0)),
        scratch_shapes=[
            pltpu.VMEM((2, d, D_EXPERT), F32),
            pltpu.VMEM((2, d, D_EXPERT), F32),
            pltpu.VMEM((2, D_EXPERT, d), F32),
            pltpu.VMEM((d, D_EXPERT), BF16),
            pltpu.VMEM((d, D_EXPERT), BF16),
            pltpu.VMEM((D_EXPERT, d), BF16),
            pltpu.SemaphoreType.DMA((2, 3)),
        ],
    )
    y = pl.pallas_call(
        functools.partial(_experts_kernel, layer),
        grid_spec=grid_spec,
        out_shape=jax.ShapeDtypeStruct((n_rows * ROW_CHUNKS, V7X_LANES), U32),
        compiler_params=pltpu.CompilerParams(
            dimension_semantics=("arbitrary",),
            vmem_limit_bytes=V7X_VMEM_LIMIT_BYTES),
        name="experts_l%d" % layer,
    )(sched, rows2, w_gate, w_up, w_down)
    return y.reshape(n_rows, ROW_CHUNKS, V7X_LANES)


def _moe_rows(layer, mixer_outs, w_gate, w_up, w_down):
    plans = [_sorted_positions(eidx, rank, counts)
             for (_, _, eidx, _, rank, counts) in mixer_outs]
    sorted_rows = []
    for (_, h_rows, eidx, _, _, _), (pos_flat, _, n_rows) in zip(mixer_outs, plans):
        n_tok = eidx.shape[1]
        sorted_rows.append(_sc_move_rows(
            True, h_rows.reshape(n_tok, ROW_CHUNKS, V7X_LANES), pos_flat, n_rows, "sc_scatter_rows"))
    y_sorted = [_experts_call(layer, rows, sched, w_gate, w_up, w_down)
                for rows, (_, sched, _) in zip(sorted_rows, plans)]
    y_pairs = []
    for y, (pos_flat, _, _) in zip(y_sorted, plans):
        n_pairs = pos_flat.shape[0]
        moved = _sc_move_rows(False, y, pos_flat, n_pairs, "sc_gather_rows")
        y_pairs.append(moved.reshape(n_pairs * ROW_CHUNKS, V7X_LANES))
    return y_pairs


def _final_kernel(x_ref, y0_ref, y1_ref, wt_ref, modp_ref, fg_ref, *rest):
    o_ref = rest[-1]
    y = _moe_residual(x_ref, y0_ref, y1_ref, wt_ref, modp_ref[0, 0][5:6])
    ms = jnp.mean(y * y, axis=-1, keepdims=True)
    o_ref[...] = y * lax.rsqrt(ms + EPS) * fg_ref[...]


def _final_call(layer, batch, n_batch, x, y_pairs, wt, mod4, fg, out_prev):
    seq_len, d = x.shape
    tm = FINAL_TILE
    tiles_per_seq = seq_len // tm
    in_specs = _combine_specs(seq_len, tm) + [
        pl.BlockSpec((1, 1, 6, d), lambda i: (layer, batch, 0, 0)),
        _const_spec(fg.shape),
    ]
    args = [x, y_pairs, y_pairs, wt, mod4, fg]
    aliases = {}
    if out_prev is not None:
        in_specs.append(pl.BlockSpec(memory_space=pl.ANY))
        aliases = {len(args): 0}
        args.append(out_prev)
    return pl.pallas_call(
        _final_kernel,
        grid=(tiles_per_seq,),
        in_specs=in_specs,
        out_specs=pl.BlockSpec((tm, d), lambda i: (batch * tiles_per_seq + i, 0)),
        out_shape=jax.ShapeDtypeStruct((n_batch * seq_len, d), F32),
        input_output_aliases=aliases,
        compiler_params=pltpu.CompilerParams(
            dimension_semantics=("arbitrary",),
            vmem_limit_bytes=V7X_VMEM_LIMIT_BYTES),
        name="final_norm",
    )(*args)


def kernel(x, c, norm1_g, norm2_g, ada_w, ada_b, ab_w_in, pool_w, pool_scale, conf_conv_w, conf_conv_b, conf_ln_g, conf_ln_b, ab_w_out, cd_w_in, sconv_w, gmlp_ln_g, gmlp_ln_b, gmlp_ws, gmlp_bs, cd_w_out, router_w, router_bias, exp_w_gate, exp_w_up, exp_w_down, final_g):
    bsz, seq_len, d = x.shape
    n_tok = bsz * seq_len
    tm = MIX_TILE
    tiles_per_seq = seq_len // tm
    xf = x.reshape(n_tok, d)

    mod = _ada_mod(c, ada_w, ada_b)
    mod4 = mod.reshape(mod.shape[0], bsz, 6, d)

    rw_hi = router_w.astype(BF16)
    rw_lo = (router_w - rw_hi.astype(F32)).astype(BF16)
    rwt = jnp.concatenate([rw_hi.T, rw_lo.T], axis=0)
    rbias = router_bias.reshape(N_EXPERTS, 1)
    fg = final_g.reshape(1, d)

    weights_ab = [
        ab_w_in[0].astype(BF16), pool_w[0].astype(BF16), pool_scale[0].reshape(1, D_HALF),
        conf_conv_w[0], conf_conv_b[0].reshape(1, D_HALF), conf_ln_g[0].reshape(1, D_HALF),
        conf_ln_b[0].reshape(1, D_HALF), ab_w_out[0].astype(BF16),
    ]
    scratch_ab = [pltpu.VMEM((POOL_HIST + tm, D_HALF), F32),
                  pltpu.VMEM((CONV_HIST + tm, D_HALF), F32)]
    bsf = jnp.repeat(gmlp_bs[0].T, POOL_GROUP, axis=1)
    weights_cd = [
        cd_w_in[0].astype(BF16), sconv_w[0], gmlp_ln_g[0].reshape(1, D_HALF),
        gmlp_ln_b[0].reshape(1, D_HALF), gmlp_ws[0], bsf, cd_w_out[0].astype(BF16),
    ]
    scratch_cd = [pltpu.VMEM((SCONV_HIST + tm, D_HALF), F32)]
    experts = (exp_w_gate, exp_w_up, exp_w_down)

    batches = range(bsz)
    stage_ab = []
    for b in batches:
        x_spec = pl.BlockSpec((tm, d), lambda i, b=b: (b * tiles_per_seq + i, 0))
        stage_ab.append(_mixer_call(
            _mixer_ab_kernel, 0, b, [xf], [x_spec], mod4, norm1_g[0:1], norm2_g[0:1],
            weights_ab, rwt, rbias, scratch_ab, seq_len, "mixer_ab"))
    y_pairs0 = _moe_rows(0, stage_ab, *experts)

    stage_cd = []
    for b in batches:
        x1, _, _, wsel0, _, _ = stage_ab[b]
        prev_mod_spec = pl.BlockSpec((1, 1, 6, d), lambda i, b=b: (0, b, 0, 0))
        stage_cd.append(_mixer_call(
            _mixer_cd_kernel, 1, b, [x1, y_pairs0[b], y_pairs0[b], wsel0, mod4],
            _combine_specs(seq_len, tm) + [prev_mod_spec], mod4, norm1_g[1:2], norm2_g[1:2],
            weights_cd, rwt, rbias, scratch_cd, seq_len, "mixer_cd"))
    y_pairs1 = _moe_rows(1, stage_cd, *experts)

    out = None
    for b in batches:
        x3, _, _, wsel1, _, _ = stage_cd[b]
        out = _final_call(1, b, bsz, x3, y_pairs1[b], wsel1, mod4, fg, out)
    return out.reshape(bsz, seq_len, d)
```

```python
import functools

import jax
import jax.numpy as jnp
from jax import lax
from jax.experimental import pallas as pl
from jax.experimental.pallas import tpu as pltpu
from jax.experimental.pallas import tpu_sc as plsc

D_MODEL = 1024
EPS = 1e-6
POOL_WINDOWS = (2, 4, 8, 16)
POOL_GROUP = 128
D_HALF = 512
CONF_KERNEL = 31
SCONV_KERNEL = 3
CHUNK = 128
GMLP_HEADS = 4
N_EXPERTS = 16
N_GROUPS = 4
EXPERTS_PER_GROUP = 4
TOP_K = 2
D_EXPERT = 512

V7X_LANES = 128
V7X_SUBLANES = 8
V7X_VMEM_LIMIT_BYTES = 56 * 1024 * 1024

MIX_TILE = 512
FINAL_TILE = 1024
MOE_TILE = 512
MOE_STEP_TILES = 4
SC_CHUNK = 64
ROW_CHUNKS = D_MODEL // (2 * V7X_LANES)
CONV_HIST = 32
POOL_HIST = 16
SCONV_HIST = 8

BF16 = jnp.bfloat16
F32 = jnp.float32
U32 = jnp.uint32


def _rms_mod(x, g_row, shift_row, scale_row):
    ms = jnp.mean(x * x, axis=-1, keepdims=True)
    gain = g_row * (1.0 + scale_row)
    return (x * lax.rsqrt(ms + EPS)) * gain + shift_row


def _layer_norm(x, g_row, b_row):
    mu = jnp.mean(x, axis=-1, keepdims=True)
    xc = x - mu
    var = jnp.mean(xc * xc, axis=-1, keepdims=True)
    return xc * lax.rsqrt(var + EPS) * g_row + b_row


def _sigmoid(x):
    return 1.0 / (1.0 + jnp.exp(-x))


def _silu(x):
    return x * _sigmoid(x)


def _gelu_tanh(x):
    c = 0.7978845608028654
    return 0.5 * x * (1.0 + jnp.tanh(c * (x + 0.044715 * (x * x * x))))


def _shift_rows(x, r):
    n, c = x.shape
    if r == V7X_SUBLANES:
        return jnp.concatenate([x[:r], x[:n - r]], axis=0)
    g = x.reshape(n // V7X_SUBLANES, V7X_SUBLANES, c)
    rot = pltpu.roll(g, r, axis=1)
    prev = jnp.concatenate([rot[:1], rot[:-1]], axis=0)
    sub = lax.broadcasted_iota(jnp.int32, g.shape, 1)
    return jnp.where(sub < r, prev, rot).reshape(n, c)


def _load_words(ref, n_rows, row0=0):
    return jnp.concatenate(
        [ref[pl.ds(row0 * ROW_CHUNKS + c, n_rows, stride=ROW_CHUNKS), :]
         for c in range(ROW_CHUNKS)], axis=1)


def _store_words(ref, words, row0=0):
    n_rows = words.shape[0]
    for c in range(ROW_CHUNKS):
        ref[pl.ds(row0 * ROW_CHUNKS + c, n_rows, stride=ROW_CHUNKS), :] = (
            words[:, c * V7X_LANES:(c + 1) * V7X_LANES])


def _pack_rows(val):
    half = val.shape[1] // 2
    return pltpu.pack_elementwise([val[:, :half], val[:, half:]], packed_dtype=BF16)


def _unpack_rows(words):
    halves = [pltpu.unpack_elementwise(words, index=i, packed_dtype=BF16, unpacked_dtype=F32)
              for i in range(2)]
    return jnp.concatenate(halves, axis=1)


def _ada_kernel(ct_ref, w_ref, b_ref, o_ref):
    ct = ct_ref[...]
    cond = _silu(ct)
    w = w_ref[0]
    nb = ct.shape[1]
    for b in range(nb):
        col = cond[:, b:b + 1]
        o_ref[0, b:b + 1, :] = jnp.sum(col * w, axis=0, keepdims=True) + b_ref[0]


def _ada_mod(c, ada_w, ada_b):
    depth, d, six_d = ada_w.shape
    bsz = c.shape[0]
    nb = D_MODEL
    return pl.pallas_call(
        _ada_kernel,
        grid=(depth, six_d // nb),
        in_specs=[
            pl.BlockSpec((d, bsz), lambda l, j: (0, 0)),
            pl.BlockSpec((1, d, nb), lambda l, j: (l, 0, j)),
            pl.BlockSpec((1, 1, nb), lambda l, j: (l, 0, j)),
        ],
        out_specs=pl.BlockSpec((1, bsz, nb), lambda l, j: (l, 0, j)),
        out_shape=jax.ShapeDtypeStruct((depth, bsz, six_d), F32),
        compiler_params=pltpu.CompilerParams(
            dimension_semantics=("arbitrary", "arbitrary"),
            vmem_limit_bytes=V7X_VMEM_LIMIT_BYTES),
        name="ada_mod",
    )(c.T, ada_w, ada_b.reshape(depth, 1, six_d))


def _route(h2_bf, rwt_ref, rbias_ref, eidx_ref, wsel_ref, rank_ref, counts_ref, cnt_ref):
    nt = (((1,), (1,)), ((), ()))
    r = lax.dot_general(rwt_ref[...], h2_bf, nt, preferred_element_type=F32)
    logits = r[:N_EXPERTS] + r[N_EXPERTS:]
    m = jnp.max(logits, axis=0, keepdims=True)
    ex = jnp.exp(logits - m)
    probs = ex / jnp.sum(ex, axis=0, keepdims=True)
    sel = probs + rbias_ref[...]
    s = [sel[e:e + 1] for e in range(N_EXPERTS)]
    p = [probs[e:e + 1] for e in range(N_EXPERTS)]
    best = None
    gi = None
    for g in range(N_GROUPS):
        a, b, c, d = s[4 * g:4 * g + 4]
        hi1, lo1 = jnp.maximum(a, b), jnp.minimum(a, b)
        hi2, lo2 = jnp.maximum(c, d), jnp.minimum(c, d)
        top1 = jnp.maximum(hi1, hi2)
        top2 = jnp.maximum(jnp.minimum(hi1, hi2), jnp.maximum(lo1, lo2))
        score = top1 + top2
        if g == 0:
            best, gi = score, jnp.zeros(score.shape, jnp.int32)
        else:
            upd = score > best
            gi = jnp.where(upd, g, gi)
            best = jnp.where(upd, score, best)
    v, q = [], []
    for j in range(EXPERTS_PER_GROUP):
        vj, qj = s[j], p[j]
        for g in range(1, N_GROUPS):
            pick = gi == g
            vj = jnp.where(pick, s[4 * g + j], vj)
            qj = jnp.where(pick, p[4 * g + j], qj)
        v.append(vj)
        q.append(qj)
    i1 = jnp.zeros(gi.shape, jnp.int32)
    m1 = v[0]
    for j in range(1, EXPERTS_PER_GROUP):
        upd = v[j] > m1
        i1 = jnp.where(upd, j, i1)
        m1 = jnp.where(upd, v[j], m1)
    i2 = jnp.zeros(gi.shape, jnp.int32)
    m2 = jnp.full(m1.shape, -jnp.inf, F32)
    for j in range(EXPERTS_PER_GROUP):
        cand = (i1 != j) & (v[j] > m2)
        i2 = jnp.where(cand, j, i2)
        m2 = jnp.where(cand, v[j], m2)
    pa = q[0]
    pb = q[0]
    for j in range(1, EXPERTS_PER_GROUP):
        pa = jnp.where(i1 == j, q[j], pa)
        pb = jnp.where(i2 == j, q[j], pb)
    tot = pa + pb
    e0 = gi * EXPERTS_PER_GROUP + i1
    e1 = gi * EXPERTS_PER_GROUP + i2
    t = h2_bf.shape[0]
    eidx_ref[0:1, :] = e0
    eidx_ref[1:2, :] = e1
    w_rows = jnp.concatenate(
        [pa / tot, pb / tot, jnp.zeros((V7X_LANES - TOP_K, t), F32)], axis=0)
    wsel_ref[...] = w_rows.T

    e_iota = lax.broadcasted_iota(jnp.int32, (N_EXPERTS, t), 0)
    oh0 = e_iota == e0
    oh1 = e_iota == e1
    both = jnp.where(oh0 | oh1, 1.0, 0.0)
    r_i = lax.broadcasted_iota(jnp.int32, (V7X_LANES, V7X_LANES), 0)
    c_i = lax.broadcasted_iota(jnp.int32, (V7X_LANES, V7X_LANES), 1)
    before = jnp.where(r_i < c_i, 1.0, 0.0).astype(BF16)
    run = cnt_ref[...]
    rank0, rank1 = [], []
    for blk in range(t // V7X_LANES):
        lanes = slice(blk * V7X_LANES, (blk + 1) * V7X_LANES)
        b = both[:, lanes]
        pre = jnp.dot(b.astype(BF16), before, preferred_element_type=F32) + run
        rank0.append(jnp.sum(jnp.where(oh0[:, lanes], pre, 0.0), axis=0, keepdims=True))
        rank1.append(jnp.sum(jnp.where(oh1[:, lanes], pre, 0.0), axis=0, keepdims=True))
        run = run + jnp.sum(b, axis=1, keepdims=True)
    cnt_ref[...] = run
    rank_ref[0:1, :] = jnp.concatenate(rank0, axis=1).astype(jnp.int32)
    rank_ref[1:2, :] = jnp.concatenate(rank1, axis=1).astype(jnp.int32)
    counts_ref[...] = jnp.broadcast_to(run, counts_ref.shape).astype(jnp.int32)


def _finish_mixer(x, m, mod, n2g_ref, rwt_ref, rbias_ref,
                  x1_ref, h2_ref, eidx_ref, wsel_ref, rank_ref, counts_ref, cnt_ref):
    x1 = x + mod[2:3] * m
    x1_ref[...] = x1
    h2 = _rms_mod(x1, n2g_ref[...], mod[3:4], mod[4:5])
    h2_bf = h2.astype(BF16)
    _store_words(h2_ref, _pack_rows(h2))
    _route(h2_bf, rwt_ref, rbias_ref, eidx_ref, wsel_ref, rank_ref, counts_ref, cnt_ref)


def _moe_residual(x_ref, y0_ref, y1_ref, wt_ref, g2_row):
    tm = x_ref.shape[0]
    wt = wt_ref[...]
    y0 = _unpack_rows(_load_words(y0_ref, tm))
    y1 = _unpack_rows(_load_words(y1_ref, tm))
    y = wt[:, 0:1] * y0 + wt[:, 1:2] * y1
    return x_ref[...] + g2_row * y


def _mixer_ab_kernel(x_ref, mod_ref, n1g_ref, n2g_ref, win_ref, poolw_ref, pscale_ref,
                     convw_ref, convb_ref, lng_ref, lnb_ref, wout_ref, rwt_ref, rbias_ref,
                     x1_ref, h2_ref, eidx_ref, wsel_ref, rank_ref, counts_ref,
                     pool_ext, conv_ext, cnt_ref):
    seq_tile = pl.program_id(0)
    tm = x_ref.shape[0]

    @pl.when(seq_tile == 0)
    def _():
        pool_ext[0:POOL_HIST, :] = jnp.zeros((POOL_HIST, D_HALF), F32)
        conv_ext[0:CONV_HIST, :] = jnp.zeros((CONV_HIST, D_HALF), F32)
        cnt_ref[...] = jnp.zeros_like(cnt_ref)

    x = x_ref[...]
    mod = mod_ref[0, 0]
    h = _rms_mod(x, n1g_ref[...], mod[0:1], mod[1:2]).astype(BF16)
    z = jnp.dot(h, win_ref[...], preferred_element_type=F32)
    zp = z[:, :D_HALF]
    glu = z[:, D_HALF:2 * D_HALF] * _sigmoid(z[:, 2 * D_HALF:])
    pool_ext[POOL_HIST:POOL_HIST + tm, :] = zp
    conv_ext[CONV_HIST:CONV_HIST + tm, :] = glu

    row = lax.broadcasted_iota(jnp.int32, (tm, 1), 0)
    pos1 = (seq_tile * tm + row + 1).astype(F32)
    pool_out = []
    for g, w in enumerate(POOL_WINDOWS):
        cols = slice(g * POOL_GROUP, (g + 1) * POOL_GROUP)
        acc = pool_ext[:, cols]
        span = 1
        while span < w:
            acc = acc + _shift_rows(acc, span)
            span *= 2
        wsum = acc[POOL_HIST:POOL_HIST + tm]
        inv_cnt = 1.0 / jnp.minimum(pos1, float(w))
        diff = wsum * inv_cnt - zp[:, cols]
        po = jnp.dot(diff.astype(BF16), poolw_ref[g], preferred_element_type=F32)
        pool_out.append(po * pscale_ref[:, cols])

    convw = convw_ref[...]
    ext_rows = tm + V7X_SUBLANES
    conv = None
    for r in range(V7X_SUBLANES):
        vr = None
        for a in range(CONV_HIST // V7X_SUBLANES):
            lag = V7X_SUBLANES * a + r
            if lag >= CONF_KERNEL:
                continue
            k = CONF_KERNEL - 1 - lag
            start = CONV_HIST - V7X_SUBLANES - V7X_SUBLANES * a
            term = convw[k:k + 1, :] * conv_ext[start:start + ext_rows, :]
            vr = term if vr is None else vr + term
        if r:
            vr = _shift_rows(vr, r)
        conv = vr if conv is None else conv + vr
    conv = conv[V7X_SUBLANES:V7X_SUBLANES + tm] + convb_ref[...]
    conf = _silu(_layer_norm(conv, lng_ref[...], lnb_ref[...]))

    pool_ext[0:POOL_HIST, :] = zp[tm - POOL_HIST:tm]
    conv_ext[0:CONV_HIST, :] = glu[tm - CONV_HIST:tm]

    m = jnp.dot(conf.astype(BF16), wout_ref[D_HALF:, :], preferred_element_type=F32)
    for g in range(len(POOL_WINDOWS)):
        rows = slice(g * POOL_GROUP, (g + 1) * POOL_GROUP)
        m = m + jnp.dot(pool_out[g].astype(BF16), wout_ref[rows, :], preferred_element_type=F32)
    _finish_mixer(x, m, mod, n2g_ref, rwt_ref, rbias_ref,
                  x1_ref, h2_ref, eidx_ref, wsel_ref, rank_ref, counts_ref, cnt_ref)


def _mixer_cd_kernel(x_ref, y0_ref, y1_ref, wt_ref, modp_ref,
                     mod_ref, n1g_ref, n2g_ref, win_ref, sconvw_ref, lng_ref, lnb_ref,
                     ws_ref, bsf_ref, wout_ref, rwt_ref, rbias_ref,
                     x1_ref, h2_ref, eidx_ref, wsel_ref, rank_ref, counts_ref,
                     sconv_ext, cnt_ref):
    tm = x_ref.shape[0]

    @pl.when(pl.program_id(0) == 0)
    def _():
        sconv_ext[0:SCONV_HIST, :] = jnp.zeros((SCONV_HIST, D_HALF), F32)
        cnt_ref[...] = jnp.zeros_like(cnt_ref)

    x = _moe_residual(x_ref, y0_ref, y1_ref, wt_ref, modp_ref[0, 0][5:6])
    mod = mod_ref[0, 0]
    h = _rms_mod(x, n1g_ref[...], mod[0:1], mod[1:2]).astype(BF16)
    def proj(lo, hi):
        return jnp.dot(h, win_ref[:, lo:hi], preferred_element_type=F32)

    v = _layer_norm(_gelu_tanh(proj(4 * D_HALF, 5 * D_HALF)), lng_ref[...], lnb_ref[...])
    u = _gelu_tanh(proj(3 * D_HALF, 4 * D_HALF))
    ch = proj(D_HALF, 2 * D_HALF) * proj(2 * D_HALF, 3 * D_HALF)
    bg = proj(0, D_HALF)

    sconv_ext[SCONV_HIST:SCONV_HIST + tm, :] = ch
    sw = sconvw_ref[...]
    ext = sconv_ext[...]
    conv = sw[2:3, :] * ext
    conv = conv + sw[1:2, :] * _shift_rows(ext, 1)
    conv = conv + sw[0:1, :] * _shift_rows(ext, 2)
    sc_out = bg * conv[SCONV_HIST:SCONV_HIST + tm]
    sconv_ext[0:SCONV_HIST, :] = ch[tm - SCONV_HIST:tm]

    r_i = lax.broadcasted_iota(jnp.int32, (CHUNK, CHUNK), 0)
    c_i = lax.broadcasted_iota(jnp.int32, (CHUNK, CHUNK), 1)
    tril = c_i <= r_i
    wm = [jnp.where(tril, ws_ref[hd], 0.0).astype(BF16) for hd in range(GMLP_HEADS)]
    v_bf = v.astype(BF16)
    bsf = bsf_ref[...]
    gm_rows = []
    for n in range(tm // CHUNK):
        rows = slice(n * CHUNK, (n + 1) * CHUNK)
        heads = []
        for hd in range(GMLP_HEADS):
            cols = slice(hd * POOL_GROUP, (hd + 1) * POOL_GROUP)
            heads.append(jnp.dot(wm[hd], v_bf[rows, cols], preferred_element_type=F32))
        mixed = jnp.concatenate(heads, axis=1) + bsf
        gm_rows.append(u[rows] * mixed)
    gm_out = jnp.concatenate(gm_rows, axis=0)

    m = jnp.dot(sc_out.astype(BF16), wout_ref[:D_HALF, :], preferred_element_type=F32)
    m = m + jnp.dot(gm_out.astype(BF16), wout_ref[D_HALF:, :], preferred_element_type=F32)
    _finish_mixer(x, m, mod, n2g_ref, rwt_ref, rbias_ref,
                  x1_ref, h2_ref, eidx_ref, wsel_ref, rank_ref, counts_ref, cnt_ref)


def _const_spec(shape):
    nd = len(shape)
    return pl.BlockSpec(shape, lambda i: (0,) * nd)


def _mixer_call(kernel_fn, layer, batch, stream_inputs, stream_specs, mod4, n1g, n2g, weights,
                rwt, rbias, scratch, seq_len, name):
    n_tok = seq_len
    d = D_MODEL
    tm = MIX_TILE
    in_specs = stream_specs + [
        pl.BlockSpec((1, 1, 6, d), lambda i: (layer, batch, 0, 0)),
        _const_spec(n1g.shape),
        _const_spec(n2g.shape),
    ] + [_const_spec(w.shape) for w in weights] + [_const_spec(rwt.shape), _const_spec(rbias.shape)]
    out_specs = [
        pl.BlockSpec((tm, d), lambda i: (i, 0)),
        pl.BlockSpec((tm * ROW_CHUNKS, V7X_LANES), lambda i: (i, 0)),
        pl.BlockSpec((TOP_K, tm), lambda i: (0, i)),
        pl.BlockSpec((tm, V7X_LANES), lambda i: (i, 0)),
        pl.BlockSpec((TOP_K, tm), lambda i: (0, i)),
        pl.BlockSpec((N_EXPERTS, V7X_LANES), lambda i: (0, 0)),
    ]
    out_shape = [
        jax.ShapeDtypeStruct((n_tok, d), F32),
        jax.ShapeDtypeStruct((n_tok * ROW_CHUNKS, V7X_LANES), U32),
        jax.ShapeDtypeStruct((TOP_K, n_tok), jnp.int32),
        jax.ShapeDtypeStruct((n_tok, V7X_LANES), F32),
        jax.ShapeDtypeStruct((TOP_K, n_tok), jnp.int32),
        jax.ShapeDtypeStruct((N_EXPERTS, V7X_LANES), jnp.int32),
    ]
    return pl.pallas_call(
        kernel_fn,
        grid=(n_tok // tm,),
        in_specs=in_specs,
        out_specs=out_specs,
        out_shape=out_shape,
        scratch_shapes=scratch + [pltpu.VMEM((N_EXPERTS, 1), F32)],
        compiler_params=pltpu.CompilerParams(
            dimension_semantics=("arbitrary",),
            vmem_limit_bytes=V7X_VMEM_LIMIT_BYTES),
        name=name,
    )(*stream_inputs, mod4, n1g, n2g, *weights, rwt, rbias)


def _combine_specs(n_tok, tm):
    n_tiles = n_tok // tm
    return [
        pl.BlockSpec((tm, D_MODEL), lambda i: (i, 0)),
        pl.BlockSpec((tm * ROW_CHUNKS, V7X_LANES), lambda i: (i, 0)),
        pl.BlockSpec((tm * ROW_CHUNKS, V7X_LANES), lambda i: (n_tiles + i, 0)),
        pl.BlockSpec((tm, V7X_LANES), lambda i: (i, 0)),
    ]


SCHED_EXPERT, SCHED_VALID, SCHED_USED, SCHED_FIRST, SCHED_NEXT, SCHED_SLOT = range(6)
SCHED_ROWS = V7X_SUBLANES


def _plan_kernel(eidx_ref, rank_ref, counts_ref, pos_ref, sched_ref):
    i32 = jnp.int32
    shift = MOE_TILE.bit_length() - 1
    cnt = counts_ref[:, 0:1]
    padded = ((cnt + (MOE_TILE - 1)) >> shift) << shift
    seg = [padded[e:e + 1] for e in range(N_EXPERTS)]
    starts, ends = [], []
    run = jnp.zeros((1, 1), i32)
    for e in range(N_EXPERTS):
        starts.append(run)
        run = run + seg[e]
        ends.append(run)
    total = run
    nexts = [None] * N_EXPERTS
    nxt = jnp.full((1, 1), -1, i32)
    for e in reversed(range(N_EXPERTS)):
        nexts[e] = nxt
        nxt = jnp.where(seg[e] > 0, e, nxt)
    slots = []
    seen = jnp.zeros((1, 1), i32)
    for e in range(N_EXPERTS):
        slots.append(seen & 1)
        seen = seen + (seg[e] > 0).astype(i32)

    eidx = eidx_ref[...]
    pos = rank_ref[...]
    tile_row0 = lax.broadcasted_iota(i32, (1, V7X_LANES), 1) * MOE_TILE
    te = jnp.zeros((1, V7X_LANES), i32)
    for e in range(N_EXPERTS):
        pos = pos + jnp.where(eidx == e, starts[e], 0)
        te = te + (tile_row0 >= ends[e]).astype(i32)
    pos_ref[...] = pos
    te = jnp.minimum(te, N_EXPERTS - 1)

    def of_tile(per_expert):
        acc = jnp.zeros((1, V7X_LANES), i32)
        for e in range(N_EXPERTS):
            acc = acc + jnp.where(te == e, per_expert[e], 0)
        return acc

    valid_end = of_tile([starts[e] + cnt[e:e + 1] for e in range(N_EXPERTS)])
    used = tile_row0 < total
    rows = {
        SCHED_EXPERT: te,
        SCHED_VALID: jnp.clip(valid_end - tile_row0, 0, MOE_TILE),
        SCHED_USED: jnp.broadcast_to(total >> shift, (1, V7X_LANES)),
        SCHED_FIRST: (used & (of_tile(starts) == tile_row0)).astype(i32),
        SCHED_NEXT: of_tile(nexts),
        SCHED_SLOT: of_tile(slots),
    }
    for r in range(SCHED_ROWS):
        sched_ref[r:r + 1, :] = rows.get(r, jnp.zeros((1, V7X_LANES), i32))


def _sorted_positions(eidx, rank, counts):
    n_pairs = eidx.shape[0] * eidx.shape[1]
    n_rows = n_pairs + N_EXPERTS * MOE_TILE
    assert n_rows // MOE_TILE <= V7X_LANES
    pos, sched = pl.pallas_call(
        _plan_kernel,
        out_shape=[jax.ShapeDtypeStruct(eidx.shape, jnp.int32),
                   jax.ShapeDtypeStruct((SCHED_ROWS, V7X_LANES), jnp.int32)],
        name="moe_plan",
    )(eidx, rank, counts)
    return pos.reshape(n_pairs), sched, n_rows


def _sc_move_rows(scatter, src, pos_flat, n_out_rows, name):
    info = plsc.get_sparse_core_info()
    n_workers = info.num_cores * info.num_subcores
    n_pairs = pos_flat.shape[0]
    n_src = src.shape[0]
    per_worker = n_pairs // n_workers
    n_chunks = per_worker // SC_CHUNK
    assert per_worker * n_workers == n_pairs and n_chunks * SC_CHUNK == per_worker
    assert n_src % per_worker == 0
    idx = pos_flat.reshape(n_workers, n_chunks, SC_CHUNK)
    mesh = plsc.VectorSubcoreMesh(core_axis_name="core", subcore_axis_name="subcore")

    @functools.partial(
        pl.kernel,
        out_type=jax.ShapeDtypeStruct((n_out_rows, ROW_CHUNKS, V7X_LANES), U32),
        mesh=mesh,
        scratch_types=[
            pltpu.VMEM((n_chunks, SC_CHUNK), jnp.int32),
            pltpu.VMEM((2, SC_CHUNK, ROW_CHUNKS, V7X_LANES), U32),
            pltpu.SemaphoreType.DMA((2,)),
            pltpu.SemaphoreType.DMA((2,)),
        ],
        name=name)
    def move(src_hbm, i_hbm, o_hbm, idx_v, buf, in_sem, out_sem):
        wid = lax.axis_index("subcore") * info.num_cores + lax.axis_index("core")
        base = wid * per_worker
        src_base = lax.rem(base, n_src)
        pltpu.sync_copy(i_hbm.at[wid], idx_v)

        def fetch(s, slot):
            if scatter:
                rows = src_hbm.at[pl.ds(src_base + s * SC_CHUNK, SC_CHUNK)]
            else:
                rows = src_hbm.at[idx_v.at[s]]
            return pltpu.make_async_copy(rows, buf.at[slot], in_sem.at[slot])

        def flush(s, slot):
            if scatter:
                rows = o_hbm.at[idx_v.at[s]]
            else:
                rows = o_hbm.at[pl.ds(base + s * SC_CHUNK, SC_CHUNK)]
            return pltpu.make_async_copy(buf.at[slot], rows, out_sem.at[slot])

        fetch(0, 0).start()
        for s in range(n_chunks):
            slot = s % 2
            fetch(s, slot).wait()
            flush(s, slot).start()
            if s + 1 < n_chunks:
                if s >= 1:
                    flush(s - 1, 1 - slot).wait()
                fetch(s + 1, 1 - slot).start()
        flush(n_chunks - 2, n_chunks % 2).wait()
        flush(n_chunks - 1, (n_chunks - 1) % 2).wait()

    return move(src, idx)


def _experts_kernel(layer, sched_ref, x_ref, wg_hbm, wu_hbm, wd_hbm, y_ref,
                    wg_f32, wu_f32, wd_f32, wg_bf, wu_bf, wd_bf, sems):
    tm = MOE_TILE

    def weight_copies(expert, slot):
        pairs = ((wg_hbm, wg_f32), (wu_hbm, wu_f32), (wd_hbm, wd_f32))
        return [pltpu.make_async_copy(src.at[layer, expert], dst.at[slot], sems.at[slot, m])
                for m, (src, dst) in enumerate(pairs)]

    for part in range(MOE_STEP_TILES):
        j = pl.program_id(0) * MOE_STEP_TILES + part
        used = j < sched_ref[SCHED_USED, j]
        expert = sched_ref[SCHED_EXPERT, j]

        @pl.when(used & (sched_ref[SCHED_FIRST, j] == 1))
        def _():
            slot = sched_ref[SCHED_SLOT, j]
            next_expert = sched_ref[SCHED_NEXT, j]

            @pl.when(j == 0)
            def _():
                for cp in weight_copies(expert, slot):
                    cp.start()

            for cp in weight_copies(expert, slot):
                cp.wait()
            wg_bf[...] = wg_f32[slot].astype(BF16)
            wu_bf[...] = wu_f32[slot].astype(BF16)
            wd_bf[...] = wd_f32[slot].astype(BF16)

            @pl.when(next_expert >= 0)
            def _():
                for cp in weight_copies(next_expert, 1 - slot):
                    cp.start()

        @pl.when(used)
        def _():
            row = lax.broadcasted_iota(jnp.int32, (tm, 1), 0)
            words = jnp.where(row < sched_ref[SCHED_VALID, j],
                              _load_words(x_ref, tm, part * tm), jnp.uint32(0))
            h = _unpack_rows(words).astype(BF16)
            a = jnp.dot(h, wg_bf[...], preferred_element_type=F32)
            b = jnp.dot(h, wu_bf[...], preferred_element_type=F32)
            t = (_silu(a) * b).astype(BF16)
            y = jnp.dot(t, wd_bf[...], preferred_element_type=F32)
            _store_words(y_ref, _pack_rows(y), part * tm)

        @pl.when(jnp.logical_not(used))
        def _():
            lo = part * tm * ROW_CHUNKS
            y_ref[lo:lo + tm * ROW_CHUNKS, :] = jnp.zeros((tm * ROW_CHUNKS, V7X_LANES), U32)


def _experts_call(layer, rows, sched, w_gate, w_up, w_down):
    n_rows = rows.shape[0]
    tm = MOE_TILE * MOE_STEP_TILES
    d = D_MODEL
    assert n_rows % tm == 0
    rows2 = rows.reshape(n_rows * ROW_CHUNKS, V7X_LANES)

    def last_used_step(sc):
        return (sc[SCHED_USED, 0] + MOE_STEP_TILES - 1) // MOE_STEP_TILES - 1

    grid_spec = pltpu.PrefetchScalarGridSpec(
        num_scalar_prefetch=1,
        grid=(n_rows // tm,),
        in_specs=[
            pl.BlockSpec((tm * ROW_CHUNKS, V7X_LANES),
                         lambda j, sc: (jnp.minimum(j, last_used_step(sc)), 0)),
            pl.BlockSpec(memory_space=pl.ANY),
            pl.BlockSpec(memory_space=pl.ANY),
            pl.BlockSpec(memory_space=pl.ANY),
        ],
        out_specs=pl.BlockSpec((tm * ROW_CHUNKS, V7X_LANES), lambda j, sc: (j, 0)),
        scratch_shapes=[
            pltpu.VMEM((2, d, D_EXPERT), F32),
            pltpu.VMEM((2, d, D_EXPERT), F32),
            pltpu.VMEM((2, D_EXPERT, d), F32),
            pltpu.VMEM((d, D_EXPERT), BF16),
            pltpu.VMEM((d, D_EXPERT), BF16),
            pltpu.VMEM((D_EXPERT, d), BF16),
            pltpu.SemaphoreType.DMA((2, 3)),
        ],
    )
    y = pl.pallas_call(
        functools.partial(_experts_kernel, layer),
        grid_spec=grid_spec,
        out_shape=jax.ShapeDtypeStruct((n_rows * ROW_CHUNKS, V7X_LANES), U32),
        compiler_params=pltpu.CompilerParams(
            dimension_semantics=("arbitrary",),
            vmem_limit_bytes=V7X_VMEM_LIMIT_BYTES),
        name="experts_l%d" % layer,
    )(sched, rows2, w_gate, w_up, w_down)
    return y.reshape(n_rows, ROW_CHUNKS, V7X_LANES)


def _moe_rows(layer, mixer_outs, w_gate, w_up, w_down):
    plans = [_sorted_positions(eidx, rank, counts)
             for (_, _, eidx, _, rank, counts) in mixer_outs]
    sorted_rows = []
    for (_, h_rows, eidx, _, _, _), (pos_flat, _, n_rows) in zip(mixer_outs, plans):
        n_tok = eidx.shape[1]
        sorted_rows.append(_sc_move_rows(
            True, h_rows.reshape(n_tok, ROW_CHUNKS, V7X_LANES), pos_flat, n_rows, "sc_scatter_rows"))
    y_sorted = [_experts_call(layer, rows, sched, w_gate, w_up, w_down)
                for rows, (_, sched, _) in zip(sorted_rows, plans)]
    y_pairs = []
    for y, (pos_flat, _, _) in zip(y_sorted, plans):
        n_pairs = pos_flat.shape[0]
        moved = _sc_move_rows(False, y, pos_flat, n_pairs, "sc_gather_rows")
        y_pairs.append(moved.reshape(n_pairs * ROW_CHUNKS, V7X_LANES))
    return y_pairs


def _final_kernel(x_ref, y0_ref, y1_ref, wt_ref, modp_ref, fg_ref, *rest):
    o_ref = rest[-1]
    y = _moe_residual(x_ref, y0_ref, y1_ref, wt_ref, modp_ref[0, 0][5:6])
    ms = jnp.mean(y * y, axis=-1, keepdims=True)
    o_ref[...] = y * lax.rsqrt(ms + EPS) * fg_ref[...]


def _final_call(layer, batch, n_batch, x, y_pairs, wt, mod4, fg, out_prev):
    seq_len, d = x.shape
    tm = FINAL_TILE
    tiles_per_seq = seq_len // tm
    in_specs = _combine_specs(seq_len, tm) + [
        pl.BlockSpec((1, 1, 6, d), lambda i: (layer, batch, 0, 0)),
        _const_spec(fg.shape),
    ]
    args = [x, y_pairs, y_pairs, wt, mod4, fg]
    aliases = {}
    if out_prev is not None:
        in_specs.append(pl.BlockSpec(memory_space=pl.ANY))
        aliases = {len(args): 0}
        args.append(out_prev)
    return pl.pallas_call(
        _final_kernel,
        grid=(tiles_per_seq,),
        in_specs=in_specs,
        out_specs=pl.BlockSpec((tm, d), lambda i: (batch * tiles_per_seq + i, 0)),
        out_shape=jax.ShapeDtypeStruct((n_batch * seq_len, d), F32),
        input_output_aliases=aliases,
        compiler_params=pltpu.CompilerParams(
            dimension_semantics=("arbitrary",),
            vmem_limit_bytes=V7X_VMEM_LIMIT_BYTES),
        name="final_norm",
    )(*args)


def kernel(x, c, norm1_g, norm2_g, ada_w, ada_b, ab_w_in, pool_w, pool_scale, conf_conv_w, conf_conv_b, conf_ln_g, conf_ln_b, ab_w_out, cd_w_in, sconv_w, gmlp_ln_g, gmlp_ln_b, gmlp_ws, gmlp_bs, cd_w_out, router_w, router_bias, exp_w_gate, exp_w_up, exp_w_down, final_g):
    bsz, seq_len, d = x.shape
    n_tok = bsz * seq_len
    tm = MIX_TILE
    tiles_per_seq = seq_len // tm
    xf = x.reshape(n_tok, d)

    mod = _ada_mod(c, ada_w, ada_b)
    mod4 = mod.reshape(mod.shape[0], bsz, 6, d)

    rw_hi = router_w.astype(BF16)
    rw_lo = (router_w - rw_hi.astype(F32)).astype(BF16)
    rwt = jnp.concatenate([rw_hi.T, rw_lo.T], axis=0)
    rbias = router_bias.reshape(N_EXPERTS, 1)
    fg = final_g.reshape(1, d)

    weights_ab = [
        ab_w_in[0].astype(BF16), pool_w[0].astype(BF16), pool_scale[0].reshape(1, D_HALF),
        conf_conv_w[0], conf_conv_b[0].reshape(1, D_HALF), conf_ln_g[0].reshape(1, D_HALF),
        conf_ln_b[0].reshape(1, D_HALF), ab_w_out[0].astype(BF16),
    ]
    scratch_ab = [pltpu.VMEM((POOL_HIST + tm, D_HALF), F32),
                  pltpu.VMEM((CONV_HIST + tm, D_HALF), F32)]
    bsf = jnp.repeat(gmlp_bs[0].T, POOL_GROUP, axis=1)
    weights_cd = [
        cd_w_in[0].astype(BF16), sconv_w[0], gmlp_ln_g[0].reshape(1, D_HALF),
        gmlp_ln_b[0].reshape(1, D_HALF), gmlp_ws[0], bsf, cd_w_out[0].astype(BF16),
    ]
    scratch_cd = [pltpu.VMEM((SCONV_HIST + tm, D_HALF), F32)]
    experts = (exp_w_gate, exp_w_up, exp_w_down)

    batches = range(bsz)
    stage_ab = []
    for b in batches:
        x_spec = pl.BlockSpec((tm, d), lambda i, b=b: (b * tiles_per_seq + i, 0))
        stage_ab.append(_mixer_call(
            _mixer_ab_kernel, 0, b, [xf], [x_spec], mod4, norm1_g[0:1], norm2_g[0:1],
            weights_ab, rwt, rbias, scratch_ab, seq_len, "mixer_ab"))
    y_pairs0 = _moe_rows(0, stage_ab, *experts)

    stage_cd = []
    for b in batches:
        x1, _, _, wsel0, _, _ = stage_ab[b]
        prev_mod_spec = pl.BlockSpec((1, 1, 6, d), lambda i, b=b: (0, b, 0, 0))
        stage_cd.append(_mixer_call(
            _mixer_cd_kernel, 1, b, [x1, y_pairs0[b], y_pairs0[b], wsel0, mod4],
            _combine_specs(seq_len, tm) + [prev_mod_spec], mod4, norm1_g[1:2], norm2_g[1:2],
            weights_cd, rwt, rbias, scratch_cd, seq_len, "mixer_cd"))
    y_pairs1 = _moe_rows(1, stage_cd, *experts)

    out = None
    for b in batches:
        x3, _, _, wsel1, _, _ = stage_cd[b]
        out = _final_call(1, b, bsz, x3, y_pairs1[b], wsel1, mod4, fg, out)
    return out.reshape(bsz, seq_len, d)
```

```python
import functools

import jax
import jax.numpy as jnp
from jax import lax
from jax.experimental import pallas as pl
from jax.experimental.pallas import tpu as pltpu
from jax.experimental.pallas import tpu_sc as plsc

D_MODEL = 1024
EPS = 1e-6
POOL_WINDOWS = (2, 4, 8, 16)
POOL_GROUP = 128
D_HALF = 512
CONF_KERNEL = 31
SCONV_KERNEL = 3
CHUNK = 128
GMLP_HEADS = 4
N_EXPERTS = 16
N_GROUPS = 4
EXPERTS_PER_GROUP = 4
TOP_K = 2
D_EXPERT = 512

V7X_LANES = 128
V7X_SUBLANES = 8
V7X_VMEM_LIMIT_BYTES = 56 * 1024 * 1024

MIX_TILE = 512
FINAL_TILE = 1024
MOE_TILE = 512
MOE_STEP_TILES = 4
SC_CHUNK = 64
ROW_CHUNKS = D_MODEL // (2 * V7X_LANES)
CONV_HIST = 32
POOL_HIST = 16
SCONV_HIST = 8

BF16 = jnp.bfloat16
F32 = jnp.float32
U32 = jnp.uint32


def _rms_mod(x, g_row, shift_row, scale_row):
    ms = jnp.mean(x * x, axis=-1, keepdims=True)
    gain = g_row * (1.0 + scale_row)
    return (x * lax.rsqrt(ms + EPS)) * gain + shift_row


def _layer_norm(x, g_row, b_row):
    mu = jnp.mean(x, axis=-1, keepdims=True)
    xc = x - mu
    var = jnp.mean(xc * xc, axis=-1, keepdims=True)
    return xc * lax.rsqrt(var + EPS) * g_row + b_row


def _sigmoid(x):
    return 0.5 * jnp.tanh(0.5 * x) + 0.5


def _silu(x):
    return x * _sigmoid(x)


def _gelu_tanh(x):
    c = 0.7978845608028654
    return 0.5 * x * (1.0 + jnp.tanh(c * (x + 0.044715 * (x * x * x))))


def _shift_rows(x, r):
    n, c = x.shape
    if r == V7X_SUBLANES:
        return jnp.concatenate([x[:r], x[:n - r]], axis=0)
    g = x.reshape(n // V7X_SUBLANES, V7X_SUBLANES, c)
    rot = pltpu.roll(g, r, axis=1)
    prev = jnp.concatenate([rot[:1], rot[:-1]], axis=0)
    sub = lax.broadcasted_iota(jnp.int32, g.shape, 1)
    return jnp.where(sub < r, prev, rot).reshape(n, c)


def _load_words(ref, n_rows, row0=0):
    return jnp.concatenate(
        [ref[pl.ds(row0 * ROW_CHUNKS + c, n_rows, stride=ROW_CHUNKS), :]
         for c in range(ROW_CHUNKS)], axis=1)


def _store_words(ref, words, row0=0):
    n_rows = words.shape[0]
    for c in range(ROW_CHUNKS):
        ref[pl.ds(row0 * ROW_CHUNKS + c, n_rows, stride=ROW_CHUNKS), :] = (
            words[:, c * V7X_LANES:(c + 1) * V7X_LANES])


def _pack_rows(val):
    half = val.shape[1] // 2
    return pltpu.pack_elementwise([val[:, :half], val[:, half:]], packed_dtype=BF16)


def _unpack_rows(words):
    halves = [pltpu.unpack_elementwise(words, index=i, packed_dtype=BF16, unpacked_dtype=F32)
              for i in range(2)]
    return jnp.concatenate(halves, axis=1)


def _ada_kernel(ct_ref, w_ref, b_ref, o_ref):
    ct = ct_ref[...]
    cond = _silu(ct)
    w = w_ref[0]
    nb = ct.shape[1]
    for b in range(nb):
        col = cond[:, b:b + 1]
        o_ref[0, b:b + 1, :] = jnp.sum(col * w, axis=0, keepdims=True) + b_ref[0]


def _ada_mod(c, ada_w, ada_b):
    depth, d, six_d = ada_w.shape
    bsz = c.shape[0]
    nb = D_MODEL
    return pl.pallas_call(
        _ada_kernel,
        grid=(depth, six_d // nb),
        in_specs=[
            pl.BlockSpec((d, bsz), lambda l, j: (0, 0)),
            pl.BlockSpec((1, d, nb), lambda l, j: (l, 0, j)),
            pl.BlockSpec((1, 1, nb), lambda l, j: (l, 0, j)),
        ],
        out_specs=pl.BlockSpec((1, bsz, nb), lambda l, j: (l, 0, j)),
        out_shape=jax.ShapeDtypeStruct((depth, bsz, six_d), F32),
        compiler_params=pltpu.CompilerParams(
            dimension_semantics=("arbitrary", "arbitrary"),
            vmem_limit_bytes=V7X_VMEM_LIMIT_BYTES),
        name="ada_mod",
    )(c.T, ada_w, ada_b.reshape(depth, 1, six_d))


def _route(h2_bf, rwt_ref, rbias_ref, eidx_ref, wsel_ref, rank_ref, counts_ref, cnt_ref):
    nt = (((1,), (1,)), ((), ()))
    r = lax.dot_general(rwt_ref[...], h2_bf, nt, preferred_element_type=F32)
    logits = r[:N_EXPERTS] + r[N_EXPERTS:]
    m = jnp.max(logits, axis=0, keepdims=True)
    ex = jnp.exp(logits - m)
    probs = ex / jnp.sum(ex, axis=0, keepdims=True)
    sel = probs + rbias_ref[...]
    s = [sel[e:e + 1] for e in range(N_EXPERTS)]
    p = [probs[e:e + 1] for e in range(N_EXPERTS)]
    best = None
    gi = None
    for g in range(N_GROUPS):
        a, b, c, d = s[4 * g:4 * g + 4]
        hi1, lo1 = jnp.maximum(a, b), jnp.minimum(a, b)
        hi2, lo2 = jnp.maximum(c, d), jnp.minimum(c, d)
        top1 = jnp.maximum(hi1, hi2)
        top2 = jnp.maximum(jnp.minimum(hi1, hi2), jnp.maximum(lo1, lo2))
        score = top1 + top2
        if g == 0:
            best, gi = score, jnp.zeros(score.shape, jnp.int32)
        else:
            upd = score > best
            gi = jnp.where(upd, g, gi)
            best = jnp.where(upd, score, best)
    v, q = [], []
    for j in range(EXPERTS_PER_GROUP):
        vj, qj = s[j], p[j]
        for g in range(1, N_GROUPS):
            pick = gi == g
            vj = jnp.where(pick, s[4 * g + j], vj)
            qj = jnp.where(pick, p[4 * g + j], qj)
        v.append(vj)
        q.append(qj)
    i1 = jnp.zeros(gi.shape, jnp.int32)
    m1 = v[0]
    for j in range(1, EXPERTS_PER_GROUP):
        upd = v[j] > m1
        i1 = jnp.where(upd, j, i1)
        m1 = jnp.where(upd, v[j], m1)
    i2 = jnp.zeros(gi.shape, jnp.int32)
    m2 = jnp.full(m1.shape, -jnp.inf, F32)
    for j in range(EXPERTS_PER_GROUP):
        cand = (i1 != j) & (v[j] > m2)
        i2 = jnp.where(cand, j, i2)
        m2 = jnp.where(cand, v[j], m2)
    pa = q[0]
    pb = q[0]
    for j in range(1, EXPERTS_PER_GROUP):
        pa = jnp.where(i1 == j, q[j], pa)
        pb = jnp.where(i2 == j, q[j], pb)
    tot = pa + pb
    e0 = gi * EXPERTS_PER_GROUP + i1
    e1 = gi * EXPERTS_PER_GROUP + i2
    t = h2_bf.shape[0]
    eidx_ref[0:1, :] = e0
    eidx_ref[1:2, :] = e1
    w_rows = jnp.concatenate(
        [pa / tot, pb / tot, jnp.zeros((V7X_LANES - TOP_K, t), F32)], axis=0)
    wsel_ref[...] = w_rows.T

    e_iota = lax.broadcasted_iota(jnp.int32, (N_EXPERTS, t), 0)
    oh0 = e_iota == e0
    oh1 = e_iota == e1
    both = jnp.where(oh0 | oh1, 1.0, 0.0)
    r_i = lax.broadcasted_iota(jnp.int32, (V7X_LANES, V7X_LANES), 0)
    c_i = lax.broadcasted_iota(jnp.int32, (V7X_LANES, V7X_LANES), 1)
    before = jnp.where(r_i < c_i, 1.0, 0.0).astype(BF16)
    run = cnt_ref[...]
    rank0, rank1 = [], []
    for blk in range(t // V7X_LANES):
        lanes = slice(blk * V7X_LANES, (blk + 1) * V7X_LANES)
        b = both[:, lanes]
        pre = jnp.dot(b.astype(BF16), before, preferred_element_type=F32) + run
        rank0.append(jnp.sum(jnp.where(oh0[:, lanes], pre, 0.0), axis=0, keepdims=True))
        rank1.append(jnp.sum(jnp.where(oh1[:, lanes], pre, 0.0), axis=0, keepdims=True))
        run = run + jnp.sum(b, axis=1, keepdims=True)
    cnt_ref[...] = run
    rank_ref[0:1, :] = jnp.concatenate(rank0, axis=1).astype(jnp.int32)
    rank_ref[1:2, :] = jnp.concatenate(rank1, axis=1).astype(jnp.int32)
    counts_ref[...] = jnp.broadcast_to(run, counts_ref.shape).astype(jnp.int32)


def _finish_mixer(x, m, mod, n2g_ref, rwt_ref, rbias_ref,
                  x1_ref, h2_ref, eidx_ref, wsel_ref, rank_ref, counts_ref, cnt_ref):
    x1 = x + mod[2:3] * m
    x1_ref[...] = x1
    h2 = _rms_mod(x1, n2g_ref[...], mod[3:4], mod[4:5])
    h2_bf = h2.astype(BF16)
    _store_words(h2_ref, _pack_rows(h2))
    _route(h2_bf, rwt_ref, rbias_ref, eidx_ref, wsel_ref, rank_ref, counts_ref, cnt_ref)


def _moe_residual(x_ref, y0_ref, y1_ref, wt_ref, g2_row):
    tm = x_ref.shape[0]
    wt = wt_ref[...]
    y0 = _unpack_rows(_load_words(y0_ref, tm))
    y1 = _unpack_rows(_load_words(y1_ref, tm))
    y = wt[:, 0:1] * y0 + wt[:, 1:2] * y1
    return x_ref[...] + g2_row * y


def _mixer_ab_kernel(x_ref, mod_ref, n1g_ref, n2g_ref, win_ref, poolw_ref, pscale_ref,
                     convw_ref, convb_ref, lng_ref, lnb_ref, wout_ref, rwt_ref, rbias_ref,
                     x1_ref, h2_ref, eidx_ref, wsel_ref, rank_ref, counts_ref,
                     pool_ext, conv_ext, cnt_ref):
    seq_tile = pl.program_id(0)
    tm = x_ref.shape[0]

    @pl.when(seq_tile == 0)
    def _():
        pool_ext[0:POOL_HIST, :] = jnp.zeros((POOL_HIST, D_HALF), F32)
        conv_ext[0:CONV_HIST, :] = jnp.zeros((CONV_HIST, D_HALF), F32)
        cnt_ref[...] = jnp.zeros_like(cnt_ref)

    x = x_ref[...]
    mod = mod_ref[0, 0]
    h = _rms_mod(x, n1g_ref[...], mod[0:1], mod[1:2]).astype(BF16)
    z = jnp.dot(h, win_ref[...], preferred_element_type=F32)
    zp = z[:, :D_HALF]
    glu = z[:, D_HALF:2 * D_HALF] * _sigmoid(z[:, 2 * D_HALF:])
    pool_ext[POOL_HIST:POOL_HIST + tm, :] = zp
    conv_ext[CONV_HIST:CONV_HIST + tm, :] = glu

    row = lax.broadcasted_iota(jnp.int32, (tm, 1), 0)
    pos1 = (seq_tile * tm + row + 1).astype(F32)
    pool_out = []
    for g, w in enumerate(POOL_WINDOWS):
        cols = slice(g * POOL_GROUP, (g + 1) * POOL_GROUP)
        acc = pool_ext[:, cols]
        span = 1
        while span < w:
            acc = acc + _shift_rows(acc, span)
            span *= 2
        wsum = acc[POOL_HIST:POOL_HIST + tm]
        inv_cnt = 1.0 / jnp.minimum(pos1, float(w))
        diff = wsum * inv_cnt - zp[:, cols]
        po = jnp.dot(diff.astype(BF16), poolw_ref[g], preferred_element_type=F32)
        pool_out.append(po * pscale_ref[:, cols])

    convw = convw_ref[...]
    ext_rows = tm + V7X_SUBLANES
    conv = None
    for r in range(V7X_SUBLANES):
        vr = None
        for a in range(CONV_HIST // V7X_SUBLANES):
            lag = V7X_SUBLANES * a + r
            if lag >= CONF_KERNEL:
                continue
            k = CONF_KERNEL - 1 - lag
            start = CONV_HIST - V7X_SUBLANES - V7X_SUBLANES * a
            term = convw[k:k + 1, :] * conv_ext[start:start + ext_rows, :]
            vr = term if vr is None else vr + term
        if r:
            vr = _shift_rows(vr, r)
        conv = vr if conv is None else conv + vr
    conv = conv[V7X_SUBLANES:V7X_SUBLANES + tm] + convb_ref[...]
    conf = _silu(_layer_norm(conv, lng_ref[...], lnb_ref[...]))

    pool_ext[0:POOL_HIST, :] = zp[tm - POOL_HIST:tm]
    conv_ext[0:CONV_HIST, :] = glu[tm - CONV_HIST:tm]

    m = jnp.dot(conf.astype(BF16), wout_ref[D_HALF:, :], preferred_element_type=F32)
    for g in range(len(POOL_WINDOWS)):
        rows = slice(g * POOL_GROUP, (g + 1) * POOL_GROUP)
        m = m + jnp.dot(pool_out[g].astype(BF16), wout_ref[rows, :], preferred_element_type=F32)
    _finish_mixer(x, m, mod, n2g_ref, rwt_ref, rbias_ref,
                  x1_ref, h2_ref, eidx_ref, wsel_ref, rank_ref, counts_ref, cnt_ref)


def _mixer_cd_kernel(x_ref, y0_ref, y1_ref, wt_ref, modp_ref,
                     mod_ref, n1g_ref, n2g_ref, win_ref, sconvw_ref, lng_ref, lnb_ref,
                     ws_ref, bsf_ref, wout_ref, rwt_ref, rbias_ref,
                     x1_ref, h2_ref, eidx_ref, wsel_ref, rank_ref, counts_ref,
                     sconv_ext, cnt_ref):
    tm = x_ref.shape[0]

    @pl.when(pl.program_id(0) == 0)
    def _():
        sconv_ext[0:SCONV_HIST, :] = jnp.zeros((SCONV_HIST, D_HALF), F32)
        cnt_ref[...] = jnp.zeros_like(cnt_ref)

    x = _moe_residual(x_ref, y0_ref, y1_ref, wt_ref, modp_ref[0, 0][5:6])
    mod = mod_ref[0, 0]
    h = _rms_mod(x, n1g_ref[...], mod[0:1], mod[1:2]).astype(BF16)
    def proj(lo, hi):
        return jnp.dot(h, win_ref[:, lo:hi], preferred_element_type=F32)

    v = _layer_norm(_gelu_tanh(proj(4 * D_HALF, 5 * D_HALF)), lng_ref[...], lnb_ref[...])
    u = _gelu_tanh(proj(3 * D_HALF, 4 * D_HALF))
    ch = proj(D_HALF, 2 * D_HALF) * proj(2 * D_HALF, 3 * D_HALF)
    bg = proj(0, D_HALF)

    sconv_ext[SCONV_HIST:SCONV_HIST + tm, :] = ch
    sw = sconvw_ref[...]
    ext = sconv_ext[...]
    conv = sw[2:3, :] * ext
    conv = conv + sw[1:2, :] * _shift_rows(ext, 1)
    conv = conv + sw[0:1, :] * _shift_rows(ext, 2)
    sc_out = bg * conv[SCONV_HIST:SCONV_HIST + tm]
    sconv_ext[0:SCONV_HIST, :] = ch[tm - SCONV_HIST:tm]

    r_i = lax.broadcasted_iota(jnp.int32, (CHUNK, CHUNK), 0)
    c_i = lax.broadcasted_iota(jnp.int32, (CHUNK, CHUNK), 1)
    tril = c_i <= r_i
    wm = [jnp.where(tril, ws_ref[hd], 0.0).astype(BF16) for hd in range(GMLP_HEADS)]
    v_bf = v.astype(BF16)
    bsf = bsf_ref[...]
    gm_rows = []
    for n in range(tm // CHUNK):
        rows = slice(n * CHUNK, (n + 1) * CHUNK)
        heads = []
        for hd in range(GMLP_HEADS):
            cols = slice(hd * POOL_GROUP, (hd + 1) * POOL_GROUP)
            heads.append(jnp.dot(wm[hd], v_bf[rows, cols], preferred_element_type=F32))
        mixed = jnp.concatenate(heads, axis=1) + bsf
        gm_rows.append(u[rows] * mixed)
    gm_out = jnp.concatenate(gm_rows, axis=0)

    m = jnp.dot(sc_out.astype(BF16), wout_ref[:D_HALF, :], preferred_element_type=F32)
    m = m + jnp.dot(gm_out.astype(BF16), wout_ref[D_HALF:, :], preferred_element_type=F32)
    _finish_mixer(x, m, mod, n2g_ref, rwt_ref, rbias_ref,
                  x1_ref, h2_ref, eidx_ref, wsel_ref, rank_ref, counts_ref, cnt_ref)


def _const_spec(shape):
    nd = len(shape)
    return pl.BlockSpec(shape, lambda i: (0,) * nd)


def _mixer_call(kernel_fn, layer, batch, stream_inputs, stream_specs, mod4, n1g, n2g, weights,
                rwt, rbias, scratch, seq_len, name):
    n_tok = seq_len
    d = D_MODEL
    tm = MIX_TILE
    in_specs = stream_specs + [
        pl.BlockSpec((1, 1, 6, d), lambda i: (layer, batch, 0, 0)),
        _const_spec(n1g.shape),
        _const_spec(n2g.shape),
    ] + [_const_spec(w.shape) for w in weights] + [_const_spec(rwt.shape), _const_spec(rbias.shape)]
    out_specs = [
        pl.BlockSpec((tm, d), lambda i: (i, 0)),
        pl.BlockSpec((tm * ROW_CHUNKS, V7X_LANES), lambda i: (i, 0)),
        pl.BlockSpec((TOP_K, tm), lambda i: (0, i)),
        pl.BlockSpec((tm, V7X_LANES), lambda i: (i, 0)),
        pl.BlockSpec((TOP_K, tm), lambda i: (0, i)),
        pl.BlockSpec((N_EXPERTS, V7X_LANES), lambda i: (0, 0)),
    ]
    out_shape = [
        jax.ShapeDtypeStruct((n_tok, d), F32),
        jax.ShapeDtypeStruct((n_tok * ROW_CHUNKS, V7X_LANES), U32),
        jax.ShapeDtypeStruct((TOP_K, n_tok), jnp.int32),
        jax.ShapeDtypeStruct((n_tok, V7X_LANES), F32),
        jax.ShapeDtypeStruct((TOP_K, n_tok), jnp.int32),
        jax.ShapeDtypeStruct((N_EXPERTS, V7X_LANES), jnp.int32),
    ]
    return pl.pallas_call(
        kernel_fn,
        grid=(n_tok // tm,),
        in_specs=in_specs,
        out_specs=out_specs,
        out_shape=out_shape,
        scratch_shapes=scratch + [pltpu.VMEM((N_EXPERTS, 1), F32)],
        compiler_params=pltpu.CompilerParams(
            dimension_semantics=("arbitrary",),
            vmem_limit_bytes=V7X_VMEM_LIMIT_BYTES),
        name=name,
    )(*stream_inputs, mod4, n1g, n2g, *weights, rwt, rbias)


def _combine_specs(n_tok, tm):
    n_tiles = n_tok // tm
    return [
        pl.BlockSpec((tm, D_MODEL), lambda i: (i, 0)),
        pl.BlockSpec((tm * ROW_CHUNKS, V7X_LANES), lambda i: (i, 0)),
        pl.BlockSpec((tm * ROW_CHUNKS, V7X_LANES), lambda i: (n_tiles + i, 0)),
        pl.BlockSpec((tm, V7X_LANES), lambda i: (i, 0)),
    ]


SCHED_EXPERT, SCHED_VALID, SCHED_USED, SCHED_FIRST, SCHED_NEXT, SCHED_SLOT = range(6)
SCHED_ROWS = V7X_SUBLANES


def _plan_kernel(eidx_ref, rank_ref, counts_ref, pos_ref, sched_ref):
    i32 = jnp.int32
    shift = MOE_TILE.bit_length() - 1
    cnt = counts_ref[:, 0:1]
    padded = ((cnt + (MOE_TILE - 1)) >> shift) << shift
    seg = [padded[e:e + 1] for e in range(N_EXPERTS)]
    starts, ends = [], []
    run = jnp.zeros((1, 1), i32)
    for e in range(N_EXPERTS):
        starts.append(run)
        run = run + seg[e]
        ends.append(run)
    total = run
    nexts = [None] * N_EXPERTS
    nxt = jnp.full((1, 1), -1, i32)
    for e in reversed(range(N_EXPERTS)):
        nexts[e] = nxt
        nxt = jnp.where(seg[e] > 0, e, nxt)
    slots = []
    seen = jnp.zeros((1, 1), i32)
    for e in range(N_EXPERTS):
        slots.append(seen & 1)
        seen = seen + (seg[e] > 0).astype(i32)

    eidx = eidx_ref[...]
    pos = rank_ref[...]
    tile_row0 = lax.broadcasted_iota(i32, (1, V7X_LANES), 1) * MOE_TILE
    te = jnp.zeros((1, V7X_LANES), i32)
    for e in range(N_EXPERTS):
        pos = pos + jnp.where(eidx == e, starts[e], 0)
        te = te + (tile_row0 >= ends[e]).astype(i32)
    pos_ref[...] = pos
    te = jnp.minimum(te, N_EXPERTS - 1)

    def of_tile(per_expert):
        acc = jnp.zeros((1, V7X_LANES), i32)
        for e in range(N_EXPERTS):
            acc = acc + jnp.where(te == e, per_expert[e], 0)
        return acc

    valid_end = of_tile([starts[e] + cnt[e:e + 1] for e in range(N_EXPERTS)])
    used = tile_row0 < total
    rows = {
        SCHED_EXPERT: te,
        SCHED_VALID: jnp.clip(valid_end - tile_row0, 0, MOE_TILE),
        SCHED_USED: jnp.broadcast_to(total >> shift, (1, V7X_LANES)),
        SCHED_FIRST: (used & (of_tile(starts) == tile_row0)).astype(i32),
        SCHED_NEXT: of_tile(nexts),
        SCHED_SLOT: of_tile(slots),
    }
    for r in range(SCHED_ROWS):
        sched_ref[r:r + 1, :] = rows.get(r, jnp.zeros((1, V7X_LANES), i32))


def _sorted_positions(eidx, rank, counts):
    n_pairs = eidx.shape[0] * eidx.shape[1]
    n_rows = n_pairs + N_EXPERTS * MOE_TILE
    assert n_rows // MOE_TILE <= V7X_LANES
    pos, sched = pl.pallas_call(
        _plan_kernel,
        out_shape=[jax.ShapeDtypeStruct(eidx.shape, jnp.int32),
                   jax.ShapeDtypeStruct((SCHED_ROWS, V7X_LANES), jnp.int32)],
        name="moe_plan",
    )(eidx, rank, counts)
    return pos.reshape(n_pairs), sched, n_rows


def _sc_move_rows(scatter, src, pos_flat, n_out_rows, name):
    info = plsc.get_sparse_core_info()
    n_workers = info.num_cores * info.num_subcores
    n_pairs = pos_flat.shape[0]
    n_src = src.shape[0]
    per_worker = n_pairs // n_workers
    n_chunks = per_worker // SC_CHUNK
    assert per_worker * n_workers == n_pairs and n_chunks * SC_CHUNK == per_worker
    assert n_src % per_worker == 0
    idx = pos_flat.reshape(n_workers, n_chunks, SC_CHUNK)
    mesh = plsc.VectorSubcoreMesh(core_axis_name="core", subcore_axis_name="subcore")

    @functools.partial(
        pl.kernel,
        out_type=jax.ShapeDtypeStruct((n_out_rows, ROW_CHUNKS, V7X_LANES), U32),
        mesh=mesh,
        scratch_types=[
            pltpu.VMEM((n_chunks, SC_CHUNK), jnp.int32),
            pltpu.VMEM((2, SC_CHUNK, ROW_CHUNKS, V7X_LANES), U32),
            pltpu.SemaphoreType.DMA((2,)),
            pltpu.SemaphoreType.DMA((2,)),
        ],
        name=name)
    def move(src_hbm, i_hbm, o_hbm, idx_v, buf, in_sem, out_sem):
        wid = lax.axis_index("subcore") * info.num_cores + lax.axis_index("core")
        base = wid * per_worker
        src_base = lax.rem(base, n_src)
        pltpu.sync_copy(i_hbm.at[wid], idx_v)

        def fetch(s, slot):
            if scatter:
                rows = src_hbm.at[pl.ds(src_base + s * SC_CHUNK, SC_CHUNK)]
            else:
                rows = src_hbm.at[idx_v.at[s]]
            return pltpu.make_async_copy(rows, buf.at[slot], in_sem.at[slot])

        def flush(s, slot):
            if scatter:
                rows = o_hbm.at[idx_v.at[s]]
            else:
                rows = o_hbm.at[pl.ds(base + s * SC_CHUNK, SC_CHUNK)]
            return pltpu.make_async_copy(buf.at[slot], rows, out_sem.at[slot])

        fetch(0, 0).start()
        for s in range(n_chunks):
            slot = s % 2
            fetch(s, slot).wait()
            flush(s, slot).start()
            if s + 1 < n_chunks:
                if s >= 1:
                    flush(s - 1, 1 - slot).wait()
                fetch(s + 1, 1 - slot).start()
        flush(n_chunks - 2, n_chunks % 2).wait()
        flush(n_chunks - 1, (n_chunks - 1) % 2).wait()

    return move(src, idx)


def _experts_kernel(layer, sched_ref, x_ref, wg_hbm, wu_hbm, wd_hbm, y_ref,
                    wg_f32, wu_f32, wd_f32, wg_bf, wu_bf, wd_bf, sems):
    tm = MOE_TILE

    def weight_copies(expert, slot):
        pairs = ((wg_hbm, wg_f32), (wu_hbm, wu_f32), (wd_hbm, wd_f32))
        return [pltpu.make_async_copy(src.at[layer, expert], dst.at[slot], sems.at[slot, m])
                for m, (src, dst) in enumerate(pairs)]

    for part in range(MOE_STEP_TILES):
        j = pl.program_id(0) * MOE_STEP_TILES + part
        used = j < sched_ref[SCHED_USED, j]
        expert = sched_ref[SCHED_EXPERT, j]

        @pl.when(used & (sched_ref[SCHED_FIRST, j] == 1))
        def _():
            slot = sched_ref[SCHED_SLOT, j]
            next_expert = sched_ref[SCHED_NEXT, j]

            @pl.when(j == 0)
            def _():
                for cp in weight_copies(expert, slot):
                    cp.start()

            for cp in weight_copies(expert, slot):
                cp.wait()
            wg_bf[...] = wg_f32[slot].astype(BF16)
            wu_bf[...] = wu_f32[slot].astype(BF16)
            wd_bf[...] = wd_f32[slot].astype(BF16)

            @pl.when(next_expert >= 0)
            def _():
                for cp in weight_copies(next_expert, 1 - slot):
                    cp.start()

        @pl.when(used)
        def _():
            row = lax.broadcasted_iota(jnp.int32, (tm, 1), 0)
            words = jnp.where(row < sched_ref[SCHED_VALID, j],
                              _load_words(x_ref, tm, part * tm), jnp.uint32(0))
            h = _unpack_rows(words).astype(BF16)
            a = jnp.dot(h, wg_bf[...], preferred_element_type=F32)
            b = jnp.dot(h, wu_bf[...], preferred_element_type=F32)
            t = (_silu(a) * b).astype(BF16)
            y = jnp.dot(t, wd_bf[...], preferred_element_type=F32)
            _store_words(y_ref, _pack_rows(y), part * tm)

        @pl.when(jnp.logical_not(used))
        def _():
            lo = part * tm * ROW_CHUNKS
            y_ref[lo:lo + tm * ROW_CHUNKS, :] = jnp.zeros((tm * ROW_CHUNKS, V7X_LANES), U32)


def _experts_call(layer, rows, sched, w_gate, w_up, w_down):
    n_rows = rows.shape[0]
    tm = MOE_TILE * MOE_STEP_TILES
    d = D_MODEL
    assert n_rows % tm == 0
    rows2 = rows.reshape(n_rows * ROW_CHUNKS, V7X_LANES)

    def last_used_step(sc):
        return (sc[SCHED_USED, 0] + MOE_STEP_TILES - 1) // MOE_STEP_TILES - 1

    grid_spec = pltpu.PrefetchScalarGridSpec(
        num_scalar_prefetch=1,
        grid=(n_rows // tm,),
        in_specs=[
            pl.BlockSpec((tm * ROW_CHUNKS, V7X_LANES),
                         lambda j, sc: (jnp.minimum(j, last_used_step(sc)), 0)),
            pl.BlockSpec(memory_space=pl.ANY),
            pl.BlockSpec(memory_space=pl.ANY),
            pl.BlockSpec(memory_space=pl.ANY),
        ],
        out_specs=pl.BlockSpec((tm * ROW_CHUNKS, V7X_LANES), lambda j, sc: (j, 0)),
        scratch_shapes=[
            pltpu.VMEM((2, d, D_EXPERT), F32),
            pltpu.VMEM((2, d, D_EXPERT), F32),
            pltpu.VMEM((2, D_EXPERT, d), F32),
            pltpu.VMEM((d, D_EXPERT), BF16),
            pltpu.VMEM((d, D_EXPERT), BF16),
            pltpu.VMEM((D_EXPERT, d), BF16),
            pltpu.SemaphoreType.DMA((2, 3)),
        ],
    )
    y = pl.pallas_call(
        functools.partial(_experts_kernel, layer),
        grid_spec=grid_spec,
        out_shape=jax.ShapeDtypeStruct((n_rows * ROW_CHUNKS, V7X_LANES), U32),
        compiler_params=pltpu.CompilerParams(
            dimension_semantics=("arbitrary",),
            vmem_limit_bytes=V7X_VMEM_LIMIT_BYTES),
        name="experts_l%d" % layer,
    )(sched, rows2, w_gate, w_up, w_down)
    return y.reshape(n_rows, ROW_CHUNKS, V7X_LANES)


def _moe_rows(layer, mixer_outs, w_gate, w_up, w_down):
    plans = [_sorted_positions(eidx, rank, counts)
             for (_, _, eidx, _, rank, counts) in mixer_outs]
    sorted_rows = []
    for (_, h_rows, eidx, _, _, _), (pos_flat, _, n_rows) in zip(mixer_outs, plans):
        n_tok = eidx.shape[1]
        sorted_rows.append(_sc_move_rows(
            True, h_rows.reshape(n_tok, ROW_CHUNKS, V7X_LANES), pos_flat, n_rows, "sc_scatter_rows"))
    y_sorted = [_experts_call(layer, rows, sched, w_gate, w_up, w_down)
                for rows, (_, sched, _) in zip(sorted_rows, plans)]
    y_pairs = []
    for y, (pos_flat, _, _) in zip(y_sorted, plans):
        n_pairs = pos_flat.shape[0]
        moved = _sc_move_rows(False, y, pos_flat, n_pairs, "sc_gather_rows")
        y_pairs.append(moved.reshape(n_pairs * ROW_CHUNKS, V7X_LANES))
    return y_pairs


def _final_kernel(x_ref, y0_ref, y1_ref, wt_ref, modp_ref, fg_ref, *rest):
    o_ref = rest[-1]
    y = _moe_residual(x_ref, y0_ref, y1_ref, wt_ref, modp_ref[0, 0][5:6])
    ms = jnp.mean(y * y, axis=-1, keepdims=True)
    o_ref[...] = y * lax.rsqrt(ms + EPS) * fg_ref[...]


def _final_call(layer, batch, n_batch, x, y_pairs, wt, mod4, fg, out_prev):
    seq_len, d = x.shape
    tm = FINAL_TILE
    tiles_per_seq = seq_len // tm
    in_specs = _combine_specs(seq_len, tm) + [
        pl.BlockSpec((1, 1, 6, d), lambda i: (layer, batch, 0, 0)),
        _const_spec(fg.shape),
    ]
    args = [x, y_pairs, y_pairs, wt, mod4, fg]
    aliases = {}
    if out_prev is not None:
        in_specs.append(pl.BlockSpec(memory_space=pl.ANY))
        aliases = {len(args): 0}
        args.append(out_prev)
    return pl.pallas_call(
        _final_kernel,
        grid=(tiles_per_seq,),
        in_specs=in_specs,
        out_specs=pl.BlockSpec((tm, d), lambda i: (batch * tiles_per_seq + i, 0)),
        out_shape=jax.ShapeDtypeStruct((n_batch * seq_len, d), F32),
        input_output_aliases=aliases,
        compiler_params=pltpu.CompilerParams(
            dimension_semantics=("arbitrary",),
            vmem_limit_bytes=V7X_VMEM_LIMIT_BYTES),
        name="final_norm",
    )(*args)


def kernel(x, c, norm1_g, norm2_g, ada_w, ada_b, ab_w_in, pool_w, pool_scale, conf_conv_w, conf_conv_b, conf_ln_g, conf_ln_b, ab_w_out, cd_w_in, sconv_w, gmlp_ln_g, gmlp_ln_b, gmlp_ws, gmlp_bs, cd_w_out, router_w, router_bias, exp_w_gate, exp_w_up, exp_w_down, final_g):
    bsz, seq_len, d = x.shape
    n_tok = bsz * seq_len
    tm = MIX_TILE
    tiles_per_seq = seq_len // tm
    xf = x.reshape(n_tok, d)

    mod = _ada_mod(c, ada_w, ada_b)
    mod4 = mod.reshape(mod.shape[0], bsz, 6, d)

    rw_hi = router_w.astype(BF16)
    rw_lo = (router_w - rw_hi.astype(F32)).astype(BF16)
    rwt = jnp.concatenate([rw_hi.T, rw_lo.T], axis=0)
    rbias = router_bias.reshape(N_EXPERTS, 1)
    fg = final_g.reshape(1, d)

    weights_ab = [
        ab_w_in[0].astype(BF16), pool_w[0].astype(BF16), pool_scale[0].reshape(1, D_HALF),
        conf_conv_w[0], conf_conv_b[0].reshape(1, D_HALF), conf_ln_g[0].reshape(1, D_HALF),
        conf_ln_b[0].reshape(1, D_HALF), ab_w_out[0].astype(BF16),
    ]
    scratch_ab = [pltpu.VMEM((POOL_HIST + tm, D_HALF), F32),
                  pltpu.VMEM((CONV_HIST + tm, D_HALF), F32)]
    bsf = jnp.repeat(gmlp_bs[0].T, POOL_GROUP, axis=1)
    weights_cd = [
        cd_w_in[0].astype(BF16), sconv_w[0], gmlp_ln_g[0].reshape(1, D_HALF),
        gmlp_ln_b[0].reshape(1, D_HALF), gmlp_ws[0], bsf, cd_w_out[0].astype(BF16),
    ]
    scratch_cd = [pltpu.VMEM((SCONV_HIST + tm, D_HALF), F32)]
    experts = (exp_w_gate, exp_w_up, exp_w_down)

    batches = range(bsz)
    stage_ab = []
    for b in batches:
        x_spec = pl.BlockSpec((tm, d), lambda i, b=b: (b * tiles_per_seq + i, 0))
        stage_ab.append(_mixer_call(
            _mixer_ab_kernel, 0, b, [xf], [x_spec], mod4, norm1_g[0:1], norm2_g[0:1],
            weights_ab, rwt, rbias, scratch_ab, seq_len, "mixer_ab"))
    y_pairs0 = _moe_rows(0, stage_ab, *experts)

    stage_cd = []
    for b in batches:
        x1, _, _, wsel0, _, _ = stage_ab[b]
        prev_mod_spec = pl.BlockSpec((1, 1, 6, d), lambda i, b=b: (0, b, 0, 0))
        stage_cd.append(_mixer_call(
            _mixer_cd_kernel, 1, b, [x1, y_pairs0[b], y_pairs0[b], wsel0, mod4],
            _combine_specs(seq_len, tm) + [prev_mod_spec], mod4, norm1_g[1:2], norm2_g[1:2],
            weights_cd, rwt, rbias, scratch_cd, seq_len, "mixer_cd"))
    y_pairs1 = _moe_rows(1, stage_cd, *experts)

    out = None
    for b in batches:
        x3, _, _, wsel1, _, _ = stage_cd[b]
        out = _final_call(1, b, bsz, x3, y_pairs1[b], wsel1, mod4, fg, out)
    return out.reshape(bsz, seq_len, d)
```

```python
import functools

import jax
import jax.numpy as jnp
from jax import lax
from jax.experimental import pallas as pl
from jax.experimental.pallas import tpu as pltpu
from jax.experimental.pallas import tpu_sc as plsc

D_MODEL = 1024
EPS = 1e-6
POOL_WINDOWS = (2, 4, 8, 16)
POOL_GROUP = 128
D_HALF = 512
CONF_KERNEL = 31
SCONV_KERNEL = 3
CHUNK = 128
GMLP_HEADS = 4
N_EXPERTS = 16
N_GROUPS = 4
EXPERTS_PER_GROUP = 4
TOP_K = 2
D_EXPERT = 512

V7X_LANES = 128
V7X_SUBLANES = 8
V7X_VMEM_LIMIT_BYTES = 56 * 1024 * 1024

MIX_TILE = 1024
FINAL_TILE = 1024
MOE_TILE = 512
MOE_STEP_TILES = 4
SC_CHUNK = 64
ROW_CHUNKS = D_MODEL // (2 * V7X_LANES)
CONV_HIST = 32
POOL_HIST = 16
SCONV_HIST = 8

BF16 = jnp.bfloat16
F32 = jnp.float32
U32 = jnp.uint32


def _rms_mod(x, g_row, shift_row, scale_row):
    ms = jnp.mean(x * x, axis=-1, keepdims=True)
    gain = g_row * (1.0 + scale_row)
    return (x * lax.rsqrt(ms + EPS)) * gain + shift_row


def _layer_norm(x, g_row, b_row):
    mu = jnp.mean(x, axis=-1, keepdims=True)
    xc = x - mu
    var = jnp.mean(xc * xc, axis=-1, keepdims=True)
    return xc * lax.rsqrt(var + EPS) * g_row + b_row


def _sigmoid(x):
    return 0.5 * jnp.tanh(0.5 * x) + 0.5


def _silu(x):
    return x * _sigmoid(x)


def _gelu_tanh(x):
    c = 0.7978845608028654
    return 0.5 * x * (1.0 + jnp.tanh(c * (x + 0.044715 * (x * x * x))))


def _shift_rows(x, r):
    n, c = x.shape
    if r == V7X_SUBLANES:
        return jnp.concatenate([x[:r], x[:n - r]], axis=0)
    g = x.reshape(n // V7X_SUBLANES, V7X_SUBLANES, c)
    rot = pltpu.roll(g, r, axis=1)
    prev = jnp.concatenate([rot[:1], rot[:-1]], axis=0)
    sub = lax.broadcasted_iota(jnp.int32, g.shape, 1)
    return jnp.where(sub < r, prev, rot).reshape(n, c)


def _load_words(ref, n_rows, row0=0):
    return jnp.concatenate(
        [ref[pl.ds(row0 * ROW_CHUNKS + c, n_rows, stride=ROW_CHUNKS), :]
         for c in range(ROW_CHUNKS)], axis=1)


def _store_words(ref, words, row0=0):
    n_rows = words.shape[0]
    for c in range(ROW_CHUNKS):
        ref[pl.ds(row0 * ROW_CHUNKS + c, n_rows, stride=ROW_CHUNKS), :] = (
            words[:, c * V7X_LANES:(c + 1) * V7X_LANES])


def _pack_rows(val):
    half = val.shape[1] // 2
    return pltpu.pack_elementwise([val[:, :half], val[:, half:]], packed_dtype=BF16)


def _unpack_rows(words):
    halves = [pltpu.unpack_elementwise(words, index=i, packed_dtype=BF16, unpacked_dtype=F32)
              for i in range(2)]
    return jnp.concatenate(halves, axis=1)


def _ada_kernel(ct_ref, w_ref, b_ref, o_ref):
    ct = ct_ref[...]
    cond = _silu(ct)
    w = w_ref[0]
    nb = ct.shape[1]
    for b in range(nb):
        col = cond[:, b:b + 1]
        o_ref[0, b:b + 1, :] = jnp.sum(col * w, axis=0, keepdims=True) + b_ref[0]


def _ada_mod(c, ada_w, ada_b):
    depth, d, six_d = ada_w.shape
    bsz = c.shape[0]
    nb = D_MODEL
    return pl.pallas_call(
        _ada_kernel,
        grid=(depth, six_d // nb),
        in_specs=[
            pl.BlockSpec((d, bsz), lambda l, j: (0, 0)),
            pl.BlockSpec((1, d, nb), lambda l, j: (l, 0, j)),
            pl.BlockSpec((1, 1, nb), lambda l, j: (l, 0, j)),
        ],
        out_specs=pl.BlockSpec((1, bsz, nb), lambda l, j: (l, 0, j)),
        out_shape=jax.ShapeDtypeStruct((depth, bsz, six_d), F32),
        compiler_params=pltpu.CompilerParams(
            dimension_semantics=("arbitrary", "arbitrary"),
            vmem_limit_bytes=V7X_VMEM_LIMIT_BYTES),
        name="ada_mod",
    )(c.T, ada_w, ada_b.reshape(depth, 1, six_d))


def _route(h2_bf, rwt_ref, rbias_ref, eidx_ref, wsel_ref, rank_ref, counts_ref, cnt_ref):
    nt = (((1,), (1,)), ((), ()))
    r = lax.dot_general(rwt_ref[...], h2_bf, nt, preferred_element_type=F32)
    logits = r[:N_EXPERTS] + r[N_EXPERTS:]
    m = jnp.max(logits, axis=0, keepdims=True)
    ex = jnp.exp(logits - m)
    probs = ex / jnp.sum(ex, axis=0, keepdims=True)
    sel = probs + rbias_ref[...]
    s = [sel[e:e + 1] for e in range(N_EXPERTS)]
    p = [probs[e:e + 1] for e in range(N_EXPERTS)]
    best = None
    gi = None
    for g in range(N_GROUPS):
        a, b, c, d = s[4 * g:4 * g + 4]
        hi1, lo1 = jnp.maximum(a, b), jnp.minimum(a, b)
        hi2, lo2 = jnp.maximum(c, d), jnp.minimum(c, d)
        top1 = jnp.maximum(hi1, hi2)
        top2 = jnp.maximum(jnp.minimum(hi1, hi2), jnp.maximum(lo1, lo2))
        score = top1 + top2
        if g == 0:
            best, gi = score, jnp.zeros(score.shape, jnp.int32)
        else:
            upd = score > best
            gi = jnp.where(upd, g, gi)
            best = jnp.where(upd, score, best)
    v, q = [], []
    for j in range(EXPERTS_PER_GROUP):
        vj, qj = s[j], p[j]
        for g in range(1, N_GROUPS):
            pick = gi == g
            vj = jnp.where(pick, s[4 * g + j], vj)
            qj = jnp.where(pick, p[4 * g + j], qj)
        v.append(vj)
        q.append(qj)
    i1 = jnp.zeros(gi.shape, jnp.int32)
    m1 = v[0]
    for j in range(1, EXPERTS_PER_GROUP):
        upd = v[j] > m1
        i1 = jnp.where(upd, j, i1)
        m1 = jnp.where(upd, v[j], m1)
    i2 = jnp.zeros(gi.shape, jnp.int32)
    m2 = jnp.full(m1.shape, -jnp.inf, F32)
    for j in range(EXPERTS_PER_GROUP):
        cand = (i1 != j) & (v[j] > m2)
        i2 = jnp.where(cand, j, i2)
        m2 = jnp.where(cand, v[j], m2)
    pa = q[0]
    pb = q[0]
    for j in range(1, EXPERTS_PER_GROUP):
        pa = jnp.where(i1 == j, q[j], pa)
        pb = jnp.where(i2 == j, q[j], pb)
    tot = pa + pb
    e0 = gi * EXPERTS_PER_GROUP + i1
    e1 = gi * EXPERTS_PER_GROUP + i2
    t = h2_bf.shape[0]
    eidx_ref[0:1, :] = e0
    eidx_ref[1:2, :] = e1
    w_rows = jnp.concatenate(
        [pa / tot, pb / tot, jnp.zeros((V7X_LANES - TOP_K, t), F32)], axis=0)
    wsel_ref[...] = w_rows.T

    e_iota = lax.broadcasted_iota(jnp.int32, (N_EXPERTS, t), 0)
    oh0 = e_iota == e0
    oh1 = e_iota == e1
    both = jnp.where(oh0 | oh1, 1.0, 0.0)
    r_i = lax.broadcasted_iota(jnp.int32, (V7X_LANES, V7X_LANES), 0)
    c_i = lax.broadcasted_iota(jnp.int32, (V7X_LANES, V7X_LANES), 1)
    before = jnp.where(r_i < c_i, 1.0, 0.0).astype(BF16)
    run = cnt_ref[...]
    rank0, rank1 = [], []
    for blk in range(t // V7X_LANES):
        lanes = slice(blk * V7X_LANES, (blk + 1) * V7X_LANES)
        b = both[:, lanes]
        pre = jnp.dot(b.astype(BF16), before, preferred_element_type=F32) + run
        rank0.append(jnp.sum(jnp.where(oh0[:, lanes], pre, 0.0), axis=0, keepdims=True))
        rank1.append(jnp.sum(jnp.where(oh1[:, lanes], pre, 0.0), axis=0, keepdims=True))
        run = run + jnp.sum(b, axis=1, keepdims=True)
    cnt_ref[...] = run
    rank_ref[0:1, :] = jnp.concatenate(rank0, axis=1).astype(jnp.int32)
    rank_ref[1:2, :] = jnp.concatenate(rank1, axis=1).astype(jnp.int32)
    counts_ref[...] = jnp.broadcast_to(run, counts_ref.shape).astype(jnp.int32)


def _finish_mixer(x, m, mod, n2g_ref, rwt_ref, rbias_ref,
                  x1_ref, h2_ref, eidx_ref, wsel_ref, rank_ref, counts_ref, cnt_ref):
    x1 = x + mod[2:3] * m
    x1_ref[...] = x1
    h2 = _rms_mod(x1, n2g_ref[...], mod[3:4], mod[4:5])
    h2_bf = h2.astype(BF16)
    _store_words(h2_ref, _pack_rows(h2))
    _route(h2_bf, rwt_ref, rbias_ref, eidx_ref, wsel_ref, rank_ref, counts_ref, cnt_ref)


def _moe_residual(x_ref, y0_ref, y1_ref, wt_ref, g2_row):
    tm = x_ref.shape[0]
    wt = wt_ref[...]
    y0 = _unpack_rows(_load_words(y0_ref, tm))
    y1 = _unpack_rows(_load_words(y1_ref, tm))
    y = wt[:, 0:1] * y0 + wt[:, 1:2] * y1
    return x_ref[...] + g2_row * y


def _mixer_ab_kernel(x_ref, mod_ref, n1g_ref, n2g_ref, win_ref, poolw_ref, pscale_ref,
                     convw_ref, convb_ref, lng_ref, lnb_ref, wout_ref, rwt_ref, rbias_ref,
                     x1_ref, h2_ref, eidx_ref, wsel_ref, rank_ref, counts_ref,
                     pool_ext, conv_ext, cnt_ref):
    seq_tile = pl.program_id(0)
    tm = x_ref.shape[0]

    @pl.when(seq_tile == 0)
    def _():
        pool_ext[0:POOL_HIST, :] = jnp.zeros((POOL_HIST, D_HALF), F32)
        conv_ext[0:CONV_HIST, :] = jnp.zeros((CONV_HIST, D_HALF), F32)
        cnt_ref[...] = jnp.zeros_like(cnt_ref)

    x = x_ref[...]
    mod = mod_ref[0, 0]
    h = _rms_mod(x, n1g_ref[...], mod[0:1], mod[1:2]).astype(BF16)
    z = jnp.dot(h, win_ref[...], preferred_element_type=F32)
    zp = z[:, :D_HALF]
    glu = z[:, D_HALF:2 * D_HALF] * _sigmoid(z[:, 2 * D_HALF:])
    pool_ext[POOL_HIST:POOL_HIST + tm, :] = zp
    conv_ext[CONV_HIST:CONV_HIST + tm, :] = glu

    row = lax.broadcasted_iota(jnp.int32, (tm, 1), 0)
    pos1 = (seq_tile * tm + row + 1).astype(F32)
    pool_out = []
    for g, w in enumerate(POOL_WINDOWS):
        cols = slice(g * POOL_GROUP, (g + 1) * POOL_GROUP)
        acc = pool_ext[:, cols]
        span = 1
        while span < w:
            acc = acc + _shift_rows(acc, span)
            span *= 2
        wsum = acc[POOL_HIST:POOL_HIST + tm]
        inv_cnt = 1.0 / jnp.minimum(pos1, float(w))
        diff = wsum * inv_cnt - zp[:, cols]
        po = jnp.dot(diff.astype(BF16), poolw_ref[g], preferred_element_type=F32)
        pool_out.append(po * pscale_ref[:, cols])

    convw = convw_ref[...]
    ext_rows = tm + V7X_SUBLANES
    conv = None
    for r in range(V7X_SUBLANES):
        vr = None
        for a in range(CONV_HIST // V7X_SUBLANES):
            lag = V7X_SUBLANES * a + r
            if lag >= CONF_KERNEL:
                continue
            k = CONF_KERNEL - 1 - lag
            start = CONV_HIST - V7X_SUBLANES - V7X_SUBLANES * a
            term = convw[k:k + 1, :] * conv_ext[start:start + ext_rows, :]
            vr = term if vr is None else vr + term
        if r:
            vr = _shift_rows(vr, r)
        conv = vr if conv is None else conv + vr
    conv = conv[V7X_SUBLANES:V7X_SUBLANES + tm] + convb_ref[...]
    conf = _silu(_layer_norm(conv, lng_ref[...], lnb_ref[...]))

    pool_ext[0:POOL_HIST, :] = zp[tm - POOL_HIST:tm]
    conv_ext[0:CONV_HIST, :] = glu[tm - CONV_HIST:tm]

    m = jnp.dot(conf.astype(BF16), wout_ref[D_HALF:, :], preferred_element_type=F32)
    for g in range(len(POOL_WINDOWS)):
        rows = slice(g * POOL_GROUP, (g + 1) * POOL_GROUP)
        m = m + jnp.dot(pool_out[g].astype(BF16), wout_ref[rows, :], preferred_element_type=F32)
    _finish_mixer(x, m, mod, n2g_ref, rwt_ref, rbias_ref,
                  x1_ref, h2_ref, eidx_ref, wsel_ref, rank_ref, counts_ref, cnt_ref)


def _mixer_cd_kernel(x_ref, y0_ref, y1_ref, wt_ref, modp_ref,
                     mod_ref, n1g_ref, n2g_ref, win_ref, sconvw_ref, lng_ref, lnb_ref,
                     ws_ref, bsf_ref, wout_ref, rwt_ref, rbias_ref,
                     x1_ref, h2_ref, eidx_ref, wsel_ref, rank_ref, counts_ref,
                     sconv_ext, cnt_ref):
    tm = x_ref.shape[0]

    @pl.when(pl.program_id(0) == 0)
    def _():
        sconv_ext[0:SCONV_HIST, :] = jnp.zeros((SCONV_HIST, D_HALF), F32)
        cnt_ref[...] = jnp.zeros_like(cnt_ref)

    x = _moe_residual(x_ref, y0_ref, y1_ref, wt_ref, modp_ref[0, 0][5:6])
    mod = mod_ref[0, 0]
    h = _rms_mod(x, n1g_ref[...], mod[0:1], mod[1:2]).astype(BF16)
    def proj(lo, hi):
        return jnp.dot(h, win_ref[:, lo:hi], preferred_element_type=F32)

    v = _layer_norm(_gelu_tanh(proj(4 * D_HALF, 5 * D_HALF)), lng_ref[...], lnb_ref[...])
    u = _gelu_tanh(proj(3 * D_HALF, 4 * D_HALF))
    ch = proj(D_HALF, 2 * D_HALF) * proj(2 * D_HALF, 3 * D_HALF)
    bg = proj(0, D_HALF)

    sconv_ext[SCONV_HIST:SCONV_HIST + tm, :] = ch
    sw = sconvw_ref[...]
    ext = sconv_ext[...]
    conv = sw[2:3, :] * ext
    conv = conv + sw[1:2, :] * _shift_rows(ext, 1)
    conv = conv + sw[0:1, :] * _shift_rows(ext, 2)
    sc_out = bg * conv[SCONV_HIST:SCONV_HIST + tm]
    sconv_ext[0:SCONV_HIST, :] = ch[tm - SCONV_HIST:tm]

    r_i = lax.broadcasted_iota(jnp.int32, (CHUNK, CHUNK), 0)
    c_i = lax.broadcasted_iota(jnp.int32, (CHUNK, CHUNK), 1)
    tril = c_i <= r_i
    wm = [jnp.where(tril, ws_ref[hd], 0.0).astype(BF16) for hd in range(GMLP_HEADS)]
    v_bf = v.astype(BF16)
    bsf = bsf_ref[...]
    gm_rows = []
    for n in range(tm // CHUNK):
        rows = slice(n * CHUNK, (n + 1) * CHUNK)
        heads = []
        for hd in range(GMLP_HEADS):
            cols = slice(hd * POOL_GROUP, (hd + 1) * POOL_GROUP)
            heads.append(jnp.dot(wm[hd], v_bf[rows, cols], preferred_element_type=F32))
        mixed = jnp.concatenate(heads, axis=1) + bsf
        gm_rows.append(u[rows] * mixed)
    gm_out = jnp.concatenate(gm_rows, axis=0)

    m = jnp.dot(sc_out.astype(BF16), wout_ref[:D_HALF, :], preferred_element_type=F32)
    m = m + jnp.dot(gm_out.astype(BF16), wout_ref[D_HALF:, :], preferred_element_type=F32)
    _finish_mixer(x, m, mod, n2g_ref, rwt_ref, rbias_ref,
                  x1_ref, h2_ref, eidx_ref, wsel_ref, rank_ref, counts_ref, cnt_ref)


def _const_spec(shape):
    nd = len(shape)
    return pl.BlockSpec(shape, lambda i: (0,) * nd)


def _mixer_call(kernel_fn, layer, batch, stream_inputs, stream_specs, mod4, n1g, n2g, weights,
                rwt, rbias, scratch, seq_len, name):
    n_tok = seq_len
    d = D_MODEL
    tm = MIX_TILE
    in_specs = stream_specs + [
        pl.BlockSpec((1, 1, 6, d), lambda i: (layer, batch, 0, 0)),
        _const_spec(n1g.shape),
        _const_spec(n2g.shape),
    ] + [_const_spec(w.shape) for w in weights] + [_const_spec(rwt.shape), _const_spec(rbias.shape)]
    out_specs = [
        pl.BlockSpec((tm, d), lambda i: (i, 0)),
        pl.BlockSpec((tm * ROW_CHUNKS, V7X_LANES), lambda i: (i, 0)),
        pl.BlockSpec((TOP_K, tm), lambda i: (0, i)),
        pl.BlockSpec((tm, V7X_LANES), lambda i: (i, 0)),
        pl.BlockSpec((TOP_K, tm), lambda i: (0, i)),
        pl.BlockSpec((N_EXPERTS, V7X_LANES), lambda i: (0, 0)),
    ]
    out_shape = [
        jax.ShapeDtypeStruct((n_tok, d), F32),
        jax.ShapeDtypeStruct((n_tok * ROW_CHUNKS, V7X_LANES), U32),
        jax.ShapeDtypeStruct((TOP_K, n_tok), jnp.int32),
        jax.ShapeDtypeStruct((n_tok, V7X_LANES), F32),
        jax.ShapeDtypeStruct((TOP_K, n_tok), jnp.int32),
        jax.ShapeDtypeStruct((N_EXPERTS, V7X_LANES), jnp.int32),
    ]
    return pl.pallas_call(
        kernel_fn,
        grid=(n_tok // tm,),
        in_specs=in_specs,
        out_specs=out_specs,
        out_shape=out_shape,
        scratch_shapes=scratch + [pltpu.VMEM((N_EXPERTS, 1), F32)],
        compiler_params=pltpu.CompilerParams(
            dimension_semantics=("arbitrary",),
            vmem_limit_bytes=V7X_VMEM_LIMIT_BYTES),
        name=name,
    )(*stream_inputs, mod4, n1g, n2g, *weights, rwt, rbias)


def _combine_specs(n_tok, tm):
    n_tiles = n_tok // tm
    return [
        pl.BlockSpec((tm, D_MODEL), lambda i: (i, 0)),
        pl.BlockSpec((tm * ROW_CHUNKS, V7X_LANES), lambda i: (i, 0)),
        pl.BlockSpec((tm * ROW_CHUNKS, V7X_LANES), lambda i: (n_tiles + i, 0)),
        pl.BlockSpec((tm, V7X_LANES), lambda i: (i, 0)),
    ]


SCHED_EXPERT, SCHED_VALID, SCHED_USED, SCHED_FIRST, SCHED_NEXT, SCHED_SLOT = range(6)
SCHED_ROWS = V7X_SUBLANES


def _plan_kernel(eidx_ref, rank_ref, counts_ref, pos_ref, sched_ref):
    i32 = jnp.int32
    shift = MOE_TILE.bit_length() - 1
    cnt = counts_ref[:, 0:1]
    padded = ((cnt + (MOE_TILE - 1)) >> shift) << shift
    seg = [padded[e:e + 1] for e in range(N_EXPERTS)]
    starts, ends = [], []
    run = jnp.zeros((1, 1), i32)
    for e in range(N_EXPERTS):
        starts.append(run)
        run = run + seg[e]
        ends.append(run)
    total = run
    nexts = [None] * N_EXPERTS
    nxt = jnp.full((1, 1), -1, i32)
    for e in reversed(range(N_EXPERTS)):
        nexts[e] = nxt
        nxt = jnp.where(seg[e] > 0, e, nxt)
    slots = []
    seen = jnp.zeros((1, 1), i32)
    for e in range(N_EXPERTS):
        slots.append(seen & 1)
        seen = seen + (seg[e] > 0).astype(i32)

    eidx = eidx_ref[...]
    pos = rank_ref[...]
    tile_row0 = lax.broadcasted_iota(i32, (1, V7X_LANES), 1) * MOE_TILE
    te = jnp.zeros((1, V7X_LANES), i32)
    for e in range(N_EXPERTS):
        pos = pos + jnp.where(eidx == e, starts[e], 0)
        te = te + (tile_row0 >= ends[e]).astype(i32)
    pos_ref[...] = pos
    te = jnp.minimum(te, N_EXPERTS - 1)

    def of_tile(per_expert):
        acc = jnp.zeros((1, V7X_LANES), i32)
        for e in range(N_EXPERTS):
            acc = acc + jnp.where(te == e, per_expert[e], 0)
        return acc

    valid_end = of_tile([starts[e] + cnt[e:e + 1] for e in range(N_EXPERTS)])
    used = tile_row0 < total
    rows = {
        SCHED_EXPERT: te,
        SCHED_VALID: jnp.clip(valid_end - tile_row0, 0, MOE_TILE),
        SCHED_USED: jnp.broadcast_to(total >> shift, (1, V7X_LANES)),
        SCHED_FIRST: (used & (of_tile(starts) == tile_row0)).astype(i32),
        SCHED_NEXT: of_tile(nexts),
        SCHED_SLOT: of_tile(slots),
    }
    for r in range(SCHED_ROWS):
        sched_ref[r:r + 1, :] = rows.get(r, jnp.zeros((1, V7X_LANES), i32))


def _sorted_positions(eidx, rank, counts):
    n_pairs = eidx.shape[0] * eidx.shape[1]
    n_rows = n_pairs + N_EXPERTS * MOE_TILE
    assert n_rows // MOE_TILE <= V7X_LANES
    pos, sched = pl.pallas_call(
        _plan_kernel,
        out_shape=[jax.ShapeDtypeStruct(eidx.shape, jnp.int32),
                   jax.ShapeDtypeStruct((SCHED_ROWS, V7X_LANES), jnp.int32)],
        name="moe_plan",
    )(eidx, rank, counts)
    return pos.reshape(n_pairs), sched, n_rows


def _sc_move_rows(scatter, src, pos_flat, n_out_rows, name):
    info = plsc.get_sparse_core_info()
    n_workers = info.num_cores * info.num_subcores
    n_pairs = pos_flat.shape[0]
    n_src = src.shape[0]
    per_worker = n_pairs // n_workers
    n_chunks = per_worker // SC_CHUNK
    assert per_worker * n_workers == n_pairs and n_chunks * SC_CHUNK == per_worker
    assert n_src % per_worker == 0
    idx = pos_flat.reshape(n_workers, n_chunks, SC_CHUNK)
    mesh = plsc.VectorSubcoreMesh(core_axis_name="core", subcore_axis_name="subcore")

    @functools.partial(
        pl.kernel,
        out_type=jax.ShapeDtypeStruct((n_out_rows, ROW_CHUNKS, V7X_LANES), U32),
        mesh=mesh,
        scratch_types=[
            pltpu.VMEM((n_chunks, SC_CHUNK), jnp.int32),
            pltpu.VMEM((2, SC_CHUNK, ROW_CHUNKS, V7X_LANES), U32),
            pltpu.SemaphoreType.DMA((2,)),
            pltpu.SemaphoreType.DMA((2,)),
        ],
        name=name)
    def move(src_hbm, i_hbm, o_hbm, idx_v, buf, in_sem, out_sem):
        wid = lax.axis_index("subcore") * info.num_cores + lax.axis_index("core")
        base = wid * per_worker
        src_base = lax.rem(base, n_src)
        pltpu.sync_copy(i_hbm.at[wid], idx_v)

        def fetch(s, slot):
            if scatter:
                rows = src_hbm.at[pl.ds(src_base + s * SC_CHUNK, SC_CHUNK)]
            else:
                rows = src_hbm.at[idx_v.at[s]]
            return pltpu.make_async_copy(rows, buf.at[slot], in_sem.at[slot])

        def flush(s, slot):
            if scatter:
                rows = o_hbm.at[idx_v.at[s]]
            else:
                rows = o_hbm.at[pl.ds(base + s * SC_CHUNK, SC_CHUNK)]
            return pltpu.make_async_copy(buf.at[slot], rows, out_sem.at[slot])

        fetch(0, 0).start()
        for s in range(n_chunks):
            slot = s % 2
            fetch(s, slot).wait()
            flush(s, slot).start()
            if s + 1 < n_chunks:
                if s >= 1:
                    flush(s - 1, 1 - slot).wait()
                fetch(s + 1, 1 - slot).start()
        flush(n_chunks - 2, n_chunks % 2).wait()
        flush(n_chunks - 1, (n_chunks - 1) % 2).wait()

    return move(src, idx)


def _experts_kernel(layer, sched_ref, x_ref, wg_hbm, wu_hbm, wd_hbm, y_ref,
                    wg_f32, wu_f32, wd_f32, wg_bf, wu_bf, wd_bf, sems):
    tm = MOE_TILE

    def weight_copies(expert, slot):
        pairs = ((wg_hbm, wg_f32), (wu_hbm, wu_f32), (wd_hbm, wd_f32))
        return [pltpu.make_async_copy(src.at[layer, expert], dst.at[slot], sems.at[slot, m])
                for m, (src, dst) in enumerate(pairs)]

    for part in range(MOE_STEP_TILES):
        j = pl.program_id(0) * MOE_STEP_TILES + part
        used = j < sched_ref[SCHED_USED, j]
        expert = sched_ref[SCHED_EXPERT, j]

        @pl.when(used & (sched_ref[SCHED_FIRST, j] == 1))
        def _():
            slot = sched_ref[SCHED_SLOT, j]
            next_expert = sched_ref[SCHED_NEXT, j]

            @pl.when(j == 0)
            def _():
                for cp in weight_copies(expert, slot):
                    cp.start()

            for cp in weight_copies(expert, slot):
                cp.wait()
            wg_bf[...] = wg_f32[slot].astype(BF16)
            wu_bf[...] = wu_f32[slot].astype(BF16)
            wd_bf[...] = wd_f32[slot].astype(BF16)

            @pl.when(next_expert >= 0)
            def _():
                for cp in weight_copies(next_expert, 1 - slot):
                    cp.start()

        @pl.when(used)
        def _():
            row = lax.broadcasted_iota(jnp.int32, (tm, 1), 0)
            words = jnp.where(row < sched_ref[SCHED_VALID, j],
                              _load_words(x_ref, tm, part * tm), jnp.uint32(0))
            h = _unpack_rows(words).astype(BF16)
            a = jnp.dot(h, wg_bf[...], preferred_element_type=F32)
            b = jnp.dot(h, wu_bf[...], preferred_element_type=F32)
            t = (_silu(a) * b).astype(BF16)
            y = jnp.dot(t, wd_bf[...], preferred_element_type=F32)
            _store_words(y_ref, _pack_rows(y), part * tm)

        @pl.when(jnp.logical_not(used))
        def _():
            lo = part * tm * ROW_CHUNKS
            y_ref[lo:lo + tm * ROW_CHUNKS, :] = jnp.zeros((tm * ROW_CHUNKS, V7X_LANES), U32)


def _experts_call(layer, rows, sched, w_gate, w_up, w_down):
    n_rows = rows.shape[0]
    tm = MOE_TILE * MOE_STEP_TILES
    d = D_MODEL
    assert n_rows % tm == 0
    rows2 = rows.reshape(n_rows * ROW_CHUNKS, V7X_LANES)

    def last_used_step(sc):
        return (sc[SCHED_USED, 0] + MOE_STEP_TILES - 1) // MOE_STEP_TILES - 1

    grid_spec = pltpu.PrefetchScalarGridSpec(
        num_scalar_prefetch=1,
        grid=(n_rows // tm,),
        in_specs=[
            pl.BlockSpec((tm * ROW_CHUNKS, V7X_LANES),
                         lambda j, sc: (jnp.minimum(j, last_used_step(sc)), 0)),
            pl.BlockSpec(memory_space=pl.ANY),
            pl.BlockSpec(memory_space=pl.ANY),
            pl.BlockSpec(memory_space=pl.ANY),
        ],
        out_specs=pl.BlockSpec((tm * ROW_CHUNKS, V7X_LANES), lambda j, sc: (j, 0)),
        scratch_shapes=[
            pltpu.VMEM((2, d, D_EXPERT), F32),
            pltpu.VMEM((2, d, D_EXPERT), F32),
            pltpu.VMEM((2, D_EXPERT, d), F32),
            pltpu.VMEM((d, D_EXPERT), BF16),
            pltpu.VMEM((d, D_EXPERT), BF16),
            pltpu.VMEM((D_EXPERT, d), BF16),
            pltpu.SemaphoreType.DMA((2, 3)),
        ],
    )
    y = pl.pallas_call(
        functools.partial(_experts_kernel, layer),
        grid_spec=grid_spec,
        out_shape=jax.ShapeDtypeStruct((n_rows * ROW_CHUNKS, V7X_LANES), U32),
        compiler_params=pltpu.CompilerParams(
            dimension_semantics=("arbitrary",),
            vmem_limit_bytes=V7X_VMEM_LIMIT_BYTES),
        name="experts_l%d" % layer,
    )(sched, rows2, w_gate, w_up, w_down)
    return y.reshape(n_rows, ROW_CHUNKS, V7X_LANES)


def _moe_rows(layer, mixer_outs, w_gate, w_up, w_down):
    plans = [_sorted_positions(eidx, rank, counts)
             for (_, _, eidx, _, rank, counts) in mixer_outs]
    sorted_rows = []
    for (_, h_rows, eidx, _, _, _), (pos_flat, _, n_rows) in zip(mixer_outs, plans):
        n_tok = eidx.shape[1]
        sorted_rows.append(_sc_move_rows(
            True, h_rows.reshape(n_tok, ROW_CHUNKS, V7X_LANES), pos_flat, n_rows, "sc_scatter_rows"))
    y_sorted = [_experts_call(layer, rows, sched, w_gate, w_up, w_down)
                for rows, (_, sched, _) in zip(sorted_rows, plans)]
    y_pairs = []
    for y, (pos_flat, _, _) in zip(y_sorted, plans):
        n_pairs = pos_flat.shape[0]
        moved = _sc_move_rows(False, y, pos_flat, n_pairs, "sc_gather_rows")
        y_pairs.append(moved.reshape(n_pairs * ROW_CHUNKS, V7X_LANES))
    return y_pairs


def _final_kernel(x_ref, y0_ref, y1_ref, wt_ref, modp_ref, fg_ref, *rest):
    o_ref = rest[-1]
    y = _moe_residual(x_ref, y0_ref, y1_ref, wt_ref, modp_ref[0, 0][5:6])
    ms = jnp.mean(y * y, axis=-1, keepdims=True)
    o_ref[...] = y * lax.rsqrt(ms + EPS) * fg_ref[...]


def _final_call(layer, batch, n_batch, x, y_pairs, wt, mod4, fg, out_prev):
    seq_len, d = x.shape
    tm = FINAL_TILE
    tiles_per_seq = seq_len // tm
    in_specs = _combine_specs(seq_len, tm) + [
        pl.BlockSpec((1, 1, 6, d), lambda i: (layer, batch, 0, 0)),
        _const_spec(fg.shape),
    ]
    args = [x, y_pairs, y_pairs, wt, mod4, fg]
    aliases = {}
    if out_prev is not None:
        in_specs.append(pl.BlockSpec(memory_space=pl.ANY))
        aliases = {len(args): 0}
        args.append(out_prev)
    return pl.pallas_call(
        _final_kernel,
        grid=(tiles_per_seq,),
        in_specs=in_specs,
        out_specs=pl.BlockSpec((tm, d), lambda i: (batch * tiles_per_seq + i, 0)),
        out_shape=jax.ShapeDtypeStruct((n_batch * seq_len, d), F32),
        input_output_aliases=aliases,
        compiler_params=pltpu.CompilerParams(
            dimension_semantics=("arbitrary",),
            vmem_limit_bytes=V7X_VMEM_LIMIT_BYTES),
        name="final_norm",
    )(*args)


def kernel(x, c, norm1_g, norm2_g, ada_w, ada_b, ab_w_in, pool_w, pool_scale, conf_conv_w, conf_conv_b, conf_ln_g, conf_ln_b, ab_w_out, cd_w_in, sconv_w, gmlp_ln_g, gmlp_ln_b, gmlp_ws, gmlp_bs, cd_w_out, router_w, router_bias, exp_w_gate, exp_w_up, exp_w_down, final_g):
    bsz, seq_len, d = x.shape
    n_tok = bsz * seq_len
    tm = MIX_TILE
    tiles_per_seq = seq_len // tm
    xf = x.reshape(n_tok, d)

    mod = _ada_mod(c, ada_w, ada_b)
    mod4 = mod.reshape(mod.shape[0], bsz, 6, d)

    rw_hi = router_w.astype(BF16)
    rw_lo = (router_w - rw_hi.astype(F32)).astype(BF16)
    rwt = jnp.concatenate([rw_hi.T, rw_lo.T], axis=0)
    rbias = router_bias.reshape(N_EXPERTS, 1)
    fg = final_g.reshape(1, d)

    weights_ab = [
        ab_w_in[0].astype(BF16), pool_w[0].astype(BF16), pool_scale[0].reshape(1, D_HALF),
        conf_conv_w[0], conf_conv_b[0].reshape(1, D_HALF), conf_ln_g[0].reshape(1, D_HALF),
        conf_ln_b[0].reshape(1, D_HALF), ab_w_out[0].astype(BF16),
    ]
    scratch_ab = [pltpu.VMEM((POOL_HIST + tm, D_HALF), F32),
                  pltpu.VMEM((CONV_HIST + tm, D_HALF), F32)]
    bsf = jnp.repeat(gmlp_bs[0].T, POOL_GROUP, axis=1)
    weights_cd = [
        cd_w_in[0].astype(BF16), sconv_w[0], gmlp_ln_g[0].reshape(1, D_HALF),
        gmlp_ln_b[0].reshape(1, D_HALF), gmlp_ws[0], bsf, cd_w_out[0].astype(BF16),
    ]
    scratch_cd = [pltpu.VMEM((SCONV_HIST + tm, D_HALF), F32)]
    experts = (exp_w_gate, exp_w_up, exp_w_down)

    batches = range(bsz)
    stage_ab = []
    for b in batches:
        x_spec = pl.BlockSpec((tm, d), lambda i, b=b: (b * tiles_per_seq + i, 0))
        stage_ab.append(_mixer_call(
            _mixer_ab_kernel, 0, b, [xf], [x_spec], mod4, norm1_g[0:1], norm2_g[0:1],
            weights_ab, rwt, rbias, scratch_ab, seq_len, "mixer_ab"))
    y_pairs0 = _moe_rows(0, stage_ab, *experts)

    stage_cd = []
    for b in batches:
        x1, _, _, wsel0, _, _ = stage_ab[b]
        prev_mod_spec = pl.BlockSpec((1, 1, 6, d), lambda i, b=b: (0, b, 0, 0))
        stage_cd.append(_mixer_call(
            _mixer_cd_kernel, 1, b, [x1, y_pairs0[b], y_pairs0[b], wsel0, mod4],
            _combine_specs(seq_len, tm) + [prev_mod_spec], mod4, norm1_g[1:2], norm2_g[1:2],
            weights_cd, rwt, rbias, scratch_cd, seq_len, "mixer_cd"))
    y_pairs1 = _moe_rows(1, stage_cd, *experts)

    out = None
    for b in batches:
        x3, _, _, wsel1, _, _ = stage_cd[b]
        out = _final_call(1, b, bsz, x3, y_pairs1[b], wsel1, mod4, fg, out)
    return out.reshape(bsz, seq_len, d)
```

```python
import functools

import jax
import jax.numpy as jnp
from jax import lax
from jax.experimental import pallas as pl
from jax.experimental.pallas import tpu as pltpu
from jax.experimental.pallas import tpu_sc as plsc

D_MODEL = 1024
EPS = 1e-6
POOL_WINDOWS = (2, 4, 8, 16)
POOL_GROUP = 128
D_HALF = 512
CONF_KERNEL = 31
SCONV_KERNEL = 3
CHUNK = 128
GMLP_HEADS = 4
N_EXPERTS = 16
N_GROUPS = 4
EXPERTS_PER_GROUP = 4
TOP_K = 2
D_EXPERT = 512

V7X_LANES = 128
V7X_SUBLANES = 8
V7X_VMEM_LIMIT_BYTES = 56 * 1024 * 1024

MIX_TILE = 1024
FINAL_TILE = 1024
MOE_TILE = 512
MOE_STEP_TILES = 8
SC_CHUNK = 64
ROW_CHUNKS = D_MODEL // (2 * V7X_LANES)
CONV_HIST = 32
POOL_HIST = 16
SCONV_HIST = 8

BF16 = jnp.bfloat16
F32 = jnp.float32
U32 = jnp.uint32


def _rms_mod(x, g_row, shift_row, scale_row):
    ms = jnp.mean(x * x, axis=-1, keepdims=True)
    gain = g_row * (1.0 + scale_row)
    return (x * lax.rsqrt(ms + EPS)) * gain + shift_row


def _layer_norm(x, g_row, b_row):
    mu = jnp.mean(x, axis=-1, keepdims=True)
    xc = x - mu
    var = jnp.mean(xc * xc, axis=-1, keepdims=True)
    return xc * lax.rsqrt(var + EPS) * g_row + b_row


def _sigmoid(x):
    return 0.5 * jnp.tanh(0.5 * x) + 0.5


def _silu(x):
    return x * _sigmoid(x)


def _gelu_tanh(x):
    c = 0.7978845608028654
    return 0.5 * x * (1.0 + jnp.tanh(c * (x + 0.044715 * (x * x * x))))


def _shift_rows(x, r):
    n, c = x.shape
    if r == V7X_SUBLANES:
        return jnp.concatenate([x[:r], x[:n - r]], axis=0)
    g = x.reshape(n // V7X_SUBLANES, V7X_SUBLANES, c)
    rot = pltpu.roll(g, r, axis=1)
    prev = jnp.concatenate([rot[:1], rot[:-1]], axis=0)
    sub = lax.broadcasted_iota(jnp.int32, g.shape, 1)
    return jnp.where(sub < r, prev, rot).reshape(n, c)


def _load_words(ref, n_rows, row0=0):
    return jnp.concatenate(
        [ref[pl.ds(row0 * ROW_CHUNKS + c, n_rows, stride=ROW_CHUNKS), :]
         for c in range(ROW_CHUNKS)], axis=1)


def _store_words(ref, words, row0=0):
    n_rows = words.shape[0]
    for c in range(ROW_CHUNKS):
        ref[pl.ds(row0 * ROW_CHUNKS + c, n_rows, stride=ROW_CHUNKS), :] = (
            words[:, c * V7X_LANES:(c + 1) * V7X_LANES])


def _pack_rows(val):
    half = val.shape[1] // 2
    return pltpu.pack_elementwise([val[:, :half], val[:, half:]], packed_dtype=BF16)


def _unpack_rows(words):
    halves = [pltpu.unpack_elementwise(words, index=i, packed_dtype=BF16, unpacked_dtype=F32)
              for i in range(2)]
    return jnp.concatenate(halves, axis=1)


def _ada_kernel(ct_ref, w_ref, b_ref, o_ref):
    ct = ct_ref[...]
    cond = _silu(ct)
    w = w_ref[0]
    nb = ct.shape[1]
    for b in range(nb):
        col = cond[:, b:b + 1]
        o_ref[0, b:b + 1, :] = jnp.sum(col * w, axis=0, keepdims=True) + b_ref[0]


def _ada_mod(c, ada_w, ada_b):
    depth, d, six_d = ada_w.shape
    bsz = c.shape[0]
    nb = D_MODEL
    return pl.pallas_call(
        _ada_kernel,
        grid=(depth, six_d // nb),
        in_specs=[
            pl.BlockSpec((d, bsz), lambda l, j: (0, 0)),
            pl.BlockSpec((1, d, nb), lambda l, j: (l, 0, j)),
            pl.BlockSpec((1, 1, nb), lambda l, j: (l, 0, j)),
        ],
        out_specs=pl.BlockSpec((1, bsz, nb), lambda l, j: (l, 0, j)),
        out_shape=jax.ShapeDtypeStruct((depth, bsz, six_d), F32),
        compiler_params=pltpu.CompilerParams(
            dimension_semantics=("arbitrary", "arbitrary"),
            vmem_limit_bytes=V7X_VMEM_LIMIT_BYTES),
        name="ada_mod",
    )(c.T, ada_w, ada_b.reshape(depth, 1, six_d))


def _route(h2_bf, rwt_ref, rbias_ref, eidx_ref, wsel_ref, rank_ref, counts_ref, cnt_ref):
    nt = (((1,), (1,)), ((), ()))
    r = lax.dot_general(rwt_ref[...], h2_bf, nt, preferred_element_type=F32)
    logits = r[:N_EXPERTS] + r[N_EXPERTS:]
    m = jnp.max(logits, axis=0, keepdims=True)
    ex = jnp.exp(logits - m)
    probs = ex / jnp.sum(ex, axis=0, keepdims=True)
    sel = probs + rbias_ref[...]
    s = [sel[e:e + 1] for e in range(N_EXPERTS)]
    p = [probs[e:e + 1] for e in range(N_EXPERTS)]
    best = None
    gi = None
    for g in range(N_GROUPS):
        a, b, c, d = s[4 * g:4 * g + 4]
        hi1, lo1 = jnp.maximum(a, b), jnp.minimum(a, b)
        hi2, lo2 = jnp.maximum(c, d), jnp.minimum(c, d)
        top1 = jnp.maximum(hi1, hi2)
        top2 = jnp.maximum(jnp.minimum(hi1, hi2), jnp.maximum(lo1, lo2))
        score = top1 + top2
        if g == 0:
            best, gi = score, jnp.zeros(score.shape, jnp.int32)
        else:
            upd = score > best
            gi = jnp.where(upd, g, gi)
            best = jnp.where(upd, score, best)
    v, q = [], []
    for j in range(EXPERTS_PER_GROUP):
        vj, qj = s[j], p[j]
        for g in range(1, N_GROUPS):
            pick = gi == g
            vj = jnp.where(pick, s[4 * g + j], vj)
            qj = jnp.where(pick, p[4 * g + j], qj)
        v.append(vj)
        q.append(qj)
    i1 = jnp.zeros(gi.shape, jnp.int32)
    m1 = v[0]
    for j in range(1, EXPERTS_PER_GROUP):
        upd = v[j] > m1
        i1 = jnp.where(upd, j, i1)
        m1 = jnp.where(upd, v[j], m1)
    i2 = jnp.zeros(gi.shape, jnp.int32)
    m2 = jnp.full(m1.shape, -jnp.inf, F32)
    for j in range(EXPERTS_PER_GROUP):
        cand = (i1 != j) & (v[j] > m2)
        i2 = jnp.where(cand, j, i2)
        m2 = jnp.where(cand, v[j], m2)
    pa = q[0]
    pb = q[0]
    for j in range(1, EXPERTS_PER_GROUP):
        pa = jnp.where(i1 == j, q[j], pa)
        pb = jnp.where(i2 == j, q[j], pb)
    tot = pa + pb
    e0 = gi * EXPERTS_PER_GROUP + i1
    e1 = gi * EXPERTS_PER_GROUP + i2
    t = h2_bf.shape[0]
    eidx_ref[0:1, :] = e0
    eidx_ref[1:2, :] = e1
    w_rows = jnp.concatenate(
        [pa / tot, pb / tot, jnp.zeros((V7X_LANES - TOP_K, t), F32)], axis=0)
    wsel_ref[...] = w_rows.T

    e_iota = lax.broadcasted_iota(jnp.int32, (N_EXPERTS, t), 0)
    oh0 = e_iota == e0
    oh1 = e_iota == e1
    both = jnp.where(oh0 | oh1, 1.0, 0.0)
    r_i = lax.broadcasted_iota(jnp.int32, (V7X_LANES, V7X_LANES), 0)
    c_i = lax.broadcasted_iota(jnp.int32, (V7X_LANES, V7X_LANES), 1)
    before = jnp.where(r_i < c_i, 1.0, 0.0).astype(BF16)
    run = cnt_ref[...]
    rank0, rank1 = [], []
    for blk in range(t // V7X_LANES):
        lanes = slice(blk * V7X_LANES, (blk + 1) * V7X_LANES)
        b = both[:, lanes]
        pre = jnp.dot(b.astype(BF16), before, preferred_element_type=F32) + run
        rank0.append(jnp.sum(jnp.where(oh0[:, lanes], pre, 0.0), axis=0, keepdims=True))
        rank1.append(jnp.sum(jnp.where(oh1[:, lanes], pre, 0.0), axis=0, keepdims=True))
        run = run + jnp.sum(b, axis=1, keepdims=True)
    cnt_ref[...] = run
    rank_ref[0:1, :] = jnp.concatenate(rank0, axis=1).astype(jnp.int32)
    rank_ref[1:2, :] = jnp.concatenate(rank1, axis=1).astype(jnp.int32)
    counts_ref[...] = jnp.broadcast_to(run, counts_ref.shape).astype(jnp.int32)


def _finish_mixer(x, m, mod, n2g_ref, rwt_ref, rbias_ref,
                  x1_ref, h2_ref, eidx_ref, wsel_ref, rank_ref, counts_ref, cnt_ref):
    x1 = x + mod[2:3] * m
    x1_ref[...] = x1
    h2 = _rms_mod(x1, n2g_ref[...], mod[3:4], mod[4:5])
    h2_bf = h2.astype(BF16)
    _store_words(h2_ref, _pack_rows(h2))
    _route(h2_bf, rwt_ref, rbias_ref, eidx_ref, wsel_ref, rank_ref, counts_ref, cnt_ref)


def _moe_residual(x_ref, y0_ref, y1_ref, wt_ref, g2_row):
    tm = x_ref.shape[0]
    wt = wt_ref[...]
    y0 = _unpack_rows(_load_words(y0_ref, tm))
    y1 = _unpack_rows(_load_words(y1_ref, tm))
    y = wt[:, 0:1] * y0 + wt[:, 1:2] * y1
    return x_ref[...] + g2_row * y


def _mixer_ab_kernel(x_ref, mod_ref, n1g_ref, n2g_ref, win_ref, poolw_ref, pscale_ref,
                     convw_ref, convb_ref, lng_ref, lnb_ref, wout_ref, rwt_ref, rbias_ref,
                     x1_ref, h2_ref, eidx_ref, wsel_ref, rank_ref, counts_ref,
                     pool_ext, conv_ext, cnt_ref):
    seq_tile = pl.program_id(0)
    tm = x_ref.shape[0]

    @pl.when(seq_tile == 0)
    def _():
        pool_ext[0:POOL_HIST, :] = jnp.zeros((POOL_HIST, D_HALF), F32)
        conv_ext[0:CONV_HIST, :] = jnp.zeros((CONV_HIST, D_HALF), F32)
        cnt_ref[...] = jnp.zeros_like(cnt_ref)

    x = x_ref[...]
    mod = mod_ref[0, 0]
    h = _rms_mod(x, n1g_ref[...], mod[0:1], mod[1:2]).astype(BF16)
    z = jnp.dot(h, win_ref[...], preferred_element_type=F32)
    zp = z[:, :D_HALF]
    glu = z[:, D_HALF:2 * D_HALF] * _sigmoid(z[:, 2 * D_HALF:])
    pool_ext[POOL_HIST:POOL_HIST + tm, :] = zp
    conv_ext[CONV_HIST:CONV_HIST + tm, :] = glu

    row = lax.broadcasted_iota(jnp.int32, (tm, 1), 0)
    pos1 = (seq_tile * tm + row + 1).astype(F32)
    pool_out = []
    for g, w in enumerate(POOL_WINDOWS):
        cols = slice(g * POOL_GROUP, (g + 1) * POOL_GROUP)
        acc = pool_ext[:, cols]
        span = 1
        while span < w:
            acc = acc + _shift_rows(acc, span)
            span *= 2
        wsum = acc[POOL_HIST:POOL_HIST + tm]
        inv_cnt = 1.0 / jnp.minimum(pos1, float(w))
        diff = wsum * inv_cnt - zp[:, cols]
        po = jnp.dot(diff.astype(BF16), poolw_ref[g], preferred_element_type=F32)
        pool_out.append(po * pscale_ref[:, cols])

    convw = convw_ref[...]
    ext_rows = tm + V7X_SUBLANES
    conv = None
    for r in range(V7X_SUBLANES):
        vr = None
        for a in range(CONV_HIST // V7X_SUBLANES):
            lag = V7X_SUBLANES * a + r
            if lag >= CONF_KERNEL:
                continue
            k = CONF_KERNEL - 1 - lag
            start = CONV_HIST - V7X_SUBLANES - V7X_SUBLANES * a
            term = convw[k:k + 1, :] * conv_ext[start:start + ext_rows, :]
            vr = term if vr is None else vr + term
        if r:
            vr = _shift_rows(vr, r)
        conv = vr if conv is None else conv + vr
    conv = conv[V7X_SUBLANES:V7X_SUBLANES + tm] + convb_ref[...]
    conf = _silu(_layer_norm(conv, lng_ref[...], lnb_ref[...]))

    pool_ext[0:POOL_HIST, :] = zp[tm - POOL_HIST:tm]
    conv_ext[0:CONV_HIST, :] = glu[tm - CONV_HIST:tm]

    m = jnp.dot(conf.astype(BF16), wout_ref[D_HALF:, :], preferred_element_type=F32)
    for g in range(len(POOL_WINDOWS)):
        rows = slice(g * POOL_GROUP, (g + 1) * POOL_GROUP)
        m = m + jnp.dot(pool_out[g].astype(BF16), wout_ref[rows, :], preferred_element_type=F32)
    _finish_mixer(x, m, mod, n2g_ref, rwt_ref, rbias_ref,
                  x1_ref, h2_ref, eidx_ref, wsel_ref, rank_ref, counts_ref, cnt_ref)


def _mixer_cd_kernel(x_ref, y0_ref, y1_ref, wt_ref, modp_ref,
                     mod_ref, n1g_ref, n2g_ref, win_ref, sconvw_ref, lng_ref, lnb_ref,
                     ws_ref, bsf_ref, wout_ref, rwt_ref, rbias_ref,
                     x1_ref, h2_ref, eidx_ref, wsel_ref, rank_ref, counts_ref,
                     sconv_ext, cnt_ref):
    tm = x_ref.shape[0]

    @pl.when(pl.program_id(0) == 0)
    def _():
        sconv_ext[0:SCONV_HIST, :] = jnp.zeros((SCONV_HIST, D_HALF), F32)
        cnt_ref[...] = jnp.zeros_like(cnt_ref)

    x = _moe_residual(x_ref, y0_ref, y1_ref, wt_ref, modp_ref[0, 0][5:6])
    mod = mod_ref[0, 0]
    h = _rms_mod(x, n1g_ref[...], mod[0:1], mod[1:2]).astype(BF16)
    def proj(lo, hi):
        return jnp.dot(h, win_ref[:, lo:hi], preferred_element_type=F32)

    v = _layer_norm(_gelu_tanh(proj(4 * D_HALF, 5 * D_HALF)), lng_ref[...], lnb_ref[...])
    u = _gelu_tanh(proj(3 * D_HALF, 4 * D_HALF))
    ch = proj(D_HALF, 2 * D_HALF) * proj(2 * D_HALF, 3 * D_HALF)
    bg = proj(0, D_HALF)

    sconv_ext[SCONV_HIST:SCONV_HIST + tm, :] = ch
    sw = sconvw_ref[...]
    ext = sconv_ext[...]
    conv = sw[2:3, :] * ext
    conv = conv + sw[1:2, :] * _shift_rows(ext, 1)
    conv = conv + sw[0:1, :] * _shift_rows(ext, 2)
    sc_out = bg * conv[SCONV_HIST:SCONV_HIST + tm]
    sconv_ext[0:SCONV_HIST, :] = ch[tm - SCONV_HIST:tm]

    r_i = lax.broadcasted_iota(jnp.int32, (CHUNK, CHUNK), 0)
    c_i = lax.broadcasted_iota(jnp.int32, (CHUNK, CHUNK), 1)
    tril = c_i <= r_i
    wm = [jnp.where(tril, ws_ref[hd], 0.0).astype(BF16) for hd in range(GMLP_HEADS)]
    v_bf = v.astype(BF16)
    bsf = bsf_ref[...]
    gm_rows = []
    for n in range(tm // CHUNK):
        rows = slice(n * CHUNK, (n + 1) * CHUNK)
        heads = []
        for hd in range(GMLP_HEADS):
            cols = slice(hd * POOL_GROUP, (hd + 1) * POOL_GROUP)
            heads.append(jnp.dot(wm[hd], v_bf[rows, cols], preferred_element_type=F32))
        mixed = jnp.concatenate(heads, axis=1) + bsf
        gm_rows.append(u[rows] * mixed)
    gm_out = jnp.concatenate(gm_rows, axis=0)

    m = jnp.dot(sc_out.astype(BF16), wout_ref[:D_HALF, :], preferred_element_type=F32)
    m = m + jnp.dot(gm_out.astype(BF16), wout_ref[D_HALF:, :], preferred_element_type=F32)
    _finish_mixer(x, m, mod, n2g_ref, rwt_ref, rbias_ref,
                  x1_ref, h2_ref, eidx_ref, wsel_ref, rank_ref, counts_ref, cnt_ref)


def _const_spec(shape):
    nd = len(shape)
    return pl.BlockSpec(shape, lambda i: (0,) * nd)


def _mixer_call(kernel_fn, layer, batch, stream_inputs, stream_specs, mod4, n1g, n2g, weights,
                rwt, rbias, scratch, seq_len, name):
    n_tok = seq_len
    d = D_MODEL
    tm = MIX_TILE
    in_specs = stream_specs + [
        pl.BlockSpec((1, 1, 6, d), lambda i: (layer, batch, 0, 0)),
        _const_spec(n1g.shape),
        _const_spec(n2g.shape),
    ] + [_const_spec(w.shape) for w in weights] + [_const_spec(rwt.shape), _const_spec(rbias.shape)]
    out_specs = [
        pl.BlockSpec((tm, d), lambda i: (i, 0)),
        pl.BlockSpec((tm * ROW_CHUNKS, V7X_LANES), lambda i: (i, 0)),
        pl.BlockSpec((TOP_K, tm), lambda i: (0, i)),
        pl.BlockSpec((tm, V7X_LANES), lambda i: (i, 0)),
        pl.BlockSpec((TOP_K, tm), lambda i: (0, i)),
        pl.BlockSpec((N_EXPERTS, V7X_LANES), lambda i: (0, 0)),
    ]
    out_shape = [
        jax.ShapeDtypeStruct((n_tok, d), F32),
        jax.ShapeDtypeStruct((n_tok * ROW_CHUNKS, V7X_LANES), U32),
        jax.ShapeDtypeStruct((TOP_K, n_tok), jnp.int32),
        jax.ShapeDtypeStruct((n_tok, V7X_LANES), F32),
        jax.ShapeDtypeStruct((TOP_K, n_tok), jnp.int32),
        jax.ShapeDtypeStruct((N_EXPERTS, V7X_LANES), jnp.int32),
    ]
    return pl.pallas_call(
        kernel_fn,
        grid=(n_tok // tm,),
        in_specs=in_specs,
        out_specs=out_specs,
        out_shape=out_shape,
        scratch_shapes=scratch + [pltpu.VMEM((N_EXPERTS, 1), F32)],
        compiler_params=pltpu.CompilerParams(
            dimension_semantics=("arbitrary",),
            vmem_limit_bytes=V7X_VMEM_LIMIT_BYTES),
        name=name,
    )(*stream_inputs, mod4, n1g, n2g, *weights, rwt, rbias)


def _combine_specs(n_tok, tm):
    n_tiles = n_tok // tm
    return [
        pl.BlockSpec((tm, D_MODEL), lambda i: (i, 0)),
        pl.BlockSpec((tm * ROW_CHUNKS, V7X_LANES), lambda i: (i, 0)),
        pl.BlockSpec((tm * ROW_CHUNKS, V7X_LANES), lambda i: (n_tiles + i, 0)),
        pl.BlockSpec((tm, V7X_LANES), lambda i: (i, 0)),
    ]


SCHED_EXPERT, SCHED_VALID, SCHED_USED, SCHED_FIRST, SCHED_NEXT, SCHED_SLOT = range(6)
SCHED_ROWS = V7X_SUBLANES


def _plan_kernel(eidx_ref, rank_ref, counts_ref, pos_ref, sched_ref):
    i32 = jnp.int32
    shift = MOE_TILE.bit_length() - 1
    cnt = counts_ref[:, 0:1]
    padded = ((cnt + (MOE_TILE - 1)) >> shift) << shift
    seg = [padded[e:e + 1] for e in range(N_EXPERTS)]
    starts, ends = [], []
    run = jnp.zeros((1, 1), i32)
    for e in range(N_EXPERTS):
        starts.append(run)
        run = run + seg[e]
        ends.append(run)
    total = run
    nexts = [None] * N_EXPERTS
    nxt = jnp.full((1, 1), -1, i32)
    for e in reversed(range(N_EXPERTS)):
        nexts[e] = nxt
        nxt = jnp.where(seg[e] > 0, e, nxt)
    slots = []
    seen = jnp.zeros((1, 1), i32)
    for e in range(N_EXPERTS):
        slots.append(seen & 1)
        seen = seen + (seg[e] > 0).astype(i32)

    eidx = eidx_ref[...]
    pos = rank_ref[...]
    tile_row0 = lax.broadcasted_iota(i32, (1, V7X_LANES), 1) * MOE_TILE
    te = jnp.zeros((1, V7X_LANES), i32)
    for e in range(N_EXPERTS):
        pos = pos + jnp.where(eidx == e, starts[e], 0)
        te = te + (tile_row0 >= ends[e]).astype(i32)
    pos_ref[...] = pos
    te = jnp.minimum(te, N_EXPERTS - 1)

    def of_tile(per_expert):
        acc = jnp.zeros((1, V7X_LANES), i32)
        for e in range(N_EXPERTS):
            acc = acc + jnp.where(te == e, per_expert[e], 0)
        return acc

    valid_end = of_tile([starts[e] + cnt[e:e + 1] for e in range(N_EXPERTS)])
    used = tile_row0 < total
    rows = {
        SCHED_EXPERT: te,
        SCHED_VALID: jnp.clip(valid_end - tile_row0, 0, MOE_TILE),
        SCHED_USED: jnp.broadcast_to(total >> shift, (1, V7X_LANES)),
        SCHED_FIRST: (used & (of_tile(starts) == tile_row0)).astype(i32),
        SCHED_NEXT: of_tile(nexts),
        SCHED_SLOT: of_tile(slots),
    }
    for r in range(SCHED_ROWS):
        sched_ref[r:r + 1, :] = rows.get(r, jnp.zeros((1, V7X_LANES), i32))


def _sorted_positions(eidx, rank, counts):
    n_pairs = eidx.shape[0] * eidx.shape[1]
    n_rows = n_pairs + N_EXPERTS * MOE_TILE
    assert n_rows // MOE_TILE <= V7X_LANES
    pos, sched = pl.pallas_call(
        _plan_kernel,
        out_shape=[jax.ShapeDtypeStruct(eidx.shape, jnp.int32),
                   jax.ShapeDtypeStruct((SCHED_ROWS, V7X_LANES), jnp.int32)],
        name="moe_plan",
    )(eidx, rank, counts)
    return pos.reshape(n_pairs), sched, n_rows


def _sc_move_rows(scatter, src, pos_flat, n_out_rows, name):
    info = plsc.get_sparse_core_info()
    n_workers = info.num_cores * info.num_subcores
    n_pairs = pos_flat.shape[0]
    n_src = src.shape[0]
    per_worker = n_pairs // n_workers
    n_chunks = per_worker // SC_CHUNK
    assert per_worker * n_workers == n_pairs and n_chunks * SC_CHUNK == per_worker
    assert n_src % per_worker == 0
    idx = pos_flat.reshape(n_workers, n_chunks, SC_CHUNK)
    mesh = plsc.VectorSubcoreMesh(core_axis_name="core", subcore_axis_name="subcore")

    @functools.partial(
        pl.kernel,
        out_type=jax.ShapeDtypeStruct((n_out_rows, ROW_CHUNKS, V7X_LANES), U32),
        mesh=mesh,
        scratch_types=[
            pltpu.VMEM((n_chunks, SC_CHUNK), jnp.int32),
            pltpu.VMEM((2, SC_CHUNK, ROW_CHUNKS, V7X_LANES), U32),
            pltpu.SemaphoreType.DMA((2,)),
            pltpu.SemaphoreType.DMA((2,)),
        ],
        name=name)
    def move(src_hbm, i_hbm, o_hbm, idx_v, buf, in_sem, out_sem):
        wid = lax.axis_index("subcore") * info.num_cores + lax.axis_index("core")
        base = wid * per_worker
        src_base = lax.rem(base, n_src)
        pltpu.sync_copy(i_hbm.at[wid], idx_v)

        def fetch(s, slot):
            if scatter:
                rows = src_hbm.at[pl.ds(src_base + s * SC_CHUNK, SC_CHUNK)]
            else:
                rows = src_hbm.at[idx_v.at[s]]
            return pltpu.make_async_copy(rows, buf.at[slot], in_sem.at[slot])

        def flush(s, slot):
            if scatter:
                rows = o_hbm.at[idx_v.at[s]]
            else:
                rows = o_hbm.at[pl.ds(base + s * SC_CHUNK, SC_CHUNK)]
            return pltpu.make_async_copy(buf.at[slot], rows, out_sem.at[slot])

        fetch(0, 0).start()
        for s in range(n_chunks):
            slot = s % 2
            fetch(s, slot).wait()
            flush(s, slot).start()
            if s + 1 < n_chunks:
                if s >= 1:
                    flush(s - 1, 1 - slot).wait()
                fetch(s + 1, 1 - slot).start()
        flush(n_chunks - 2, n_chunks % 2).wait()
        flush(n_chunks - 1, (n_chunks - 1) % 2).wait()

    return move(src, idx)


def _experts_kernel(layer, sched_ref, x_ref, wg_hbm, wu_hbm, wd_hbm, y_ref,
                    wg_f32, wu_f32, wd_f32, wg_bf, wu_bf, wd_bf, sems):
    tm = MOE_TILE

    def weight_copies(expert, slot):
        pairs = ((wg_hbm, wg_f32), (wu_hbm, wu_f32), (wd_hbm, wd_f32))
        return [pltpu.make_async_copy(src.at[layer, expert], dst.at[slot], sems.at[slot, m])
                for m, (src, dst) in enumerate(pairs)]

    for part in range(MOE_STEP_TILES):
        j = pl.program_id(0) * MOE_STEP_TILES + part
        used = j < sched_ref[SCHED_USED, j]
        expert = sched_ref[SCHED_EXPERT, j]

        @pl.when(used & (sched_ref[SCHED_FIRST, j] == 1))
        def _():
            slot = sched_ref[SCHED_SLOT, j]
            next_expert = sched_ref[SCHED_NEXT, j]

            @pl.when(j == 0)
            def _():
                for cp in weight_copies(expert, slot):
                    cp.start()

            for cp in weight_copies(expert, slot):
                cp.wait()
            wg_bf[...] = wg_f32[slot].astype(BF16)
            wu_bf[...] = wu_f32[slot].astype(BF16)
            wd_bf[...] = wd_f32[slot].astype(BF16)

            @pl.when(next_expert >= 0)
            def _():
                for cp in weight_copies(next_expert, 1 - slot):
                    cp.start()

        @pl.when(used)
        def _():
            row = lax.broadcasted_iota(jnp.int32, (tm, 1), 0)
            words = jnp.where(row < sched_ref[SCHED_VALID, j],
                              _load_words(x_ref, tm, part * tm), jnp.uint32(0))
            h = _unpack_rows(words).astype(BF16)
            a = jnp.dot(h, wg_bf[...], preferred_element_type=F32)
            b = jnp.dot(h, wu_bf[...], preferred_element_type=F32)
            t = (_silu(a) * b).astype(BF16)
            y = jnp.dot(t, wd_bf[...], preferred_element_type=F32)
            _store_words(y_ref, _pack_rows(y), part * tm)

        @pl.when(jnp.logical_not(used))
        def _():
            lo = part * tm * ROW_CHUNKS
            y_ref[lo:lo + tm * ROW_CHUNKS, :] = jnp.zeros((tm * ROW_CHUNKS, V7X_LANES), U32)


def _experts_call(layer, rows, sched, w_gate, w_up, w_down):
    n_rows = rows.shape[0]
    tm = MOE_TILE * MOE_STEP_TILES
    d = D_MODEL
    assert n_rows % tm == 0
    rows2 = rows.reshape(n_rows * ROW_CHUNKS, V7X_LANES)

    def last_used_step(sc):
        return (sc[SCHED_USED, 0] + MOE_STEP_TILES - 1) // MOE_STEP_TILES - 1

    grid_spec = pltpu.PrefetchScalarGridSpec(
        num_scalar_prefetch=1,
        grid=(n_rows // tm,),
        in_specs=[
            pl.BlockSpec((tm * ROW_CHUNKS, V7X_LANES),
                         lambda j, sc: (jnp.minimum(j, last_used_step(sc)), 0)),
            pl.BlockSpec(memory_space=pl.ANY),
            pl.BlockSpec(memory_space=pl.ANY),
            pl.BlockSpec(memory_space=pl.ANY),
        ],
        out_specs=pl.BlockSpec((tm * ROW_CHUNKS, V7X_LANES), lambda j, sc: (j, 0)),
        scratch_shapes=[
            pltpu.VMEM((2, d, D_EXPERT), F32),
            pltpu.VMEM((2, d, D_EXPERT), F32),
            pltpu.VMEM((2, D_EXPERT, d), F32),
            pltpu.VMEM((d, D_EXPERT), BF16),
            pltpu.VMEM((d, D_EXPERT), BF16),
            pltpu.VMEM((D_EXPERT, d), BF16),
            pltpu.SemaphoreType.DMA((2, 3)),
        ],
    )
    y = pl.pallas_call(
        functools.partial(_experts_kernel, layer),
        grid_spec=grid_spec,
        out_shape=jax.ShapeDtypeStruct((n_rows * ROW_CHUNKS, V7X_LANES), U32),
        compiler_params=pltpu.CompilerParams(
            dimension_semantics=("arbitrary",),
            vmem_limit_bytes=V7X_VMEM_LIMIT_BYTES),
        name="experts_l%d" % layer,
    )(sched, rows2, w_gate, w_up, w_down)
    return y.reshape(n_rows, ROW_CHUNKS, V7X_LANES)


def _moe_rows(layer, mixer_outs, w_gate, w_up, w_down):
    plans = [_sorted_positions(eidx, rank, counts)
             for (_, _, eidx, _, rank, counts) in mixer_outs]
    sorted_rows = []
    for (_, h_rows, eidx, _, _, _), (pos_flat, _, n_rows) in zip(mixer_outs, plans):
        n_tok = eidx.shape[1]
        sorted_rows.append(_sc_move_rows(
            True, h_rows.reshape(n_tok, ROW_CHUNKS, V7X_LANES), pos_flat, n_rows, "sc_scatter_rows"))
    y_sorted = [_experts_call(layer, rows, sched, w_gate, w_up, w_down)
                for rows, (_, sched, _) in zip(sorted_rows, plans)]
    y_pairs = []
    for y, (pos_flat, _, _) in zip(y_sorted, plans):
        n_pairs = pos_flat.shape[0]
        moved = _sc_move_rows(False, y, pos_flat, n_pairs, "sc_gather_rows")
        y_pairs.append(moved.reshape(n_pairs * ROW_CHUNKS, V7X_LANES))
    return y_pairs


def _final_kernel(x_ref, y0_ref, y1_ref, wt_ref, modp_ref, fg_ref, *rest):
    o_ref = rest[-1]
    y = _moe_residual(x_ref, y0_ref, y1_ref, wt_ref, modp_ref[0, 0][5:6])
    ms = jnp.mean(y * y, axis=-1, keepdims=True)
    o_ref[...] = y * lax.rsqrt(ms + EPS) * fg_ref[...]


def _final_call(layer, batch, n_batch, x, y_pairs, wt, mod4, fg, out_prev):
    seq_len, d = x.shape
    tm = FINAL_TILE
    tiles_per_seq = seq_len // tm
    in_specs = _combine_specs(seq_len, tm) + [
        pl.BlockSpec((1, 1, 6, d), lambda i: (layer, batch, 0, 0)),
        _const_spec(fg.shape),
    ]
    args = [x, y_pairs, y_pairs, wt, mod4, fg]
    aliases = {}
    if out_prev is not None:
        in_specs.append(pl.BlockSpec(memory_space=pl.ANY))
        aliases = {len(args): 0}
        args.append(out_prev)
    return pl.pallas_call(
        _final_kernel,
        grid=(tiles_per_seq,),
        in_specs=in_specs,
        out_specs=pl.BlockSpec((tm, d), lambda i: (batch * tiles_per_seq + i, 0)),
        out_shape=jax.ShapeDtypeStruct((n_batch * seq_len, d), F32),
        input_output_aliases=aliases,
        compiler_params=pltpu.CompilerParams(
            dimension_semantics=("arbitrary",),
            vmem_limit_bytes=V7X_VMEM_LIMIT_BYTES),
        name="final_norm",
    )(*args)


def kernel(x, c, norm1_g, norm2_g, ada_w, ada_b, ab_w_in, pool_w, pool_scale, conf_conv_w, conf_conv_b, conf_ln_g, conf_ln_b, ab_w_out, cd_w_in, sconv_w, gmlp_ln_g, gmlp_ln_b, gmlp_ws, gmlp_bs, cd_w_out, router_w, router_bias, exp_w_gate, exp_w_up, exp_w_down, final_g):
    bsz, seq_len, d = x.shape
    n_tok = bsz * seq_len
    tm = MIX_TILE
    tiles_per_seq = seq_len // tm
    xf = x.reshape(n_tok, d)

    mod = _ada_mod(c, ada_w, ada_b)
    mod4 = mod.reshape(mod.shape[0], bsz, 6, d)

    rw_hi = router_w.astype(BF16)
    rw_lo = (router_w - rw_hi.astype(F32)).astype(BF16)
    rwt = jnp.concatenate([rw_hi.T, rw_lo.T], axis=0)
    rbias = router_bias.reshape(N_EXPERTS, 1)
    fg = final_g.reshape(1, d)

    weights_ab = [
        ab_w_in[0].astype(BF16), pool_w[0].astype(BF16), pool_scale[0].reshape(1, D_HALF),
        conf_conv_w[0], conf_conv_b[0].reshape(1, D_HALF), conf_ln_g[0].reshape(1, D_HALF),
        conf_ln_b[0].reshape(1, D_HALF), ab_w_out[0].astype(BF16),
    ]
    scratch_ab = [pltpu.VMEM((POOL_HIST + tm, D_HALF), F32),
                  pltpu.VMEM((CONV_HIST + tm, D_HALF), F32)]
    bsf = jnp.repeat(gmlp_bs[0].T, POOL_GROUP, axis=1)
    weights_cd = [
        cd_w_in[0].astype(BF16), sconv_w[0], gmlp_ln_g[0].reshape(1, D_HALF),
        gmlp_ln_b[0].reshape(1, D_HALF), gmlp_ws[0], bsf, cd_w_out[0].astype(BF16),
    ]
    scratch_cd = [pltpu.VMEM((SCONV_HIST + tm, D_HALF), F32)]
    experts = (exp_w_gate, exp_w_up, exp_w_down)

    batches = range(bsz)
    stage_ab = []
    for b in batches:
        x_spec = pl.BlockSpec((tm, d), lambda i, b=b: (b * tiles_per_seq + i, 0))
        stage_ab.append(_mixer_call(
            _mixer_ab_kernel, 0, b, [xf], [x_spec], mod4, norm1_g[0:1], norm2_g[0:1],
            weights_ab, rwt, rbias, scratch_ab, seq_len, "mixer_ab"))
    y_pairs0 = _moe_rows(0, stage_ab, *experts)

    stage_cd = []
    for b in batches:
        x1, _, _, wsel0, _, _ = stage_ab[b]
        prev_mod_spec = pl.BlockSpec((1, 1, 6, d), lambda i, b=b: (0, b, 0, 0))
        stage_cd.append(_mixer_call(
            _mixer_cd_kernel, 1, b, [x1, y_pairs0[b], y_pairs0[b], wsel0, mod4],
            _combine_specs(seq_len, tm) + [prev_mod_spec], mod4, norm1_g[1:2], norm2_g[1:2],
            weights_cd, rwt, rbias, scratch_cd, seq_len, "mixer_cd"))
    y_pairs1 = _moe_rows(1, stage_cd, *experts)

    out = None
    for b in batches:
        x3, _, _, wsel1, _, _ = stage_cd[b]
        out = _final_call(1, b, bsz, x3, y_pairs1[b], wsel1, mod4, fg, out)
    return out.reshape(bsz, seq_len, d)
```

```python
import functools

import jax
import jax.numpy as jnp
from jax import lax
from jax.experimental import pallas as pl
from jax.experimental.pallas import tpu as pltpu
from jax.experimental.pallas import tpu_sc as plsc

D_MODEL = 1024
EPS = 1e-6
POOL_WINDOWS = (2, 4, 8, 16)
POOL_GROUP = 128
D_HALF = 512
CONF_KERNEL = 31
SCONV_KERNEL = 3
CHUNK = 128
GMLP_HEADS = 4
N_EXPERTS = 16
N_GROUPS = 4
EXPERTS_PER_GROUP = 4
TOP_K = 2
D_EXPERT = 512

V7X_LANES = 128
V7X_SUBLANES = 8
V7X_VMEM_LIMIT_BYTES = 56 * 1024 * 1024

MIX_TILE = 1024
FINAL_TILE = 1024
MOE_TILE = 512
MOE_STEP_TILES = 4
SC_CHUNK = 64
ROW_CHUNKS = D_MODEL // (2 * V7X_LANES)
CONV_HIST = 32
POOL_HIST = 16
SCONV_HIST = 8

BF16 = jnp.bfloat16
F32 = jnp.float32
U32 = jnp.uint32


def _rms_mod(x, g_row, shift_row, scale_row):
    ms = jnp.mean(x * x, axis=-1, keepdims=True)
    gain = g_row * (1.0 + scale_row)
    return (x * lax.rsqrt(ms + EPS)) * gain + shift_row


def _layer_norm(x, g_row, b_row):
    mu = jnp.mean(x, axis=-1, keepdims=True)
    xc = x - mu
    var = jnp.mean(xc * xc, axis=-1, keepdims=True)
    return xc * lax.rsqrt(var + EPS) * g_row + b_row


def _sigmoid(x):
    return 0.5 * jnp.tanh(0.5 * x) + 0.5


def _silu(x):
    return x * _sigmoid(x)


def _gelu_tanh(x):
    c = 0.7978845608028654
    return 0.5 * x * (1.0 + jnp.tanh(c * (x + 0.044715 * (x * x * x))))


def _shift_rows(x, r):
    n, c = x.shape
    if r == V7X_SUBLANES:
        return jnp.concatenate([x[:r], x[:n - r]], axis=0)
    g = x.reshape(n // V7X_SUBLANES, V7X_SUBLANES, c)
    rot = pltpu.roll(g, r, axis=1)
    prev = jnp.concatenate([rot[:1], rot[:-1]], axis=0)
    sub = lax.broadcasted_iota(jnp.int32, g.shape, 1)
    return jnp.where(sub < r, prev, rot).reshape(n, c)


def _load_words(ref, n_rows, row0=0):
    return jnp.concatenate(
        [ref[pl.ds(row0 * ROW_CHUNKS + c, n_rows, stride=ROW_CHUNKS), :]
         for c in range(ROW_CHUNKS)], axis=1)


def _store_words(ref, words, row0=0):
    n_rows = words.shape[0]
    for c in range(ROW_CHUNKS):
        ref[pl.ds(row0 * ROW_CHUNKS + c, n_rows, stride=ROW_CHUNKS), :] = (
            words[:, c * V7X_LANES:(c + 1) * V7X_LANES])


def _pack_rows(val):
    half = val.shape[1] // 2
    return pltpu.pack_elementwise([val[:, :half], val[:, half:]], packed_dtype=BF16)


def _unpack_rows(words):
    halves = [pltpu.unpack_elementwise(words, index=i, packed_dtype=BF16, unpacked_dtype=F32)
              for i in range(2)]
    return jnp.concatenate(halves, axis=1)


def _ada_kernel(ct_ref, w_ref, b_ref, o_ref):
    ct = ct_ref[...]
    cond = _silu(ct)
    w = w_ref[0]
    nb = ct.shape[1]
    for b in range(nb):
        col = cond[:, b:b + 1]
        o_ref[0, b:b + 1, :] = jnp.sum(col * w, axis=0, keepdims=True) + b_ref[0]


def _ada_mod(c, ada_w, ada_b):
    depth, d, six_d = ada_w.shape
    bsz = c.shape[0]
    nb = D_MODEL
    return pl.pallas_call(
        _ada_kernel,
        grid=(depth, six_d // nb),
        in_specs=[
            pl.BlockSpec((d, bsz), lambda l, j: (0, 0)),
            pl.BlockSpec((1, d, nb), lambda l, j: (l, 0, j)),
            pl.BlockSpec((1, 1, nb), lambda l, j: (l, 0, j)),
        ],
        out_specs=pl.BlockSpec((1, bsz, nb), lambda l, j: (l, 0, j)),
        out_shape=jax.ShapeDtypeStruct((depth, bsz, six_d), F32),
        compiler_params=pltpu.CompilerParams(
            dimension_semantics=("arbitrary", "arbitrary"),
            vmem_limit_bytes=V7X_VMEM_LIMIT_BYTES),
        name="ada_mod",
    )(c.T, ada_w, ada_b.reshape(depth, 1, six_d))


def _route(h2_bf, rwt_ref, rbias_ref, eidx_ref, wsel_ref, rank_ref, counts_ref, cnt_ref):
    nt = (((1,), (1,)), ((), ()))
    r = lax.dot_general(rwt_ref[...], h2_bf, nt, preferred_element_type=F32)
    logits = r[:N_EXPERTS] + r[N_EXPERTS:]
    m = jnp.max(logits, axis=0, keepdims=True)
    ex = jnp.exp(logits - m)
    probs = ex / jnp.sum(ex, axis=0, keepdims=True)
    sel = probs + rbias_ref[...]
    s = [sel[e:e + 1] for e in range(N_EXPERTS)]
    p = [probs[e:e + 1] for e in range(N_EXPERTS)]
    best = None
    gi = None
    for g in range(N_GROUPS):
        a, b, c, d = s[4 * g:4 * g + 4]
        hi1, lo1 = jnp.maximum(a, b), jnp.minimum(a, b)
        hi2, lo2 = jnp.maximum(c, d), jnp.minimum(c, d)
        top1 = jnp.maximum(hi1, hi2)
        top2 = jnp.maximum(jnp.minimum(hi1, hi2), jnp.maximum(lo1, lo2))
        score = top1 + top2
        if g == 0:
            best, gi = score, jnp.zeros(score.shape, jnp.int32)
        else:
            upd = score > best
            gi = jnp.where(upd, g, gi)
            best = jnp.where(upd, score, best)
    v, q = [], []
    for j in range(EXPERTS_PER_GROUP):
        vj, qj = s[j], p[j]
        for g in range(1, N_GROUPS):
            pick = gi == g
            vj = jnp.where(pick, s[4 * g + j], vj)
            qj = jnp.where(pick, p[4 * g + j], qj)
        v.append(vj)
        q.append(qj)
    i1 = jnp.zeros(gi.shape, jnp.int32)
    m1 = v[0]
    for j in range(1, EXPERTS_PER_GROUP):
        upd = v[j] > m1
        i1 = jnp.where(upd, j, i1)
        m1 = jnp.where(upd, v[j], m1)
    i2 = jnp.zeros(gi.shape, jnp.int32)
    m2 = jnp.full(m1.shape, -jnp.inf, F32)
    for j in range(EXPERTS_PER_GROUP):
        cand = (i1 != j) & (v[j] > m2)
        i2 = jnp.where(cand, j, i2)
        m2 = jnp.where(cand, v[j], m2)
    pa = q[0]
    pb = q[0]
    for j in range(1, EXPERTS_PER_GROUP):
        pa = jnp.where(i1 == j, q[j], pa)
        pb = jnp.where(i2 == j, q[j], pb)
    tot = pa + pb
    e0 = gi * EXPERTS_PER_GROUP + i1
    e1 = gi * EXPERTS_PER_GROUP + i2
    t = h2_bf.shape[0]
    eidx_ref[0:1, :] = e0
    eidx_ref[1:2, :] = e1
    w_rows = jnp.concatenate(
        [pa / tot, pb / tot, jnp.zeros((V7X_LANES - TOP_K, t), F32)], axis=0)
    wsel_ref[...] = w_rows.T

    e_iota = lax.broadcasted_iota(jnp.int32, (N_EXPERTS, t), 0)
    oh0 = e_iota == e0
    oh1 = e_iota == e1
    both = jnp.where(oh0 | oh1, 1.0, 0.0)
    r_i = lax.broadcasted_iota(jnp.int32, (V7X_LANES, V7X_LANES), 0)
    c_i = lax.broadcasted_iota(jnp.int32, (V7X_LANES, V7X_LANES), 1)
    before = jnp.where(r_i < c_i, 1.0, 0.0).astype(BF16)
    run = cnt_ref[...]
    rank0, rank1 = [], []
    for blk in range(t // V7X_LANES):
        lanes = slice(blk * V7X_LANES, (blk + 1) * V7X_LANES)
        b = both[:, lanes]
        pre = jnp.dot(b.astype(BF16), before, preferred_element_type=F32) + run
        rank0.append(jnp.sum(jnp.where(oh0[:, lanes], pre, 0.0), axis=0, keepdims=True))
        rank1.append(jnp.sum(jnp.where(oh1[:, lanes], pre, 0.0), axis=0, keepdims=True))
        run = run + jnp.sum(b, axis=1, keepdims=True)
    cnt_ref[...] = run
    rank_ref[0:1, :] = jnp.concatenate(rank0, axis=1).astype(jnp.int32)
    rank_ref[1:2, :] = jnp.concatenate(rank1, axis=1).astype(jnp.int32)
    counts_ref[...] = jnp.broadcast_to(run, counts_ref.shape).astype(jnp.int32)


def _finish_mixer(x, m, mod, n2g_ref, rwt_ref, rbias_ref,
                  x1_ref, h2_ref, eidx_ref, wsel_ref, rank_ref, counts_ref, cnt_ref):
    x1 = x + mod[2:3] * m
    x1_ref[...] = x1
    h2 = _rms_mod(x1, n2g_ref[...], mod[3:4], mod[4:5])
    h2_bf = h2.astype(BF16)
    _store_words(h2_ref, _pack_rows(h2))
    _route(h2_bf, rwt_ref, rbias_ref, eidx_ref, wsel_ref, rank_ref, counts_ref, cnt_ref)


def _moe_residual(x_ref, y0_ref, y1_ref, wt_ref, g2_row):
    tm = x_ref.shape[0]
    wt = wt_ref[...]
    y0 = _unpack_rows(_load_words(y0_ref, tm))
    y1 = _unpack_rows(_load_words(y1_ref, tm))
    y = wt[:, 0:1] * y0 + wt[:, 1:2] * y1
    return x_ref[...] + g2_row * y


def _mixer_ab_kernel(x_ref, mod_ref, n1g_ref, n2g_ref, win_ref, poolw_ref, pscale_ref,
                     convw_ref, convb_ref, lng_ref, lnb_ref, wout_ref, rwt_ref, rbias_ref,
                     x1_ref, h2_ref, eidx_ref, wsel_ref, rank_ref, counts_ref,
                     pool_ext, conv_ext, cnt_ref):
    seq_tile = pl.program_id(0)
    tm = x_ref.shape[0]

    @pl.when(seq_tile == 0)
    def _():
        pool_ext[0:POOL_HIST, :] = jnp.zeros((POOL_HIST, D_HALF), F32)
        conv_ext[0:CONV_HIST, :] = jnp.zeros((CONV_HIST, D_HALF), F32)
        cnt_ref[...] = jnp.zeros_like(cnt_ref)

    x = x_ref[...]
    mod = mod_ref[0, 0]
    h = _rms_mod(x, n1g_ref[...], mod[0:1], mod[1:2]).astype(BF16)
    z = jnp.dot(h, win_ref[...], preferred_element_type=F32)
    zp = z[:, :D_HALF]
    glu = z[:, D_HALF:2 * D_HALF] * _sigmoid(z[:, 2 * D_HALF:])
    pool_ext[POOL_HIST:POOL_HIST + tm, :] = zp
    conv_ext[CONV_HIST:CONV_HIST + tm, :] = glu

    row = lax.broadcasted_iota(jnp.int32, (tm, 1), 0)
    pos1 = (seq_tile * tm + row + 1).astype(F32)
    pool_out = []
    for g, w in enumerate(POOL_WINDOWS):
        cols = slice(g * POOL_GROUP, (g + 1) * POOL_GROUP)
        acc = pool_ext[:, cols]
        span = 1
        while span < w:
            acc = acc + _shift_rows(acc, span)
            span *= 2
        wsum = acc[POOL_HIST:POOL_HIST + tm]
        inv_cnt = 1.0 / jnp.minimum(pos1, float(w))
        diff = wsum * inv_cnt - zp[:, cols]
        po = jnp.dot(diff.astype(BF16), poolw_ref[g], preferred_element_type=F32)
        pool_out.append(po * pscale_ref[:, cols])

    convw = convw_ref[...]
    ext_rows = tm + V7X_SUBLANES
    conv = None
    for r in range(V7X_SUBLANES):
        vr = None
        for a in range(CONV_HIST // V7X_SUBLANES):
            lag = V7X_SUBLANES * a + r
            if lag >= CONF_KERNEL:
                continue
            k = CONF_KERNEL - 1 - lag
            start = CONV_HIST - V7X_SUBLANES - V7X_SUBLANES * a
            term = convw[k:k + 1, :] * conv_ext[start:start + ext_rows, :]
            vr = term if vr is None else vr + term
        if r:
            vr = _shift_rows(vr, r)
        conv = vr if conv is None else conv + vr
    conv = conv[V7X_SUBLANES:V7X_SUBLANES + tm] + convb_ref[...]
    conf = _silu(_layer_norm(conv, lng_ref[...], lnb_ref[...]))

    pool_ext[0:POOL_HIST, :] = zp[tm - POOL_HIST:tm]
    conv_ext[0:CONV_HIST, :] = glu[tm - CONV_HIST:tm]

    m = jnp.dot(conf.astype(BF16), wout_ref[D_HALF:, :], preferred_element_type=F32)
    for g in range(len(POOL_WINDOWS)):
        rows = slice(g * POOL_GROUP, (g + 1) * POOL_GROUP)
        m = m + jnp.dot(pool_out[g].astype(BF16), wout_ref[rows, :], preferred_element_type=F32)
    _finish_mixer(x, m, mod, n2g_ref, rwt_ref, rbias_ref,
                  x1_ref, h2_ref, eidx_ref, wsel_ref, rank_ref, counts_ref, cnt_ref)


def _mixer_cd_kernel(x_ref, y0_ref, y1_ref, wt_ref, modp_ref,
                     mod_ref, n1g_ref, n2g_ref, win_ref, sconvw_ref, lng_ref, lnb_ref,
                     ws_ref, bsf_ref, wout_ref, rwt_ref, rbias_ref,
                     x1_ref, h2_ref, eidx_ref, wsel_ref, rank_ref, counts_ref,
                     sconv_ext, cnt_ref):
    tm = x_ref.shape[0]

    @pl.when(pl.program_id(0) == 0)
    def _():
        sconv_ext[0:SCONV_HIST, :] = jnp.zeros((SCONV_HIST, D_HALF), F32)
        cnt_ref[...] = jnp.zeros_like(cnt_ref)

    x = _moe_residual(x_ref, y0_ref, y1_ref, wt_ref, modp_ref[0, 0][5:6])
    mod = mod_ref[0, 0]
    h = _rms_mod(x, n1g_ref[...], mod[0:1], mod[1:2]).astype(BF16)
    def proj(lo, hi):
        return jnp.dot(h, win_ref[:, lo:hi], preferred_element_type=F32)

    v = _layer_norm(_gelu_tanh(proj(4 * D_HALF, 5 * D_HALF)), lng_ref[...], lnb_ref[...])
    u = _gelu_tanh(proj(3 * D_HALF, 4 * D_HALF))
    ch = proj(D_HALF, 2 * D_HALF) * proj(2 * D_HALF, 3 * D_HALF)
    bg = proj(0, D_HALF)

    sconv_ext[SCONV_HIST:SCONV_HIST + tm, :] = ch
    sw = sconvw_ref[...]
    ext = sconv_ext[...]
    conv = sw[2:3, :] * ext
    conv = conv + sw[1:2, :] * _shift_rows(ext, 1)
    conv = conv + sw[0:1, :] * _shift_rows(ext, 2)
    sc_out = bg * conv[SCONV_HIST:SCONV_HIST + tm]
    sconv_ext[0:SCONV_HIST, :] = ch[tm - SCONV_HIST:tm]

    r_i = lax.broadcasted_iota(jnp.int32, (CHUNK, CHUNK), 0)
    c_i = lax.broadcasted_iota(jnp.int32, (CHUNK, CHUNK), 1)
    tril = c_i <= r_i
    wm = [jnp.where(tril, ws_ref[hd], 0.0).astype(BF16) for hd in range(GMLP_HEADS)]
    v_bf = v.astype(BF16)
    bsf = bsf_ref[...]
    gm_rows = []
    for n in range(tm // CHUNK):
        rows = slice(n * CHUNK, (n + 1) * CHUNK)
        heads = []
        for hd in range(GMLP_HEADS):
            cols = slice(hd * POOL_GROUP, (hd + 1) * POOL_GROUP)
            heads.append(jnp.dot(wm[hd], v_bf[rows, cols], preferred_element_type=F32))
        mixed = jnp.concatenate(heads, axis=1) + bsf
        gm_rows.append(u[rows] * mixed)
    gm_out = jnp.concatenate(gm_rows, axis=0)

    m = jnp.dot(sc_out.astype(BF16), wout_ref[:D_HALF, :], preferred_element_type=F32)
    m = m + jnp.dot(gm_out.astype(BF16), wout_ref[D_HALF:, :], preferred_element_type=F32)
    _finish_mixer(x, m, mod, n2g_ref, rwt_ref, rbias_ref,
                  x1_ref, h2_ref, eidx_ref, wsel_ref, rank_ref, counts_ref, cnt_ref)


def _const_spec(shape):
    nd = len(shape)
    return pl.BlockSpec(shape, lambda i: (0,) * nd)


def _mixer_call(kernel_fn, layer, batch, stream_inputs, stream_specs, mod4, n1g, n2g, weights,
                rwt, rbias, scratch, seq_len, name):
    n_tok = seq_len
    d = D_MODEL
    tm = MIX_TILE
    in_specs = stream_specs + [
        pl.BlockSpec((1, 1, 6, d), lambda i: (layer, batch, 0, 0)),
        _const_spec(n1g.shape),
        _const_spec(n2g.shape),
    ] + [_const_spec(w.shape) for w in weights] + [_const_spec(rwt.shape), _const_spec(rbias.shape)]
    out_specs = [
        pl.BlockSpec((tm, d), lambda i: (i, 0)),
        pl.BlockSpec((tm * ROW_CHUNKS, V7X_LANES), lambda i: (i, 0)),
        pl.BlockSpec((TOP_K, tm), lambda i: (0, i)),
        pl.BlockSpec((tm, V7X_LANES), lambda i: (i, 0)),
        pl.BlockSpec((TOP_K, tm), lambda i: (0, i)),
        pl.BlockSpec((N_EXPERTS, V7X_LANES), lambda i: (0, 0)),
    ]
    out_shape = [
        jax.ShapeDtypeStruct((n_tok, d), F32),
        jax.ShapeDtypeStruct((n_tok * ROW_CHUNKS, V7X_LANES), U32),
        jax.ShapeDtypeStruct((TOP_K, n_tok), jnp.int32),
        jax.ShapeDtypeStruct((n_tok, V7X_LANES), F32),
        jax.ShapeDtypeStruct((TOP_K, n_tok), jnp.int32),
        jax.ShapeDtypeStruct((N_EXPERTS, V7X_LANES), jnp.int32),
    ]
    return pl.pallas_call(
        kernel_fn,
        grid=(n_tok // tm,),
        in_specs=in_specs,
        out_specs=out_specs,
        out_shape=out_shape,
        scratch_shapes=scratch + [pltpu.VMEM((N_EXPERTS, 1), F32)],
        compiler_params=pltpu.CompilerParams(
            dimension_semantics=("arbitrary",),
            vmem_limit_bytes=V7X_VMEM_LIMIT_BYTES),
        name=name,
    )(*stream_inputs, mod4, n1g, n2g, *weights, rwt, rbias)


def _combine_specs(n_tok, tm):
    n_tiles = n_tok // tm
    return [
        pl.BlockSpec((tm, D_MODEL), lambda i: (i, 0)),
        pl.BlockSpec((tm * ROW_CHUNKS, V7X_LANES), lambda i: (i, 0)),
        pl.BlockSpec((tm * ROW_CHUNKS, V7X_LANES), lambda i: (n_tiles + i, 0)),
        pl.BlockSpec((tm, V7X_LANES), lambda i: (i, 0)),
    ]


SCHED_EXPERT, SCHED_VALID, SCHED_USED, SCHED_FIRST, SCHED_NEXT, SCHED_SLOT = range(6)
SCHED_ROWS = V7X_SUBLANES


def _plan_kernel(eidx_ref, rank_ref, counts_ref, pos_ref, sched_ref):
    i32 = jnp.int32
    shift = MOE_TILE.bit_length() - 1
    cnt = counts_ref[:, 0:1]
    padded = ((cnt + (MOE_TILE - 1)) >> shift) << shift
    seg = [padded[e:e + 1] for e in range(N_EXPERTS)]
    starts, ends = [], []
    run = jnp.zeros((1, 1), i32)
    for e in range(N_EXPERTS):
        starts.append(run)
        run = run + seg[e]
        ends.append(run)
    total = run
    nexts = [None] * N_EXPERTS
    nxt = jnp.full((1, 1), -1, i32)
    for e in reversed(range(N_EXPERTS)):
        nexts[e] = nxt
        nxt = jnp.where(seg[e] > 0, e, nxt)
    slots = []
    seen = jnp.zeros((1, 1), i32)
    for e in range(N_EXPERTS):
        slots.append(seen & 1)
        seen = seen + (seg[e] > 0).astype(i32)

    eidx = eidx_ref[...]
    pos = rank_ref[...]
    tile_row0 = lax.broadcasted_iota(i32, (1, V7X_LANES), 1) * MOE_TILE
    te = jnp.zeros((1, V7X_LANES), i32)
    for e in range(N_EXPERTS):
        pos = pos + jnp.where(eidx == e, starts[e], 0)
        te = te + (tile_row0 >= ends[e]).astype(i32)
    pos_ref[...] = pos
    te = jnp.minimum(te, N_EXPERTS - 1)

    def of_tile(per_expert):
        acc = jnp.zeros((1, V7X_LANES), i32)
        for e in range(N_EXPERTS):
            acc = acc + jnp.where(te == e, per_expert[e], 0)
        return acc

    valid_end = of_tile([starts[e] + cnt[e:e + 1] for e in range(N_EXPERTS)])
    used = tile_row0 < total
    rows = {
        SCHED_EXPERT: te,
        SCHED_VALID: jnp.clip(valid_end - tile_row0, 0, MOE_TILE),
        SCHED_USED: jnp.broadcast_to(total >> shift, (1, V7X_LANES)),
        SCHED_FIRST: (used & (of_tile(starts) == tile_row0)).astype(i32),
        SCHED_NEXT: of_tile(nexts),
        SCHED_SLOT: of_tile(slots),
    }
    for r in range(SCHED_ROWS):
        sched_ref[r:r + 1, :] = rows.get(r, jnp.zeros((1, V7X_LANES), i32))


def _sorted_positions(eidx, rank, counts):
    n_pairs = eidx.shape[0] * eidx.shape[1]
    n_rows = n_pairs + N_EXPERTS * MOE_TILE
    assert n_rows // MOE_TILE <= V7X_LANES
    pos, sched = pl.pallas_call(
        _plan_kernel,
        out_shape=[jax.ShapeDtypeStruct(eidx.shape, jnp.int32),
                   jax.ShapeDtypeStruct((SCHED_ROWS, V7X_LANES), jnp.int32)],
        name="moe_plan",
    )(eidx, rank, counts)
    return pos.reshape(n_pairs), sched, n_rows


def _sc_move_rows(scatter, src, pos_flat, n_out_rows, name):
    info = plsc.get_sparse_core_info()
    n_workers = info.num_cores * info.num_subcores
    n_pairs = pos_flat.shape[0]
    n_src = src.shape[0]
    per_worker = n_pairs // n_workers
    n_chunks = per_worker // SC_CHUNK
    assert per_worker * n_workers == n_pairs and n_chunks * SC_CHUNK == per_worker
    assert n_src % per_worker == 0
    idx = pos_flat.reshape(n_workers, n_chunks, SC_CHUNK)
    mesh = plsc.VectorSubcoreMesh(core_axis_name="core", subcore_axis_name="subcore")

    @functools.partial(
        pl.kernel,
        out_type=jax.ShapeDtypeStruct((n_out_rows, ROW_CHUNKS, V7X_LANES), U32),
        mesh=mesh,
        scratch_types=[
            pltpu.VMEM((n_chunks, SC_CHUNK), jnp.int32),
            pltpu.VMEM((2, SC_CHUNK, ROW_CHUNKS, V7X_LANES), U32),
            pltpu.SemaphoreType.DMA((2,)),
            pltpu.SemaphoreType.DMA((2,)),
        ],
        name=name)
    def move(src_hbm, i_hbm, o_hbm, idx_v, buf, in_sem, out_sem):
        wid = lax.axis_index("subcore") * info.num_cores + lax.axis_index("core")
        base = wid * per_worker
        src_base = lax.rem(base, n_src)
        pltpu.sync_copy(i_hbm.at[wid], idx_v)

        def fetch(s, slot):
            if scatter:
                rows = src_hbm.at[pl.ds(src_base + s * SC_CHUNK, SC_CHUNK)]
            else:
                rows = src_hbm.at[idx_v.at[s]]
            return pltpu.make_async_copy(rows, buf.at[slot], in_sem.at[slot])

        def flush(s, slot):
            if scatter:
                rows = o_hbm.at[idx_v.at[s]]
            else:
                rows = o_hbm.at[pl.ds(base + s * SC_CHUNK, SC_CHUNK)]
            return pltpu.make_async_copy(buf.at[slot], rows, out_sem.at[slot])

        fetch(0, 0).start()
        for s in range(n_chunks):
            slot = s % 2
            fetch(s, slot).wait()
            flush(s, slot).start()
            if s + 1 < n_chunks:
                if s >= 1:
                    flush(s - 1, 1 - slot).wait()
                fetch(s + 1, 1 - slot).start()
        flush(n_chunks - 2, n_chunks % 2).wait()
        flush(n_chunks - 1, (n_chunks - 1) % 2).wait()

    return move(src, idx)


def _experts_kernel(layer, sched_ref, x_ref, wg_hbm, wu_hbm, wd_hbm, y_ref,
                    wg_f32, wu_f32, wd_f32, wgu_bf, wd_bf, sems):
    tm = MOE_TILE

    def weight_copies(expert, slot):
        pairs = ((wg_hbm, wg_f32), (wu_hbm, wu_f32), (wd_hbm, wd_f32))
        return [pltpu.make_async_copy(src.at[layer, expert], dst.at[slot], sems.at[slot, m])
                for m, (src, dst) in enumerate(pairs)]

    for part in range(MOE_STEP_TILES):
        j = pl.program_id(0) * MOE_STEP_TILES + part
        used = j < sched_ref[SCHED_USED, j]
        expert = sched_ref[SCHED_EXPERT, j]

        @pl.when(used & (sched_ref[SCHED_FIRST, j] == 1))
        def _():
            slot = sched_ref[SCHED_SLOT, j]
            next_expert = sched_ref[SCHED_NEXT, j]

            @pl.when(j == 0)
            def _():
                for cp in weight_copies(expert, slot):
                    cp.start()

            for cp in weight_copies(expert, slot):
                cp.wait()
            wgu_bf[:, :D_EXPERT] = wg_f32[slot].astype(BF16)
            wgu_bf[:, D_EXPERT:] = wu_f32[slot].astype(BF16)
            wd_bf[...] = wd_f32[slot].astype(BF16)

            @pl.when(next_expert >= 0)
            def _():
                for cp in weight_copies(next_expert, 1 - slot):
                    cp.start()

        @pl.when(used)
        def _():
            row = lax.broadcasted_iota(jnp.int32, (tm, 1), 0)
            words = jnp.where(row < sched_ref[SCHED_VALID, j],
                              _load_words(x_ref, tm, part * tm), jnp.uint32(0))
            h = _unpack_rows(words).astype(BF16)
            ab = jnp.dot(h, wgu_bf[...], preferred_element_type=F32)
            t = (_silu(ab[:, :D_EXPERT]) * ab[:, D_EXPERT:]).astype(BF16)
            y = jnp.dot(t, wd_bf[...], preferred_element_type=F32)
            _store_words(y_ref, _pack_rows(y), part * tm)

        @pl.when(jnp.logical_not(used))
        def _():
            lo = part * tm * ROW_CHUNKS
            y_ref[lo:lo + tm * ROW_CHUNKS, :] = jnp.zeros((tm * ROW_CHUNKS, V7X_LANES), U32)


def _experts_call(layer, rows, sched, w_gate, w_up, w_down):
    n_rows = rows.shape[0]
    tm = MOE_TILE * MOE_STEP_TILES
    d = D_MODEL
    assert n_rows % tm == 0
    rows2 = rows.reshape(n_rows * ROW_CHUNKS, V7X_LANES)

    def last_used_step(sc):
        return (sc[SCHED_USED, 0] + MOE_STEP_TILES - 1) // MOE_STEP_TILES - 1

    grid_spec = pltpu.PrefetchScalarGridSpec(
        num_scalar_prefetch=1,
        grid=(n_rows // tm,),
        in_specs=[
            pl.BlockSpec((tm * ROW_CHUNKS, V7X_LANES),
                         lambda j, sc: (jnp.minimum(j, last_used_step(sc)), 0)),
            pl.BlockSpec(memory_space=pl.ANY),
            pl.BlockSpec(memory_space=pl.ANY),
            pl.BlockSpec(memory_space=pl.ANY),
        ],
        out_specs=pl.BlockSpec((tm * ROW_CHUNKS, V7X_LANES), lambda j, sc: (j, 0)),
        scratch_shapes=[
            pltpu.VMEM((2, d, D_EXPERT), F32),
            pltpu.VMEM((2, d, D_EXPERT), F32),
            pltpu.VMEM((2, D_EXPERT, d), F32),
            pltpu.VMEM((d, 2 * D_EXPERT), BF16),
            pltpu.VMEM((D_EXPERT, d), BF16),
            pltpu.SemaphoreType.DMA((2, 3)),
        ],
    )
    y = pl.pallas_call(
        functools.partial(_experts_kernel, layer),
        grid_spec=grid_spec,
        out_shape=jax.ShapeDtypeStruct((n_rows * ROW_CHUNKS, V7X_LANES), U32),
        compiler_params=pltpu.CompilerParams(
            dimension_semantics=("arbitrary",),
            vmem_limit_bytes=V7X_VMEM_LIMIT_BYTES),
        name="experts_l%d" % layer,
    )(sched, rows2, w_gate, w_up, w_down)
    return y.reshape(n_rows, ROW_CHUNKS, V7X_LANES)


def _moe_rows(layer, mixer_outs, w_gate, w_up, w_down):
    plans = [_sorted_positions(eidx, rank, counts)
             for (_, _, eidx, _, rank, counts) in mixer_outs]
    sorted_rows = []
    for (_, h_rows, eidx, _, _, _), (pos_flat, _, n_rows) in zip(mixer_outs, plans):
        n_tok = eidx.shape[1]
        sorted_rows.append(_sc_move_rows(
            True, h_rows.reshape(n_tok, ROW_CHUNKS, V7X_LANES), pos_flat, n_rows, "sc_scatter_rows"))
    y_sorted = [_experts_call(layer, rows, sched, w_gate, w_up, w_down)
                for rows, (_, sched, _) in zip(sorted_rows, plans)]
    y_pairs = []
    for y, (pos_flat, _, _) in zip(y_sorted, plans):
        n_pairs = pos_flat.shape[0]
        moved = _sc_move_rows(False, y, pos_flat, n_pairs, "sc_gather_rows")
        y_pairs.append(moved.reshape(n_pairs * ROW_CHUNKS, V7X_LANES))
    return y_pairs


def _final_kernel(x_ref, y0_ref, y1_ref, wt_ref, modp_ref, fg_ref, *rest):
    o_ref = rest[-1]
    y = _moe_residual(x_ref, y0_ref, y1_ref, wt_ref, modp_ref[0, 0][5:6])
    ms = jnp.mean(y * y, axis=-1, keepdims=True)
    o_ref[...] = y * lax.rsqrt(ms + EPS) * fg_ref[...]


def _final_call(layer, batch, n_batch, x, y_pairs, wt, mod4, fg, out_prev):
    seq_len, d = x.shape
    tm = FINAL_TILE
    tiles_per_seq = seq_len // tm
    in_specs = _combine_specs(seq_len, tm) + [
        pl.BlockSpec((1, 1, 6, d), lambda i: (layer, batch, 0, 0)),
        _const_spec(fg.shape),
    ]
    args = [x, y_pairs, y_pairs, wt, mod4, fg]
    aliases = {}
    if out_prev is not None:
        in_specs.append(pl.BlockSpec(memory_space=pl.ANY))
        aliases = {len(args): 0}
        args.append(out_prev)
    return pl.pallas_call(
        _final_kernel,
        grid=(tiles_per_seq,),
        in_specs=in_specs,
        out_specs=pl.BlockSpec((tm, d), lambda i: (batch * tiles_per_seq + i, 0)),
        out_shape=jax.ShapeDtypeStruct((n_batch * seq_len, d), F32),
        input_output_aliases=aliases,
        compiler_params=pltpu.CompilerParams(
            dimension_semantics=("arbitrary",),
            vmem_limit_bytes=V7X_VMEM_LIMIT_BYTES),
        name="final_norm",
    )(*args)


def kernel(x, c, norm1_g, norm2_g, ada_w, ada_b, ab_w_in, pool_w, pool_scale, conf_conv_w, conf_conv_b, conf_ln_g, conf_ln_b, ab_w_out, cd_w_in, sconv_w, gmlp_ln_g, gmlp_ln_b, gmlp_ws, gmlp_bs, cd_w_out, router_w, router_bias, exp_w_gate, exp_w_up, exp_w_down, final_g):
    bsz, seq_len, d = x.shape
    n_tok = bsz * seq_len
    tm = MIX_TILE
    tiles_per_seq = seq_len // tm
    xf = x.reshape(n_tok, d)

    mod = _ada_mod(c, ada_w, ada_b)
    mod4 = mod.reshape(mod.shape[0], bsz, 6, d)

    rw_hi = router_w.astype(BF16)
    rw_lo = (router_w - rw_hi.astype(F32)).astype(BF16)
    rwt = jnp.concatenate([rw_hi.T, rw_lo.T], axis=0)
    rbias = router_bias.reshape(N_EXPERTS, 1)
    fg = final_g.reshape(1, d)

    weights_ab = [
        ab_w_in[0].astype(BF16), pool_w[0].astype(BF16), pool_scale[0].reshape(1, D_HALF),
        conf_conv_w[0], conf_conv_b[0].reshape(1, D_HALF), conf_ln_g[0].reshape(1, D_HALF),
        conf_ln_b[0].reshape(1, D_HALF), ab_w_out[0].astype(BF16),
    ]
    scratch_ab = [pltpu.VMEM((POOL_HIST + tm, D_HALF), F32),
                  pltpu.VMEM((CONV_HIST + tm, D_HALF), F32)]
    bsf = jnp.repeat(gmlp_bs[0].T, POOL_GROUP, axis=1)
    weights_cd = [
        cd_w_in[0].astype(BF16), sconv_w[0], gmlp_ln_g[0].reshape(1, D_HALF),
        gmlp_ln_b[0].reshape(1, D_HALF), gmlp_ws[0], bsf, cd_w_out[0].astype(BF16),
    ]
    scratch_cd = [pltpu.VMEM((SCONV_HIST + tm, D_HALF), F32)]
    experts = (exp_w_gate, exp_w_up, exp_w_down)

    batches = range(bsz)
    stage_ab = []
    for b in batches:
        x_spec = pl.BlockSpec((tm, d), lambda i, b=b: (b * tiles_per_seq + i, 0))
        stage_ab.append(_mixer_call(
            _mixer_ab_kernel, 0, b, [xf], [x_spec], mod4, norm1_g[0:1], norm2_g[0:1],
            weights_ab, rwt, rbias, scratch_ab, seq_len, "mixer_ab"))
    y_pairs0 = _moe_rows(0, stage_ab, *experts)

    stage_cd = []
    for b in batches:
        x1, _, _, wsel0, _, _ = stage_ab[b]
        prev_mod_spec = pl.BlockSpec((1, 1, 6, d), lambda i, b=b: (0, b, 0, 0))
        stage_cd.append(_mixer_call(
            _mixer_cd_kernel, 1, b, [x1, y_pairs0[b], y_pairs0[b], wsel0, mod4],
            _combine_specs(seq_len, tm) + [prev_mod_spec], mod4, norm1_g[1:2], norm2_g[1:2],
            weights_cd, rwt, rbias, scratch_cd, seq_len, "mixer_cd"))
    y_pairs1 = _moe_rows(1, stage_cd, *experts)

    out = None
    for b in batches:
        x3, _, _, wsel1, _, _ = stage_cd[b]
        out = _final_call(1, b, bsz, x3, y_pairs1[b], wsel1, mod4, fg, out)
    return out.reshape(bsz, seq_len, d)
```

```python
import functools

import jax
import jax.numpy as jnp
from jax import lax
from jax.experimental import pallas as pl
from jax.experimental.pallas import tpu as pltpu
from jax.experimental.pallas import tpu_sc as plsc

D_MODEL = 1024
EPS = 1e-6
POOL_WINDOWS = (2, 4, 8, 16)
POOL_GROUP = 128
D_HALF = 512
CONF_KERNEL = 31
SCONV_KERNEL = 3
CHUNK = 128
GMLP_HEADS = 4
N_EXPERTS = 16
N_GROUPS = 4
EXPERTS_PER_GROUP = 4
TOP_K = 2
D_EXPERT = 512

V7X_LANES = 128
V7X_SUBLANES = 8
V7X_VMEM_LIMIT_BYTES = 56 * 1024 * 1024

MIX_TILE = 1024
FINAL_TILE = 1024
MOE_TILE = 512
MOE_STEP_TILES = 4
SC_CHUNK = 64
ROW_CHUNKS = D_MODEL // (2 * V7X_LANES)
CONV_HIST = 32
POOL_HIST = 16
SCONV_HIST = 8

BF16 = jnp.bfloat16
F32 = jnp.float32
U32 = jnp.uint32


def _rms_mod(x, g_row, shift_row, scale_row):
    ms = jnp.mean(x * x, axis=-1, keepdims=True)
    gain = g_row * (1.0 + scale_row)
    return (x * lax.rsqrt(ms + EPS)) * gain + shift_row


def _layer_norm(x, g_row, b_row):
    mu = jnp.mean(x, axis=-1, keepdims=True)
    xc = x - mu
    var = jnp.mean(xc * xc, axis=-1, keepdims=True)
    return xc * lax.rsqrt(var + EPS) * g_row + b_row


def _sigmoid(x):
    return 0.5 * jnp.tanh(0.5 * x) + 0.5


def _silu(x):
    return x * _sigmoid(x)


def _gelu_tanh(x):
    c = 0.7978845608028654
    return 0.5 * x * (1.0 + jnp.tanh(c * (x + 0.044715 * (x * x * x))))


def _shift_rows(x, r):
    n, c = x.shape
    if r == V7X_SUBLANES:
        return jnp.concatenate([x[:r], x[:n - r]], axis=0)
    g = x.reshape(n // V7X_SUBLANES, V7X_SUBLANES, c)
    rot = pltpu.roll(g, r, axis=1)
    prev = jnp.concatenate([rot[:1], rot[:-1]], axis=0)
    sub = lax.broadcasted_iota(jnp.int32, g.shape, 1)
    return jnp.where(sub < r, prev, rot).reshape(n, c)


def _load_words(ref, n_rows, row0=0):
    return jnp.concatenate(
        [ref[pl.ds(row0 * ROW_CHUNKS + c, n_rows, stride=ROW_CHUNKS), :]
         for c in range(ROW_CHUNKS)], axis=1)


def _store_words(ref, words, row0=0):
    n_rows = words.shape[0]
    for c in range(ROW_CHUNKS):
        ref[pl.ds(row0 * ROW_CHUNKS + c, n_rows, stride=ROW_CHUNKS), :] = (
            words[:, c * V7X_LANES:(c + 1) * V7X_LANES])


def _pack_rows(val):
    half = val.shape[1] // 2
    return pltpu.pack_elementwise([val[:, :half], val[:, half:]], packed_dtype=BF16)


def _unpack_rows(words):
    halves = [pltpu.unpack_elementwise(words, index=i, packed_dtype=BF16, unpacked_dtype=F32)
              for i in range(2)]
    return jnp.concatenate(halves, axis=1)


def _ada_kernel(ct_ref, w_ref, b_ref, o_ref):
    ct = ct_ref[...]
    cond = _silu(ct)
    w = w_ref[0]
    nb = ct.shape[1]
    for b in range(nb):
        col = cond[:, b:b + 1]
        o_ref[0, b:b + 1, :] = jnp.sum(col * w, axis=0, keepdims=True) + b_ref[0]


def _ada_mod(c, ada_w, ada_b):
    depth, d, six_d = ada_w.shape
    bsz = c.shape[0]
    nb = D_MODEL
    return pl.pallas_call(
        _ada_kernel,
        grid=(depth, six_d // nb),
        in_specs=[
            pl.BlockSpec((d, bsz), lambda l, j: (0, 0)),
            pl.BlockSpec((1, d, nb), lambda l, j: (l, 0, j)),
            pl.BlockSpec((1, 1, nb), lambda l, j: (l, 0, j)),
        ],
        out_specs=pl.BlockSpec((1, bsz, nb), lambda l, j: (l, 0, j)),
        out_shape=jax.ShapeDtypeStruct((depth, bsz, six_d), F32),
        compiler_params=pltpu.CompilerParams(
            dimension_semantics=("arbitrary", "arbitrary"),
            vmem_limit_bytes=V7X_VMEM_LIMIT_BYTES),
        name="ada_mod",
    )(c.T, ada_w, ada_b.reshape(depth, 1, six_d))


def _route(h2_bf, rwt_ref, rbias_ref, eidx_ref, wsel_ref, rank_ref, counts_ref, cnt_ref):
    nt = (((1,), (1,)), ((), ()))
    r = lax.dot_general(rwt_ref[...], h2_bf, nt, preferred_element_type=F32)
    logits = r[:N_EXPERTS] + r[N_EXPERTS:]
    m = jnp.max(logits, axis=0, keepdims=True)
    ex = jnp.exp(logits - m)
    probs = ex / jnp.sum(ex, axis=0, keepdims=True)
    sel = probs + rbias_ref[...]
    s = [sel[e:e + 1] for e in range(N_EXPERTS)]
    p = [probs[e:e + 1] for e in range(N_EXPERTS)]
    best = None
    gi = None
    for g in range(N_GROUPS):
        a, b, c, d = s[4 * g:4 * g + 4]
        hi1, lo1 = jnp.maximum(a, b), jnp.minimum(a, b)
        hi2, lo2 = jnp.maximum(c, d), jnp.minimum(c, d)
        top1 = jnp.maximum(hi1, hi2)
        top2 = jnp.maximum(jnp.minimum(hi1, hi2), jnp.maximum(lo1, lo2))
        score = top1 + top2
        if g == 0:
            best, gi = score, jnp.zeros(score.shape, jnp.int32)
        else:
            upd = score > best
            gi = jnp.where(upd, g, gi)
            best = jnp.where(upd, score, best)
    v, q = [], []
    for j in range(EXPERTS_PER_GROUP):
        vj, qj = s[j], p[j]
        for g in range(1, N_GROUPS):
            pick = gi == g
            vj = jnp.where(pick, s[4 * g + j], vj)
            qj = jnp.where(pick, p[4 * g + j], qj)
        v.append(vj)
        q.append(qj)
    i1 = jnp.zeros(gi.shape, jnp.int32)
    m1 = v[0]
    for j in range(1, EXPERTS_PER_GROUP):
        upd = v[j] > m1
        i1 = jnp.where(upd, j, i1)
        m1 = jnp.where(upd, v[j], m1)
    i2 = jnp.zeros(gi.shape, jnp.int32)
    m2 = jnp.full(m1.shape, -jnp.inf, F32)
    for j in range(EXPERTS_PER_GROUP):
        cand = (i1 != j) & (v[j] > m2)
        i2 = jnp.where(cand, j, i2)
        m2 = jnp.where(cand, v[j], m2)
    pa = q[0]
    pb = q[0]
    for j in range(1, EXPERTS_PER_GROUP):
        pa = jnp.where(i1 == j, q[j], pa)
        pb = jnp.where(i2 == j, q[j], pb)
    tot = pa + pb
    e0 = gi * EXPERTS_PER_GROUP + i1
    e1 = gi * EXPERTS_PER_GROUP + i2
    t = h2_bf.shape[0]
    eidx_ref[0:1, :] = e0
    eidx_ref[1:2, :] = e1
    w_rows = jnp.concatenate(
        [pa / tot, pb / tot, jnp.zeros((V7X_LANES - TOP_K, t), F32)], axis=0)
    wsel_ref[...] = w_rows.T

    e_iota = lax.broadcasted_iota(jnp.int32, (N_EXPERTS, t), 0)
    oh0 = e_iota == e0
    oh1 = e_iota == e1
    both = jnp.where(oh0 | oh1, 1.0, 0.0)
    r_i = lax.broadcasted_iota(jnp.int32, (V7X_LANES, V7X_LANES), 0)
    c_i = lax.broadcasted_iota(jnp.int32, (V7X_LANES, V7X_LANES), 1)
    before = jnp.where(r_i < c_i, 1.0, 0.0).astype(BF16)
    run = cnt_ref[...]
    rank0, rank1 = [], []
    for blk in range(t // V7X_LANES):
        lanes = slice(blk * V7X_LANES, (blk + 1) * V7X_LANES)
        b = both[:, lanes]
        pre = jnp.dot(b.astype(BF16), before, preferred_element_type=F32) + run
        rank0.append(jnp.sum(jnp.where(oh0[:, lanes], pre, 0.0), axis=0, keepdims=True))
        rank1.append(jnp.sum(jnp.where(oh1[:, lanes], pre, 0.0), axis=0, keepdims=True))
        run = run + jnp.sum(b, axis=1, keepdims=True)
    cnt_ref[...] = run
    rank_ref[0:1, :] = jnp.concatenate(rank0, axis=1).astype(jnp.int32)
    rank_ref[1:2, :] = jnp.concatenate(rank1, axis=1).astype(jnp.int32)
    counts_ref[...] = jnp.broadcast_to(run, counts_ref.shape).astype(jnp.int32)


def _finish_mixer(x, m, mod, n2g_ref, rwt_ref, rbias_ref,
                  x1_ref, h2_ref, eidx_ref, wsel_ref, rank_ref, counts_ref, cnt_ref):
    x1 = x + mod[2:3] * m
    x1_ref[...] = x1
    h2 = _rms_mod(x1, n2g_ref[...], mod[3:4], mod[4:5])
    h2_bf = h2.astype(BF16)
    _store_words(h2_ref, _pack_rows(h2))
    _route(h2_bf, rwt_ref, rbias_ref, eidx_ref, wsel_ref, rank_ref, counts_ref, cnt_ref)


def _moe_residual(x_ref, y0_ref, y1_ref, wt_ref, g2_row):
    tm = x_ref.shape[0]
    wt = wt_ref[...]
    y0 = _unpack_rows(_load_words(y0_ref, tm))
    y1 = _unpack_rows(_load_words(y1_ref, tm))
    y = wt[:, 0:1] * y0 + wt[:, 1:2] * y1
    return x_ref[...] + g2_row * y


def _mixer_ab_kernel(x_ref, mod_ref, n1g_ref, n2g_ref, win_ref, poolw_ref, pscale_ref,
                     convw_ref, convb_ref, lng_ref, lnb_ref, wout_ref, rwt_ref, rbias_ref,
                     x1_ref, h2_ref, eidx_ref, wsel_ref, rank_ref, counts_ref,
                     pool_ext, conv_ext, cnt_ref):
    seq_tile = pl.program_id(0)
    tm = x_ref.shape[0]

    @pl.when(seq_tile == 0)
    def _():
        pool_ext[0:POOL_HIST, :] = jnp.zeros((POOL_HIST, D_HALF), F32)
        conv_ext[0:CONV_HIST, :] = jnp.zeros((CONV_HIST, D_HALF), F32)
        cnt_ref[...] = jnp.zeros_like(cnt_ref)

    x = x_ref[...]
    mod = mod_ref[0, 0]
    h = _rms_mod(x, n1g_ref[...], mod[0:1], mod[1:2]).astype(BF16)
    z = jnp.dot(h, win_ref[...], preferred_element_type=F32)
    zp = z[:, :D_HALF]
    glu = z[:, D_HALF:2 * D_HALF] * _sigmoid(z[:, 2 * D_HALF:])
    pool_ext[POOL_HIST:POOL_HIST + tm, :] = zp
    conv_ext[CONV_HIST:CONV_HIST + tm, :] = glu

    row = lax.broadcasted_iota(jnp.int32, (tm, 1), 0)
    pos1 = (seq_tile * tm + row + 1).astype(F32)
    pool_out = []
    for g, w in enumerate(POOL_WINDOWS):
        cols = slice(g * POOL_GROUP, (g + 1) * POOL_GROUP)
        acc = pool_ext[:, cols]
        span = 1
        while span < w:
            acc = acc + _shift_rows(acc, span)
            span *= 2
        wsum = acc[POOL_HIST:POOL_HIST + tm]
        inv_cnt = 1.0 / jnp.minimum(pos1, float(w))
        diff = wsum * inv_cnt - zp[:, cols]
        po = jnp.dot(diff.astype(BF16), poolw_ref[g], preferred_element_type=F32)
        pool_out.append(po * pscale_ref[:, cols])

    convw = convw_ref[...]
    ext_rows = tm + V7X_SUBLANES
    conv = None
    for r in range(V7X_SUBLANES):
        vr = None
        for a in range(CONV_HIST // V7X_SUBLANES):
            lag = V7X_SUBLANES * a + r
            if lag >= CONF_KERNEL:
                continue
            k = CONF_KERNEL - 1 - lag
            start = CONV_HIST - V7X_SUBLANES - V7X_SUBLANES * a
            term = convw[k:k + 1, :] * conv_ext[start:start + ext_rows, :]
            vr = term if vr is None else vr + term
        if r:
            vr = _shift_rows(vr, r)
        conv = vr if conv is None else conv + vr
    conv = conv[V7X_SUBLANES:V7X_SUBLANES + tm] + convb_ref[...]
    conf = _silu(_layer_norm(conv, lng_ref[...], lnb_ref[...]))

    pool_ext[0:POOL_HIST, :] = zp[tm - POOL_HIST:tm]
    conv_ext[0:CONV_HIST, :] = glu[tm - CONV_HIST:tm]

    m = jnp.dot(conf.astype(BF16), wout_ref[D_HALF:, :], preferred_element_type=F32)
    for g in range(len(POOL_WINDOWS)):
        rows = slice(g * POOL_GROUP, (g + 1) * POOL_GROUP)
        m = m + jnp.dot(pool_out[g].astype(BF16), wout_ref[rows, :], preferred_element_type=F32)
    _finish_mixer(x, m, mod, n2g_ref, rwt_ref, rbias_ref,
                  x1_ref, h2_ref, eidx_ref, wsel_ref, rank_ref, counts_ref, cnt_ref)


def _mixer_cd_kernel(x_ref, y0_ref, y1_ref, wt_ref, modp_ref,
                     mod_ref, n1g_ref, n2g_ref, win_ref, sconvw_ref, lng_ref, lnb_ref,
                     ws_ref, bsf_ref, wout_ref, rwt_ref, rbias_ref,
                     x1_ref, h2_ref, eidx_ref, wsel_ref, rank_ref, counts_ref,
                     sconv_ext, cnt_ref):
    tm = x_ref.shape[0]

    @pl.when(pl.program_id(0) == 0)
    def _():
        sconv_ext[0:SCONV_HIST, :] = jnp.zeros((SCONV_HIST, D_HALF), F32)
        cnt_ref[...] = jnp.zeros_like(cnt_ref)

    x = _moe_residual(x_ref, y0_ref, y1_ref, wt_ref, modp_ref[0, 0][5:6])
    mod = mod_ref[0, 0]
    h = _rms_mod(x, n1g_ref[...], mod[0:1], mod[1:2]).astype(BF16)
    def proj(lo, hi):
        return jnp.dot(h, win_ref[:, lo:hi], preferred_element_type=F32)

    v = _layer_norm(_gelu_tanh(proj(4 * D_HALF, 5 * D_HALF)), lng_ref[...], lnb_ref[...])
    u = _gelu_tanh(proj(3 * D_HALF, 4 * D_HALF))
    ch = proj(D_HALF, 2 * D_HALF) * proj(2 * D_HALF, 3 * D_HALF)
    bg = proj(0, D_HALF)

    sconv_ext[SCONV_HIST:SCONV_HIST + tm, :] = ch
    sw = sconvw_ref[...]
    ext = sconv_ext[...]
    conv = sw[2:3, :] * ext
    conv = conv + sw[1:2, :] * _shift_rows(ext, 1)
    conv = conv + sw[0:1, :] * _shift_rows(ext, 2)
    sc_out = bg * conv[SCONV_HIST:SCONV_HIST + tm]
    sconv_ext[0:SCONV_HIST, :] = ch[tm - SCONV_HIST:tm]

    r_i = lax.broadcasted_iota(jnp.int32, (CHUNK, CHUNK), 0)
    c_i = lax.broadcasted_iota(jnp.int32, (CHUNK, CHUNK), 1)
    tril = c_i <= r_i
    wm = [jnp.where(tril, ws_ref[hd], 0.0).astype(BF16) for hd in range(GMLP_HEADS)]
    v_bf = v.astype(BF16)
    bsf = bsf_ref[...]
    gm_rows = []
    for n in range(tm // CHUNK):
        rows = slice(n * CHUNK, (n + 1) * CHUNK)
        heads = []
        for hd in range(GMLP_HEADS):
            cols = slice(hd * POOL_GROUP, (hd + 1) * POOL_GROUP)
            heads.append(jnp.dot(wm[hd], v_bf[rows, cols], preferred_element_type=F32))
        mixed = jnp.concatenate(heads, axis=1) + bsf
        gm_rows.append(u[rows] * mixed)
    gm_out = jnp.concatenate(gm_rows, axis=0)

    m = jnp.dot(sc_out.astype(BF16), wout_ref[:D_HALF, :], preferred_element_type=F32)
    m = m + jnp.dot(gm_out.astype(BF16), wout_ref[D_HALF:, :], preferred_element_type=F32)
    _finish_mixer(x, m, mod, n2g_ref, rwt_ref, rbias_ref,
                  x1_ref, h2_ref, eidx_ref, wsel_ref, rank_ref, counts_ref, cnt_ref)


def _const_spec(shape):
    nd = len(shape)
    return pl.BlockSpec(shape, lambda i: (0,) * nd)


def _mixer_call(kernel_fn, layer, batch, stream_inputs, stream_specs, mod4, n1g, n2g, weights,
                rwt, rbias, scratch, seq_len, name):
    n_tok = seq_len
    d = D_MODEL
    tm = MIX_TILE
    in_specs = stream_specs + [
        pl.BlockSpec((1, 1, 6, d), lambda i: (layer, batch, 0, 0)),
        _const_spec(n1g.shape),
        _const_spec(n2g.shape),
    ] + [_const_spec(w.shape) for w in weights] + [_const_spec(rwt.shape), _const_spec(rbias.shape)]
    out_specs = [
        pl.BlockSpec((tm, d), lambda i: (i, 0)),
        pl.BlockSpec((tm * ROW_CHUNKS, V7X_LANES), lambda i: (i, 0)),
        pl.BlockSpec((TOP_K, tm), lambda i: (0, i)),
        pl.BlockSpec((tm, V7X_LANES), lambda i: (i, 0)),
        pl.BlockSpec((TOP_K, tm), lambda i: (0, i)),
        pl.BlockSpec((N_EXPERTS, V7X_LANES), lambda i: (0, 0)),
    ]
    out_shape = [
        jax.ShapeDtypeStruct((n_tok, d), F32),
        jax.ShapeDtypeStruct((n_tok * ROW_CHUNKS, V7X_LANES), U32),
        jax.ShapeDtypeStruct((TOP_K, n_tok), jnp.int32),
        jax.ShapeDtypeStruct((n_tok, V7X_LANES), F32),
        jax.ShapeDtypeStruct((TOP_K, n_tok), jnp.int32),
        jax.ShapeDtypeStruct((N_EXPERTS, V7X_LANES), jnp.int32),
    ]
    return pl.pallas_call(
        kernel_fn,
        grid=(n_tok // tm,),
        in_specs=in_specs,
        out_specs=out_specs,
        out_shape=out_shape,
        scratch_shapes=scratch + [pltpu.VMEM((N_EXPERTS, 1), F32)],
        compiler_params=pltpu.CompilerParams(
            dimension_semantics=("arbitrary",),
            vmem_limit_bytes=V7X_VMEM_LIMIT_BYTES),
        name=name,
    )(*stream_inputs, mod4, n1g, n2g, *weights, rwt, rbias)


def _combine_specs(n_tok, tm):
    n_tiles = n_tok // tm
    return [
        pl.BlockSpec((tm, D_MODEL), lambda i: (i, 0)),
        pl.BlockSpec((tm * ROW_CHUNKS, V7X_LANES), lambda i: (i, 0)),
        pl.BlockSpec((tm * ROW_CHUNKS, V7X_LANES), lambda i: (n_tiles + i, 0)),
        pl.BlockSpec((tm, V7X_LANES), lambda i: (i, 0)),
    ]


SCHED_EXPERT, SCHED_VALID, SCHED_USED, SCHED_FIRST, SCHED_NEXT, SCHED_SLOT = range(6)
SCHED_ROWS = V7X_SUBLANES


def _plan_kernel(eidx_ref, rank_ref, counts_ref, pos_ref, sched_ref):
    i32 = jnp.int32
    shift = MOE_TILE.bit_length() - 1
    cnt = counts_ref[:, 0:1]
    padded = ((cnt + (MOE_TILE - 1)) >> shift) << shift
    seg = [padded[e:e + 1] for e in range(N_EXPERTS)]
    starts, ends = [], []
    run = jnp.zeros((1, 1), i32)
    for e in range(N_EXPERTS):
        starts.append(run)
        run = run + seg[e]
        ends.append(run)
    total = run
    nexts = [None] * N_EXPERTS
    nxt = jnp.full((1, 1), -1, i32)
    for e in reversed(range(N_EXPERTS)):
        nexts[e] = nxt
        nxt = jnp.where(seg[e] > 0, e, nxt)
    slots = []
    seen = jnp.zeros((1, 1), i32)
    for e in range(N_EXPERTS):
        slots.append(seen & 1)
        seen = seen + (seg[e] > 0).astype(i32)

    eidx = eidx_ref[...]
    pos = rank_ref[...]
    tile_row0 = lax.broadcasted_iota(i32, (1, V7X_LANES), 1) * MOE_TILE
    te = jnp.zeros((1, V7X_LANES), i32)
    for e in range(N_EXPERTS):
        pos = pos + jnp.where(eidx == e, starts[e], 0)
        te = te + (tile_row0 >= ends[e]).astype(i32)
    pos_ref[...] = pos
    te = jnp.minimum(te, N_EXPERTS - 1)

    def of_tile(per_expert):
        acc = jnp.zeros((1, V7X_LANES), i32)
        for e in range(N_EXPERTS):
            acc = acc + jnp.where(te == e, per_expert[e], 0)
        return acc

    valid_end = of_tile([starts[e] + cnt[e:e + 1] for e in range(N_EXPERTS)])
    used = tile_row0 < total
    rows = {
        SCHED_EXPERT: te,
        SCHED_VALID: jnp.clip(valid_end - tile_row0, 0, MOE_TILE),
        SCHED_USED: jnp.broadcast_to(total >> shift, (1, V7X_LANES)),
        SCHED_FIRST: (used & (of_tile(starts) == tile_row0)).astype(i32),
        SCHED_NEXT: of_tile(nexts),
        SCHED_SLOT: of_tile(slots),
    }
    for r in range(SCHED_ROWS):
        sched_ref[r:r + 1, :] = rows.get(r, jnp.zeros((1, V7X_LANES), i32))


def _sorted_positions(eidx, rank, counts):
    n_pairs = eidx.shape[0] * eidx.shape[1]
    n_rows = n_pairs + N_EXPERTS * MOE_TILE
    assert n_rows // MOE_TILE <= V7X_LANES
    pos, sched = pl.pallas_call(
        _plan_kernel,
        out_shape=[jax.ShapeDtypeStruct(eidx.shape, jnp.int32),
                   jax.ShapeDtypeStruct((SCHED_ROWS, V7X_LANES), jnp.int32)],
        name="moe_plan",
    )(eidx, rank, counts)
    return pos.reshape(n_pairs), sched, n_rows


def _sc_move_rows(scatter, src, pos_flat, n_out_rows, name):
    info = plsc.get_sparse_core_info()
    n_workers = info.num_cores * info.num_subcores
    n_pairs = pos_flat.shape[0]
    n_src = src.shape[0]
    per_worker = n_pairs // n_workers
    n_chunks = per_worker // SC_CHUNK
    assert per_worker * n_workers == n_pairs and n_chunks * SC_CHUNK == per_worker
    assert n_src % per_worker == 0
    idx = pos_flat.reshape(n_workers, n_chunks, SC_CHUNK)
    mesh = plsc.VectorSubcoreMesh(core_axis_name="core", subcore_axis_name="subcore")

    @functools.partial(
        pl.kernel,
        out_type=jax.ShapeDtypeStruct((n_out_rows, ROW_CHUNKS, V7X_LANES), U32),
        mesh=mesh,
        scratch_types=[
            pltpu.VMEM((n_chunks, SC_CHUNK), jnp.int32),
            pltpu.VMEM((2, SC_CHUNK, ROW_CHUNKS, V7X_LANES), U32),
            pltpu.SemaphoreType.DMA((2,)),
            pltpu.SemaphoreType.DMA((2,)),
        ],
        name=name)
    def move(src_hbm, i_hbm, o_hbm, idx_v, buf, in_sem, out_sem):
        wid = lax.axis_index("subcore") * info.num_cores + lax.axis_index("core")
        base = wid * per_worker
        src_base = lax.rem(base, n_src)
        pltpu.sync_copy(i_hbm.at[wid], idx_v)

        def fetch(s, slot):
            if scatter:
                rows = src_hbm.at[pl.ds(src_base + s * SC_CHUNK, SC_CHUNK)]
            else:
                rows = src_hbm.at[idx_v.at[s]]
            return pltpu.make_async_copy(rows, buf.at[slot], in_sem.at[slot])

        def flush(s, slot):
            if scatter:
                rows = o_hbm.at[idx_v.at[s]]
            else:
                rows = o_hbm.at[pl.ds(base + s * SC_CHUNK, SC_CHUNK)]
            return pltpu.make_async_copy(buf.at[slot], rows, out_sem.at[slot])

        fetch(0, 0).start()
        for s in range(n_chunks):
            slot = s % 2
            fetch(s, slot).wait()
            flush(s, slot).start()
            if s + 1 < n_chunks:
                if s >= 1:
                    flush(s - 1, 1 - slot).wait()
                fetch(s + 1, 1 - slot).start()
        flush(n_chunks - 2, n_chunks % 2).wait()
        flush(n_chunks - 1, (n_chunks - 1) % 2).wait()

    return move(src, idx)


def _experts_kernel(layer, sched_ref, x_ref, wg_hbm, wu_hbm, wd_hbm, y_ref,
                    wg_f32, wu_f32, wd_f32, wg_bf, wu_bf, wd_bf, sems):
    tm = MOE_TILE

    def weight_copies(expert, slot):
        pairs = ((wg_hbm, wg_f32), (wu_hbm, wu_f32), (wd_hbm, wd_f32))
        return [pltpu.make_async_copy(src.at[layer, expert], dst.at[slot], sems.at[slot, m])
                for m, (src, dst) in enumerate(pairs)]

    for part in range(MOE_STEP_TILES):
        j = pl.program_id(0) * MOE_STEP_TILES + part
        used = j < sched_ref[SCHED_USED, j]
        expert = sched_ref[SCHED_EXPERT, j]

        @pl.when(used & (sched_ref[SCHED_FIRST, j] == 1))
        def _():
            slot = sched_ref[SCHED_SLOT, j]
            next_expert = sched_ref[SCHED_NEXT, j]

            @pl.when(j == 0)
            def _():
                for cp in weight_copies(expert, slot):
                    cp.start(priority=1)

            for cp in weight_copies(expert, slot):
                cp.wait()
            wg_bf[...] = wg_f32[slot].astype(BF16)
            wu_bf[...] = wu_f32[slot].astype(BF16)
            wd_bf[...] = wd_f32[slot].astype(BF16)

            @pl.when(next_expert >= 0)
            def _():
                for cp in weight_copies(next_expert, 1 - slot):
                    cp.start(priority=1)

        @pl.when(used)
        def _():
            row = lax.broadcasted_iota(jnp.int32, (tm, 1), 0)
            words = jnp.where(row < sched_ref[SCHED_VALID, j],
                              _load_words(x_ref, tm, part * tm), jnp.uint32(0))
            h = _unpack_rows(words).astype(BF16)
            a = jnp.dot(h, wg_bf[...], preferred_element_type=F32)
            b = jnp.dot(h, wu_bf[...], preferred_element_type=F32)
            t = (_silu(a) * b).astype(BF16)
            y = jnp.dot(t, wd_bf[...], preferred_element_type=F32)
            _store_words(y_ref, _pack_rows(y), part * tm)

        @pl.when(jnp.logical_not(used))
        def _():
            lo = part * tm * ROW_CHUNKS
            y_ref[lo:lo + tm * ROW_CHUNKS, :] = jnp.zeros((tm * ROW_CHUNKS, V7X_LANES), U32)


def _experts_call(layer, rows, sched, w_gate, w_up, w_down):
    n_rows = rows.shape[0]
    tm = MOE_TILE * MOE_STEP_TILES
    d = D_MODEL
    assert n_rows % tm == 0
    rows2 = rows.reshape(n_rows * ROW_CHUNKS, V7X_LANES)

    def last_used_step(sc):
        return (sc[SCHED_USED, 0] + MOE_STEP_TILES - 1) // MOE_STEP_TILES - 1

    grid_spec = pltpu.PrefetchScalarGridSpec(
        num_scalar_prefetch=1,
        grid=(n_rows // tm,),
        in_specs=[
            pl.BlockSpec((tm * ROW_CHUNKS, V7X_LANES),
                         lambda j, sc: (jnp.minimum(j, last_used_step(sc)), 0)),
            pl.BlockSpec(memory_space=pl.ANY),
            pl.BlockSpec(memory_space=pl.ANY),
            pl.BlockSpec(memory_space=pl.ANY),
        ],
        out_specs=pl.BlockSpec((tm * ROW_CHUNKS, V7X_LANES), lambda j, sc: (j, 0)),
        scratch_shapes=[
            pltpu.VMEM((2, d, D_EXPERT), F32),
            pltpu.VMEM((2, d, D_EXPERT), F32),
            pltpu.VMEM((2, D_EXPERT, d), F32),
            pltpu.VMEM((d, D_EXPERT), BF16),
            pltpu.VMEM((d, D_EXPERT), BF16),
            pltpu.VMEM((D_EXPERT, d), BF16),
            pltpu.SemaphoreType.DMA((2, 3)),
        ],
    )
    y = pl.pallas_call(
        functools.partial(_experts_kernel, layer),
        grid_spec=grid_spec,
        out_shape=jax.ShapeDtypeStruct((n_rows * ROW_CHUNKS, V7X_LANES), U32),
        compiler_params=pltpu.CompilerParams(
            dimension_semantics=("arbitrary",),
            vmem_limit_bytes=V7X_VMEM_LIMIT_BYTES),
        name="experts_l%d" % layer,
    )(sched, rows2, w_gate, w_up, w_down)
    return y.reshape(n_rows, ROW_CHUNKS, V7X_LANES)


def _moe_rows(layer, mixer_outs, w_gate, w_up, w_down):
    plans = [_sorted_positions(eidx, rank, counts)
             for (_, _, eidx, _, rank, counts) in mixer_outs]
    sorted_rows = []
    for (_, h_rows, eidx, _, _, _), (pos_flat, _, n_rows) in zip(mixer_outs, plans):
        n_tok = eidx.shape[1]
        sorted_rows.append(_sc_move_rows(
            True, h_rows.reshape(n_tok, ROW_CHUNKS, V7X_LANES), pos_flat, n_rows, "sc_scatter_rows"))
    y_sorted = [_experts_call(layer, rows, sched, w_gate, w_up, w_down)
                for rows, (_, sched, _) in zip(sorted_rows, plans)]
    y_pairs = []
    for y, (pos_flat, _, _) in zip(y_sorted, plans):
        n_pairs = pos_flat.shape[0]
        moved = _sc_move_rows(False, y, pos_flat, n_pairs, "sc_gather_rows")
        y_pairs.append(moved.reshape(n_pairs * ROW_CHUNKS, V7X_LANES))
    return y_pairs


def _final_kernel(x_ref, y0_ref, y1_ref, wt_ref, modp_ref, fg_ref, *rest):
    o_ref = rest[-1]
    y = _moe_residual(x_ref, y0_ref, y1_ref, wt_ref, modp_ref[0, 0][5:6])
    ms = jnp.mean(y * y, axis=-1, keepdims=True)
    o_ref[...] = y * lax.rsqrt(ms + EPS) * fg_ref[...]


def _final_call(layer, batch, n_batch, x, y_pairs, wt, mod4, fg, out_prev):
    seq_len, d = x.shape
    tm = FINAL_TILE
    tiles_per_seq = seq_len // tm
    in_specs = _combine_specs(seq_len, tm) + [
        pl.BlockSpec((1, 1, 6, d), lambda i: (layer, batch, 0, 0)),
        _const_spec(fg.shape),
    ]
    args = [x, y_pairs, y_pairs, wt, mod4, fg]
    aliases = {}
    if out_prev is not None:
        in_specs.append(pl.BlockSpec(memory_space=pl.ANY))
        aliases = {len(args): 0}
        args.append(out_prev)
    return pl.pallas_call(
        _final_kernel,
        grid=(tiles_per_seq,),
        in_specs=in_specs,
        out_specs=pl.BlockSpec((tm, d), lambda i: (batch * tiles_per_seq + i, 0)),
        out_shape=jax.ShapeDtypeStruct((n_batch * seq_len, d), F32),
        input_output_aliases=aliases,
        compiler_params=pltpu.CompilerParams(
            dimension_semantics=("arbitrary",),
            vmem_limit_bytes=V7X_VMEM_LIMIT_BYTES),
        name="final_norm",
    )(*args)


def kernel(x, c, norm1_g, norm2_g, ada_w, ada_b, ab_w_in, pool_w, pool_scale, conf_conv_w, conf_conv_b, conf_ln_g, conf_ln_b, ab_w_out, cd_w_in, sconv_w, gmlp_ln_g, gmlp_ln_b, gmlp_ws, gmlp_bs, cd_w_out, router_w, router_bias, exp_w_gate, exp_w_up, exp_w_down, final_g):
    bsz, seq_len, d = x.shape
    n_tok = bsz * seq_len
    tm = MIX_TILE
    tiles_per_seq = seq_len // tm
    xf = x.reshape(n_tok, d)

    mod = _ada_mod(c, ada_w, ada_b)
    mod4 = mod.reshape(mod.shape[0], bsz, 6, d)

    rw_hi = router_w.astype(BF16)
    rw_lo = (router_w - rw_hi.astype(F32)).astype(BF16)
    rwt = jnp.concatenate([rw_hi.T, rw_lo.T], axis=0)
    rbias = router_bias.reshape(N_EXPERTS, 1)
    fg = final_g.reshape(1, d)

    weights_ab = [
        ab_w_in[0].astype(BF16), pool_w[0].astype(BF16), pool_scale[0].reshape(1, D_HALF),
        conf_conv_w[0], conf_conv_b[0].reshape(1, D_HALF), conf_ln_g[0].reshape(1, D_HALF),
        conf_ln_b[0].reshape(1, D_HALF), ab_w_out[0].astype(BF16),
    ]
    scratch_ab = [pltpu.VMEM((POOL_HIST + tm, D_HALF), F32),
                  pltpu.VMEM((CONV_HIST + tm, D_HALF), F32)]
    bsf = jnp.repeat(gmlp_bs[0].T, POOL_GROUP, axis=1)
    weights_cd = [
        cd_w_in[0].astype(BF16), sconv_w[0], gmlp_ln_g[0].reshape(1, D_HALF),
        gmlp_ln_b[0].reshape(1, D_HALF), gmlp_ws[0], bsf, cd_w_out[0].astype(BF16),
    ]
    scratch_cd = [pltpu.VMEM((SCONV_HIST + tm, D_HALF), F32)]
    experts = (exp_w_gate, exp_w_up, exp_w_down)

    batches = range(bsz)
    stage_ab = []
    for b in batches:
        x_spec = pl.BlockSpec((tm, d), lambda i, b=b: (b * tiles_per_seq + i, 0))
        stage_ab.append(_mixer_call(
            _mixer_ab_kernel, 0, b, [xf], [x_spec], mod4, norm1_g[0:1], norm2_g[0:1],
            weights_ab, rwt, rbias, scratch_ab, seq_len, "mixer_ab"))
    y_pairs0 = _moe_rows(0, stage_ab, *experts)

    stage_cd = []
    for b in batches:
        x1, _, _, wsel0, _, _ = stage_ab[b]
        prev_mod_spec = pl.BlockSpec((1, 1, 6, d), lambda i, b=b: (0, b, 0, 0))
        stage_cd.append(_mixer_call(
            _mixer_cd_kernel, 1, b, [x1, y_pairs0[b], y_pairs0[b], wsel0, mod4],
            _combine_specs(seq_len, tm) + [prev_mod_spec], mod4, norm1_g[1:2], norm2_g[1:2],
            weights_cd, rwt, rbias, scratch_cd, seq_len, "mixer_cd"))
    y_pairs1 = _moe_rows(1, stage_cd, *experts)

    out = None
    for b in batches:
        x3, _, _, wsel1, _, _ = stage_cd[b]
        out = _final_call(1, b, bsz, x3, y_pairs1[b], wsel1, mod4, fg, out)
    return out.reshape(bsz, seq_len, d)
```
